```python
import jax
import jax.numpy as jnp
from jax import lax
import numpy as np

D_MODEL = 1024
BATCH = 2
SEQ = 8192
DEPTH = 1
DEC_BATCH = 32
DEC_SEQ = 4
PAST_LEN = 16384
PAGE_SIZE = 128

GLA_HEADS = 4
GLA_DK = D_MODEL // (2 * GLA_HEADS)
GLA_DV = D_MODEL // GLA_HEADS
GLA_RANK = 16
GLA_TAU = 16.0
GLA_CHUNK = 64
SWA_GROUPS = ((128, 1), (512, 4), (2048, 16))
SWA_HPG = 4
SWA_HEAD_DIM = 64
SWA_HEADS = len(SWA_GROUPS) * SWA_HPG
SWA_WIDTH = SWA_HEADS * SWA_HEAD_DIM
SWA_OUT = SWA_HPG * SWA_HEAD_DIM
SWA_BLOCK = 128
ROPE_THETA = 10000.0
N_GROUPS = 4
EXPERTS_PER_GROUP = 8
N_EXPERTS = N_GROUPS * EXPERTS_PER_GROUP
TOP_K = 2
D_EXPERT = D_MODEL // 2
EPS = 1e-6

IN_SPLITS = (GLA_HEADS * GLA_DK, GLA_HEADS * GLA_DK, GLA_HEADS * GLA_DV, GLA_RANK, GLA_HEADS * GLA_DV,
             SWA_WIDTH, SWA_WIDTH, SWA_WIDTH, D_MODEL, D_MODEL)
IN_WIDTH = sum(IN_SPLITS)

kernel_name = 'hybrid_gla_dilated_swa_hmoe_step'


def rms_norm(x, w):
    xf = x.astype(jnp.float32)
    y = xf * lax.rsqrt(jnp.mean(xf * xf, axis=-1, keepdims=True) + EPS)
    return (y * w.astype(jnp.float32)).astype(x.dtype)


def rope(x, pos):
    half = x.shape[-1] // 2
    inv_freq = ROPE_THETA ** (-jnp.arange(half, dtype=jnp.float32) / half)
    ang = pos.astype(jnp.float32)[:, None] * inv_freq[None, :]
    cos = jnp.cos(ang)[None, :, None, :]
    sin = jnp.sin(ang)[None, :, None, :]
    xf = x.astype(jnp.float32)
    x1, x2 = xf[..., :half], xf[..., half:]
    return jnp.concatenate([x1 * cos - x2 * sin, x2 * cos + x1 * sin], axis=-1).astype(x.dtype)


def project(h, pos, w_in, w_gla_lr, b_gla_lr, q_norm_w, k_norm_w):
    B, L, _ = h.shape
    cuts = [int(c) for c in np.cumsum(IN_SPLITS)[:-1]]
    gq, gk, gv, glr, gog, sq, sk, sv, ga, gb = jnp.split(h @ w_in, cuts, axis=-1)
    heads = lambda t, n: t.reshape(B, L, n, t.shape[-1] // n)
    log_a = jax.nn.log_sigmoid((glr @ w_gla_lr + b_gla_lr).astype(jnp.float32)) / GLA_TAU
    q = rope(rms_norm(heads(sq, SWA_HEADS), q_norm_w), pos)
    k = rope(rms_norm(heads(sk, SWA_HEADS), k_norm_w), pos)
    return (heads(gq, GLA_HEADS), heads(gk, GLA_HEADS), heads(gv, GLA_HEADS), heads(log_a, GLA_HEADS), gog,
            q, k, heads(sv, SWA_HEADS), ga, gb)


def gla_chunked(q, k, v, log_a, s0):
    B, L, H = q.shape[:3]
    C = min(GLA_CHUNK, L)
    n = -(-L // C)
    pad = n * C - L

    def prep(t):
        t = jnp.pad(t.astype(jnp.float32), ((0, 0), (0, pad), (0, 0), (0, 0)))
        return t.reshape(B, n, C, H, t.shape[-1]).transpose(0, 3, 1, 2, 4)

    q, k, v, la = prep(q), prep(k), prep(v), prep(log_a)
    q = q * GLA_DK ** -0.5
    b = jnp.cumsum(la, axis=3)
    b_last = b[:, :, :, -1:, :]
    qd = q * jnp.exp(b)
    kd = k * jnp.exp(-b)
    causal = np.tril(np.ones((C, C), dtype=bool))
    att = jnp.where(causal, jnp.einsum('bhncd,bhnsd->bhncs', qd, kd), 0.0)
    o_intra = jnp.einsum('bhncs,bhnse->bhnce', att, v)
    ds = jnp.einsum('bhnsd,bhnse->nbhde', k * jnp.exp(b_last - b), v)
    decay = jnp.exp(b_last[:, :, :, 0, :]).transpose(2, 0, 1, 3)

    def step(s, inp):
        dec, d = inp
        return dec[..., None] * s + d, s

    s_fin, s_prev = lax.scan(step, s0.astype(jnp.float32), (decay, ds))
    o_inter = jnp.einsum('bhncd,nbhde->bhnce', qd, s_prev)
    o = (o_intra + o_inter).transpose(0, 2, 3, 1, 4).reshape(B, n * C, H, v.shape[-1])[:, :L]
    return o, s_fin


def gla_branch(gq, gk, gv, log_a, gog, s0, gla_onorm_w, w_branch_a):
    o, s = gla_chunked(gq, gk, gv, log_a, s0)
    B, L = gog.shape[:2]
    o = rms_norm(o, gla_onorm_w).astype(gog.dtype).reshape(B, L, GLA_HEADS * GLA_DV) * jax.nn.silu(gog)
    return o @ w_branch_a, s


def dilated_band_attention(q, k, v, dil, n_back):
    B, S, H, Dh = q.shape
    n = S // dil
    to_res = lambda t: t.reshape(B, n, dil, H, Dh).transpose(0, 2, 3, 1, 4).reshape(B * dil, H, n, Dh)
    q, k, v = to_res(q), to_res(k), to_res(v)
    blk = SWA_BLOCK
    nb = -(-n // blk)
    pad = nb * blk - n
    BD = B * dil
    qb = jnp.pad(q, ((0, 0), (0, 0), (0, pad), (0, 0))).reshape(BD, H, nb, blk, Dh)
    kb = jnp.pad(k, ((0, 0), (0, 0), (blk, pad), (0, 0))).reshape(BD, H, nb + 1, blk, Dh)
    vb = jnp.pad(v, ((0, 0), (0, 0), (blk, pad), (0, 0))).reshape(BD, H, nb + 1, blk, Dh)
    kwin = jnp.concatenate([kb[:, :, :-1], kb[:, :, 1:]], axis=3)
    vwin = jnp.concatenate([vb[:, :, :-1], vb[:, :, 1:]], axis=3)
    qpos = np.arange(blk)[:, None] + blk
    kpos = np.arange(2 * blk)[None, :]
    dist = qpos - kpos
    band = (dist >= 0) & (dist <= n_back)
    valid = band[None] & ((np.arange(nb)[:, None, None] * blk + kpos[None] - blk) >= 0)
    s = jnp.einsum('xhnqd,xhnkd->xhnqk', qb, kwin, preferred_element_type=jnp.float32) * (Dh ** -0.5)
    s = jnp.where(valid, s, -jnp.inf)
    lse = jax.nn.logsumexp(s, axis=-1)
    p = jnp.exp(s - lse[..., None])
    o = jnp.einsum('xhnqk,xhnkd->xhnqd', p, vwin.astype(jnp.float32))
    o = o.reshape(BD, H, nb * blk, Dh)[:, :, :n].reshape(B, dil, H, n, Dh).transpose(0, 3, 1, 2, 4).reshape(B, S, H, Dh)
    lse = lse.reshape(BD, H, nb * blk)[:, :, :n].reshape(B, dil, H, n).transpose(0, 3, 1, 2).reshape(B, S, H)
    return o, lse


def dilated_gather_attention(q, k_new, v_new, k_buf, v_buf, dil, n_back):
    L = q.shape[1]
    Lb = k_buf.shape[1]
    kc = jnp.concatenate([k_buf, k_new], axis=1)
    vc = jnp.concatenate([v_buf, v_new], axis=1)
    idx = Lb + np.arange(L)[:, None] - dil * np.arange(n_back + 1)[None, :]
    valid = idx >= 0
    idx_c = np.maximum(idx, 0)
    kg = kc[:, idx_c]
    vg = vc[:, idx_c]
    s = jnp.einsum('blhd,blmhd->blhm', q, kg, preferred_element_type=jnp.float32) * (q.shape[-1] ** -0.5)
    s = jnp.where(valid[None, :, None, :], s, -jnp.inf)
    lse = jax.nn.logsumexp(s, axis=-1)
    p = jnp.exp(s - lse[..., None])
    o = jnp.einsum('blhm,blmhd->blhd', p, vg.astype(jnp.float32))
    return o, lse, kc[:, L:], vc[:, L:]


def combine_groups(outs, lses, w_branch_b):
    w = jax.nn.softmax(jnp.stack(lses, axis=0), axis=0)
    o = jnp.einsum('gblh,gblhd->blhd', w, jnp.stack(outs, axis=0))
    B, L = o.shape[:2]
    return o.reshape(B, L, SWA_OUT).astype(w_branch_b.dtype) @ w_branch_b


def gated_merge(y_a, y_b, g_a, g_b, w_out):
    return (jax.nn.sigmoid(g_a) * y_a + jax.nn.sigmoid(g_b) * y_b) @ w_out


def hier_moe(h, w_rg, b_rg, w_re, b_re, w_g, w_u, w_d):
    T = h.shape[0]
    gl = (h @ w_rg).astype(jnp.float32) + b_rg.astype(jnp.float32)
    gp = jax.nn.softmax(gl, axis=-1)
    g_idx = jnp.argmax(gl, axis=-1)
    g_w = jnp.take_along_axis(gp, g_idx[:, None], axis=-1)[:, 0]
    el = ((h @ w_re).astype(jnp.float32) + b_re.astype(jnp.float32)).reshape(T, N_GROUPS, EXPERTS_PER_GROUP)
    el = jnp.take_along_axis(el, g_idx[:, None, None], axis=1)[:, 0]
    top_v, top_i = lax.top_k(el, TOP_K)
    top_w = jax.nn.softmax(top_v, axis=-1) * g_w[:, None]
    within = jnp.sum(jax.nn.one_hot(top_i, EXPERTS_PER_GROUP, dtype=jnp.float32) * top_w[..., None], axis=1)
    comb = (jax.nn.one_hot(g_idx, N_GROUPS, dtype=jnp.float32)[:, :, None] * within[:, None, :]).reshape(T, N_EXPERTS)

    def body(acc, inp):
        wg, wu, wd, c = inp
        a = jax.nn.silu(h @ wg) * (h @ wu)
        return acc + c[:, None].astype(h.dtype) * (a @ wd), None

    y, _ = lax.scan(body, jnp.zeros_like(h), (w_g, w_u, w_d, comb.T))
    return y


def setup_inputs(seed: int = 0) -> dict:
    key = jax.random.key(seed)
    ks = jax.random.split(key, 32)
    f32 = jnp.float32
    nrm = lambda k, shape, scale: jax.random.normal(k, shape, f32) * scale
    lb = [min(w, PAST_LEN) for w, _ in SWA_GROUPS]
    cshape = lambda n: (DEPTH, DEC_BATCH, n, SWA_HPG, SWA_HEAD_DIM)
    return {
        'x_prompt': nrm(ks[0], (BATCH, SEQ, D_MODEL), 1.0),
        'x_sample': nrm(ks[1], (DEC_BATCH, DEC_SEQ, D_MODEL), 1.0),
        'state_gla': nrm(ks[2], (DEPTH, DEC_BATCH, GLA_HEADS, GLA_DK, GLA_DV), 0.5),
        'cache_swa_k0': nrm(ks[3], cshape(lb[0]), 1.0),
        'cache_swa_v0': nrm(ks[4], cshape(lb[0]), 1.0),
        'cache_swa_k1': nrm(ks[5], cshape(lb[1]), 1.0),
        'cache_swa_v1': nrm(ks[6], cshape(lb[1]), 1.0),
        'cache_swa_k2': nrm(ks[7], cshape(lb[2]), 1.0),
        'cache_swa_v2': nrm(ks[8], cshape(lb[2]), 1.0),
        'ln1_w': 1.0 + nrm(ks[9], (DEPTH, D_MODEL), 0.02),
        'w_in': nrm(ks[10], (DEPTH, D_MODEL, IN_WIDTH), D_MODEL ** -0.5),
        'w_gla_lr': nrm(ks[11], (DEPTH, GLA_RANK, GLA_HEADS * GLA_DK), GLA_RANK ** -0.5),
        'b_gla_lr': nrm(ks[12], (DEPTH, GLA_HEADS * GLA_DK), 0.1),
        'gla_onorm_w': 1.0 + nrm(ks[13], (DEPTH, GLA_DV), 0.02),
        'q_norm_w': 1.0 + nrm(ks[14], (DEPTH, SWA_HEAD_DIM), 0.02),
        'k_norm_w': 1.0 + nrm(ks[15], (DEPTH, SWA_HEAD_DIM), 0.02),
        'w_branch_a': nrm(ks[16], (DEPTH, GLA_HEADS * GLA_DV, D_MODEL), (GLA_HEADS * GLA_DV) ** -0.5),
        'w_branch_b': nrm(ks[17], (DEPTH, SWA_OUT, D_MODEL), SWA_OUT ** -0.5),
        'w_out': nrm(ks[18], (DEPTH, D_MODEL, D_MODEL), D_MODEL ** -0.5),
        'ln2_w': 1.0 + nrm(ks[19], (DEPTH, D_MODEL), 0.02),
        'w_router_group': nrm(ks[20], (DEPTH, D_MODEL, N_GROUPS), D_MODEL ** -0.5),
        'b_router_group': nrm(ks[21], (DEPTH, N_GROUPS), 0.01),
        'w_router_expert': nrm(ks[22], (DEPTH, D_MODEL, N_EXPERTS), D_MODEL ** -0.5),
        'b_router_expert': nrm(ks[23], (DEPTH, N_EXPERTS), 0.01),
        'w_exp_gate': nrm(ks[24], (DEPTH, N_EXPERTS, D_MODEL, D_EXPERT), D_MODEL ** -0.5),
        'w_exp_up': nrm(ks[25], (DEPTH, N_EXPERTS, D_MODEL, D_EXPERT), D_MODEL ** -0.5),
        'w_exp_down': nrm(ks[26], (DEPTH, N_EXPERTS, D_EXPERT, D_MODEL), D_EXPERT ** -0.5),
    }


def reference(x_prompt, x_sample, state_gla, cache_swa_k0, cache_swa_v0, cache_swa_k1, cache_swa_v1, cache_swa_k2, cache_swa_v2,
              ln1_w, w_in, w_gla_lr, b_gla_lr, gla_onorm_w, q_norm_w, k_norm_w, w_branch_a, w_branch_b, w_out, ln2_w,
              w_router_group, b_router_group, w_router_expert, b_router_expert, w_exp_gate, w_exp_up, w_exp_down):
    xp, xs = x_prompt, x_sample
    B, S, D = xp.shape
    Bd, Ls, _ = xs.shape
    pos_p = jnp.arange(S, dtype=jnp.int32)
    pos_s = PAST_LEN + jnp.arange(Ls, dtype=jnp.int32)
    k_bufs = (cache_swa_k0, cache_swa_k1, cache_swa_k2)
    v_bufs = (cache_swa_v0, cache_swa_v1, cache_swa_v2)
    n_g = len(SWA_GROUPS)
    gla_p, gla_s = [], []
    kp_new = [[] for _ in range(n_g)]
    vp_new = [[] for _ in range(n_g)]
    ks_new = [[] for _ in range(n_g)]
    vs_new = [[] for _ in range(n_g)]
    for l in range(DEPTH):
        hp = rms_norm(xp, ln1_w[l])
        hs = rms_norm(xs, ln1_w[l])
        pp = project(hp, pos_p, w_in[l], w_gla_lr[l], b_gla_lr[l], q_norm_w[l], k_norm_w[l])
        ps = project(hs, pos_s, w_in[l], w_gla_lr[l], b_gla_lr[l], q_norm_w[l], k_norm_w[l])
        s0 = jnp.zeros((B, GLA_HEADS, GLA_DK, GLA_DV), jnp.float32)
        ya_p, st_p = gla_branch(pp[0], pp[1], pp[2], pp[3], pp[4], s0, gla_onorm_w[l], w_branch_a[l])
        ya_s, st_s = gla_branch(ps[0], ps[1], ps[2], ps[3], ps[4], state_gla[l], gla_onorm_w[l], w_branch_a[l])
        gla_p.append(st_p.astype(xp.dtype))
        gla_s.append(st_s.astype(state_gla.dtype))
        o_p, lse_p, o_s, lse_s = [], [], [], []
        for g, (win, dil) in enumerate(SWA_GROUPS):
            hsl = slice(g * SWA_HPG, (g + 1) * SWA_HPG)
            q, k, v = pp[5][:, :, hsl], pp[6][:, :, hsl], pp[7][:, :, hsl]
            o, lse = dilated_band_attention(q, k, v, dil, win // dil)
            o_p.append(o)
            lse_p.append(lse)
            keep = min(win, S)
            kp_new[g].append(k[:, S - keep:])
            vp_new[g].append(v[:, S - keep:])
            q, k, v = ps[5][:, :, hsl], ps[6][:, :, hsl], ps[7][:, :, hsl]
            o, lse, kc, vc = dilated_gather_attention(q, k, v, k_bufs[g][l], v_bufs[g][l], dil, win // dil)
            o_s.append(o)
            lse_s.append(lse)
            ks_new[g].append(kc)
            vs_new[g].append(vc)
        yb_p = combine_groups(o_p, lse_p, w_branch_b[l])
        yb_s = combine_groups(o_s, lse_s, w_branch_b[l])
        xp = xp + gated_merge(ya_p, yb_p, pp[8], pp[9], w_out[l])
        xs = xs + gated_merge(ya_s, yb_s, ps[8], ps[9], w_out[l])
        h2 = jnp.concatenate([rms_norm(xp, ln2_w[l]).reshape(B * S, D), rms_norm(xs, ln2_w[l]).reshape(Bd * Ls, D)], axis=0)
        y2 = hier_moe(h2, w_router_group[l], b_router_group[l], w_router_expert[l], b_router_expert[l],
                      w_exp_gate[l], w_exp_up[l], w_exp_down[l])
        xp = xp + y2[:B * S].reshape(B, S, D)
        xs = xs + y2[B * S:].reshape(Bd, Ls, D)
    state_gla_prompt = jnp.stack(gla_p)
    state_gla_sample = jnp.stack(gla_s)
    k0_p, v0_p = jnp.stack(kp_new[0]), jnp.stack(vp_new[0])
    k1_p, v1_p = jnp.stack(kp_new[1]), jnp.stack(vp_new[1])
    k2_p, v2_p = jnp.stack(kp_new[2]), jnp.stack(vp_new[2])
    k0_s, v0_s = jnp.stack(ks_new[0]), jnp.stack(vs_new[0])
    k1_s, v1_s = jnp.stack(ks_new[1]), jnp.stack(vs_new[1])
    k2_s, v2_s = jnp.stack(ks_new[2]), jnp.stack(vs_new[2])
    return (xp, xs, state_gla_prompt, k0_p, v0_p, k1_p, v1_p, k2_p, v2_p,
            state_gla_sample, k0_s, v0_s, k1_s, v1_s, k2_s, v2_s)
```

```python
import functools

import numpy as np
import jax
import jax.numpy as jnp
from jax import lax
from jax.experimental import pallas as pl
from jax.experimental.pallas import tpu as pltpu

F32 = jnp.float32
BF16 = jnp.bfloat16

D_MODEL = 1024
PAST_LEN = 16384
GLA_HEADS = 4
GLA_DK = 128
GLA_DV = 256
GLA_RANK = 16
GLA_TAU = 16.0
GLA_CHUNK = 64
SWA_GROUPS = ((128, 1), (512, 4), (2048, 16))
SWA_HPG = 4
SWA_HEAD_DIM = 64
SWA_WIDTH = 768
SWA_OUT = 256
SWA_BLOCK = 128
ROPE_THETA = 10000.0
N_GROUPS = 4
EXPERTS_PER_GROUP = 8
N_EXPERTS = 32
D_EXPERT = 512
EPS = 1e-6

LANES = 128
SUBLANES = 8
VMEM_LIMIT = 56 * 1024 * 1024
NEG = -1e30
EXPERT_TILE = 256

_C_GQKV = (0, 2048)
_C_GOG = (2048, 3072)
_C_QK = (3072, 4608)
_C_V = (4608, 5376)
_C_GAB = (5376, 7424)
_C_LR = (7424, 7552)


def _dot(a, b):
    return jnp.dot(a, b, preferred_element_type=F32)


def _dot_nt(a, b):
    return lax.dot_general(a, b, (((1,), (1,)), ((), ())), preferred_element_type=F32)


def _dot_tn(a, b):
    return lax.dot_general(a, b, (((0,), (0,)), ((), ())), preferred_element_type=F32)


def _sigmoid(x):
    return 1.0 / (1.0 + jnp.exp(-x))


def _split_bf16(x):
    hi = x.astype(BF16)
    lo = (x - hi.astype(F32)).astype(BF16)
    return hi, lo


def _params(sem):
    return pltpu.CompilerParams(dimension_semantics=sem, vmem_limit_bytes=VMEM_LIMIT)


def _resident(shape):
    nd = len(shape)
    return pl.BlockSpec(shape, lambda *_: (0,) * nd, pipeline_mode=pl.Buffered(1))


def _proj_kernel(x_ref, cos_ref, sin_ref, ln_ref, w_ref, wlr_ref, blr_ref, nw_ref, g_ref,
                 gqkv_ref, la_ref, sog_ref, q_ref, k_ref, v_ref, sgab_ref):
    x = x_ref[...]
    h = (x * lax.rsqrt(jnp.mean(x * x, axis=-1, keepdims=True) + EPS) * ln_ref[...]).astype(BF16)
    gqkv_ref[...] = _dot(h, w_ref[:, _C_GQKV[0]:_C_GQKV[1]]).astype(BF16)
    og = _dot(h, w_ref[:, _C_GOG[0]:_C_GOG[1]])
    sog_ref[...] = (og * _sigmoid(og)).astype(BF16)
    lr = _dot(h, w_ref[:, _C_LR[0]:_C_LR[1]]).astype(BF16)
    z = _dot(lr, wlr_ref[...]) + blr_ref[...]
    la_ref[...] = (jnp.minimum(z, 0.0) - jnp.log(1.0 + jnp.exp(-jnp.abs(z)))) / GLA_TAU
    qk = _dot(h, w_ref[:, _C_QK[0]:_C_QK[1]])
    sq = (qk * qk).astype(BF16)
    ms = jnp.concatenate([_dot(sq[:, c * 256:(c + 1) * 256], g_ref[...]) for c in range(6)], axis=1)
    qn = qk * lax.rsqrt(ms + EPS) * nw_ref[...]
    width = 2 * SWA_WIDTH
    cos = jnp.tile(cos_ref[...], (1, width // LANES))
    sin = jnp.tile(sin_ref[...], (1, width // LANES))
    lane = lax.broadcasted_iota(jnp.int32, qn.shape, 1)
    half = SWA_HEAD_DIM // 2
    rot = jnp.where(lane % SWA_HEAD_DIM < half, pltpu.roll(qn, width - half, 1), pltpu.roll(qn, half, 1))
    qr = qn * cos + rot * sin
    q_ref[...] = qr[:, :SWA_WIDTH].astype(BF16)
    k_ref[...] = qr[:, SWA_WIDTH:]
    v_ref[...] = _dot(h, w_ref[:, _C_V[0]:_C_V[1]])
    gab = _dot(h, w_ref[:, _C_GAB[0]:_C_GAB[1]])
    sgab_ref[...] = _sigmoid(gab).astype(BF16)


def _proj(x, cos, sin, rope_blocks, ln, w, wlr, blr, nw, g, tm):
    t = x.shape[0]
    row = lambda width: pl.BlockSpec((tm, width), lambda i: (i, 0))
    outs = [(2048, BF16), (512, F32), (1024, BF16), (768, BF16), (768, F32), (768, F32), (2048, BF16)]
    return pl.pallas_call(
        _proj_kernel,
        grid=(t // tm,),
        in_specs=[row(D_MODEL),
                  pl.BlockSpec((tm, LANES), lambda i: (i % rope_blocks, 0)),
                  pl.BlockSpec((tm, LANES), lambda i: (i % rope_blocks, 0)),
                  _resident(ln.shape), _resident(w.shape), _resident(wlr.shape), _resident(blr.shape),
                  _resident(nw.shape), _resident(g.shape)],
        out_specs=[row(wd) for wd, _ in outs],
        out_shape=[jax.ShapeDtypeStruct((t, wd), dt) for wd, dt in outs],
        compiler_params=_params(("arbitrary",)),
        name="proj",
    )(x, cos, sin, ln, w, wlr, blr, nw, g)


def _gla_kernel(gqkv_ref, la_ref, sog_ref, s0_ref, onw_ref, y_ref, sfin_ref, s_scr, *, chunk, n_chunks):
    j = pl.program_id(1)

    @pl.when(j == 0)
    def _():
        s_scr[...] = s0_ref[0]

    r = lax.broadcasted_iota(jnp.int32, (chunk, chunk), 0)
    c = lax.broadcasted_iota(jnp.int32, (chunk, chunk), 1)
    causal = r >= c
    tri = causal.astype(BF16)
    ones = jnp.ones((chunk, GLA_DV), BF16)
    hk = GLA_HEADS * GLA_DK

    def body(ci, carry):
        r0 = pl.multiple_of(ci * chunk, chunk)
        blk = gqkv_ref[0, pl.ds(r0, chunk), :]
        la_hi, la_lo = _split_bf16(la_ref[0, pl.ds(r0, chunk), :])
        b = _dot(tri, la_hi) + _dot(tri, la_lo)
        blast = b[chunk - 1:chunk, :]
        q = blk[:, :hk].astype(F32) * GLA_DK ** -0.5
        k = blk[:, hk:2 * hk].astype(F32)
        qd = (q * jnp.exp(b)).astype(BF16)
        kd = (k * jnp.exp(-b)).astype(BF16)
        kdec = (k * jnp.exp(blast - b)).astype(BF16)
        outs = []
        for h in range(GLA_HEADS):
            sl = slice(h * GLA_DK, (h + 1) * GLA_DK)
            v_h = blk[:, 2 * hk + h * GLA_DV:2 * hk + (h + 1) * GLA_DV]
            att = jnp.where(causal, _dot_nt(qd[:, sl], kd[:, sl]), 0.0).astype(BF16)
            s_prev = s_scr[h]
            o = _dot(att, v_h) + _dot(qd[:, sl], s_prev.astype(BF16))
            dsum = _dot_tn(la_hi[:, sl], ones) + _dot_tn(la_lo[:, sl], ones)
            s_scr[h] = s_prev * jnp.exp(dsum) + _dot_tn(kdec[:, sl], v_h)
            ms = jnp.mean(o * o, axis=-1, keepdims=True)
            outs.append(o * lax.rsqrt(ms + EPS) * onw_ref[...])
        o_all = jnp.concatenate(outs, axis=1) * sog_ref[0, pl.ds(r0, chunk), :].astype(F32)
        y_ref[0, pl.ds(r0, chunk), :] = o_all.astype(BF16)
        return carry

    lax.fori_loop(0, n_chunks, body, 0)

    @pl.when(j == pl.num_programs(1) - 1)
    def _():
        sfin_ref[0] = s_scr[...]


def _gla(gqkv, la, sog, s0, onw, chunk, block):
    b, l, _ = gqkv.shape
    tok = lambda width: pl.BlockSpec((1, block, width), lambda bi, j: (bi, j, 0))
    st = pl.BlockSpec((1, GLA_HEADS, GLA_DK, GLA_DV), lambda bi, j: (bi, 0, 0, 0))
    return pl.pallas_call(
        functools.partial(_gla_kernel, chunk=chunk, n_chunks=block // chunk),
        grid=(b, l // block),
        in_specs=[tok(2048), tok(512), tok(1024), st, _resident(onw.shape)],
        out_specs=[tok(1024), st],
        out_shape=[jax.ShapeDtypeStruct((b, l, 1024), BF16),
                   jax.ShapeDtypeStruct((b, GLA_HEADS, GLA_DK, GLA_DV), F32)],
        scratch_shapes=[pltpu.VMEM((GLA_HEADS, GLA_DK, GLA_DV), F32)],
        compiler_params=_params(("arbitrary", "arbitrary")),
        name="gla",
    )(gqkv, la, sog, s0, onw)


def _band_heads(q, kw, vw, valid):
    outs, lses = [], []
    for h in range(SWA_HPG):
        sl = slice(h * SWA_HEAD_DIM, (h + 1) * SWA_HEAD_DIM)
        s = _dot_nt(q[:, sl], kw[:, sl]) * SWA_HEAD_DIM ** -0.5
        s = jnp.where(valid, s, NEG)
        m = jnp.max(s, axis=-1, keepdims=True)
        p = jnp.exp(s - m)
        l = jnp.sum(p, axis=-1, keepdims=True)
        outs.append(_dot(p.astype(BF16), vw[:, sl]) / l)
        lses.append(jnp.broadcast_to(m + jnp.log(l), (q.shape[0], SWA_HEAD_DIM)))
    return jnp.concatenate(outs, axis=1), jnp.concatenate(lses, axis=1)


def _swa_kernel(q_ref, k_ref, v_ref, kp_ref, vp_ref, o_ref, lse_ref, *, nsub):
    blk = SWA_BLOCK
    first = pl.program_id(2) == 0
    qi = lax.broadcasted_iota(jnp.int32, (blk, 2 * blk), 0)
    kj = lax.broadcasted_iota(jnp.int32, (blk, 2 * blk), 1)
    band = (kj >= qi) & (kj <= qi + blk)

    kw = jnp.concatenate([kp_ref[0], k_ref[0, 0:blk, :]], axis=0).astype(BF16)
    vw = jnp.concatenate([vp_ref[0], v_ref[0, 0:blk, :]], axis=0).astype(BF16)
    valid0 = band & (kj >= jnp.where(first, blk, 0))
    o, lse = _band_heads(q_ref[0, 0:blk, :], kw, vw, valid0)
    o_ref[0, 0:blk, :] = o
    lse_ref[0, 0:blk, :] = lse

    def body(j, carry):
        r0 = pl.multiple_of(j * blk, blk)
        kw = k_ref[0, pl.ds(r0 - blk, 2 * blk), :].astype(BF16)
        vw = v_ref[0, pl.ds(r0 - blk, 2 * blk), :].astype(BF16)
        o, lse = _band_heads(q_ref[0, pl.ds(r0, blk), :], kw, vw, band)
        o_ref[0, pl.ds(r0, blk), :] = o
        lse_ref[0, pl.ds(r0, blk), :] = lse
        return carry

    lax.fori_loop(1, nsub, body, 0)


def _swa_prompt(q, k, v, g, dil, qb):
    b, s, _ = q.shape
    n = s // dil
    nsub = qb // SWA_BLOCK
    ncol = SWA_WIDTH // SWA_OUT
    view = lambda t: t.reshape(b, n, dil * SWA_WIDTH)
    cur = pl.BlockSpec((1, qb, SWA_OUT), lambda bi, r, i: (bi, i, r * ncol + g))
    prev = pl.BlockSpec((1, SWA_BLOCK, SWA_OUT), lambda bi, r, i: (bi, jnp.maximum(i * nsub - 1, 0), r * ncol + g))
    out = pl.BlockSpec((1, qb, SWA_OUT), lambda bi, r, i: (bi, i, r))
    o, lse = pl.pallas_call(
        functools.partial(_swa_kernel, nsub=nsub),
        grid=(b, dil, n // qb),
        in_specs=[cur, cur, cur, prev, prev],
        out_specs=[out, out],
        out_shape=[jax.ShapeDtypeStruct((b, n, dil * SWA_OUT), F32)] * 2,
        compiler_params=_params(("arbitrary", "arbitrary", "arbitrary")),
        name=f"swa_prompt_g{g}",
    )(view(q), view(k), view(v), view(k), view(v))
    return o.reshape(b * s, SWA_OUT), lse.reshape(b * s, SWA_OUT)


_SAMPLE_ROWS = 16


def _swa_sample_kernel(q_ref, kn_ref, vn_ref, k0_ref, v0_ref, k1_ref, v1_ref, k2_ref, v2_ref, ob_ref, *, n_new):
    rows = _SAMPLE_ROWS
    nb = SWA_BLOCK
    row = lax.broadcasted_iota(jnp.int32, (rows, SWA_OUT), 0)
    lane = lax.broadcasted_iota(jnp.int32, (rows, SWA_OUT), 1)
    headmask = (lane // SWA_HEAD_DIM) == row
    col = lax.broadcasted_iota(jnp.int32, (rows, 2 * nb), 1)
    zpad = jnp.zeros((nb - rows, SWA_OUT), BF16)
    bufs = ((k0_ref, v0_ref), (k1_ref, v1_ref), (k2_ref, v2_ref))
    o_g, lse_g = [], []
    for g in range(len(SWA_GROUPS)):
        gsl = slice(g * SWA_OUT, (g + 1) * SWA_OUT)
        qg = q_ref[0, :, gsl].astype(F32)
        knew = jnp.concatenate([kn_ref[0, :, gsl].astype(BF16), zpad], axis=0)
        vnew = jnp.concatenate([vn_ref[0, :, gsl].astype(BF16), zpad], axis=0)
        kb_ref, vb_ref = bufs[g]
        o_rows, lse_rows = [], []
        for l in range(n_new):
            qexp = jnp.where(headmask, jnp.broadcast_to(qg[l:l + 1, :], (rows, SWA_OUT)), 0.0).astype(BF16)
            if g == 0:
                kb, vb = kb_ref[0], vb_ref[0]
                valid = ((col < nb) & (col >= l)) | ((col >= nb) & (col <= nb + l))
            else:
                kb = kb_ref[0, :, l * SWA_OUT:(l + 1) * SWA_OUT]
                vb = vb_ref[0, :, l * SWA_OUT:(l + 1) * SWA_OUT]
                valid = (col < nb) | (col == nb + l)
            kall = jnp.concatenate([kb.astype(BF16), knew], axis=0)
            vall = jnp.concatenate([vb.astype(BF16), vnew], axis=0)
            s = _dot_nt(qexp, kall) * SWA_HEAD_DIM ** -0.5
            s = jnp.where(valid, s, NEG)
            m = jnp.max(s, axis=-1, keepdims=True)
            p = jnp.exp(s - m)
            den = jnp.sum(p, axis=-1, keepdims=True)
            o = _dot(p.astype(BF16), vall) / den
            lse = m + jnp.log(den)
            o_rows.append(jnp.sum(jnp.where(headmask, o, 0.0), axis=0, keepdims=True))
            lse_rows.append(jnp.sum(jnp.where(headmask, lse, 0.0), axis=0, keepdims=True))
        o_g.append(jnp.concatenate(o_rows, axis=0))
        lse_g.append(jnp.concatenate(lse_rows, axis=0))
    lmax = jnp.maximum(jnp.maximum(lse_g[0], lse_g[1]), lse_g[2])
    e = [jnp.exp(x - lmax) for x in lse_g]
    ob = (e[0] * o_g[0] + e[1] * o_g[1] + e[2] * o_g[2]) / (e[0] + e[1] + e[2])
    ob_ref[0] = jnp.concatenate([ob, jnp.zeros((rows - n_new, SWA_OUT), F32)], axis=0).astype(BF16)


def _swa_sample(q, kn, vn, caches, n_new):
    bd = q.shape[0]
    rows = _SAMPLE_ROWS
    tok = pl.BlockSpec((1, rows, SWA_WIDTH), lambda bi: (bi, 0, 0))
    specs, args = [tok, tok, tok], [q, kn, vn]
    for g, (win, dil) in enumerate(SWA_GROUPS):
        for t in caches[g]:
            assert t.shape == (bd, win, SWA_OUT) and win == SWA_BLOCK * dil and (dil == 1 or n_new <= dil)
            width = min(dil, n_new) * SWA_OUT
            args.append(t.reshape(bd, SWA_BLOCK, dil * SWA_OUT))
            specs.append(pl.BlockSpec((1, SWA_BLOCK, width), lambda bi: (bi, 0, 0)))
    return pl.pallas_call(
        functools.partial(_swa_sample_kernel, n_new=n_new),
        grid=(bd,),
        in_specs=specs,
        out_specs=pl.BlockSpec((1, rows, SWA_OUT), lambda bi: (bi, 0, 0)),
        out_shape=jax.ShapeDtypeStruct((bd, rows, SWA_OUT), BF16),
        compiler_params=_params(("arbitrary",)),
        name="swa_sample",
    )(*args)


def _merge_kernel(*refs, combine):
    if combine:
        (x_ref, ya_ref, o0, o1, o2, l0, l1, l2, sgab_ref, wa_ref, wb_ref, wo_ref, ln_ref, wr_hi_ref, wr_lo_ref,
         br_ref, x1_ref, h2p_ref, lg_ref) = refs
        lmax = jnp.maximum(jnp.maximum(l0[...], l1[...]), l2[...])
        e0, e1, e2 = jnp.exp(l0[...] - lmax), jnp.exp(l1[...] - lmax), jnp.exp(l2[...] - lmax)
        ob = ((e0 * o0[...] + e1 * o1[...] + e2 * o2[...]) / (e0 + e1 + e2)).astype(BF16)
    else:
        (x_ref, ya_ref, ob_ref, sgab_ref, wa_ref, wb_ref, wo_ref, ln_ref, wr_hi_ref, wr_lo_ref,
         br_ref, x1_ref, h2p_ref, lg_ref) = refs
        ob = ob_ref[...]
    ya = _dot(ya_ref[...], wa_ref[...])
    yb = _dot(ob, wb_ref[...])
    sga = sgab_ref[:, :D_MODEL].astype(F32)
    sgb = sgab_ref[:, D_MODEL:].astype(F32)
    x1 = x_ref[...] + _dot((sga * ya + sgb * yb).astype(BF16), wo_ref[...])
    x1_ref[...] = x1
    h2 = x1 * lax.rsqrt(jnp.mean(x1 * x1, axis=-1, keepdims=True) + EPS) * ln_ref[...]
    h_hi, h_lo = _split_bf16(h2)
    lg_ref[...] = _dot(h_hi, wr_hi_ref[...]) + _dot(h_hi, wr_lo_ref[...]) + _dot(h_lo, wr_hi_ref[...]) + br_ref[...]
    bits = lax.bitcast_convert_type(h_hi.astype(F32), jnp.uint32)
    half = D_MODEL // 2
    packed = (bits[:, :half] >> 16) | (bits[:, half:] & jnp.uint32(0xFFFF0000))
    for c in range(half // LANES):
        h2p_ref[c] = packed[:, c * LANES:(c + 1) * LANES]


def _merge(x, ya_in, swa, sgab, wa, wb, wo, ln, wr_hi, wr_lo, br, tm):
    t = x.shape[0]
    combine = len(swa) > 1
    row = lambda width: pl.BlockSpec((tm, width), lambda i: (i, 0))
    weights = [wa, wb, wo, ln, wr_hi, wr_lo, br]
    nchunk = D_MODEL // 2 // LANES
    return pl.pallas_call(
        functools.partial(_merge_kernel, combine=combine),
        grid=(t // tm,),
        in_specs=[row(D_MODEL), row(D_MODEL)] + [row(SWA_OUT)] * len(swa) + [row(2 * D_MODEL)]
                 + [_resident(w.shape) for w in weights],
        out_specs=[row(D_MODEL), pl.BlockSpec((nchunk, tm, LANES), lambda i: (0, i, 0)), row(LANES)],
        out_shape=[jax.ShapeDtypeStruct((t, D_MODEL), F32),
                   jax.ShapeDtypeStruct((nchunk, t, LANES), jnp.uint32),
                   jax.ShapeDtypeStruct((t, LANES), F32)],
        compiler_params=_params(("arbitrary",)),
        name="merge",
    )(x, ya_in, *swa, sgab, *weights)


def _router_kernel(lg_ref, mi_ref, mw_ref, cnt_ref, carry):
    i = pl.program_id(0)

    @pl.when(i == 0)
    def _():
        carry[...] = jnp.zeros_like(carry)

    lg = lg_ref[...]
    tr = lg.shape[0]
    lane = lax.broadcasted_iota(jnp.int32, lg.shape, 1)
    big = jnp.int32(LANES)
    gl = jnp.where(lane < N_GROUPS, lg, NEG)
    gmax = jnp.max(gl, axis=-1, keepdims=True)
    g_idx = jnp.min(jnp.where(gl == gmax, lane, big), axis=-1, keepdims=True)
    g_w = 1.0 / jnp.sum(jnp.exp(gl - gmax), axis=-1, keepdims=True)
    e_lane = lane - N_GROUPS
    in_group = (e_lane >= 0) & (e_lane < N_EXPERTS) & (e_lane // EXPERTS_PER_GROUP == g_idx)
    el = jnp.where(in_group, lg, NEG)
    v1 = jnp.max(el, axis=-1, keepdims=True)
    i1 = jnp.min(jnp.where(el == v1, lane, big), axis=-1, keepdims=True)
    el2 = jnp.where(lane == i1, NEG, el)
    v2 = jnp.max(el2, axis=-1, keepdims=True)
    i2 = jnp.min(jnp.where(el2 == v2, lane, big), axis=-1, keepdims=True)
    r21 = jnp.exp(v2 - v1)
    w1 = g_w / (1.0 + r21)
    w2 = g_w * r21 / (1.0 + r21)
    e1 = i1 - N_GROUPS
    e2 = i2 - N_GROUPS
    hot1 = lane == e1
    hot2 = lane == e2
    hot = (hot1 | hot2).astype(BF16)
    r = lax.broadcasted_iota(jnp.int32, (tr, tr), 0)
    c = lax.broadcasted_iota(jnp.int32, (tr, tr), 1)
    csum = _dot((r > c).astype(BF16), hot) + carry[...]
    rank1 = jnp.sum(jnp.where(hot1, csum, 0.0), axis=-1, keepdims=True).astype(jnp.int32)
    rank2 = jnp.sum(jnp.where(hot2, csum, 0.0), axis=-1, keepdims=True).astype(jnp.int32)
    total = carry[...] + jnp.sum(hot.astype(F32), axis=0, keepdims=True)
    carry[...] = total
    cnt_ref[...] = jnp.broadcast_to(total, cnt_ref.shape)
    mi_ref[...] = jnp.where(lane == 0, e1, jnp.where(lane == 1, e2, jnp.where(lane == 2, rank1, rank2)))
    mw_ref[...] = jnp.where(lane == 0, w1, w2)


def _router(logits, tr):
    t = logits.shape[0]
    assert t % tr == 0
    row = pl.BlockSpec((tr, LANES), lambda i: (i, 0))
    return pl.pallas_call(
        _router_kernel,
        grid=(t // tr,),
        in_specs=[row],
        out_specs=[row, row, pl.BlockSpec((SUBLANES, LANES), lambda i: (0, 0))],
        out_shape=[jax.ShapeDtypeStruct((t, LANES), jnp.int32), jax.ShapeDtypeStruct((t, LANES), F32),
                   jax.ShapeDtypeStruct((SUBLANES, LANES), F32)],
        scratch_shapes=[pltpu.VMEM((1, LANES), F32)],
        compiler_params=_params(("arbitrary",)),
        name="router",
    )(logits)


def _expert_kernel(te_ref, tv_ref, s2t_ref, src_ref, sw_ref, wg_ref, wu_ref, wd_ref, out_ref, xs, rows, *, n_tok):
    i = pl.program_id(0)
    tm = EXPERT_TILE
    nchunk = D_MODEL // 2 // LANES
    nout = D_MODEL // LANES

    @pl.when(tv_ref[i] == 0)
    def _():
        out_ref[...] = jnp.zeros_like(out_ref)

    @pl.when(tv_ref[i] != 0)
    def _():
        base = i * tm

        def gather(j, carry):
            tok = s2t_ref[base + j]
            xs[pl.ds(j, nchunk, stride=tm), :] = src_ref[pl.ds(tok, nchunk, stride=n_tok), :]
            return carry

        lax.fori_loop(0, tm, gather, 0, unroll=8)
        u = [xs[c * tm:(c + 1) * tm, :] for c in range(nchunk)]
        lo = [lax.bitcast_convert_type(x << 16, F32) for x in u]
        hi = [lax.bitcast_convert_type(x & jnp.uint32(0xFFFF0000), F32) for x in u]
        h = jnp.concatenate(lo + hi, axis=1).astype(BF16)
        gate = _dot(h, wg_ref[0].astype(BF16))
        up = _dot(h, wu_ref[0].astype(BF16))
        a = (gate * _sigmoid(gate) * up).astype(BF16)
        o = _dot(a, wd_ref[0].astype(BF16)) * sw_ref[...]
        for c in range(nout):
            rows[c * tm:(c + 1) * tm, :] = o[:, c * LANES:(c + 1) * LANES]

        def emit(j, carry):
            out_ref[pl.ds(pl.multiple_of(j * nout, nout), nout), :] = rows[pl.ds(j, nout, stride=tm), :]
            return carry

        lax.fori_loop(0, tm, emit, 0, unroll=8)


def _experts(tile_expert, tile_valid, slot2tok, src, slot_w, wg, wu, wd):
    n_tiles = tile_expert.shape[0]
    tm = EXPERT_TILE
    n_tok = src.shape[0] // (D_MODEL // 2 // LANES)
    nout = D_MODEL // LANES
    wspec = lambda shape: pl.BlockSpec((1,) + shape, lambda i, te, tv, s2t: (te[i], 0, 0))
    grid_spec = pltpu.PrefetchScalarGridSpec(
        num_scalar_prefetch=3,
        grid=(n_tiles,),
        in_specs=[pl.BlockSpec(src.shape, lambda i, te, tv, s2t: (0, 0), pipeline_mode=pl.Buffered(1)),
                  pl.BlockSpec((tm, 1), lambda i, te, tv, s2t: (i, 0)),
                  wspec((D_MODEL, D_EXPERT)), wspec((D_MODEL, D_EXPERT)), wspec((D_EXPERT, D_MODEL))],
        out_specs=pl.BlockSpec((tm * nout, LANES), lambda i, te, tv, s2t: (i, 0)),
        scratch_shapes=[pltpu.VMEM((tm * D_MODEL // 2 // LANES, LANES), jnp.uint32),
                        pltpu.VMEM((tm * nout, LANES), F32)],
    )
    return pl.pallas_call(
        functools.partial(_expert_kernel, n_tok=n_tok),
        grid_spec=grid_spec,
        out_shape=jax.ShapeDtypeStruct((n_tiles * tm * nout, LANES), F32),
        compiler_params=_params(("arbitrary",)),
        name="experts",
    )(tile_expert, tile_valid, slot2tok, src, slot_w, wg, wu, wd)


def _combine_kernel(p0_ref, p1_ref, x1_ref, rows_hbm, y_ref, g0, g1, sem, *, tc):
    i = pl.program_id(0)
    nout = D_MODEL // LANES
    base = i * tc

    def row_copy(pos_ref, j, dst, s):
        src0 = pl.multiple_of(pos_ref[base + j] * nout, nout)
        return pltpu.make_async_copy(rows_hbm.at[pl.ds(src0, nout), :],
                                     dst.at[pl.ds(pl.multiple_of(j * nout, nout), nout), :], sem.at[s])

    def issue(j, carry):
        row_copy(p0_ref, j, g0, 0).start()
        row_copy(p1_ref, j, g1, 1).start()
        return carry

    lax.fori_loop(0, tc, issue, 0, unroll=4)

    def drain(j, carry):
        row_copy(p0_ref, j, g0, 0).wait()
        row_copy(p1_ref, j, g1, 1).wait()
        return carry

    lax.fori_loop(0, tc, drain, 0, unroll=4)
    for c in range(nout):
        sl = slice(c * LANES, (c + 1) * LANES)
        y_ref[:, sl] = x1_ref[:, sl] + g0[pl.ds(c, tc, stride=nout), :] + g1[pl.ds(c, tc, stride=nout), :]


def _combine(pos0, pos1, x1, rows, tc):
    t = x1.shape[0]
    nout = D_MODEL // LANES
    row = pl.BlockSpec((tc, D_MODEL), lambda i, p0, p1: (i, 0))
    grid_spec = pltpu.PrefetchScalarGridSpec(
        num_scalar_prefetch=2,
        grid=(t // tc,),
        in_specs=[row, pl.BlockSpec(memory_space=pl.ANY)],
        out_specs=row,
        scratch_shapes=[pltpu.VMEM((tc * nout, LANES), F32), pltpu.VMEM((tc * nout, LANES), F32),
                        pltpu.SemaphoreType.DMA((2,))],
    )
    return pl.pallas_call(
        functools.partial(_combine_kernel, tc=tc),
        grid_spec=grid_spec,
        out_shape=jax.ShapeDtypeStruct((t, D_MODEL), F32),
        compiler_params=_params(("arbitrary",)),
        name="combine",
    )(pos0, pos1, x1, rows)


def _rope_tables(pos):
    half = SWA_HEAD_DIM // 2
    inv_freq = ROPE_THETA ** (-jnp.arange(half, dtype=F32) / half)
    ang = pos.astype(F32)[:, None] * inv_freq[None, :]
    cos, sin = jnp.cos(ang), jnp.sin(ang)
    reps = LANES // SWA_HEAD_DIM
    return jnp.tile(jnp.concatenate([cos, cos], axis=1), (1, reps)), jnp.tile(jnp.concatenate([-sin, sin], axis=1), (1, reps))


def kernel(x_prompt, x_sample, state_gla, cache_swa_k0, cache_swa_v0, cache_swa_k1, cache_swa_v1, cache_swa_k2, cache_swa_v2, ln1_w, w_in, w_gla_lr, b_gla_lr, gla_onorm_w, q_norm_w, k_norm_w, w_branch_a, w_branch_b, w_out, ln2_w, w_router_group, b_router_group, w_router_expert, b_router_expert, w_exp_gate, w_exp_up, w_exp_down):
    b, s, d = x_prompt.shape
    bd, ls, _ = x_sample.shape
    tp, ts = b * s, bd * ls
    assert w_in.shape[0] == 1 and d == D_MODEL and ts % SUBLANES == 0
    k_caches = (cache_swa_k0, cache_swa_k1, cache_swa_k2)
    v_caches = (cache_swa_v0, cache_swa_v1, cache_swa_v2)

    w = w_in[0]
    cuts = np.cumsum((512, 512, 1024, GLA_RANK, 1024, 768, 768, 768, 1024, 1024))
    sec = lambda a: w[:, (0 if a == 0 else cuts[a - 1]):cuts[a]]
    lr_pad = jnp.pad(sec(3), ((0, 0), (0, LANES - GLA_RANK)))
    w_packed = jnp.concatenate([sec(0), sec(1), sec(2), sec(4), sec(5), sec(6), sec(7), sec(8), sec(9), lr_pad],
                               axis=1).astype(BF16)
    wlr = jnp.pad(w_gla_lr[0], ((0, LANES - GLA_RANK), (0, 0))).astype(BF16)
    blr = b_gla_lr[0][None, :]
    nw = jnp.concatenate([jnp.tile(q_norm_w[0], SWA_WIDTH // SWA_HEAD_DIM), jnp.tile(k_norm_w[0], SWA_WIDTH // SWA_HEAD_DIM)])[None, :]
    gi = np.arange(256) // SWA_HEAD_DIM
    gmat = jnp.asarray((gi[:, None] == gi[None, :]).astype(np.float32) / SWA_HEAD_DIM, dtype=BF16)
    ln1 = ln1_w[0][None, :]
    ln2 = ln2_w[0][None, :]
    onw = gla_onorm_w[0][None, :]
    wa = w_branch_a[0].astype(BF16)
    wb = w_branch_b[0].astype(BF16)
    wo = w_out[0].astype(BF16)
    wr = jnp.pad(jnp.concatenate([w_router_group[0], w_router_expert[0]], axis=1),
                 ((0, 0), (0, LANES - N_GROUPS - N_EXPERTS)))
    wr_hi, wr_lo = _split_bf16(wr)
    br = jnp.pad(jnp.concatenate([b_router_group[0], b_router_expert[0]]), (0, LANES - N_GROUPS - N_EXPERTS))[None, :]

    cos_p, sin_p = _rope_tables(jnp.arange(s, dtype=jnp.int32))
    cos_s, sin_s = _rope_tables(PAST_LEN + jnp.arange(ts, dtype=jnp.int32) % ls)

    tm_p = 512
    proj_p = _proj(x_prompt.reshape(tp, d), cos_p, sin_p, s // tm_p, ln1, w_packed, wlr, blr, nw, gmat, tm_p)
    proj_s = _proj(x_sample.reshape(ts, d), cos_s, sin_s, 1, ln1, w_packed, wlr, blr, nw, gmat, ts)
    gqkv_p, la_p, sog_p, q_p, k_p, v_p, sgab_p = proj_p
    gqkv_s, la_s, sog_s, q_s, k_s, v_s, sgab_s = proj_s

    r3 = lambda t, nb: t.reshape(nb, t.shape[0] // nb, t.shape[1])
    ya_p, st_p = _gla(r3(gqkv_p, b), r3(la_p, b), r3(sog_p, b),
                      jnp.zeros((b, GLA_HEADS, GLA_DK, GLA_DV), F32), onw, GLA_CHUNK, 512)
    pad_s = lambda t: jnp.pad(r3(t, bd), ((0, 0), (0, _SAMPLE_ROWS - ls), (0, 0)))
    ya_s, st_s = _gla(pad_s(gqkv_s), pad_s(la_s), pad_s(sog_s), state_gla[0], onw, _SAMPLE_ROWS, _SAMPLE_ROWS)
    ya_s = ya_s[:, :ls].reshape(ts, d)

    q3, k3, v3 = r3(q_p, b), r3(k_p, b), r3(v_p, b)
    swa_p = [_swa_prompt(q3, k3, v3, g, dil, min(1024, s // dil)) for g, (_, dil) in enumerate(SWA_GROUPS)]
    caches = [(k_caches[g][0].reshape(bd, -1, SWA_OUT), v_caches[g][0].reshape(bd, -1, SWA_OUT)) for g in range(3)]
    ob_s = _swa_sample(pad_s(q_s), pad_s(k_s), pad_s(v_s), caches, ls)[:, :ls].reshape(ts, SWA_OUT)

    mw = (wa, wb, wo, ln2, wr_hi, wr_lo, br)
    x1_p, h2p_p, lg_p = _merge(x_prompt.reshape(tp, d), ya_p.reshape(tp, d),
                               [o for o, _ in swa_p] + [l for _, l in swa_p], sgab_p, *mw, 512)
    x1_s, h2p_s, lg_s = _merge(x_sample.reshape(ts, d), ya_s, [ob_s], sgab_s, *mw, ts)

    t = tp + ts
    meta_i, meta_w, cnt = _router(jnp.concatenate([lg_p, lg_s], axis=0), 3 * LANES)
    tm = EXPERT_TILE
    n_tiles = (2 * t) // tm + N_EXPERTS
    counts = cnt[0, :N_EXPERTS].astype(jnp.int32)
    padded = (counts + tm - 1) // tm * tm
    ends = jnp.cumsum(padded)
    offs = ends - padded
    e0, e1, rank0, rank1 = meta_i[:, 0], meta_i[:, 1], meta_i[:, 2], meta_i[:, 3]
    pos0 = offs[e0] + rank0
    pos1 = offs[e1] + rank1
    tok = jnp.arange(t, dtype=jnp.int32)
    slot2tok = jnp.zeros((n_tiles * tm,), jnp.int32).at[pos0].set(tok).at[pos1].set(tok)
    slot_w = jnp.zeros((n_tiles * tm,), F32).at[pos0].set(meta_w[:, 0]).at[pos1].set(meta_w[:, 1])[:, None]
    tile_start = jnp.arange(n_tiles, dtype=jnp.int32) * tm
    tile_valid = (tile_start < ends[-1]).astype(jnp.int32)
    last_slot = jnp.minimum(tile_start, ends[-1] - 1)
    tile_expert = jnp.minimum(jnp.sum((last_slot[:, None] >= ends[None, :]).astype(jnp.int32), axis=1), N_EXPERTS - 1)

    src = jnp.concatenate([h2p_p, h2p_s], axis=1).reshape(-1, LANES)
    rows = _experts(tile_expert, tile_valid, slot2tok, src, slot_w, w_exp_gate[0], w_exp_up[0], w_exp_down[0])
    y_p = _combine(pos0[:tp], pos1[:tp], x1_p, rows, 256)
    y_s = _combine(pos0[tp:], pos1[tp:], x1_s, rows, ts)

    heads = lambda a: a.reshape(1, a.shape[0], a.shape[1], SWA_HPG, SWA_HEAD_DIM)
    outs = [y_p.reshape(b, s, d), y_s.reshape(bd, ls, d), st_p[None].astype(x_prompt.dtype)]
    for g, (win, _) in enumerate(SWA_GROUPS):
        keep = min(win, s)
        gsl = slice(g * SWA_OUT, (g + 1) * SWA_OUT)
        outs += [heads(k3[:, s - keep:, gsl]), heads(v3[:, s - keep:, gsl])]
    outs.append(st_s[None].astype(state_gla.dtype))
    ks3, vs3 = r3(k_s, bd), r3(v_s, bd)
    for g in range(len(SWA_GROUPS)):
        gsl = slice(g * SWA_OUT, (g + 1) * SWA_OUT)
        outs += [heads(jnp.concatenate([caches[g][0], ks3[:, :, gsl]], axis=1)[:, ls:]),
                 heads(jnp.concatenate([caches[g][1], vs3[:, :, gsl]], axis=1)[:, ls:])]
    return tuple(outs)
```

```python
import functools

import numpy as np
import jax
import jax.numpy as jnp
from jax import lax
from jax.experimental import pallas as pl
from jax.experimental.pallas import tpu as pltpu

F32 = jnp.float32
BF16 = jnp.bfloat16

D_MODEL = 1024
PAST_LEN = 16384
GLA_HEADS = 4
GLA_DK = 128
GLA_DV = 256
GLA_RANK = 16
GLA_TAU = 16.0
GLA_CHUNK = 64
SWA_GROUPS = ((128, 1), (512, 4), (2048, 16))
SWA_HPG = 4
SWA_HEAD_DIM = 64
SWA_WIDTH = 768
SWA_OUT = 256
SWA_BLOCK = 128
ROPE_THETA = 10000.0
N_GROUPS = 4
EXPERTS_PER_GROUP = 8
N_EXPERTS = 32
D_EXPERT = 512
EPS = 1e-6

LANES = 128
SUBLANES = 8
VMEM_LIMIT = 56 * 1024 * 1024
NEG = -1e30
EXPERT_TILE = 256
MOE_PART = 5632

_C_GQKV = (0, 2048)
_C_GOG = (2048, 3072)
_C_QK = (3072, 4608)
_C_V = (4608, 5376)
_C_GAB = (5376, 7424)
_C_LR = (7424, 7552)


def _dot(a, b):
    return jnp.dot(a, b, preferred_element_type=F32)


def _dot_nt(a, b):
    return lax.dot_general(a, b, (((1,), (1,)), ((), ())), preferred_element_type=F32)


def _dot_tn(a, b):
    return lax.dot_general(a, b, (((0,), (0,)), ((), ())), preferred_element_type=F32)


def _sigmoid(x):
    return 1.0 / (1.0 + jnp.exp(-x))


def _split_bf16(x):
    hi = x.astype(BF16)
    lo = (x - hi.astype(F32)).astype(BF16)
    return hi, lo


def _params(sem):
    return pltpu.CompilerParams(dimension_semantics=sem, vmem_limit_bytes=VMEM_LIMIT)


def _resident(shape):
    nd = len(shape)
    return pl.BlockSpec(shape, lambda *_: (0,) * nd, pipeline_mode=pl.Buffered(1))


def _proj_kernel(x_ref, cos_ref, sin_ref, ln_ref, w_ref, wlr_ref, blr_ref, nw_ref, g_ref,
                 gqkv_ref, la_ref, sog_ref, q_ref, k_ref, v_ref, sgab_ref):
    x = x_ref[...]
    h = (x * lax.rsqrt(jnp.mean(x * x, axis=-1, keepdims=True) + EPS) * ln_ref[...]).astype(BF16)
    gqkv_ref[...] = _dot(h, w_ref[:, _C_GQKV[0]:_C_GQKV[1]]).astype(BF16)
    og = _dot(h, w_ref[:, _C_GOG[0]:_C_GOG[1]])
    sog_ref[...] = (og * _sigmoid(og)).astype(BF16)
    lr = _dot(h, w_ref[:, _C_LR[0]:_C_LR[1]]).astype(BF16)
    z = _dot(lr, wlr_ref[...]) + blr_ref[...]
    la_ref[...] = (jnp.minimum(z, 0.0) - jnp.log(1.0 + jnp.exp(-jnp.abs(z)))) / GLA_TAU
    qk = _dot(h, w_ref[:, _C_QK[0]:_C_QK[1]])
    sq = (qk * qk).astype(BF16)
    ms = jnp.concatenate([_dot(sq[:, c * 256:(c + 1) * 256], g_ref[...]) for c in range(6)], axis=1)
    qn = qk * lax.rsqrt(ms + EPS) * nw_ref[...]
    width = 2 * SWA_WIDTH
    cos = jnp.tile(cos_ref[...], (1, width // LANES))
    sin = jnp.tile(sin_ref[...], (1, width // LANES))
    lane = lax.broadcasted_iota(jnp.int32, qn.shape, 1)
    half = SWA_HEAD_DIM // 2
    rot = jnp.where(lane % SWA_HEAD_DIM < half, pltpu.roll(qn, width - half, 1), pltpu.roll(qn, half, 1))
    qr = qn * cos + rot * sin
    q_ref[...] = qr[:, :SWA_WIDTH]
    k_ref[...] = qr[:, SWA_WIDTH:]
    v_ref[...] = _dot(h, w_ref[:, _C_V[0]:_C_V[1]])
    gab = _dot(h, w_ref[:, _C_GAB[0]:_C_GAB[1]])
    sgab_ref[...] = _sigmoid(gab).astype(BF16)


def _proj(x, cos, sin, rope_blocks, ln, w, wlr, blr, nw, g, tm):
    t = x.shape[0]
    assert t % tm == 0
    row = lambda width: pl.BlockSpec((tm, width), lambda i: (i, 0))
    outs = [(2048, BF16), (512, F32), (1024, BF16), (768, F32), (768, F32), (768, F32), (2048, BF16)]
    return pl.pallas_call(
        _proj_kernel,
        grid=(t // tm,),
        in_specs=[row(D_MODEL),
                  pl.BlockSpec((tm, LANES), lambda i: (i % rope_blocks, 0)),
                  pl.BlockSpec((tm, LANES), lambda i: (i % rope_blocks, 0)),
                  _resident(ln.shape), _resident(w.shape), _resident(wlr.shape), _resident(blr.shape),
                  _resident(nw.shape), _resident(g.shape)],
        out_specs=[row(wd) for wd, _ in outs],
        out_shape=[jax.ShapeDtypeStruct((t, wd), dt) for wd, dt in outs],
        compiler_params=_params(("arbitrary",)),
        name="proj",
    )(x, cos, sin, ln, w, wlr, blr, nw, g)


def _gla_kernel(gqkv_ref, la_ref, sog_ref, s0_ref, onw_ref, y_ref, sfin_ref, s_scr, *, chunk, n_chunks):
    j = pl.program_id(1)

    @pl.when(j == 0)
    def _():
        s_scr[...] = s0_ref[0]

    r = lax.broadcasted_iota(jnp.int32, (chunk, chunk), 0)
    c = lax.broadcasted_iota(jnp.int32, (chunk, chunk), 1)
    causal = r >= c
    tri = causal.astype(BF16)
    ones = jnp.ones((chunk, GLA_DV), BF16)
    hk = GLA_HEADS * GLA_DK

    def body(ci, carry):
        r0 = pl.multiple_of(ci * chunk, chunk)
        blk = gqkv_ref[0, pl.ds(r0, chunk), :]
        la_hi, la_lo = _split_bf16(la_ref[0, pl.ds(r0, chunk), :])
        b = _dot(tri, la_hi) + _dot(tri, la_lo)
        blast = b[chunk - 1:chunk, :]
        q = blk[:, :hk].astype(F32) * GLA_DK ** -0.5
        k = blk[:, hk:2 * hk].astype(F32)
        qd = (q * jnp.exp(b)).astype(BF16)
        kd = (k * jnp.exp(-b)).astype(BF16)
        kdec = (k * jnp.exp(blast - b)).astype(BF16)
        outs = []
        for h in range(GLA_HEADS):
            sl = slice(h * GLA_DK, (h + 1) * GLA_DK)
            v_h = blk[:, 2 * hk + h * GLA_DV:2 * hk + (h + 1) * GLA_DV]
            att = jnp.where(causal, _dot_nt(qd[:, sl], kd[:, sl]), 0.0).astype(BF16)
            s_prev = s_scr[h]
            o = _dot(att, v_h) + _dot(qd[:, sl], s_prev.astype(BF16))
            dsum = _dot_tn(la_hi[:, sl], ones) + _dot_tn(la_lo[:, sl], ones)
            s_scr[h] = s_prev * jnp.exp(dsum) + _dot_tn(kdec[:, sl], v_h)
            ms = jnp.mean(o * o, axis=-1, keepdims=True)
            outs.append(o * lax.rsqrt(ms + EPS) * onw_ref[...])
        o_all = jnp.concatenate(outs, axis=1) * sog_ref[0, pl.ds(r0, chunk), :].astype(F32)
        y_ref[0, pl.ds(r0, chunk), :] = o_all.astype(BF16)
        return carry

    lax.fori_loop(0, n_chunks, body, 0, unroll=min(n_chunks, 2))

    @pl.when(j == pl.num_programs(1) - 1)
    def _():
        sfin_ref[0] = s_scr[...]


def _gla(gqkv, la, sog, s0, onw, chunk, block):
    b, l, _ = gqkv.shape
    tok = lambda width: pl.BlockSpec((1, block, width), lambda bi, j: (bi, j, 0))
    st = pl.BlockSpec((1, GLA_HEADS, GLA_DK, GLA_DV), lambda bi, j: (bi, 0, 0, 0))
    return pl.pallas_call(
        functools.partial(_gla_kernel, chunk=chunk, n_chunks=block // chunk),
        grid=(b, l // block),
        in_specs=[tok(2048), tok(512), tok(1024), st, _resident(onw.shape)],
        out_specs=[tok(1024), st],
        out_shape=[jax.ShapeDtypeStruct((b, l, 1024), BF16),
                   jax.ShapeDtypeStruct((b, GLA_HEADS, GLA_DK, GLA_DV), F32)],
        scratch_shapes=[pltpu.VMEM((GLA_HEADS, GLA_DK, GLA_DV), F32)],
        compiler_params=_params(("arbitrary", "arbitrary")),
        name="gla",
    )(gqkv, la, sog, s0, onw)


def _band_heads(q, kw, vw, valid):
    outs, lses = [], []
    for h in range(SWA_HPG):
        sl = slice(h * SWA_HEAD_DIM, (h + 1) * SWA_HEAD_DIM)
        s = _dot_nt(q[:, sl], kw[:, sl]) * SWA_HEAD_DIM ** -0.5
        s = jnp.where(valid, s, NEG)
        m = jnp.max(s, axis=-1, keepdims=True)
        p = jnp.exp(s - m)
        l = jnp.sum(p, axis=-1, keepdims=True)
        outs.append(_dot(p.astype(BF16), vw[:, sl]) / l)
        lses.append(jnp.broadcast_to(m + jnp.log(l), (q.shape[0], SWA_HEAD_DIM)))
    return jnp.concatenate(outs, axis=1), jnp.concatenate(lses, axis=1)


_SWA_TOKENS = 2048


def _swa_kernel(q_ref, k_ref, v_ref, kp_ref, vp_ref, o_ref, lse_ref, *stage, dil):
    blk = SWA_BLOCK
    nsub = q_ref.shape[1] // (blk * dil)
    first = pl.program_id(1) == 0
    qi = lax.broadcasted_iota(jnp.int32, (blk, 2 * blk), 0)
    kj = lax.broadcasted_iota(jnp.int32, (blk, 2 * blk), 1)
    band = (kj >= qi) & (kj <= qi + blk)
    halves = SWA_OUT // LANES

    if dil > 1:
        ins = (q_ref, k_ref, v_ref, kp_ref, vp_ref)
        q_ref, k_ref, v_ref, kp_ref, vp_ref, o_st, lse_st = stage
        for src, dst in zip(ins, stage):
            for hf in range(halves):
                dst[hf] = src[0, :, hf * LANES:(hf + 1) * LANES]

    def rows(ref, start):
        if dil == 1:
            return ref[0, pl.ds(start, blk), :]
        return jnp.concatenate([ref[hf, pl.ds(start, blk, stride=dil), :] for hf in range(halves)], axis=1)

    def unit(u, carry):
        r = u // nsub
        j = u % nsub
        start = r + dil * blk * j
        inside = r + dil * blk * jnp.maximum(j - 1, 0)
        if dil == 1:
            r, start, inside = 0, pl.multiple_of(start, blk), pl.multiple_of(inside, blk)
        head = j == 0
        kprev = jnp.where(head, rows(kp_ref, r), rows(k_ref, inside))
        vprev = jnp.where(head, rows(vp_ref, r), rows(v_ref, inside))
        kw = jnp.concatenate([kprev, rows(k_ref, start)], axis=0).astype(BF16)
        vw = jnp.concatenate([vprev, rows(v_ref, start)], axis=0).astype(BF16)
        valid = band & (kj >= jnp.where(head & first, blk, 0))
        o, lse = _band_heads(rows(q_ref, start).astype(BF16), kw, vw, valid)
        if dil == 1:
            o_ref[0, pl.ds(start, blk), :] = o
            lse_ref[0, pl.ds(start, blk), :] = lse
        else:
            for hf in range(halves):
                o_st[hf, pl.ds(start, blk, stride=dil), :] = o[:, hf * LANES:(hf + 1) * LANES]
                lse_st[hf, pl.ds(start, blk, stride=dil), :] = lse[:, hf * LANES:(hf + 1) * LANES]
        return carry

    lax.fori_loop(0, dil * nsub, unit, 0)
    if dil > 1:
        for hf in range(halves):
            o_ref[0, :, hf * LANES:(hf + 1) * LANES] = o_st[hf]
            lse_ref[0, :, hf * LANES:(hf + 1) * LANES] = lse_st[hf]


def _swa_prompt(q, k, v, g, dil):
    b, s, _ = q.shape
    tb = _SWA_TOKENS
    back = SWA_BLOCK * dil
    assert s % tb == 0 and tb % back == 0
    cur = pl.BlockSpec((1, tb, SWA_OUT), lambda bi, i: (bi, i, g))
    prev = pl.BlockSpec((1, back, SWA_OUT), lambda bi, i: (bi, jnp.maximum(i * (tb // back) - 1, 0), g))
    out = pl.BlockSpec((1, tb, SWA_OUT), lambda bi, i: (bi, i, 0))
    halves = SWA_OUT // LANES
    stage = [pltpu.VMEM((halves, n, LANES), F32) for n in (tb, tb, tb, back, back, tb, tb)] if dil > 1 else []
    o, lse = pl.pallas_call(
        functools.partial(_swa_kernel, dil=dil),
        grid=(b, s // tb),
        in_specs=[cur, cur, cur, prev, prev],
        out_specs=[out, out],
        out_shape=[jax.ShapeDtypeStruct((b, s, SWA_OUT), F32)] * 2,
        scratch_shapes=stage,
        compiler_params=_params(("arbitrary", "arbitrary")),
        name=f"swa_prompt_g{g}",
    )(q, k, v, k, v)
    return o.reshape(b * s, SWA_OUT), lse.reshape(b * s, SWA_OUT)


_SAMPLE_ROWS = 16


def _swa_sample_kernel(q_ref, kn_ref, vn_ref, knt_ref, vnt_ref, k0_ref, v0_ref, k1_ref, v1_ref, k2_ref, v2_ref,
                       ob_ref, ok0_ref, ov0_ref, ok1_ref, ov1_ref, ok2_ref, ov2_ref, *, n_new):
    rows = _SAMPLE_ROWS
    in_refs = ((k0_ref, v0_ref), (k1_ref, v1_ref), (k2_ref, v2_ref))
    out_refs = ((ok0_ref, ov0_ref), (ok1_ref, ov1_ref), (ok2_ref, ov2_ref))
    scale = SWA_HEAD_DIM ** -0.5
    jn = lax.broadcasted_iota(jnp.int32, (rows, rows), 1)
    ln = lax.broadcasted_iota(jnp.int32, (rows, rows), 0)
    tail = lax.broadcasted_iota(jnp.int32, (SWA_HEAD_DIM, LANES), 1) >= LANES - n_new
    o_g, lse_g = [], []
    for g, (win, dil) in enumerate(SWA_GROUPS):
        jc = lax.broadcasted_iota(jnp.int32, (rows, win), 1)
        lc = lax.broadcasted_iota(jnp.int32, (rows, win), 0)
        valid_c = (jc >= lc) & (((jc - lc) & (dil - 1)) == 0)
        valid_n = (jn <= ln) & (((ln - jn) & (dil - 1)) == 0) & (jn < n_new)
        o_h, lse_h = [], []
        for h in range(SWA_HPG):
            col = g * SWA_OUT + h * SWA_HEAD_DIM
            hsl = slice(col, col + SWA_HEAD_DIM)
            qh = q_ref[0, :, hsl].astype(BF16)
            knh = kn_ref[0, :, hsl].astype(BF16)
            vnh = vn_ref[0, :, hsl].astype(BF16)
            for (src, dst, new_t) in ((in_refs[g][0], out_refs[g][0], knt_ref), (in_refs[g][1], out_refs[g][1], vnt_ref)):
                old = src[0, h]
                moved = pltpu.roll(old, win - n_new, 1)
                if win > LANES:
                    dst[0, h, :, 0:win - LANES] = moved[:, 0:win - LANES]
                dst[0, h, :, win - LANES:win] = jnp.where(tail, new_t[0, hsl, :], moved[:, win - LANES:win])
            kt = in_refs[g][0][0, h].astype(BF16)
            vt = in_refs[g][1][0, h].astype(BF16)
            s_c = jnp.where(valid_c, _dot(qh, kt) * scale, NEG)
            s_n = jnp.where(valid_n, _dot_nt(qh, knh) * scale, NEG)
            m = jnp.maximum(jnp.max(s_c, axis=-1, keepdims=True), jnp.max(s_n, axis=-1, keepdims=True))
            p_c = jnp.exp(s_c - m)
            p_n = jnp.exp(s_n - m)
            den = jnp.sum(p_c, axis=-1, keepdims=True) + jnp.sum(p_n, axis=-1, keepdims=True)
            o_h.append((_dot_nt(p_c.astype(BF16), vt) + _dot(p_n.astype(BF16), vnh)) / den)
            lse_h.append(jnp.broadcast_to(m + jnp.log(den), (rows, SWA_HEAD_DIM)))
        o_g.append(jnp.concatenate(o_h, axis=1))
        lse_g.append(jnp.concatenate(lse_h, axis=1))
    lmax = jnp.maximum(jnp.maximum(lse_g[0], lse_g[1]), lse_g[2])
    e = [jnp.exp(x - lmax) for x in lse_g]
    ob_ref[0] = ((e[0] * o_g[0] + e[1] * o_g[1] + e[2] * o_g[2]) / (e[0] + e[1] + e[2])).astype(BF16)


def _swa_sample(q, kn, vn, knt, vnt, caches_t, n_new):
    bd = q.shape[0]
    rows = _SAMPLE_ROWS
    tok = pl.BlockSpec((1, rows, SWA_WIDTH), lambda bi: (bi, 0, 0))
    new_t = pl.BlockSpec((1, SWA_WIDTH, LANES), lambda bi: (bi, 0, 0))
    specs, args = [tok, tok, tok, new_t, new_t], [q, kn, vn, knt, vnt]
    out_specs = [pl.BlockSpec((1, rows, SWA_OUT), lambda bi: (bi, 0, 0))]
    out_shape = [jax.ShapeDtypeStruct((bd, rows, SWA_OUT), BF16)]
    for g, (win, dil) in enumerate(SWA_GROUPS):
        for t in caches_t[g]:
            assert t.shape == (bd, SWA_HPG, SWA_HEAD_DIM, win) and win == SWA_BLOCK * dil and win % LANES == 0
            spec = pl.BlockSpec((1, SWA_HPG, SWA_HEAD_DIM, win), lambda bi: (bi, 0, 0, 0))
            args.append(t)
            specs.append(spec)
            out_specs.append(spec)
            out_shape.append(jax.ShapeDtypeStruct(t.shape, t.dtype))
    return pl.pallas_call(
        functools.partial(_swa_sample_kernel, n_new=n_new),
        grid=(bd,),
        in_specs=specs,
        out_specs=out_specs,
        out_shape=out_shape,
        compiler_params=_params(("arbitrary",)),
        name="swa_sample",
    )(*args)


def _merge_kernel(*refs, combine, n_alias, n_real):
    x1_ref, h2c_ref, lg_ref = refs[-3:]
    refs = refs[:len(refs) - 3 - n_alias]

    @pl.when(pl.program_id(0) >= n_real)
    def _():
        h2c_ref[...] = jnp.zeros_like(h2c_ref)
        lg_ref[...] = jnp.zeros_like(lg_ref)

    pl.when(pl.program_id(0) < n_real)(functools.partial(_merge_tile, refs, x1_ref, h2c_ref, lg_ref, combine))


def _merge_tile(refs, x1_ref, h2c_ref, lg_ref, combine):
    if combine:
        (x_ref, ya_ref, o0, o1, o2, l0, l1, l2, sgab_ref, wa_ref, wb_ref, wo_ref, ln_ref, wr_hi_ref, wr_lo_ref,
         br_ref) = refs
        lmax = jnp.maximum(jnp.maximum(l0[...], l1[...]), l2[...])
        e0, e1, e2 = jnp.exp(l0[...] - lmax), jnp.exp(l1[...] - lmax), jnp.exp(l2[...] - lmax)
        ob = ((e0 * o0[...] + e1 * o1[...] + e2 * o2[...]) / (e0 + e1 + e2)).astype(BF16)
    else:
        (x_ref, ya_ref, ob_ref, sgab_ref, wa_ref, wb_ref, wo_ref, ln_ref, wr_hi_ref, wr_lo_ref, br_ref) = refs
        ob = ob_ref[...]
    ya = _dot(ya_ref[...], wa_ref[...])
    yb = _dot(ob, wb_ref[...])
    sga = sgab_ref[:, :D_MODEL].astype(F32)
    sgb = sgab_ref[:, D_MODEL:].astype(F32)
    x1 = x_ref[...] + _dot((sga * ya + sgb * yb).astype(BF16), wo_ref[...])
    x1_ref[...] = x1
    h2 = x1 * lax.rsqrt(jnp.mean(x1 * x1, axis=-1, keepdims=True) + EPS) * ln_ref[...]
    h_hi, h_lo = _split_bf16(h2)
    lg_ref[...] = _dot(h_hi, wr_hi_ref[...]) + _dot(h_hi, wr_lo_ref[...]) + _dot(h_lo, wr_hi_ref[...]) + br_ref[...]
    nsub = D_MODEL // LANES
    for c in range(nsub):
        h2c_ref[pl.ds(c, x1.shape[0], stride=nsub), :] = h2[:, c * LANES:(c + 1) * LANES]


def _merge(x, ya_in, swa, sgab, weights, tm, tok_off, t_all, shared=None):
    t = x.shape[0]
    assert t % tm == 0 and tok_off % tm == 0 and MOE_PART % tm == 0
    combine = len(swa) > 1
    blk_off = tok_off // tm
    n_parts = -(-t_all // MOE_PART)
    nsub = D_MODEL // LANES
    n_real = t // tm
    n_fill = 0 if shared is not None else (n_parts * MOE_PART - tok_off - t) // tm
    row = lambda width: pl.BlockSpec((tm, width), lambda i: (jnp.minimum(i, n_real - 1), 0))
    shared_in = [] if shared is None else list(shared)
    n_in = 3 + len(swa) + len(weights)
    return pl.pallas_call(
        functools.partial(_merge_kernel, combine=combine, n_alias=len(shared_in), n_real=n_real),
        grid=(n_real + n_fill,),
        in_specs=[row(D_MODEL), row(D_MODEL)] + [row(SWA_OUT)] * len(swa) + [row(2 * D_MODEL)]
                 + [_resident(w.shape) for w in weights] + [pl.BlockSpec(memory_space=pl.ANY)] * len(shared_in),
        out_specs=[row(D_MODEL),
                   pl.BlockSpec((tm * nsub, LANES), lambda i: (i + blk_off, 0)),
                   pl.BlockSpec((tm, LANES), lambda i: (i + blk_off, 0))],
        out_shape=[jax.ShapeDtypeStruct((t, D_MODEL), F32),
                   jax.ShapeDtypeStruct((n_parts * MOE_PART * nsub, LANES), F32),
                   jax.ShapeDtypeStruct((n_parts * MOE_PART, LANES), F32)],
        input_output_aliases={n_in + k: 1 + k for k in range(len(shared_in))},
        compiler_params=_params(("arbitrary",)),
        name="merge",
    )(x, ya_in, *swa, sgab, *weights, *shared_in)


def _router_kernel(lg_ref, mi_ref, mw_ref, cnt_ref, carry, *, n_parts):
    phase = pl.program_id(0)
    i = pl.program_id(1)

    @pl.when((phase == 0) & (i == 0))
    def _():
        carry[...] = jnp.zeros_like(carry)

    @pl.when((phase == 1) & (i == 0))
    def _():
        cnt = carry[...]
        cnt_ref[...] = jnp.broadcast_to(cnt, cnt_ref.shape)
        tiles = jnp.floor((cnt + (EXPERT_TILE - 1)) * (1.0 / EXPERT_TILE))
        r = lax.broadcasted_iota(jnp.int32, (LANES, LANES), 0)
        c = lax.broadcasted_iota(jnp.int32, (LANES, LANES), 1)
        before = _dot(jnp.broadcast_to(tiles, (SUBLANES, LANES)).astype(BF16), (r < c).astype(BF16))
        carry[...] = before[0:1, :] * EXPERT_TILE

    lg = lg_ref[...]
    tr = lg.shape[0]
    lane = lax.broadcasted_iota(jnp.int32, lg.shape, 1)
    big = jnp.int32(LANES)
    gl = jnp.where(lane < N_GROUPS, lg, NEG)
    gmax = jnp.max(gl, axis=-1, keepdims=True)
    g_idx = jnp.min(jnp.where(gl == gmax, lane, big), axis=-1, keepdims=True)
    g_w = 1.0 / jnp.sum(jnp.exp(gl - gmax), axis=-1, keepdims=True)
    e_lane = lane - N_GROUPS
    in_group = (e_lane >= 0) & (e_lane < N_EXPERTS) & (e_lane // EXPERTS_PER_GROUP == g_idx)
    el = jnp.where(in_group, lg, NEG)
    v1 = jnp.max(el, axis=-1, keepdims=True)
    i1 = jnp.min(jnp.where(el == v1, lane, big), axis=-1, keepdims=True)
    el2 = jnp.where(lane == i1, NEG, el)
    v2 = jnp.max(el2, axis=-1, keepdims=True)
    i2 = jnp.min(jnp.where(el2 == v2, lane, big), axis=-1, keepdims=True)
    r21 = jnp.exp(v2 - v1)
    w1 = g_w / (1.0 + r21)
    w2 = g_w * r21 / (1.0 + r21)
    e1 = i1 - N_GROUPS
    e2 = i2 - N_GROUPS
    tok = i * tr + lax.broadcasted_iota(jnp.int32, (tr, 1), 0)
    part = sum((tok >= k * MOE_PART).astype(jnp.int32) for k in range(1, n_parts))
    hot1 = lane == e1 + N_EXPERTS * part
    hot2 = lane == e2 + N_EXPERTS * part
    hot = (hot1 | hot2).astype(BF16)
    r = lax.broadcasted_iota(jnp.int32, (tr, tr), 0)
    c = lax.broadcasted_iota(jnp.int32, (tr, tr), 1)
    csum = _dot((r > c).astype(BF16), hot) + carry[...]
    rank1 = jnp.sum(jnp.where(hot1, csum, 0.0), axis=-1, keepdims=True).astype(jnp.int32)
    rank2 = jnp.sum(jnp.where(hot2, csum, 0.0), axis=-1, keepdims=True).astype(jnp.int32)
    carry[...] = carry[...] + jnp.sum(hot.astype(F32), axis=0, keepdims=True)
    mi_ref[...] = jnp.where(lane == 0, e1, jnp.where(lane == 1, e2, jnp.where(lane == 2, rank1, rank2)))
    mw_ref[...] = jnp.where(lane == 0, w1, w2)


def _router(logits, t, tr):
    n_parts = -(-t // MOE_PART)
    assert t % tr == 0 and n_parts * N_EXPERTS <= LANES
    out_row = pl.BlockSpec((tr, LANES), lambda p, i: (i * p, 0))
    return pl.pallas_call(
        functools.partial(_router_kernel, n_parts=n_parts),
        grid=(2, t // tr),
        in_specs=[pl.BlockSpec((tr, LANES), lambda p, i: (i, 0))],
        out_specs=[out_row, out_row, pl.BlockSpec((SUBLANES, LANES), lambda p, i: (0, 0))],
        out_shape=[jax.ShapeDtypeStruct((t, LANES), jnp.int32), jax.ShapeDtypeStruct((t, LANES), F32),
                   jax.ShapeDtypeStruct((SUBLANES, LANES), F32)],
        scratch_shapes=[pltpu.VMEM((1, LANES), F32)],
        compiler_params=_params(("arbitrary", "arbitrary")),
        name="router",
    )(logits)


def _expert_kernel(te_ref, tp_ref, tv_ref, end_ref, cnt_ref, p0_ref, p1_ref, src_ref, wg_ref, wu_ref, wd_ref,
                   out_ref, s2t, xs, wgb, wub, wdb, *, n_tok):
    i = pl.program_id(0)
    tm = EXPERT_TILE
    nsub = D_MODEL // LANES

    @pl.when(i == 0)
    def _():
        def clear_segment(sg, carry):
            end = end_ref[sg]
            cnt = cnt_ref[sg]

            def clear(j, c):
                s2t[j] = 0
                return c

            lax.fori_loop(end - (cnt + tm - 1) // tm * tm + cnt, end, clear, 0)
            return carry

        lax.fori_loop(0, end_ref.shape[0], clear_segment, 0)
        for lo in range(0, n_tok, MOE_PART):
            def place(t, carry, lo=lo):
                row0 = (t - lo) * nsub
                s2t[p0_ref[t]] = row0
                s2t[p1_ref[t]] = row0
                return carry

            lax.fori_loop(lo, min(lo + MOE_PART, n_tok), place, 0, unroll=8)

    valid = tv_ref[i] != 0

    @pl.when(jnp.logical_not(valid))
    def _():
        out_ref[...] = jnp.zeros_like(out_ref)

    @pl.when(valid & ((i == 0) | (te_ref[i] != te_ref[jnp.maximum(i - 1, 0)])))
    def _():
        wgb[...] = wg_ref[0, 0].astype(BF16)
        wub[...] = wu_ref[0, 0].astype(BF16)
        wdb[...] = wd_ref[0, 0].astype(BF16)

    @pl.when(valid)
    def _():
        base = i * tm

        def gather(j, carry):
            row0 = pl.multiple_of(s2t[base + j], nsub)
            xs[pl.ds(pl.multiple_of(j * nsub, nsub), nsub), :] = src_ref[pl.ds(row0, nsub), :]
            return carry

        lax.fori_loop(0, tm, gather, 0, unroll=8)
        h = jnp.concatenate([xs[pl.ds(c, tm, stride=nsub), :] for c in range(nsub)], axis=1).astype(BF16)
        gate = _dot(h, wgb[...])
        up = _dot(h, wub[...])
        a = (gate * _sigmoid(gate) * up).astype(BF16)
        o = _dot(a, wdb[...])
        for c in range(nsub):
            out_ref[pl.ds(c, tm, stride=nsub), :] = o[:, c * LANES:(c + 1) * LANES]


def _experts(tile_expert, tile_part, tile_valid, seg_end, seg_cnt, pos0, pos1, src, wg, wu, wd):
    n_tiles = tile_expert.shape[0]
    tm = EXPERT_TILE
    n_tok = pos0.shape[0]
    nsub = D_MODEL // LANES
    wspec = lambda shape: pl.BlockSpec((1, 1) + shape, lambda i, te, *_: (0, te[i], 0, 0))
    grid_spec = pltpu.PrefetchScalarGridSpec(
        num_scalar_prefetch=7,
        grid=(n_tiles,),
        in_specs=[pl.BlockSpec((MOE_PART * nsub, LANES), lambda i, te, tp, *_: (tp[i], 0),
                               pipeline_mode=pl.Buffered(1)),
                  wspec((D_MODEL, D_EXPERT)), wspec((D_MODEL, D_EXPERT)), wspec((D_EXPERT, D_MODEL))],
        out_specs=pl.BlockSpec((tm * nsub, LANES), lambda i, *_: (i, 0)),
        scratch_shapes=[pltpu.SMEM((n_tiles * tm,), jnp.int32),
                        pltpu.VMEM((tm * nsub, LANES), F32),
                        pltpu.VMEM((D_MODEL, D_EXPERT), BF16), pltpu.VMEM((D_MODEL, D_EXPERT), BF16),
                        pltpu.VMEM((D_EXPERT, D_MODEL), BF16)],
    )
    return pl.pallas_call(
        functools.partial(_expert_kernel, n_tok=n_tok),
        grid_spec=grid_spec,
        out_shape=jax.ShapeDtypeStruct((n_tiles * tm * nsub, LANES), F32),
        compiler_params=_params(("arbitrary",)),
        name="experts",
    )(tile_expert, tile_part, tile_valid, seg_end, seg_cnt, pos0, pos1, src, wg, wu, wd)


def _combine_kernel(p0_ref, p1_ref, x1_ref, mw_ref, rows_hbm, y_ref, g0, g1, sem, *, tc, blk_off):
    i = pl.program_id(0)
    n = pl.num_programs(0)
    nout = D_MODEL // LANES
    slot = i % 2

    def issue_all(step, sl):
        base = (step + blk_off) * tc

        def issue(j, carry):
            for pos_ref, dst, s in ((p0_ref, g0, 0), (p1_ref, g1, 1)):
                src0 = pl.multiple_of(pos_ref[base + j] * nout, nout)
                pltpu.make_async_copy(rows_hbm.at[pl.ds(src0, nout), :],
                                      dst.at[sl, pl.ds(pl.multiple_of(j * nout, nout), nout), :],
                                      sem.at[sl, s]).start()
            return carry

        lax.fori_loop(0, tc, issue, 0, unroll=4)

    @pl.when(i == 0)
    def _():
        issue_all(i, slot)

    @pl.when(i + 1 < n)
    def _():
        issue_all(i + 1, 1 - slot)

    for dst, s in ((g0, 0), (g1, 1)):
        pltpu.make_async_copy(rows_hbm.at[pl.ds(0, tc * nout), :], dst.at[slot], sem.at[slot, s]).wait()
    w0 = mw_ref[:, 0:1]
    w1 = mw_ref[:, 1:2]
    for c in range(nout):
        sl = slice(c * LANES, (c + 1) * LANES)
        y_ref[:, sl] = (x1_ref[:, sl] + w0 * g0[slot, pl.ds(c, tc, stride=nout), :]
                        + w1 * g1[slot, pl.ds(c, tc, stride=nout), :])


def _combine(pos0, pos1, x1, meta_w, rows, tc, tok_off):
    t = x1.shape[0]
    assert t % tc == 0 and tok_off % tc == 0
    blk_off = tok_off // tc
    nout = D_MODEL // LANES
    row = pl.BlockSpec((tc, D_MODEL), lambda i, p0, p1: (i, 0))
    grid_spec = pltpu.PrefetchScalarGridSpec(
        num_scalar_prefetch=2,
        grid=(t // tc,),
        in_specs=[row, pl.BlockSpec((tc, LANES), lambda i, p0, p1: (i + blk_off, 0)),
                  pl.BlockSpec(memory_space=pl.ANY)],
        out_specs=row,
        scratch_shapes=[pltpu.VMEM((2, tc * nout, LANES), F32), pltpu.VMEM((2, tc * nout, LANES), F32),
                        pltpu.SemaphoreType.DMA((2, 2))],
    )
    return pl.pallas_call(
        functools.partial(_combine_kernel, tc=tc, blk_off=blk_off),
        grid_spec=grid_spec,
        out_shape=jax.ShapeDtypeStruct((t, D_MODEL), F32),
        compiler_params=_params(("arbitrary",)),
        name="combine",
    )(pos0, pos1, x1, meta_w, rows)


def _rope_tables(pos):
    half = SWA_HEAD_DIM // 2
    inv_freq = ROPE_THETA ** (-jnp.arange(half, dtype=F32) / half)
    ang = pos.astype(F32)[:, None] * inv_freq[None, :]
    cos, sin = jnp.cos(ang), jnp.sin(ang)
    reps = LANES // SWA_HEAD_DIM
    return jnp.tile(jnp.concatenate([cos, cos], axis=1), (1, reps)), jnp.tile(jnp.concatenate([-sin, sin], axis=1), (1, reps))


def kernel(x_prompt, x_sample, state_gla, cache_swa_k0, cache_swa_v0, cache_swa_k1, cache_swa_v1, cache_swa_k2, cache_swa_v2, ln1_w, w_in, w_gla_lr, b_gla_lr, gla_onorm_w, q_norm_w, k_norm_w, w_branch_a, w_branch_b, w_out, ln2_w, w_router_group, b_router_group, w_router_expert, b_router_expert, w_exp_gate, w_exp_up, w_exp_down):
    b, s, d = x_prompt.shape
    bd, ls, _ = x_sample.shape
    tp, ts = b * s, bd * ls
    assert w_in.shape[0] == 1 and d == D_MODEL and ts % SUBLANES == 0
    k_caches = (cache_swa_k0, cache_swa_k1, cache_swa_k2)
    v_caches = (cache_swa_v0, cache_swa_v1, cache_swa_v2)

    w = w_in[0]
    cuts = np.cumsum((512, 512, 1024, GLA_RANK, 1024, 768, 768, 768, 1024, 1024))
    sec = lambda a: w[:, (0 if a == 0 else cuts[a - 1]):cuts[a]]
    lr_pad = jnp.pad(sec(3), ((0, 0), (0, LANES - GLA_RANK)))
    w_packed = jnp.concatenate([sec(0), sec(1), sec(2), sec(4), sec(5), sec(6), sec(7), sec(8), sec(9), lr_pad],
                               axis=1).astype(BF16)
    wlr = jnp.pad(w_gla_lr[0], ((0, LANES - GLA_RANK), (0, 0))).astype(BF16)
    blr = b_gla_lr[0][None, :]
    nw = jnp.concatenate([jnp.tile(q_norm_w[0], SWA_WIDTH // SWA_HEAD_DIM), jnp.tile(k_norm_w[0], SWA_WIDTH // SWA_HEAD_DIM)])[None, :]
    gi = np.arange(256) // SWA_HEAD_DIM
    gmat = jnp.asarray((gi[:, None] == gi[None, :]).astype(np.float32) / SWA_HEAD_DIM, dtype=BF16)
    ln1 = ln1_w[0][None, :]
    ln2 = ln2_w[0][None, :]
    onw = gla_onorm_w[0][None, :]
    wa = w_branch_a[0].astype(BF16)
    wb = w_branch_b[0].astype(BF16)
    wo = w_out[0].astype(BF16)
    wr = jnp.pad(jnp.concatenate([w_router_group[0], w_router_expert[0]], axis=1),
                 ((0, 0), (0, LANES - N_GROUPS - N_EXPERTS)))
    wr_hi, wr_lo = _split_bf16(wr)
    br = jnp.pad(jnp.concatenate([b_router_group[0], b_router_expert[0]]), (0, LANES - N_GROUPS - N_EXPERTS))[None, :]

    cos_p, sin_p = _rope_tables(jnp.arange(s, dtype=jnp.int32))
    cos_s, sin_s = _rope_tables(PAST_LEN + jnp.arange(ts, dtype=jnp.int32) % ls)

    tm_p = 512
    proj_p = _proj(x_prompt.reshape(tp, d), cos_p, sin_p, s // tm_p, ln1, w_packed, wlr, blr, nw, gmat, tm_p)
    proj_s = _proj(x_sample.reshape(ts, d), cos_s, sin_s, 1, ln1, w_packed, wlr, blr, nw, gmat, ts)
    gqkv_p, la_p, sog_p, q_p, k_p, v_p, sgab_p = proj_p
    gqkv_s, la_s, sog_s, q_s, k_s, v_s, sgab_s = proj_s

    r3 = lambda t, nb: t.reshape(nb, t.shape[0] // nb, t.shape[1])
    ya_p, st_p = _gla(r3(gqkv_p, b), r3(la_p, b), r3(sog_p, b),
                      jnp.zeros((b, GLA_HEADS, GLA_DK, GLA_DV), F32), onw, GLA_CHUNK, 512)
    pad_s = lambda t: jnp.pad(r3(t, bd), ((0, 0), (0, _SAMPLE_ROWS - ls), (0, 0)))
    ya_s, st_s = _gla(pad_s(gqkv_s), pad_s(la_s), pad_s(sog_s), state_gla[0], onw, _SAMPLE_ROWS, _SAMPLE_ROWS)
    ya_s = ya_s[:, :ls].reshape(ts, d)

    q3, k3, v3 = r3(q_p, b), r3(k_p, b), r3(v_p, b)
    swa_p = [_swa_prompt(q3, k3, v3, g, dil) for g, (_, dil) in enumerate(SWA_GROUPS)]
    to_t = lambda c: jnp.transpose(c[0], (0, 2, 3, 1))
    caches_t = [(to_t(k_caches[g]), to_t(v_caches[g])) for g in range(len(SWA_GROUPS))]
    new_t = lambda t: jnp.pad(jnp.transpose(r3(t, bd), (0, 2, 1)), ((0, 0), (0, 0), (LANES - ls, 0)))
    ob_s, *new_caches = _swa_sample(pad_s(q_s), pad_s(k_s), pad_s(v_s), new_t(k_s), new_t(v_s), caches_t, ls)
    ob_s = ob_s[:, :ls].reshape(ts, SWA_OUT)

    t = tp + ts
    mw = (wa, wb, wo, ln2, wr_hi, wr_lo, br)
    x1_p, h2c, lg = _merge(x_prompt.reshape(tp, d), ya_p.reshape(tp, d),
                           [o for o, _ in swa_p] + [l for _, l in swa_p], sgab_p, mw, 512, 0, t)
    x1_s, h2c, lg = _merge(x_sample.reshape(ts, d), ya_s, [ob_s], sgab_s, mw, ts, tp, t, shared=(h2c, lg))

    meta_i, meta_w, cnt = _router(lg, t, t // 12)
    tm = EXPERT_TILE
    n_seg = -(-t // MOE_PART) * N_EXPERTS
    n_tiles = (2 * t) // tm + n_seg
    counts = cnt[0, :n_seg].astype(jnp.int32)
    ends = jnp.cumsum((counts + tm - 1) // tm * tm)
    pos0, pos1 = meta_i[:, 2], meta_i[:, 3]
    tile_start = jnp.arange(n_tiles, dtype=jnp.int32) * tm
    tile_valid = (tile_start < ends[-1]).astype(jnp.int32)
    last_slot = jnp.minimum(tile_start, ends[-1] - 1)
    tile_seg = jnp.minimum(jnp.sum((last_slot[:, None] >= ends[None, :]).astype(jnp.int32), axis=1), n_seg - 1)

    rows = _experts(tile_seg % N_EXPERTS, tile_seg // N_EXPERTS, tile_valid, ends, counts, pos0, pos1, h2c,
                    w_exp_gate, w_exp_up, w_exp_down)
    y_p = _combine(pos0, pos1, x1_p, meta_w, rows, 256, 0)
    y_s = _combine(pos0, pos1, x1_s, meta_w, rows, ts, tp)

    heads = lambda a: a.reshape(1, a.shape[0], a.shape[1], SWA_HPG, SWA_HEAD_DIM)
    outs = [y_p.reshape(b, s, d), y_s.reshape(bd, ls, d), st_p[None].astype(x_prompt.dtype)]
    for g, (win, _) in enumerate(SWA_GROUPS):
        keep = min(win, s)
        gsl = slice(g * SWA_OUT, (g + 1) * SWA_OUT)
        outs += [heads(k3[:, s - keep:, gsl]), heads(v3[:, s - keep:, gsl])]
    outs.append(st_s[None].astype(state_gla.dtype))
    outs += [jnp.transpose(c, (0, 3, 1, 2))[None] for c in new_caches]
    return tuple(outs)
```

```python
import functools

import numpy as np
import jax
import jax.numpy as jnp
from jax import lax
from jax.experimental import pallas as pl
from jax.experimental.pallas import tpu as pltpu

F32 = jnp.float32
BF16 = jnp.bfloat16

D_MODEL = 1024
PAST_LEN = 16384
GLA_HEADS = 4
GLA_DK = 128
GLA_DV = 256
GLA_RANK = 16
GLA_TAU = 16.0
GLA_CHUNK = 64
SWA_GROUPS = ((128, 1), (512, 4), (2048, 16))
SWA_HPG = 4
SWA_HEAD_DIM = 64
SWA_WIDTH = 768
SWA_OUT = 256
SWA_BLOCK = 128
ROPE_THETA = 10000.0
N_GROUPS = 4
EXPERTS_PER_GROUP = 8
N_EXPERTS = 32
D_EXPERT = 512
EPS = 1e-6

LANES = 128
SUBLANES = 8
VMEM_LIMIT = 56 * 1024 * 1024
NEG = -1e30
EXPERT_TILE = 256
MERGE_TILE = 512

_C_GQKV = (0, 2048)
_C_GOG = (2048, 3072)
_C_QK = (3072, 4608)
_C_V = (4608, 5376)
_C_GAB = (5376, 7424)
_C_LR = (7424, 7552)


def _dot(a, b):
    return jnp.dot(a, b, preferred_element_type=F32)


def _dot_nt(a, b):
    return lax.dot_general(a, b, (((1,), (1,)), ((), ())), preferred_element_type=F32)


def _dot_tn(a, b):
    return lax.dot_general(a, b, (((0,), (0,)), ((), ())), preferred_element_type=F32)


def _sigmoid(x):
    return 1.0 / (1.0 + jnp.exp(-x))


def _split_bf16(x):
    hi = x.astype(BF16)
    lo = (x - hi.astype(F32)).astype(BF16)
    return hi, lo


def _params(sem):
    return pltpu.CompilerParams(dimension_semantics=sem, vmem_limit_bytes=VMEM_LIMIT)


def _resident(shape):
    nd = len(shape)
    return pl.BlockSpec(shape, lambda *_: (0,) * nd, pipeline_mode=pl.Buffered(1))


def _proj_kernel(x_ref, cos_ref, sin_ref, ln_ref, w_ref, wlr_ref, blr_ref, nw_ref, g_ref,
                 gqkv_ref, la_ref, sog_ref, q_ref, k_ref, v_ref, sgab_ref):
    x = x_ref[...]
    h = (x * lax.rsqrt(jnp.mean(x * x, axis=-1, keepdims=True) + EPS) * ln_ref[...]).astype(BF16)
    gqkv_ref[...] = _dot(h, w_ref[:, _C_GQKV[0]:_C_GQKV[1]]).astype(BF16)
    og = _dot(h, w_ref[:, _C_GOG[0]:_C_GOG[1]])
    sog_ref[...] = (og * _sigmoid(og)).astype(BF16)
    lr = _dot(h, w_ref[:, _C_LR[0]:_C_LR[1]]).astype(BF16)
    z = _dot(lr, wlr_ref[...]) + blr_ref[...]
    la_ref[...] = (jnp.minimum(z, 0.0) - jnp.log(1.0 + jnp.exp(-jnp.abs(z)))) / GLA_TAU
    qk = _dot(h, w_ref[:, _C_QK[0]:_C_QK[1]])
    sq = (qk * qk).astype(BF16)
    ms = jnp.concatenate([_dot(sq[:, c * 256:(c + 1) * 256], g_ref[...]) for c in range(6)], axis=1)
    qn = qk * lax.rsqrt(ms + EPS) * nw_ref[...]
    width = 2 * SWA_WIDTH
    cos = jnp.tile(cos_ref[...], (1, width // LANES))
    sin = jnp.tile(sin_ref[...], (1, width // LANES))
    lane = lax.broadcasted_iota(jnp.int32, qn.shape, 1)
    half = SWA_HEAD_DIM // 2
    rot = jnp.where(lane % SWA_HEAD_DIM < half, pltpu.roll(qn, width - half, 1), pltpu.roll(qn, half, 1))
    qr = qn * cos + rot * sin
    q_ref[...] = qr[:, :SWA_WIDTH]
    k_ref[...] = qr[:, SWA_WIDTH:]
    v_ref[...] = _dot(h, w_ref[:, _C_V[0]:_C_V[1]])
    gab = _dot(h, w_ref[:, _C_GAB[0]:_C_GAB[1]])
    sgab_ref[...] = _sigmoid(gab).astype(BF16)


def _proj(x, cos, sin, rope_blocks, ln, w, wlr, blr, nw, g, tm):
    t = x.shape[0]
    assert t % tm == 0
    row = lambda width: pl.BlockSpec((tm, width), lambda i: (i, 0))
    outs = [(2048, BF16), (512, F32), (1024, BF16), (768, F32), (768, F32), (768, F32), (2048, BF16)]
    return pl.pallas_call(
        _proj_kernel,
        grid=(t // tm,),
        in_specs=[row(D_MODEL),
                  pl.BlockSpec((tm, LANES), lambda i: (i % rope_blocks, 0)),
                  pl.BlockSpec((tm, LANES), lambda i: (i % rope_blocks, 0)),
                  _resident(ln.shape), _resident(w.shape), _resident(wlr.shape), _resident(blr.shape),
                  _resident(nw.shape), _resident(g.shape)],
        out_specs=[row(wd) for wd, _ in outs],
        out_shape=[jax.ShapeDtypeStruct((t, wd), dt) for wd, dt in outs],
        compiler_params=_params(("arbitrary",)),
        name="proj",
    )(x, cos, sin, ln, w, wlr, blr, nw, g)


def _gla_kernel(gqkv_ref, la_ref, sog_ref, s0_ref, onw_ref, y_ref, sfin_ref, s_scr, *, chunk, n_chunks):
    j = pl.program_id(1)

    @pl.when(j == 0)
    def _():
        s_scr[...] = s0_ref[0]

    r = lax.broadcasted_iota(jnp.int32, (chunk, chunk), 0)
    c = lax.broadcasted_iota(jnp.int32, (chunk, chunk), 1)
    causal = r >= c
    tri = causal.astype(BF16)
    ones = jnp.ones((chunk, GLA_DV), BF16)
    hk = GLA_HEADS * GLA_DK

    def body(ci, carry):
        r0 = pl.multiple_of(ci * chunk, chunk)
        blk = gqkv_ref[0, pl.ds(r0, chunk), :]
        la_hi, la_lo = _split_bf16(la_ref[0, pl.ds(r0, chunk), :])
        b = _dot(tri, la_hi) + _dot(tri, la_lo)
        blast = b[chunk - 1:chunk, :]
        q = blk[:, :hk].astype(F32) * GLA_DK ** -0.5
        k = blk[:, hk:2 * hk].astype(F32)
        qd = (q * jnp.exp(b)).astype(BF16)
        kd = (k * jnp.exp(-b)).astype(BF16)
        kdec = (k * jnp.exp(blast - b)).astype(BF16)
        outs = []
        for h in range(GLA_HEADS):
            sl = slice(h * GLA_DK, (h + 1) * GLA_DK)
            v_h = blk[:, 2 * hk + h * GLA_DV:2 * hk + (h + 1) * GLA_DV]
            att = jnp.where(causal, _dot_nt(qd[:, sl], kd[:, sl]), 0.0).astype(BF16)
            s_prev = s_scr[h]
            o = _dot(att, v_h) + _dot(qd[:, sl], s_prev.astype(BF16))
            dsum = _dot_tn(la_hi[:, sl], ones) + _dot_tn(la_lo[:, sl], ones)
            s_scr[h] = s_prev * jnp.exp(dsum) + _dot_tn(kdec[:, sl], v_h)
            ms = jnp.mean(o * o, axis=-1, keepdims=True)
            outs.append(o * lax.rsqrt(ms + EPS) * onw_ref[...])
        o_all = jnp.concatenate(outs, axis=1) * sog_ref[0, pl.ds(r0, chunk), :].astype(F32)
        y_ref[0, pl.ds(r0, chunk), :] = o_all.astype(BF16)
        return carry

    lax.fori_loop(0, n_chunks, body, 0, unroll=min(n_chunks, 2))

    @pl.when(j == pl.num_programs(1) - 1)
    def _():
        sfin_ref[0] = s_scr[...]


def _gla(gqkv, la, sog, s0, onw, chunk, block):
    b, l, _ = gqkv.shape
    tok = lambda width: pl.BlockSpec((1, block, width), lambda bi, j: (bi, j, 0))
    st = pl.BlockSpec((1, GLA_HEADS, GLA_DK, GLA_DV), lambda bi, j: (bi, 0, 0, 0))
    return pl.pallas_call(
        functools.partial(_gla_kernel, chunk=chunk, n_chunks=block // chunk),
        grid=(b, l // block),
        in_specs=[tok(2048), tok(512), tok(1024), st, _resident(onw.shape)],
        out_specs=[tok(1024), st],
        out_shape=[jax.ShapeDtypeStruct((b, l, 1024), BF16),
                   jax.ShapeDtypeStruct((b, GLA_HEADS, GLA_DK, GLA_DV), F32)],
        scratch_shapes=[pltpu.VMEM((GLA_HEADS, GLA_DK, GLA_DV), F32)],
        compiler_params=_params(("arbitrary", "arbitrary")),
        name="gla",
    )(gqkv, la, sog, s0, onw)


def _band_heads(q, kw, vw, valid):
    outs, lses = [], []
    for h in range(SWA_HPG):
        sl = slice(h * SWA_HEAD_DIM, (h + 1) * SWA_HEAD_DIM)
        s = _dot_nt(q[:, sl], kw[:, sl]) * SWA_HEAD_DIM ** -0.5
        s = jnp.where(valid, s, NEG)
        m = jnp.max(s, axis=-1, keepdims=True)
        p = jnp.exp(s - m)
        l = jnp.sum(p, axis=-1, keepdims=True)
        outs.append(_dot(p.astype(BF16), vw[:, sl]) / l)
        lses.append(jnp.broadcast_to(m + jnp.log(l), (q.shape[0], SWA_HEAD_DIM)))
    return jnp.concatenate(outs, axis=1), jnp.concatenate(lses, axis=1)


_SWA_TOKENS = 2048


def _swa_kernel(q_ref, k_ref, v_ref, kp_ref, vp_ref, o_ref, lse_ref, *stage, dil):
    blk = SWA_BLOCK
    nsub = q_ref.shape[1] // (blk * dil)
    first = pl.program_id(1) == 0
    qi = lax.broadcasted_iota(jnp.int32, (blk, 2 * blk), 0)
    kj = lax.broadcasted_iota(jnp.int32, (blk, 2 * blk), 1)
    band = (kj >= qi) & (kj <= qi + blk)
    halves = SWA_OUT // LANES

    if dil > 1:
        ins = (q_ref, k_ref, v_ref, kp_ref, vp_ref)
        q_ref, k_ref, v_ref, kp_ref, vp_ref, o_st, lse_st = stage
        for src, dst in zip(ins, stage):
            for hf in range(halves):
                dst[hf] = src[0, :, hf * LANES:(hf + 1) * LANES]

    def rows(ref, start):
        if dil == 1:
            return ref[0, pl.ds(start, blk), :]
        return jnp.concatenate([ref[hf, pl.ds(start, blk, stride=dil), :] for hf in range(halves)], axis=1)

    def unit(u, carry):
        r = u // nsub
        j = u % nsub
        start = r + dil * blk * j
        inside = r + dil * blk * jnp.maximum(j - 1, 0)
        if dil == 1:
            r, start, inside = 0, pl.multiple_of(start, blk), pl.multiple_of(inside, blk)
        head = j == 0
        kprev = jnp.where(head, rows(kp_ref, r), rows(k_ref, inside))
        vprev = jnp.where(head, rows(vp_ref, r), rows(v_ref, inside))
        kw = jnp.concatenate([kprev, rows(k_ref, start)], axis=0).astype(BF16)
        vw = jnp.concatenate([vprev, rows(v_ref, start)], axis=0).astype(BF16)
        valid = band & (kj >= jnp.where(head & first, blk, 0))
        o, lse = _band_heads(rows(q_ref, start).astype(BF16), kw, vw, valid)
        if dil == 1:
            o_ref[0, pl.ds(start, blk), :] = o
            lse_ref[0, pl.ds(start, blk), :] = lse
        else:
            for hf in range(halves):
                o_st[hf, pl.ds(start, blk, stride=dil), :] = o[:, hf * LANES:(hf + 1) * LANES]
                lse_st[hf, pl.ds(start, blk, stride=dil), :] = lse[:, hf * LANES:(hf + 1) * LANES]
        return carry

    lax.fori_loop(0, dil * nsub, unit, 0)
    if dil > 1:
        for hf in range(halves):
            o_ref[0, :, hf * LANES:(hf + 1) * LANES] = o_st[hf]
            lse_ref[0, :, hf * LANES:(hf + 1) * LANES] = lse_st[hf]


def _swa_prompt(q, k, v, g, dil):
    b, s, _ = q.shape
    tb = _SWA_TOKENS
    back = SWA_BLOCK * dil
    assert s % tb == 0 and tb % back == 0
    cur = pl.BlockSpec((1, tb, SWA_OUT), lambda bi, i: (bi, i, g))
    prev = pl.BlockSpec((1, back, SWA_OUT), lambda bi, i: (bi, jnp.maximum(i * (tb // back) - 1, 0), g))
    out = pl.BlockSpec((1, tb, SWA_OUT), lambda bi, i: (bi, i, 0))
    halves = SWA_OUT // LANES
    stage = [pltpu.VMEM((halves, n, LANES), F32) for n in (tb, tb, tb, back, back, tb, tb)] if dil > 1 else []
    o, lse = pl.pallas_call(
        functools.partial(_swa_kernel, dil=dil),
        grid=(b, s // tb),
        in_specs=[cur, cur, cur, prev, prev],
        out_specs=[out, out],
        out_shape=[jax.ShapeDtypeStruct((b, s, SWA_OUT), F32)] * 2,
        scratch_shapes=stage,
        compiler_params=_params(("arbitrary", "arbitrary")),
        name=f"swa_prompt_g{g}",
    )(q, k, v, k, v)
    return o.reshape(b * s, SWA_OUT), lse.reshape(b * s, SWA_OUT)


_SAMPLE_ROWS = 16


def _swa_sample_kernel(q_ref, kn_ref, vn_ref, knt_ref, vnt_ref, k0_ref, v0_ref, k1_ref, v1_ref, k2_ref, v2_ref,
                       ob_ref, ok0_ref, ov0_ref, ok1_ref, ov1_ref, ok2_ref, ov2_ref, *, n_new):
    rows = _SAMPLE_ROWS
    in_refs = ((k0_ref, v0_ref), (k1_ref, v1_ref), (k2_ref, v2_ref))
    out_refs = ((ok0_ref, ov0_ref), (ok1_ref, ov1_ref), (ok2_ref, ov2_ref))
    scale = SWA_HEAD_DIM ** -0.5
    jn = lax.broadcasted_iota(jnp.int32, (rows, rows), 1)
    ln = lax.broadcasted_iota(jnp.int32, (rows, rows), 0)
    tail = lax.broadcasted_iota(jnp.int32, (SWA_HEAD_DIM, LANES), 1) >= LANES - n_new
    o_g, lse_g = [], []
    for g, (win, dil) in enumerate(SWA_GROUPS):
        jc = lax.broadcasted_iota(jnp.int32, (rows, win), 1)
        lc = lax.broadcasted_iota(jnp.int32, (rows, win), 0)
        valid_c = (jc >= lc) & (((jc - lc) & (dil - 1)) == 0)
        valid_n = (jn <= ln) & (((ln - jn) & (dil - 1)) == 0) & (jn < n_new)
        o_h, lse_h = [], []
        for h in range(SWA_HPG):
            col = g * SWA_OUT + h * SWA_HEAD_DIM
            hsl = slice(col, col + SWA_HEAD_DIM)
            qh = q_ref[0, :, hsl].astype(BF16)
            knh = kn_ref[0, :, hsl].astype(BF16)
            vnh = vn_ref[0, :, hsl].astype(BF16)
            for (src, dst, new_t) in ((in_refs[g][0], out_refs[g][0], knt_ref), (in_refs[g][1], out_refs[g][1], vnt_ref)):
                old = src[0, h]
                moved = pltpu.roll(old, win - n_new, 1)
                if win > LANES:
                    dst[0, h, :, 0:win - LANES] = moved[:, 0:win - LANES]
                dst[0, h, :, win - LANES:win] = jnp.where(tail, new_t[0, hsl, :], moved[:, win - LANES:win])
            kt = in_refs[g][0][0, h].astype(BF16)
            vt = in_refs[g][1][0, h].astype(BF16)
            s_c = jnp.where(valid_c, _dot(qh, kt) * scale, NEG)
            s_n = jnp.where(valid_n, _dot_nt(qh, knh) * scale, NEG)
            m = jnp.maximum(jnp.max(s_c, axis=-1, keepdims=True), jnp.max(s_n, axis=-1, keepdims=True))
            p_c = jnp.exp(s_c - m)
            p_n = jnp.exp(s_n - m)
            den = jnp.sum(p_c, axis=-1, keepdims=True) + jnp.sum(p_n, axis=-1, keepdims=True)
            o_h.append((_dot_nt(p_c.astype(BF16), vt) + _dot(p_n.astype(BF16), vnh)) / den)
            lse_h.append(jnp.broadcast_to(m + jnp.log(den), (rows, SWA_HEAD_DIM)))
        o_g.append(jnp.concatenate(o_h, axis=1))
        lse_g.append(jnp.concatenate(lse_h, axis=1))
    lmax = jnp.maximum(jnp.maximum(lse_g[0], lse_g[1]), lse_g[2])
    e = [jnp.exp(x - lmax) for x in lse_g]
    ob_ref[0] = ((e[0] * o_g[0] + e[1] * o_g[1] + e[2] * o_g[2]) / (e[0] + e[1] + e[2])).astype(BF16)


def _swa_sample(q, kn, vn, knt, vnt, caches_t, n_new):
    bd = q.shape[0]
    rows = _SAMPLE_ROWS
    tok = pl.BlockSpec((1, rows, SWA_WIDTH), lambda bi: (bi, 0, 0))
    new_t = pl.BlockSpec((1, SWA_WIDTH, LANES), lambda bi: (bi, 0, 0))
    specs, args = [tok, tok, tok, new_t, new_t], [q, kn, vn, knt, vnt]
    out_specs = [pl.BlockSpec((1, rows, SWA_OUT), lambda bi: (bi, 0, 0))]
    out_shape = [jax.ShapeDtypeStruct((bd, rows, SWA_OUT), BF16)]
    for g, (win, dil) in enumerate(SWA_GROUPS):
        for t in caches_t[g]:
            assert t.shape == (bd, SWA_HPG, SWA_HEAD_DIM, win) and win == SWA_BLOCK * dil and win % LANES == 0
            spec = pl.BlockSpec((1, SWA_HPG, SWA_HEAD_DIM, win), lambda bi: (bi, 0, 0, 0))
            args.append(t)
            specs.append(spec)
            out_specs.append(spec)
            out_shape.append(jax.ShapeDtypeStruct(t.shape, t.dtype))
    return pl.pallas_call(
        functools.partial(_swa_sample_kernel, n_new=n_new),
        grid=(bd,),
        in_specs=specs,
        out_specs=out_specs,
        out_shape=out_shape,
        compiler_params=_params(("arbitrary",)),
        name="swa_sample",
    )(*args)


def _merge_kernel(*refs, combine, n_alias, n_real):
    x1_ref, h2c_ref, lg_ref = refs[-3:]
    refs = refs[:len(refs) - 3 - n_alias]

    @pl.when(pl.program_id(0) >= n_real)
    def _():
        h2c_ref[...] = jnp.zeros_like(h2c_ref)
        lg_ref[...] = jnp.zeros_like(lg_ref)

    pl.when(pl.program_id(0) < n_real)(functools.partial(_merge_tile, refs, x1_ref, h2c_ref, lg_ref, combine))


def _merge_tile(refs, x1_ref, h2c_ref, lg_ref, combine):
    if combine:
        (x_ref, ya_ref, o0, o1, o2, l0, l1, l2, sgab_ref, wa_ref, wb_ref, wo_ref, ln_ref, wr_hi_ref, wr_lo_ref,
         br_ref) = refs
        lmax = jnp.maximum(jnp.maximum(l0[...], l1[...]), l2[...])
        e0, e1, e2 = jnp.exp(l0[...] - lmax), jnp.exp(l1[...] - lmax), jnp.exp(l2[...] - lmax)
        ob = ((e0 * o0[...] + e1 * o1[...] + e2 * o2[...]) / (e0 + e1 + e2)).astype(BF16)
    else:
        (x_ref, ya_ref, ob_ref, sgab_ref, wa_ref, wb_ref, wo_ref, ln_ref, wr_hi_ref, wr_lo_ref, br_ref) = refs
        ob = ob_ref[...]
    ya = _dot(ya_ref[...], wa_ref[...])
    yb = _dot(ob, wb_ref[...])
    sga = sgab_ref[:, :D_MODEL].astype(F32)
    sgb = sgab_ref[:, D_MODEL:].astype(F32)
    x1 = x_ref[...] + _dot((sga * ya + sgb * yb).astype(BF16), wo_ref[...])
    x1_ref[...] = x1
    h2 = x1 * lax.rsqrt(jnp.mean(x1 * x1, axis=-1, keepdims=True) + EPS) * ln_ref[...]
    h_hi, h_lo = _split_bf16(h2)
    lg_ref[...] = _dot(h_hi, wr_hi_ref[...]) + _dot(h_hi, wr_lo_ref[...]) + _dot(h_lo, wr_hi_ref[...]) + br_ref[...]
    nsub = D_MODEL // LANES
    for c in range(nsub):
        h2c_ref[pl.ds(c, x1.shape[0], stride=nsub), :] = h2[:, c * LANES:(c + 1) * LANES]


def _merge(x, ya_in, swa, sgab, weights, tm, tok_off, t_all, shared=None):
    t = x.shape[0]
    assert t % tm == 0 and tok_off % tm == 0 and MERGE_TILE % tm == 0
    combine = len(swa) > 1
    blk_off = tok_off // tm
    t_buf = -(-t_all // MERGE_TILE) * MERGE_TILE
    nsub = D_MODEL // LANES
    n_real = t // tm
    n_fill = 0 if shared is not None else (t_buf - tok_off - t) // tm
    row = lambda width: pl.BlockSpec((tm, width), lambda i: (jnp.minimum(i, n_real - 1), 0))
    shared_in = [] if shared is None else list(shared)
    n_in = 3 + len(swa) + len(weights)
    return pl.pallas_call(
        functools.partial(_merge_kernel, combine=combine, n_alias=len(shared_in), n_real=n_real),
        grid=(n_real + n_fill,),
        in_specs=[row(D_MODEL), row(D_MODEL)] + [row(SWA_OUT)] * len(swa) + [row(2 * D_MODEL)]
                 + [_resident(w.shape) for w in weights] + [pl.BlockSpec(memory_space=pl.ANY)] * len(shared_in),
        out_specs=[row(D_MODEL),
                   pl.BlockSpec((tm * nsub, LANES), lambda i: (i + blk_off, 0)),
                   pl.BlockSpec((tm, LANES), lambda i: (i + blk_off, 0))],
        out_shape=[jax.ShapeDtypeStruct((t, D_MODEL), F32),
                   jax.ShapeDtypeStruct((t_buf * nsub, LANES), F32),
                   jax.ShapeDtypeStruct((t_buf, LANES), F32)],
        input_output_aliases={n_in + k: 1 + k for k in range(len(shared_in))},
        compiler_params=_params(("arbitrary",)),
        name="merge",
    )(x, ya_in, *swa, sgab, *weights, *shared_in)


def _router_kernel(lg_ref, mi_ref, mw_ref, cnt_ref, carry):
    phase = pl.program_id(0)
    i = pl.program_id(1)

    @pl.when((phase == 0) & (i == 0))
    def _():
        carry[...] = jnp.zeros_like(carry)

    @pl.when((phase == 1) & (i == 0))
    def _():
        cnt = carry[...]
        cnt_ref[...] = jnp.broadcast_to(cnt, cnt_ref.shape)
        tiles = jnp.floor((cnt + (EXPERT_TILE - 1)) * (1.0 / EXPERT_TILE))
        r = lax.broadcasted_iota(jnp.int32, (LANES, LANES), 0)
        c = lax.broadcasted_iota(jnp.int32, (LANES, LANES), 1)
        before = _dot(jnp.broadcast_to(tiles, (SUBLANES, LANES)).astype(BF16), (r < c).astype(BF16))
        carry[...] = before[0:1, :] * EXPERT_TILE

    lg = lg_ref[...]
    tr = lg.shape[0]
    lane = lax.broadcasted_iota(jnp.int32, lg.shape, 1)
    big = jnp.int32(LANES)
    gl = jnp.where(lane < N_GROUPS, lg, NEG)
    gmax = jnp.max(gl, axis=-1, keepdims=True)
    g_idx = jnp.min(jnp.where(gl == gmax, lane, big), axis=-1, keepdims=True)
    g_w = 1.0 / jnp.sum(jnp.exp(gl - gmax), axis=-1, keepdims=True)
    e_lane = lane - N_GROUPS
    in_group = (e_lane >= 0) & (e_lane < N_EXPERTS) & (e_lane // EXPERTS_PER_GROUP == g_idx)
    el = jnp.where(in_group, lg, NEG)
    v1 = jnp.max(el, axis=-1, keepdims=True)
    i1 = jnp.min(jnp.where(el == v1, lane, big), axis=-1, keepdims=True)
    el2 = jnp.where(lane == i1, NEG, el)
    v2 = jnp.max(el2, axis=-1, keepdims=True)
    i2 = jnp.min(jnp.where(el2 == v2, lane, big), axis=-1, keepdims=True)
    r21 = jnp.exp(v2 - v1)
    w1 = g_w / (1.0 + r21)
    w2 = g_w * r21 / (1.0 + r21)
    e1 = i1 - N_GROUPS
    e2 = i2 - N_GROUPS
    hot1 = lane == e1
    hot2 = lane == e2
    hot = (hot1 | hot2).astype(BF16)
    r = lax.broadcasted_iota(jnp.int32, (tr, tr), 0)
    c = lax.broadcasted_iota(jnp.int32, (tr, tr), 1)
    csum = _dot((r > c).astype(BF16), hot) + carry[...]
    rank1 = jnp.sum(jnp.where(hot1, csum, 0.0), axis=-1, keepdims=True).astype(jnp.int32)
    rank2 = jnp.sum(jnp.where(hot2, csum, 0.0), axis=-1, keepdims=True).astype(jnp.int32)
    carry[...] = carry[...] + jnp.sum(hot.astype(F32), axis=0, keepdims=True)
    mi_ref[...] = jnp.where(lane == 0, e1, jnp.where(lane == 1, e2, jnp.where(lane == 2, rank1, rank2)))
    mw_ref[...] = jnp.where(lane == 0, w1, w2)


def _router(logits, t, tr):
    assert t % tr == 0 and tr % SUBLANES == 0
    out_row = pl.BlockSpec((tr, LANES), lambda p, i: (i * p, 0))
    return pl.pallas_call(
        _router_kernel,
        grid=(2, t // tr),
        in_specs=[pl.BlockSpec((tr, LANES), lambda p, i: (i, 0))],
        out_specs=[out_row, out_row, pl.BlockSpec((SUBLANES, LANES), lambda p, i: (0, 0))],
        out_shape=[jax.ShapeDtypeStruct((t, LANES), jnp.int32), jax.ShapeDtypeStruct((t, LANES), F32),
                   jax.ShapeDtypeStruct((SUBLANES, LANES), F32)],
        scratch_shapes=[pltpu.VMEM((1, LANES), F32)],
        compiler_params=_params(("arbitrary", "arbitrary")),
        name="router",
    )(logits)


def _expert_kernel(te_ref, tv_ref, end_ref, cnt_ref, p0_ref, p1_ref, src_hbm, wg_ref, wu_ref, wd_ref,
                   out_ref, s2t, xs, sem, wgb, wub, wdb, *, n_tok):
    i = pl.program_id(0)
    n = pl.num_programs(0)
    tm = EXPERT_TILE
    nsub = D_MODEL // LANES
    slot = i % 2

    @pl.when(i == 0)
    def _():
        def clear_segment(sg, carry):
            end = end_ref[sg]
            cnt = cnt_ref[sg]

            def clear(j, c):
                s2t[j] = 0
                return c

            lax.fori_loop(end - (cnt + tm - 1) // tm * tm + cnt, end, clear, 0)
            return carry

        lax.fori_loop(0, end_ref.shape[0], clear_segment, 0)

        def place(t, carry):
            s2t[p0_ref[t]] = t * nsub
            s2t[p1_ref[t]] = t * nsub
            return carry

        lax.fori_loop(0, n_tok, place, 0, unroll=8)

    def fetch(step, sl):
        base = step * tm

        def issue(j, carry):
            row0 = pl.multiple_of(s2t[base + j], nsub)
            pltpu.make_async_copy(src_hbm.at[pl.ds(row0, nsub), :],
                                  xs.at[sl, pl.ds(pl.multiple_of(j * nsub, nsub), nsub), :], sem.at[sl]).start()
            return carry

        lax.fori_loop(0, tm, issue, 0, unroll=8)

    valid = tv_ref[i] != 0

    @pl.when((i == 0) & valid)
    def _():
        fetch(i, slot)

    @pl.when((i + 1 < n) & (tv_ref[jnp.minimum(i + 1, n - 1)] != 0))
    def _():
        fetch(i + 1, 1 - slot)

    @pl.when(jnp.logical_not(valid))
    def _():
        out_ref[...] = jnp.zeros_like(out_ref)

    @pl.when(valid & ((i == 0) | (te_ref[i] != te_ref[jnp.maximum(i - 1, 0)])))
    def _():
        wgb[...] = wg_ref[0, 0].astype(BF16)
        wub[...] = wu_ref[0, 0].astype(BF16)
        wdb[...] = wd_ref[0, 0].astype(BF16)

    @pl.when(valid)
    def _():
        pltpu.make_async_copy(src_hbm.at[pl.ds(0, tm * nsub), :], xs.at[slot], sem.at[slot]).wait()
        h = jnp.concatenate([xs[slot, pl.ds(c, tm, stride=nsub), :] for c in range(nsub)], axis=1).astype(BF16)
        gate = _dot(h, wgb[...])
        up = _dot(h, wub[...])
        a = (gate * _sigmoid(gate) * up).astype(BF16)
        o = _dot(a, wdb[...])
        for c in range(nsub):
            out_ref[pl.ds(c, tm, stride=nsub), :] = o[:, c * LANES:(c + 1) * LANES]


def _experts(tile_expert, tile_valid, seg_end, seg_cnt, pos0, pos1, src, wg, wu, wd):
    n_tiles = tile_expert.shape[0]
    tm = EXPERT_TILE
    n_tok = pos0.shape[0]
    nsub = D_MODEL // LANES
    wspec = lambda shape: pl.BlockSpec((1, 1) + shape, lambda i, te, *_: (0, te[i], 0, 0))
    grid_spec = pltpu.PrefetchScalarGridSpec(
        num_scalar_prefetch=6,
        grid=(n_tiles,),
        in_specs=[pl.BlockSpec(memory_space=pl.ANY),
                  wspec((D_MODEL, D_EXPERT)), wspec((D_MODEL, D_EXPERT)), wspec((D_EXPERT, D_MODEL))],
        out_specs=pl.BlockSpec((tm * nsub, LANES), lambda i, *_: (i, 0)),
        scratch_shapes=[pltpu.SMEM((n_tiles * tm,), jnp.int32),
                        pltpu.VMEM((2, tm * nsub, LANES), F32),
                        pltpu.SemaphoreType.DMA((2,)),
                        pltpu.VMEM((D_MODEL, D_EXPERT), BF16), pltpu.VMEM((D_MODEL, D_EXPERT), BF16),
                        pltpu.VMEM((D_EXPERT, D_MODEL), BF16)],
    )
    return pl.pallas_call(
        functools.partial(_expert_kernel, n_tok=n_tok),
        grid_spec=grid_spec,
        out_shape=jax.ShapeDtypeStruct((n_tiles * tm * nsub, LANES), F32),
        compiler_params=_params(("arbitrary",)),
        name="experts",
    )(tile_expert, tile_valid, seg_end, seg_cnt, pos0, pos1, src, wg, wu, wd)


def _combine_kernel(p0_ref, p1_ref, x1_ref, mw_ref, rows_hbm, y_ref, g0, g1, sem, *, tc, blk_off):
    i = pl.program_id(0)
    n = pl.num_programs(0)
    nout = D_MODEL // LANES
    slot = i % 2

    def issue_all(step, sl):
        base = (step + blk_off) * tc

        def issue(j, carry):
            for pos_ref, dst, s in ((p0_ref, g0, 0), (p1_ref, g1, 1)):
                src0 = pl.multiple_of(pos_ref[base + j] * nout, nout)
                pltpu.make_async_copy(rows_hbm.at[pl.ds(src0, nout), :],
                                      dst.at[sl, pl.ds(pl.multiple_of(j * nout, nout), nout), :],
                                      sem.at[sl, s]).start()
            return carry

        lax.fori_loop(0, tc, issue, 0, unroll=4)

    @pl.when(i == 0)
    def _():
        issue_all(i, slot)

    @pl.when(i + 1 < n)
    def _():
        issue_all(i + 1, 1 - slot)

    for dst, s in ((g0, 0), (g1, 1)):
        pltpu.make_async_copy(rows_hbm.at[pl.ds(0, tc * nout), :], dst.at[slot], sem.at[slot, s]).wait()
    w0 = mw_ref[:, 0:1]
    w1 = mw_ref[:, 1:2]
    for c in range(nout):
        sl = slice(c * LANES, (c + 1) * LANES)
        y_ref[:, sl] = (x1_ref[:, sl] + w0 * g0[slot, pl.ds(c, tc, stride=nout), :]
                        + w1 * g1[slot, pl.ds(c, tc, stride=nout), :])


def _combine(pos0, pos1, x1, meta_w, rows, tc, tok_off):
    t = x1.shape[0]
    assert t % tc == 0 and tok_off % tc == 0
    blk_off = tok_off // tc
    nout = D_MODEL // LANES
    row = pl.BlockSpec((tc, D_MODEL), lambda i, p0, p1: (i, 0))
    grid_spec = pltpu.PrefetchScalarGridSpec(
        num_scalar_prefetch=2,
        grid=(t // tc,),
        in_specs=[row, pl.BlockSpec((tc, LANES), lambda i, p0, p1: (i + blk_off, 0)),
                  pl.BlockSpec(memory_space=pl.ANY)],
        out_specs=row,
        scratch_shapes=[pltpu.VMEM((2, tc * nout, LANES), F32), pltpu.VMEM((2, tc * nout, LANES), F32),
                        pltpu.SemaphoreType.DMA((2, 2))],
    )
    return pl.pallas_call(
        functools.partial(_combine_kernel, tc=tc, blk_off=blk_off),
        grid_spec=grid_spec,
        out_shape=jax.ShapeDtypeStruct((t, D_MODEL), F32),
        compiler_params=_params(("arbitrary",)),
        name="combine",
    )(pos0, pos1, x1, meta_w, rows)


def _rope_tables(pos):
    half = SWA_HEAD_DIM // 2
    inv_freq = ROPE_THETA ** (-jnp.arange(half, dtype=F32) / half)
    ang = pos.astype(F32)[:, None] * inv_freq[None, :]
    cos, sin = jnp.cos(ang), jnp.sin(ang)
    reps = LANES // SWA_HEAD_DIM
    return jnp.tile(jnp.concatenate([cos, cos], axis=1), (1, reps)), jnp.tile(jnp.concatenate([-sin, sin], axis=1), (1, reps))


def kernel(x_prompt, x_sample, state_gla, cache_swa_k0, cache_swa_v0, cache_swa_k1, cache_swa_v1, cache_swa_k2, cache_swa_v2, ln1_w, w_in, w_gla_lr, b_gla_lr, gla_onorm_w, q_norm_w, k_norm_w, w_branch_a, w_branch_b, w_out, ln2_w, w_router_group, b_router_group, w_router_expert, b_router_expert, w_exp_gate, w_exp_up, w_exp_down):
    b, s, d = x_prompt.shape
    bd, ls, _ = x_sample.shape
    tp, ts = b * s, bd * ls
    assert w_in.shape[0] == 1 and d == D_MODEL and ts % SUBLANES == 0
    k_caches = (cache_swa_k0, cache_swa_k1, cache_swa_k2)
    v_caches = (cache_swa_v0, cache_swa_v1, cache_swa_v2)

    w = w_in[0]
    cuts = np.cumsum((512, 512, 1024, GLA_RANK, 1024, 768, 768, 768, 1024, 1024))
    sec = lambda a: w[:, (0 if a == 0 else cuts[a - 1]):cuts[a]]
    lr_pad = jnp.pad(sec(3), ((0, 0), (0, LANES - GLA_RANK)))
    w_packed = jnp.concatenate([sec(0), sec(1), sec(2), sec(4), sec(5), sec(6), sec(7), sec(8), sec(9), lr_pad],
                               axis=1).astype(BF16)
    wlr = jnp.pad(w_gla_lr[0], ((0, LANES - GLA_RANK), (0, 0))).astype(BF16)
    blr = b_gla_lr[0][None, :]
    nw = jnp.concatenate([jnp.tile(q_norm_w[0], SWA_WIDTH // SWA_HEAD_DIM), jnp.tile(k_norm_w[0], SWA_WIDTH // SWA_HEAD_DIM)])[None, :]
    gi = np.arange(256) // SWA_HEAD_DIM
    gmat = jnp.asarray((gi[:, None] == gi[None, :]).astype(np.float32) / SWA_HEAD_DIM, dtype=BF16)
    ln1 = ln1_w[0][None, :]
    ln2 = ln2_w[0][None, :]
    onw = gla_onorm_w[0][None, :]
    wa = w_branch_a[0].astype(BF16)
    wb = w_branch_b[0].astype(BF16)
    wo = w_out[0].astype(BF16)
    wr = jnp.pad(jnp.concatenate([w_router_group[0], w_router_expert[0]], axis=1),
                 ((0, 0), (0, LANES - N_GROUPS - N_EXPERTS)))
    wr_hi, wr_lo = _split_bf16(wr)
    br = jnp.pad(jnp.concatenate([b_router_group[0], b_router_expert[0]]), (0, LANES - N_GROUPS - N_EXPERTS))[None, :]

    cos_p, sin_p = _rope_tables(jnp.arange(s, dtype=jnp.int32))
    cos_s, sin_s = _rope_tables(PAST_LEN + jnp.arange(ts, dtype=jnp.int32) % ls)

    tm_p = 512
    proj_p = _proj(x_prompt.reshape(tp, d), cos_p, sin_p, s // tm_p, ln1, w_packed, wlr, blr, nw, gmat, tm_p)
    proj_s = _proj(x_sample.reshape(ts, d), cos_s, sin_s, 1, ln1, w_packed, wlr, blr, nw, gmat, ts)
    gqkv_p, la_p, sog_p, q_p, k_p, v_p, sgab_p = proj_p
    gqkv_s, la_s, sog_s, q_s, k_s, v_s, sgab_s = proj_s

    r3 = lambda t, nb: t.reshape(nb, t.shape[0] // nb, t.shape[1])
    ya_p, st_p = _gla(r3(gqkv_p, b), r3(la_p, b), r3(sog_p, b),
                      jnp.zeros((b, GLA_HEADS, GLA_DK, GLA_DV), F32), onw, GLA_CHUNK, 512)
    pad_s = lambda t: jnp.pad(r3(t, bd), ((0, 0), (0, _SAMPLE_ROWS - ls), (0, 0)))
    ya_s, st_s = _gla(pad_s(gqkv_s), pad_s(la_s), pad_s(sog_s), state_gla[0], onw, _SAMPLE_ROWS, _SAMPLE_ROWS)
    ya_s = ya_s[:, :ls].reshape(ts, d)

    q3, k3, v3 = r3(q_p, b), r3(k_p, b), r3(v_p, b)
    swa_p = [_swa_prompt(q3, k3, v3, g, dil) for g, (_, dil) in enumerate(SWA_GROUPS)]
    to_t = lambda c: jnp.transpose(c[0], (0, 2, 3, 1))
    caches_t = [(to_t(k_caches[g]), to_t(v_caches[g])) for g in range(len(SWA_GROUPS))]
    new_t = lambda t: jnp.pad(jnp.transpose(r3(t, bd), (0, 2, 1)), ((0, 0), (0, 0), (LANES - ls, 0)))
    ob_s, *new_caches = _swa_sample(pad_s(q_s), pad_s(k_s), pad_s(v_s), new_t(k_s), new_t(v_s), caches_t, ls)
    ob_s = ob_s[:, :ls].reshape(ts, SWA_OUT)

    t = tp + ts
    mw = (wa, wb, wo, ln2, wr_hi, wr_lo, br)
    x1_p, h2c, lg = _merge(x_prompt.reshape(tp, d), ya_p.reshape(tp, d),
                           [o for o, _ in swa_p] + [l for _, l in swa_p], sgab_p, mw, 512, 0, t)
    x1_s, h2c, lg = _merge(x_sample.reshape(ts, d), ya_s, [ob_s], sgab_s, mw, ts, tp, t, shared=(h2c, lg))

    meta_i, meta_w, cnt = _router(lg, t, t // 12)
    tm = EXPERT_TILE
    n_tiles = (2 * t) // tm + N_EXPERTS
    counts = cnt[0, :N_EXPERTS].astype(jnp.int32)
    ends = jnp.cumsum((counts + tm - 1) // tm * tm)
    pos0, pos1 = meta_i[:, 2], meta_i[:, 3]
    tile_start = jnp.arange(n_tiles, dtype=jnp.int32) * tm
    tile_valid = (tile_start < ends[-1]).astype(jnp.int32)
    last_slot = jnp.minimum(tile_start, ends[-1] - 1)
    tile_expert = jnp.minimum(jnp.sum((last_slot[:, None] >= ends[None, :]).astype(jnp.int32), axis=1), N_EXPERTS - 1)

    rows = _experts(tile_expert, tile_valid, ends, counts, pos0, pos1, h2c, w_exp_gate, w_exp_up, w_exp_down)
    y_p = _combine(pos0, pos1, x1_p, meta_w, rows, 256, 0)
    y_s = _combine(pos0, pos1, x1_s, meta_w, rows, ts, tp)

    heads = lambda a: a.reshape(1, a.shape[0], a.shape[1], SWA_HPG, SWA_HEAD_DIM)
    outs = [y_p.reshape(b, s, d), y_s.reshape(bd, ls, d), st_p[None].astype(x_prompt.dtype)]
    for g, (win, _) in enumerate(SWA_GROUPS):
        keep = min(win, s)
        gsl = slice(g * SWA_OUT, (g + 1) * SWA_OUT)
        outs += [heads(k3[:, s - keep:, gsl]), heads(v3[:, s - keep:, gsl])]
    outs.append(st_s[None].astype(state_gla.dtype))
    outs += [jnp.transpose(c, (0, 3, 1, 2))[None] for c in new_caches]
    return tuple(outs)
```

```python
import functools

import numpy as np
import jax
import jax.numpy as jnp
from jax import lax
from jax.experimental import pallas as pl
from jax.experimental.pallas import tpu as pltpu

F32 = jnp.float32
BF16 = jnp.bfloat16

D_MODEL = 1024
PAST_LEN = 16384
GLA_HEADS = 4
GLA_DK = 128
GLA_DV = 256
GLA_RANK = 16
GLA_TAU = 16.0
GLA_CHUNK = 64
SWA_GROUPS = ((128, 1), (512, 4), (2048, 16))
SWA_HPG = 4
SWA_HEAD_DIM = 64
SWA_WIDTH = 768
SWA_OUT = 256
SWA_BLOCK = 128
ROPE_THETA = 10000.0
N_GROUPS = 4
EXPERTS_PER_GROUP = 8
N_EXPERTS = 32
D_EXPERT = 512
EPS = 1e-6

LANES = 128
SUBLANES = 8
VMEM_LIMIT = 56 * 1024 * 1024
NEG = -1e30
MXU_WIDTH = 256
EXPERT_TILE = 256
MERGE_TILE = 512

_C_GQKV = (0, 2048)
_C_GOG = (2048, 3072)
_C_QK = (3072, 4608)
_C_V = (4608, 5376)
_C_GAB = (5376, 7424)
_C_LR = (7424, 7552)


def _dot(a, b):
    return jnp.dot(a, b, preferred_element_type=F32)


def _dot_nt(a, b):
    return lax.dot_general(a, b, (((1,), (1,)), ((), ())), preferred_element_type=F32)


def _dot_tn(a, b):
    return lax.dot_general(a, b, (((0,), (0,)), ((), ())), preferred_element_type=F32)


def _sigmoid(x):
    return 1.0 / (1.0 + jnp.exp(-x))


def _split_bf16(x):
    hi = x.astype(BF16)
    lo = (x - hi.astype(F32)).astype(BF16)
    return hi, lo


def _params(sem):
    return pltpu.CompilerParams(dimension_semantics=sem, vmem_limit_bytes=VMEM_LIMIT)


def _resident(shape):
    nd = len(shape)
    return pl.BlockSpec(shape, lambda *_: (0,) * nd, pipeline_mode=pl.Buffered(1))


def _proj_kernel(x_ref, cos_ref, sin_ref, ln_ref, w_ref, wlr_ref, blr_ref, nw_ref, g_ref,
                 gqkv_ref, la_ref, sog_ref, q_ref, k_ref, v_ref, sgab_ref):
    x = x_ref[...]
    h = (x * lax.rsqrt(jnp.mean(x * x, axis=-1, keepdims=True) + EPS) * ln_ref[...]).astype(BF16)
    gqkv_ref[...] = _dot(h, w_ref[:, _C_GQKV[0]:_C_GQKV[1]]).astype(BF16)
    og = _dot(h, w_ref[:, _C_GOG[0]:_C_GOG[1]])
    sog_ref[...] = (og * _sigmoid(og)).astype(BF16)
    lr = _dot(h, w_ref[:, _C_LR[0]:_C_LR[1]]).astype(BF16)
    z = _dot(lr, wlr_ref[...]) + blr_ref[...]
    la_ref[...] = (jnp.minimum(z, 0.0) - jnp.log(1.0 + jnp.exp(-jnp.abs(z)))) / GLA_TAU
    qk = _dot(h, w_ref[:, _C_QK[0]:_C_QK[1]])
    sq = (qk * qk).astype(BF16)
    ms = jnp.concatenate([_dot(sq[:, c * 256:(c + 1) * 256], g_ref[...]) for c in range(6)], axis=1)
    qn = qk * lax.rsqrt(ms + EPS) * nw_ref[...]
    width = 2 * SWA_WIDTH
    cos = jnp.tile(cos_ref[...], (1, width // LANES))
    sin = jnp.tile(sin_ref[...], (1, width // LANES))
    lane = lax.broadcasted_iota(jnp.int32, qn.shape, 1)
    half = SWA_HEAD_DIM // 2
    rot = jnp.where(lane % SWA_HEAD_DIM < half, pltpu.roll(qn, width - half, 1), pltpu.roll(qn, half, 1))
    qr = qn * cos + rot * sin
    q_ref[...] = qr[:, :SWA_WIDTH]
    k_ref[...] = qr[:, SWA_WIDTH:]
    v_ref[...] = _dot(h, w_ref[:, _C_V[0]:_C_V[1]])
    gab = _dot(h, w_ref[:, _C_GAB[0]:_C_GAB[1]])
    sgab_ref[...] = _sigmoid(gab).astype(BF16)


def _proj(x, cos, sin, rope_blocks, ln, w, wlr, blr, nw, g, tm):
    t = x.shape[0]
    assert t % tm == 0
    row = lambda width: pl.BlockSpec((tm, width), lambda i: (i, 0))
    outs = [(2048, BF16), (512, F32), (1024, BF16), (768, F32), (768, F32), (768, F32), (2048, BF16)]
    return pl.pallas_call(
        _proj_kernel,
        grid=(t // tm,),
        in_specs=[row(D_MODEL),
                  pl.BlockSpec((tm, LANES), lambda i: (i % rope_blocks, 0)),
                  pl.BlockSpec((tm, LANES), lambda i: (i % rope_blocks, 0)),
                  _resident(ln.shape), _resident(w.shape), _resident(wlr.shape), _resident(blr.shape),
                  _resident(nw.shape), _resident(g.shape)],
        out_specs=[row(wd) for wd, _ in outs],
        out_shape=[jax.ShapeDtypeStruct((t, wd), dt) for wd, dt in outs],
        compiler_params=_params(("arbitrary",)),
        name="proj",
    )(x, cos, sin, ln, w, wlr, blr, nw, g)


def _gla_kernel(gqkv_ref, la_ref, sog_ref, s0_ref, onw_ref, y_ref, sfin_ref, s_scr, *, chunk, n_chunks):
    j = pl.program_id(1)

    @pl.when(j == 0)
    def _():
        s_scr[...] = s0_ref[0]

    r = lax.broadcasted_iota(jnp.int32, (chunk, chunk), 0)
    c = lax.broadcasted_iota(jnp.int32, (chunk, chunk), 1)
    causal = r >= c
    tri = causal.astype(BF16)
    ones = jnp.ones((chunk, GLA_DV), BF16)
    hk = GLA_HEADS * GLA_DK

    def body(ci, carry):
        r0 = pl.multiple_of(ci * chunk, chunk)
        blk = gqkv_ref[0, pl.ds(r0, chunk), :]
        la_hi, la_lo = _split_bf16(la_ref[0, pl.ds(r0, chunk), :])
        b = _dot(tri, la_hi) + _dot(tri, la_lo)
        blast = b[chunk - 1:chunk, :]
        q = blk[:, :hk].astype(F32) * GLA_DK ** -0.5
        k = blk[:, hk:2 * hk].astype(F32)
        qd = (q * jnp.exp(b)).astype(BF16)
        kd = (k * jnp.exp(-b)).astype(BF16)
        kdec = (k * jnp.exp(blast - b)).astype(BF16)
        outs = []
        for h in range(GLA_HEADS):
            sl = slice(h * GLA_DK, (h + 1) * GLA_DK)
            v_h = blk[:, 2 * hk + h * GLA_DV:2 * hk + (h + 1) * GLA_DV]
            att = jnp.where(causal, _dot_nt(qd[:, sl], kd[:, sl]), 0.0).astype(BF16)
            s_prev = s_scr[h]
            o = _dot(att, v_h) + _dot(qd[:, sl], s_prev.astype(BF16))
            dsum = _dot_tn(la_hi[:, sl], ones) + _dot_tn(la_lo[:, sl], ones)
            s_scr[h] = s_prev * jnp.exp(dsum) + _dot_tn(kdec[:, sl], v_h)
            ms = jnp.mean(o * o, axis=-1, keepdims=True)
            outs.append(o * lax.rsqrt(ms + EPS) * onw_ref[...])
        o_all = jnp.concatenate(outs, axis=1) * sog_ref[0, pl.ds(r0, chunk), :].astype(F32)
        y_ref[0, pl.ds(r0, chunk), :] = o_all.astype(BF16)
        return carry

    lax.fori_loop(0, n_chunks, body, 0, unroll=min(n_chunks, 2))

    @pl.when(j == pl.num_programs(1) - 1)
    def _():
        sfin_ref[0] = s_scr[...]


def _gla(gqkv, la, sog, s0, onw, chunk, block):
    b, l, _ = gqkv.shape
    tok = lambda width: pl.BlockSpec((1, block, width), lambda bi, j: (bi, j, 0))
    st = pl.BlockSpec((1, GLA_HEADS, GLA_DK, GLA_DV), lambda bi, j: (bi, 0, 0, 0))
    return pl.pallas_call(
        functools.partial(_gla_kernel, chunk=chunk, n_chunks=block // chunk),
        grid=(b, l // block),
        in_specs=[tok(2048), tok(512), tok(1024), st, _resident(onw.shape)],
        out_specs=[tok(1024), st],
        out_shape=[jax.ShapeDtypeStruct((b, l, 1024), BF16),
                   jax.ShapeDtypeStruct((b, GLA_HEADS, GLA_DK, GLA_DV), F32)],
        scratch_shapes=[pltpu.VMEM((GLA_HEADS, GLA_DK, GLA_DV), F32)],
        compiler_params=_params(("arbitrary", "arbitrary")),
        name="gla",
    )(gqkv, la, sog, s0, onw)


def _band_heads(q, kw, vw, valid):
    outs, lses = [], []
    for h in range(SWA_HPG):
        sl = slice(h * SWA_HEAD_DIM, (h + 1) * SWA_HEAD_DIM)
        s = _dot_nt(q[:, sl], kw[:, sl]) * SWA_HEAD_DIM ** -0.5
        s = jnp.where(valid, s, NEG)
        m = jnp.max(s, axis=-1, keepdims=True)
        p = jnp.exp(s - m)
        l = jnp.sum(p, axis=-1, keepdims=True)
        outs.append(_dot(p.astype(BF16), vw[:, sl]) / l)
        lses.append(jnp.broadcast_to(m + jnp.log(l), (q.shape[0], SWA_HEAD_DIM)))
    return jnp.concatenate(outs, axis=1), jnp.concatenate(lses, axis=1)


_SWA_TOKENS = 2048


def _swa_kernel(q_ref, k_ref, v_ref, kp_ref, vp_ref, o_ref, lse_ref, *stage, dil):
    blk = SWA_BLOCK
    nsub = q_ref.shape[1] // (blk * dil)
    first = pl.program_id(1) == 0
    qi = lax.broadcasted_iota(jnp.int32, (blk, 2 * blk), 0)
    kj = lax.broadcasted_iota(jnp.int32, (blk, 2 * blk), 1)
    band = (kj >= qi) & (kj <= qi + blk)
    halves = SWA_OUT // LANES

    if dil > 1:
        ins = (q_ref, k_ref, v_ref, kp_ref, vp_ref)
        q_ref, k_ref, v_ref, kp_ref, vp_ref, o_st, lse_st = stage
        for src, dst in zip(ins, stage):
            for hf in range(halves):
                dst[hf] = src[0, :, hf * LANES:(hf + 1) * LANES]

    def rows(ref, start):
        if dil == 1:
            return ref[0, pl.ds(start, blk), :]
        return jnp.concatenate([ref[hf, pl.ds(start, blk, stride=dil), :] for hf in range(halves)], axis=1)

    def unit(u, carry):
        r = u // nsub
        j = u % nsub
        start = r + dil * blk * j
        inside = r + dil * blk * jnp.maximum(j - 1, 0)
        if dil == 1:
            r, start, inside = 0, pl.multiple_of(start, blk), pl.multiple_of(inside, blk)
        head = j == 0
        kprev = jnp.where(head, rows(kp_ref, r), rows(k_ref, inside))
        vprev = jnp.where(head, rows(vp_ref, r), rows(v_ref, inside))
        kw = jnp.concatenate([kprev, rows(k_ref, start)], axis=0).astype(BF16)
        vw = jnp.concatenate([vprev, rows(v_ref, start)], axis=0).astype(BF16)
        valid = band & (kj >= jnp.where(head & first, blk, 0))
        o, lse = _band_heads(rows(q_ref, start).astype(BF16), kw, vw, valid)
        if dil == 1:
            o_ref[0, pl.ds(start, blk), :] = o
            lse_ref[0, pl.ds(start, blk), :] = lse
        else:
            for hf in range(halves):
                o_st[hf, pl.ds(start, blk, stride=dil), :] = o[:, hf * LANES:(hf + 1) * LANES]
                lse_st[hf, pl.ds(start, blk, stride=dil), :] = lse[:, hf * LANES:(hf + 1) * LANES]
        return carry

    lax.fori_loop(0, dil * nsub, unit, 0)
    if dil > 1:
        for hf in range(halves):
            o_ref[0, :, hf * LANES:(hf + 1) * LANES] = o_st[hf]
            lse_ref[0, :, hf * LANES:(hf + 1) * LANES] = lse_st[hf]


def _swa_prompt(q, k, v, g, dil):
    b, s, _ = q.shape
    tb = _SWA_TOKENS
    back = SWA_BLOCK * dil
    assert s % tb == 0 and tb % back == 0
    cur = pl.BlockSpec((1, tb, SWA_OUT), lambda bi, i: (bi, i, g))
    prev = pl.BlockSpec((1, back, SWA_OUT), lambda bi, i: (bi, jnp.maximum(i * (tb // back) - 1, 0), g))
    out = pl.BlockSpec((1, tb, SWA_OUT), lambda bi, i: (bi, i, 0))
    halves = SWA_OUT // LANES
    stage = [pltpu.VMEM((halves, n, LANES), F32) for n in (tb, tb, tb, back, back, tb, tb)] if dil > 1 else []
    o, lse = pl.pallas_call(
        functools.partial(_swa_kernel, dil=dil),
        grid=(b, s // tb),
        in_specs=[cur, cur, cur, prev, prev],
        out_specs=[out, out],
        out_shape=[jax.ShapeDtypeStruct((b, s, SWA_OUT), F32)] * 2,
        scratch_shapes=stage,
        compiler_params=_params(("arbitrary", "arbitrary")),
        name=f"swa_prompt_g{g}",
    )(q, k, v, k, v)
    return o.reshape(b * s, SWA_OUT), lse.reshape(b * s, SWA_OUT)


_SAMPLE_ROWS = 16


def _swa_sample_kernel(q_ref, kn_ref, vn_ref, knt_ref, vnt_ref, k0_ref, v0_ref, k1_ref, v1_ref, k2_ref, v2_ref,
                       ob_ref, ok0_ref, ov0_ref, ok1_ref, ov1_ref, ok2_ref, ov2_ref, *, n_new):
    rows = _SAMPLE_ROWS
    in_refs = ((k0_ref, v0_ref), (k1_ref, v1_ref), (k2_ref, v2_ref))
    out_refs = ((ok0_ref, ov0_ref), (ok1_ref, ov1_ref), (ok2_ref, ov2_ref))
    scale = SWA_HEAD_DIM ** -0.5
    jn = lax.broadcasted_iota(jnp.int32, (rows, rows), 1)
    ln = lax.broadcasted_iota(jnp.int32, (rows, rows), 0)
    tail = lax.broadcasted_iota(jnp.int32, (SWA_HEAD_DIM, LANES), 1) >= LANES - n_new
    o_g, lse_g = [], []
    for g, (win, dil) in enumerate(SWA_GROUPS):
        jc = lax.broadcasted_iota(jnp.int32, (rows, win), 1)
        lc = lax.broadcasted_iota(jnp.int32, (rows, win), 0)
        valid_c = (jc >= lc) & (((jc - lc) & (dil - 1)) == 0)
        valid_n = (jn <= ln) & (((ln - jn) & (dil - 1)) == 0) & (jn < n_new)
        o_h, lse_h = [], []
        for h in range(SWA_HPG):
            col = g * SWA_OUT + h * SWA_HEAD_DIM
            hsl = slice(col, col + SWA_HEAD_DIM)
            qh = q_ref[0, :, hsl].astype(BF16)
            knh = kn_ref[0, :, hsl].astype(BF16)
            vnh = vn_ref[0, :, hsl].astype(BF16)
            for (src, dst, new_t) in ((in_refs[g][0], out_refs[g][0], knt_ref), (in_refs[g][1], out_refs[g][1], vnt_ref)):
                old = src[0, h]
                moved = pltpu.roll(old, win - n_new, 1)
                if win > LANES:
                    dst[0, h, :, 0:win - LANES] = moved[:, 0:win - LANES]
                dst[0, h, :, win - LANES:win] = jnp.where(tail, new_t[0, hsl, :], moved[:, win - LANES:win])
            kt = in_refs[g][0][0, h].astype(BF16)
            vt = in_refs[g][1][0, h].astype(BF16)
            s_c = jnp.where(valid_c, _dot(qh, kt) * scale, NEG)
            s_n = jnp.where(valid_n, _dot_nt(qh, knh) * scale, NEG)
            m = jnp.maximum(jnp.max(s_c, axis=-1, keepdims=True), jnp.max(s_n, axis=-1, keepdims=True))
            p_c = jnp.exp(s_c - m)
            p_n = jnp.exp(s_n - m)
            den = jnp.sum(p_c, axis=-1, keepdims=True) + jnp.sum(p_n, axis=-1, keepdims=True)
            o_h.append((_dot_nt(p_c.astype(BF16), vt) + _dot(p_n.astype(BF16), vnh)) / den)
            lse_h.append(jnp.broadcast_to(m + jnp.log(den), (rows, SWA_HEAD_DIM)))
        o_g.append(jnp.concatenate(o_h, axis=1))
        lse_g.append(jnp.concatenate(lse_h, axis=1))
    lmax = jnp.maximum(jnp.maximum(lse_g[0], lse_g[1]), lse_g[2])
    e = [jnp.exp(x - lmax) for x in lse_g]
    ob_ref[0] = ((e[0] * o_g[0] + e[1] * o_g[1] + e[2] * o_g[2]) / (e[0] + e[1] + e[2])).astype(BF16)


def _swa_sample(q, kn, vn, knt, vnt, caches_t, n_new):
    bd = q.shape[0]
    rows = _SAMPLE_ROWS
    tok = pl.BlockSpec((1, rows, SWA_WIDTH), lambda bi: (bi, 0, 0))
    new_t = pl.BlockSpec((1, SWA_WIDTH, LANES), lambda bi: (bi, 0, 0))
    specs, args = [tok, tok, tok, new_t, new_t], [q, kn, vn, knt, vnt]
    out_specs = [pl.BlockSpec((1, rows, SWA_OUT), lambda bi: (bi, 0, 0))]
    out_shape = [jax.ShapeDtypeStruct((bd, rows, SWA_OUT), BF16)]
    for g, (win, dil) in enumerate(SWA_GROUPS):
        for t in caches_t[g]:
            assert t.shape == (bd, SWA_HPG, SWA_HEAD_DIM, win) and win == SWA_BLOCK * dil and win % LANES == 0
            spec = pl.BlockSpec((1, SWA_HPG, SWA_HEAD_DIM, win), lambda bi: (bi, 0, 0, 0))
            args.append(t)
            specs.append(spec)
            out_specs.append(spec)
            out_shape.append(jax.ShapeDtypeStruct(t.shape, t.dtype))
    return pl.pallas_call(
        functools.partial(_swa_sample_kernel, n_new=n_new),
        grid=(bd,),
        in_specs=specs,
        out_specs=out_specs,
        out_shape=out_shape,
        compiler_params=_params(("arbitrary",)),
        name="swa_sample",
    )(*args)


def _merge_kernel(*refs, combine, n_alias, n_real):
    x1_ref, h2c_ref, lg_ref = refs[-3:]
    refs = refs[:len(refs) - 3 - n_alias]

    @pl.when(pl.program_id(0) >= n_real)
    def _():
        h2c_ref[...] = jnp.zeros_like(h2c_ref)
        lg_ref[...] = jnp.zeros_like(lg_ref)

    pl.when(pl.program_id(0) < n_real)(functools.partial(_merge_tile, refs, x1_ref, h2c_ref, lg_ref, combine))


def _merge_tile(refs, x1_ref, h2c_ref, lg_ref, combine):
    if combine:
        (x_ref, ya_ref, o0, o1, o2, l0, l1, l2, sgab_ref, wa_ref, wb_ref, wo_ref, ln_ref, wr_hi_ref, wr_lo_ref,
         br_ref) = refs
        lmax = jnp.maximum(jnp.maximum(l0[...], l1[...]), l2[...])
        e0, e1, e2 = jnp.exp(l0[...] - lmax), jnp.exp(l1[...] - lmax), jnp.exp(l2[...] - lmax)
        ob = ((e0 * o0[...] + e1 * o1[...] + e2 * o2[...]) / (e0 + e1 + e2)).astype(BF16)
    else:
        (x_ref, ya_ref, ob_ref, sgab_ref, wa_ref, wb_ref, wo_ref, ln_ref, wr_hi_ref, wr_lo_ref, br_ref) = refs
        ob = ob_ref[...]
    ya = _dot(ya_ref[...], wa_ref[...])
    yb = _dot(ob, wb_ref[...])
    sga = sgab_ref[:, :D_MODEL].astype(F32)
    sgb = sgab_ref[:, D_MODEL:].astype(F32)
    x1 = x_ref[...] + _dot((sga * ya + sgb * yb).astype(BF16), wo_ref[...])
    x1_ref[...] = x1
    h2 = x1 * lax.rsqrt(jnp.mean(x1 * x1, axis=-1, keepdims=True) + EPS) * ln_ref[...]
    h_hi, h_lo = _split_bf16(h2)
    lg_ref[...] = _dot(h_hi, wr_hi_ref[...]) + _dot(h_hi, wr_lo_ref[...]) + _dot(h_lo, wr_hi_ref[...]) + br_ref[...]
    nsub = D_MODEL // LANES
    for c in range(nsub):
        h2c_ref[pl.ds(c, x1.shape[0], stride=nsub), :] = h2[:, c * LANES:(c + 1) * LANES]


def _merge(x, ya_in, swa, sgab, weights, tm, tok_off, t_all, shared=None):
    t = x.shape[0]
    assert t % tm == 0 and tok_off % tm == 0 and MERGE_TILE % tm == 0
    combine = len(swa) > 1
    blk_off = tok_off // tm
    t_buf = -(-t_all // MERGE_TILE) * MERGE_TILE
    nsub = D_MODEL // LANES
    n_real = t // tm
    n_fill = 0 if shared is not None else (t_buf - tok_off - t) // tm
    row = lambda width: pl.BlockSpec((tm, width), lambda i: (jnp.minimum(i, n_real - 1), 0))
    shared_in = [] if shared is None else list(shared)
    n_in = 3 + len(swa) + len(weights)
    return pl.pallas_call(
        functools.partial(_merge_kernel, combine=combine, n_alias=len(shared_in), n_real=n_real),
        grid=(n_real + n_fill,),
        in_specs=[row(D_MODEL), row(D_MODEL)] + [row(SWA_OUT)] * len(swa) + [row(2 * D_MODEL)]
                 + [_resident(w.shape) for w in weights] + [pl.BlockSpec(memory_space=pl.ANY)] * len(shared_in),
        out_specs=[row(D_MODEL),
                   pl.BlockSpec((tm * nsub, LANES), lambda i: (i + blk_off, 0)),
                   pl.BlockSpec((tm, LANES), lambda i: (i + blk_off, 0))],
        out_shape=[jax.ShapeDtypeStruct((t, D_MODEL), F32),
                   jax.ShapeDtypeStruct((t_buf * nsub, LANES), F32),
                   jax.ShapeDtypeStruct((t_buf, LANES), F32)],
        input_output_aliases={n_in + k: 1 + k for k in range(len(shared_in))},
        compiler_params=_params(("arbitrary",)),
        name="merge",
    )(x, ya_in, *swa, sgab, *weights, *shared_in)


def _router_kernel(lg_ref, mi_ref, mw_ref, cnt_ref, carry):
    phase = pl.program_id(0)
    i = pl.program_id(1)

    @pl.when((phase == 0) & (i == 0))
    def _():
        carry[...] = jnp.zeros_like(carry)

    @pl.when((phase == 1) & (i == 0))
    def _():
        cnt = carry[...]
        cnt_ref[...] = jnp.broadcast_to(cnt, cnt_ref.shape)
        tiles = jnp.floor((cnt + (EXPERT_TILE - 1)) * (1.0 / EXPERT_TILE))
        r = lax.broadcasted_iota(jnp.int32, (LANES, LANES), 0)
        c = lax.broadcasted_iota(jnp.int32, (LANES, LANES), 1)
        before = _dot(jnp.broadcast_to(tiles, (SUBLANES, LANES)).astype(BF16), (r < c).astype(BF16))
        carry[...] = before[0:1, :] * EXPERT_TILE

    lg = lg_ref[...]
    tr = lg.shape[0]
    lane = lax.broadcasted_iota(jnp.int32, lg.shape, 1)
    big = jnp.int32(LANES)
    gl = jnp.where(lane < N_GROUPS, lg, NEG)
    gmax = jnp.max(gl, axis=-1, keepdims=True)
    g_idx = jnp.min(jnp.where(gl == gmax, lane, big), axis=-1, keepdims=True)
    g_w = 1.0 / jnp.sum(jnp.exp(gl - gmax), axis=-1, keepdims=True)
    e_lane = lane - N_GROUPS
    in_group = (e_lane >= 0) & (e_lane < N_EXPERTS) & (e_lane // EXPERTS_PER_GROUP == g_idx)
    el = jnp.where(in_group, lg, NEG)
    v1 = jnp.max(el, axis=-1, keepdims=True)
    i1 = jnp.min(jnp.where(el == v1, lane, big), axis=-1, keepdims=True)
    el2 = jnp.where(lane == i1, NEG, el)
    v2 = jnp.max(el2, axis=-1, keepdims=True)
    i2 = jnp.min(jnp.where(el2 == v2, lane, big), axis=-1, keepdims=True)
    r21 = jnp.exp(v2 - v1)
    w1 = g_w / (1.0 + r21)
    w2 = g_w * r21 / (1.0 + r21)
    e1 = i1 - N_GROUPS
    e2 = i2 - N_GROUPS
    hot1 = lane == e1
    hot2 = lane == e2
    hot = (hot1 | hot2).astype(BF16)
    r = lax.broadcasted_iota(jnp.int32, (tr, tr), 0)
    c = lax.broadcasted_iota(jnp.int32, (tr, tr), 1)
    csum = _dot((r > c).astype(BF16), hot) + carry[...]
    rank1 = jnp.sum(jnp.where(hot1, csum, 0.0), axis=-1, keepdims=True).astype(jnp.int32)
    rank2 = jnp.sum(jnp.where(hot2, csum, 0.0), axis=-1, keepdims=True).astype(jnp.int32)
    carry[...] = carry[...] + jnp.sum(hot.astype(F32), axis=0, keepdims=True)
    mi_ref[...] = jnp.where(lane == 0, e1, jnp.where(lane == 1, e2, jnp.where(lane == 2, rank1, rank2)))
    mw_ref[...] = jnp.where(lane == 0, w1, w2)


def _router(logits, t, tr):
    assert t % tr == 0 and tr % SUBLANES == 0
    out_row = pl.BlockSpec((tr, LANES), lambda p, i: (i * p, 0))
    return pl.pallas_call(
        _router_kernel,
        grid=(2, t // tr),
        in_specs=[pl.BlockSpec((tr, LANES), lambda p, i: (i, 0))],
        out_specs=[out_row, out_row, pl.BlockSpec((SUBLANES, LANES), lambda p, i: (0, 0))],
        out_shape=[jax.ShapeDtypeStruct((t, LANES), jnp.int32), jax.ShapeDtypeStruct((t, LANES), F32),
                   jax.ShapeDtypeStruct((SUBLANES, LANES), F32)],
        scratch_shapes=[pltpu.VMEM((1, LANES), F32)],
        compiler_params=_params(("arbitrary", "arbitrary")),
        name="router",
    )(logits)


def _expert_kernel(te_ref, tv_ref, end_ref, cnt_ref, p0_ref, p1_ref, src_hbm, wg_ref, wu_ref, wd_ref,
                   out_ref, s2t, xs, sem, wgb, wub, wdb, *, n_tok):
    i = pl.program_id(0)
    n = pl.num_programs(0)
    tm = EXPERT_TILE
    nsub = D_MODEL // LANES
    slot = i % 2

    @pl.when(i == 0)
    def _():
        def clear_segment(sg, carry):
            end = end_ref[sg]
            cnt = cnt_ref[sg]

            def clear(j, c):
                s2t[j] = 0
                return c

            lax.fori_loop(end - (cnt + tm - 1) // tm * tm + cnt, end, clear, 0)
            return carry

        lax.fori_loop(0, end_ref.shape[0], clear_segment, 0)

        def place(t, carry):
            s2t[p0_ref[t]] = t * nsub
            s2t[p1_ref[t]] = t * nsub
            return carry

        lax.fori_loop(0, n_tok, place, 0, unroll=8)

    def start_row(base, j, sl):
        row0 = pl.multiple_of(s2t[base + j], nsub)
        dst0 = j * nsub if isinstance(j, int) else pl.multiple_of(j * nsub, nsub)
        pltpu.make_async_copy(src_hbm.at[pl.ds(row0, nsub), :], xs.at[sl, pl.ds(dst0, nsub), :], sem.at[sl]).start()

    def start_rows_loop(base, sl):
        def issue(j, carry):
            start_row(base, j, sl)
            return carry

        lax.fori_loop(0, tm, issue, 0, unroll=8)

    def wait_rows(sl):
        pltpu.make_async_copy(src_hbm.at[pl.ds(0, tm * nsub), :], xs.at[sl], sem.at[sl]).wait()

    valid = tv_ref[i] != 0
    nxt = jnp.minimum(i + 1, n - 1)
    next_base = jnp.where(tv_ref[nxt] != 0, nxt, 0) * tm

    @pl.when(i == 0)
    def _():
        start_rows_loop(0, slot)

    wait_rows(slot)

    @pl.when(jnp.logical_not(valid))
    def _():
        start_rows_loop(next_base, 1 - slot)
        out_ref[...] = jnp.zeros_like(out_ref)

    @pl.when(valid & ((i == 0) | (te_ref[i] != te_ref[jnp.maximum(i - 1, 0)])))
    def _():
        wgb[...] = wg_ref[0, 0].astype(BF16)
        wub[...] = wu_ref[0, 0].astype(BF16)
        wdb[...] = wd_ref[0, 0].astype(BF16)

    @pl.when(valid)
    def _():
        pieces = 2 * (D_EXPERT // MXU_WIDTH) + D_MODEL // MXU_WIDTH
        per_piece = tm // pieces
        issued = [0]

        def issue_some():
            for j in range(issued[0], issued[0] + per_piece):
                start_row(next_base, j, 1 - slot)
            issued[0] += per_piece

        h = jnp.concatenate([xs[slot, pl.ds(c, tm, stride=nsub), :] for c in range(nsub)], axis=1).astype(BF16)
        acts = []
        for q in range(D_EXPERT // MXU_WIDTH):
            cols = slice(q * MXU_WIDTH, (q + 1) * MXU_WIDTH)
            issue_some()
            gate = _dot(h, wgb[:, cols])
            issue_some()
            up = _dot(h, wub[:, cols])
            acts.append((gate * _sigmoid(gate) * up).astype(BF16))
        a = jnp.concatenate(acts, axis=1)
        for q in range(D_MODEL // MXU_WIDTH):
            issue_some()
            o = _dot(a, wdb[:, q * MXU_WIDTH:(q + 1) * MXU_WIDTH])
            for c in range(MXU_WIDTH // LANES):
                out_ref[pl.ds(q * (MXU_WIDTH // LANES) + c, tm, stride=nsub), :] = o[:, c * LANES:(c + 1) * LANES]
        assert issued[0] == tm

    @pl.when(i == n - 1)
    def _():
        wait_rows(1 - slot)


def _experts(tile_expert, tile_valid, seg_end, seg_cnt, pos0, pos1, src, wg, wu, wd):
    n_tiles = tile_expert.shape[0]
    tm = EXPERT_TILE
    n_tok = pos0.shape[0]
    nsub = D_MODEL // LANES
    wspec = lambda shape: pl.BlockSpec((1, 1) + shape, lambda i, te, *_: (0, te[i], 0, 0))
    grid_spec = pltpu.PrefetchScalarGridSpec(
        num_scalar_prefetch=6,
        grid=(n_tiles,),
        in_specs=[pl.BlockSpec(memory_space=pl.ANY),
                  wspec((D_MODEL, D_EXPERT)), wspec((D_MODEL, D_EXPERT)), wspec((D_EXPERT, D_MODEL))],
        out_specs=pl.BlockSpec((tm * nsub, LANES), lambda i, *_: (i, 0)),
        scratch_shapes=[pltpu.SMEM((n_tiles * tm,), jnp.int32),
                        pltpu.VMEM((2, tm * nsub, LANES), F32),
                        pltpu.SemaphoreType.DMA((2,)),
                        pltpu.VMEM((D_MODEL, D_EXPERT), BF16), pltpu.VMEM((D_MODEL, D_EXPERT), BF16),
                        pltpu.VMEM((D_EXPERT, D_MODEL), BF16)],
    )
    return pl.pallas_call(
        functools.partial(_expert_kernel, n_tok=n_tok),
        grid_spec=grid_spec,
        out_shape=jax.ShapeDtypeStruct((n_tiles * tm * nsub, LANES), F32),
        compiler_params=_params(("arbitrary",)),
        name="experts",
    )(tile_expert, tile_valid, seg_end, seg_cnt, pos0, pos1, src, wg, wu, wd)


def _combine_kernel(p0_ref, p1_ref, x1_ref, mw_ref, rows_hbm, y_ref, g0, g1, sem, *, tc, blk_off):
    i = pl.program_id(0)
    n = pl.num_programs(0)
    nout = D_MODEL // LANES
    slot = i % 2

    def issue_all(step, sl):
        base = (step + blk_off) * tc

        def issue(j, carry):
            for pos_ref, dst, s in ((p0_ref, g0, 0), (p1_ref, g1, 1)):
                src0 = pl.multiple_of(pos_ref[base + j] * nout, nout)
                pltpu.make_async_copy(rows_hbm.at[pl.ds(src0, nout), :],
                                      dst.at[sl, pl.ds(pl.multiple_of(j * nout, nout), nout), :],
                                      sem.at[sl, s]).start()
            return carry

        lax.fori_loop(0, tc, issue, 0, unroll=4)

    @pl.when(i == 0)
    def _():
        issue_all(i, slot)

    @pl.when(i + 1 < n)
    def _():
        issue_all(i + 1, 1 - slot)

    for dst, s in ((g0, 0), (g1, 1)):
        pltpu.make_async_copy(rows_hbm.at[pl.ds(0, tc * nout), :], dst.at[slot], sem.at[slot, s]).wait()
    w0 = mw_ref[:, 0:1]
    w1 = mw_ref[:, 1:2]
    for c in range(nout):
        sl = slice(c * LANES, (c + 1) * LANES)
        y_ref[:, sl] = (x1_ref[:, sl] + w0 * g0[slot, pl.ds(c, tc, stride=nout), :]
                        + w1 * g1[slot, pl.ds(c, tc, stride=nout), :])


def _combine(pos0, pos1, x1, meta_w, rows, tc, tok_off):
    t = x1.shape[0]
    assert t % tc == 0 and tok_off % tc == 0
    blk_off = tok_off // tc
    nout = D_MODEL // LANES
    row = pl.BlockSpec((tc, D_MODEL), lambda i, p0, p1: (i, 0))
    grid_spec = pltpu.PrefetchScalarGridSpec(
        num_scalar_prefetch=2,
        grid=(t // tc,),
        in_specs=[row, pl.BlockSpec((tc, LANES), lambda i, p0, p1: (i + blk_off, 0)),
                  pl.BlockSpec(memory_space=pl.ANY)],
        out_specs=row,
        scratch_shapes=[pltpu.VMEM((2, tc * nout, LANES), F32), pltpu.VMEM((2, tc * nout, LANES), F32),
                        pltpu.SemaphoreType.DMA((2, 2))],
    )
    return pl.pallas_call(
        functools.partial(_combine_kernel, tc=tc, blk_off=blk_off),
        grid_spec=grid_spec,
        out_shape=jax.ShapeDtypeStruct((t, D_MODEL), F32),
        compiler_params=_params(("arbitrary",)),
        name="combine",
    )(pos0, pos1, x1, meta_w, rows)


def _rope_tables(pos):
    half = SWA_HEAD_DIM // 2
    inv_freq = ROPE_THETA ** (-jnp.arange(half, dtype=F32) / half)
    ang = pos.astype(F32)[:, None] * inv_freq[None, :]
    cos, sin = jnp.cos(ang), jnp.sin(ang)
    reps = LANES // SWA_HEAD_DIM
    return jnp.tile(jnp.concatenate([cos, cos], axis=1), (1, reps)), jnp.tile(jnp.concatenate([-sin, sin], axis=1), (1, reps))


def kernel(x_prompt, x_sample, state_gla, cache_swa_k0, cache_swa_v0, cache_swa_k1, cache_swa_v1, cache_swa_k2, cache_swa_v2, ln1_w, w_in, w_gla_lr, b_gla_lr, gla_onorm_w, q_norm_w, k_norm_w, w_branch_a, w_branch_b, w_out, ln2_w, w_router_group, b_router_group, w_router_expert, b_router_expert, w_exp_gate, w_exp_up, w_exp_down):
    b, s, d = x_prompt.shape
    bd, ls, _ = x_sample.shape
    tp, ts = b * s, bd * ls
    assert w_in.shape[0] == 1 and d == D_MODEL and ts % SUBLANES == 0
    k_caches = (cache_swa_k0, cache_swa_k1, cache_swa_k2)
    v_caches = (cache_swa_v0, cache_swa_v1, cache_swa_v2)

    w = w_in[0]
    cuts = np.cumsum((512, 512, 1024, GLA_RANK, 1024, 768, 768, 768, 1024, 1024))
    sec = lambda a: w[:, (0 if a == 0 else cuts[a - 1]):cuts[a]]
    lr_pad = jnp.pad(sec(3), ((0, 0), (0, LANES - GLA_RANK)))
    w_packed = jnp.concatenate([sec(0), sec(1), sec(2), sec(4), sec(5), sec(6), sec(7), sec(8), sec(9), lr_pad],
                               axis=1).astype(BF16)
    wlr = jnp.pad(w_gla_lr[0], ((0, LANES - GLA_RANK), (0, 0))).astype(BF16)
    blr = b_gla_lr[0][None, :]
    nw = jnp.concatenate([jnp.tile(q_norm_w[0], SWA_WIDTH // SWA_HEAD_DIM), jnp.tile(k_norm_w[0], SWA_WIDTH // SWA_HEAD_DIM)])[None, :]
    gi = np.arange(256) // SWA_HEAD_DIM
    gmat = jnp.asarray((gi[:, None] == gi[None, :]).astype(np.float32) / SWA_HEAD_DIM, dtype=BF16)
    ln1 = ln1_w[0][None, :]
    ln2 = ln2_w[0][None, :]
    onw = gla_onorm_w[0][None, :]
    wa = w_branch_a[0].astype(BF16)
    wb = w_branch_b[0].astype(BF16)
    wo = w_out[0].astype(BF16)
    wr = jnp.pad(jnp.concatenate([w_router_group[0], w_router_expert[0]], axis=1),
                 ((0, 0), (0, LANES - N_GROUPS - N_EXPERTS)))
    wr_hi, wr_lo = _split_bf16(wr)
    br = jnp.pad(jnp.concatenate([b_router_group[0], b_router_expert[0]]), (0, LANES - N_GROUPS - N_EXPERTS))[None, :]

    cos_p, sin_p = _rope_tables(jnp.arange(s, dtype=jnp.int32))
    cos_s, sin_s = _rope_tables(PAST_LEN + jnp.arange(ts, dtype=jnp.int32) % ls)

    tm_p = 512
    proj_p = _proj(x_prompt.reshape(tp, d), cos_p, sin_p, s // tm_p, ln1, w_packed, wlr, blr, nw, gmat, tm_p)
    proj_s = _proj(x_sample.reshape(ts, d), cos_s, sin_s, 1, ln1, w_packed, wlr, blr, nw, gmat, ts)
    gqkv_p, la_p, sog_p, q_p, k_p, v_p, sgab_p = proj_p
    gqkv_s, la_s, sog_s, q_s, k_s, v_s, sgab_s = proj_s

    r3 = lambda t, nb: t.reshape(nb, t.shape[0] // nb, t.shape[1])
    ya_p, st_p = _gla(r3(gqkv_p, b), r3(la_p, b), r3(sog_p, b),
                      jnp.zeros((b, GLA_HEADS, GLA_DK, GLA_DV), F32), onw, GLA_CHUNK, 512)
    pad_s = lambda t: jnp.pad(r3(t, bd), ((0, 0), (0, _SAMPLE_ROWS - ls), (0, 0)))
    ya_s, st_s = _gla(pad_s(gqkv_s), pad_s(la_s), pad_s(sog_s), state_gla[0], onw, _SAMPLE_ROWS, _SAMPLE_ROWS)
    ya_s = ya_s[:, :ls].reshape(ts, d)

    q3, k3, v3 = r3(q_p, b), r3(k_p, b), r3(v_p, b)
    swa_p = [_swa_prompt(q3, k3, v3, g, dil) for g, (_, dil) in enumerate(SWA_GROUPS)]
    to_t = lambda c: jnp.transpose(c[0], (0, 2, 3, 1))
    caches_t = [(to_t(k_caches[g]), to_t(v_caches[g])) for g in range(len(SWA_GROUPS))]
    new_t = lambda t: jnp.pad(jnp.transpose(r3(t, bd), (0, 2, 1)), ((0, 0), (0, 0), (LANES - ls, 0)))
    ob_s, *new_caches = _swa_sample(pad_s(q_s), pad_s(k_s), pad_s(v_s), new_t(k_s), new_t(v_s), caches_t, ls)
    ob_s = ob_s[:, :ls].reshape(ts, SWA_OUT)

    t = tp + ts
    mw = (wa, wb, wo, ln2, wr_hi, wr_lo, br)
    x1_p, h2c, lg = _merge(x_prompt.reshape(tp, d), ya_p.reshape(tp, d),
                           [o for o, _ in swa_p] + [l for _, l in swa_p], sgab_p, mw, 512, 0, t)
    x1_s, h2c, lg = _merge(x_sample.reshape(ts, d), ya_s, [ob_s], sgab_s, mw, ts, tp, t, shared=(h2c, lg))

    meta_i, meta_w, cnt = _router(lg, t, t // 12)
    tm = EXPERT_TILE
    n_tiles = (2 * t) // tm + N_EXPERTS
    counts = cnt[0, :N_EXPERTS].astype(jnp.int32)
    ends = jnp.cumsum((counts + tm - 1) // tm * tm)
    pos0, pos1 = meta_i[:, 2], meta_i[:, 3]
    tile_start = jnp.arange(n_tiles, dtype=jnp.int32) * tm
    tile_valid = (tile_start < ends[-1]).astype(jnp.int32)
    last_slot = jnp.minimum(tile_start, ends[-1] - 1)
    tile_expert = jnp.minimum(jnp.sum((last_slot[:, None] >= ends[None, :]).astype(jnp.int32), axis=1), N_EXPERTS - 1)

    rows = _experts(tile_expert, tile_valid, ends, counts, pos0, pos1, h2c, w_exp_gate, w_exp_up, w_exp_down)
    y_p = _combine(pos0, pos1, x1_p, meta_w, rows, 256, 0)
    y_s = _combine(pos0, pos1, x1_s, meta_w, rows, ts, tp)

    heads = lambda a: a.reshape(1, a.shape[0], a.shape[1], SWA_HPG, SWA_HEAD_DIM)
    outs = [y_p.reshape(b, s, d), y_s.reshape(bd, ls, d), st_p[None].astype(x_prompt.dtype)]
    for g, (win, _) in enumerate(SWA_GROUPS):
        keep = min(win, s)
        gsl = slice(g * SWA_OUT, (g + 1) * SWA_OUT)
        outs += [heads(k3[:, s - keep:, gsl]), heads(v3[:, s - keep:, gsl])]
    outs.append(st_s[None].astype(state_gla.dtype))
    outs += [jnp.transpose(c, (0, 3, 1, 2))[None] for c in new_caches]
    return tuple(outs)
```

```python
import functools

import numpy as np
import jax
import jax.numpy as jnp
from jax import lax
from jax.experimental import pallas as pl
from jax.experimental.pallas import tpu as pltpu

F32 = jnp.float32
BF16 = jnp.bfloat16

D_MODEL = 1024
PAST_LEN = 16384
GLA_HEADS = 4
GLA_DK = 128
GLA_DV = 256
GLA_RANK = 16
GLA_TAU = 16.0
GLA_CHUNK = 64
SWA_GROUPS = ((128, 1), (512, 4), (2048, 16))
SWA_HPG = 4
SWA_HEAD_DIM = 64
SWA_WIDTH = 768
SWA_OUT = 256
SWA_BLOCK = 128
ROPE_THETA = 10000.0
N_GROUPS = 4
EXPERTS_PER_GROUP = 8
N_EXPERTS = 32
D_EXPERT = 512
EPS = 1e-6

LANES = 128
SUBLANES = 8
VMEM_LIMIT = 56 * 1024 * 1024
NEG = -1e30
MXU_WIDTH = 256
EXPERT_TILE = 256
MERGE_TILE = 512
DISPATCH_BLOCK = 384

_C_GQKV = (0, 2048)
_C_GOG = (2048, 3072)
_C_QK = (3072, 4608)
_C_V = (4608, 5376)
_C_GAB = (5376, 7424)
_C_LR = (7424, 7552)


def _dot(a, b):
    return jnp.dot(a, b, preferred_element_type=F32)


def _dot_nt(a, b):
    return lax.dot_general(a, b, (((1,), (1,)), ((), ())), preferred_element_type=F32)


def _dot_tn(a, b):
    return lax.dot_general(a, b, (((0,), (0,)), ((), ())), preferred_element_type=F32)


def _sigmoid(x):
    return 1.0 / (1.0 + jnp.exp(-x))


def _split_bf16(x):
    hi = x.astype(BF16)
    lo = (x - hi.astype(F32)).astype(BF16)
    return hi, lo


def _params(sem):
    return pltpu.CompilerParams(dimension_semantics=sem, vmem_limit_bytes=VMEM_LIMIT)


def _resident(shape):
    nd = len(shape)
    return pl.BlockSpec(shape, lambda *_: (0,) * nd, pipeline_mode=pl.Buffered(1))


def _proj_kernel(x_ref, cos_ref, sin_ref, ln_ref, w_ref, wlr_ref, blr_ref, nw_ref, g_ref,
                 gqkv_ref, la_ref, sog_ref, q_ref, k_ref, v_ref, sgab_ref):
    x = x_ref[...]
    h = (x * lax.rsqrt(jnp.mean(x * x, axis=-1, keepdims=True) + EPS) * ln_ref[...]).astype(BF16)
    gqkv_ref[...] = _dot(h, w_ref[:, _C_GQKV[0]:_C_GQKV[1]]).astype(BF16)
    og = _dot(h, w_ref[:, _C_GOG[0]:_C_GOG[1]])
    sog_ref[...] = (og * _sigmoid(og)).astype(BF16)
    lr = _dot(h, w_ref[:, _C_LR[0]:_C_LR[1]]).astype(BF16)
    z = _dot(lr, wlr_ref[...]) + blr_ref[...]
    la_ref[...] = (jnp.minimum(z, 0.0) - jnp.log(1.0 + jnp.exp(-jnp.abs(z)))) / GLA_TAU
    qk = _dot(h, w_ref[:, _C_QK[0]:_C_QK[1]])
    sq = (qk * qk).astype(BF16)
    ms = jnp.concatenate([_dot(sq[:, c * 256:(c + 1) * 256], g_ref[...]) for c in range(6)], axis=1)
    qn = qk * lax.rsqrt(ms + EPS) * nw_ref[...]
    width = 2 * SWA_WIDTH
    cos = jnp.tile(cos_ref[...], (1, width // LANES))
    sin = jnp.tile(sin_ref[...], (1, width // LANES))
    lane = lax.broadcasted_iota(jnp.int32, qn.shape, 1)
    half = SWA_HEAD_DIM // 2
    rot = jnp.where(lane % SWA_HEAD_DIM < half, pltpu.roll(qn, width - half, 1), pltpu.roll(qn, half, 1))
    qr = qn * cos + rot * sin
    q_ref[...] = qr[:, :SWA_WIDTH]
    k_ref[...] = qr[:, SWA_WIDTH:]
    v_ref[...] = _dot(h, w_ref[:, _C_V[0]:_C_V[1]])
    gab = _dot(h, w_ref[:, _C_GAB[0]:_C_GAB[1]])
    sgab_ref[...] = _sigmoid(gab).astype(BF16)


def _proj(x, cos, sin, rope_blocks, ln, w, wlr, blr, nw, g, tm):
    t = x.shape[0]
    assert t % tm == 0
    row = lambda width: pl.BlockSpec((tm, width), lambda i: (i, 0))
    outs = [(2048, BF16), (512, F32), (1024, BF16), (768, F32), (768, F32), (768, F32), (2048, BF16)]
    return pl.pallas_call(
        _proj_kernel,
        grid=(t // tm,),
        in_specs=[row(D_MODEL),
                  pl.BlockSpec((tm, LANES), lambda i: (i % rope_blocks, 0)),
                  pl.BlockSpec((tm, LANES), lambda i: (i % rope_blocks, 0)),
                  _resident(ln.shape), _resident(w.shape), _resident(wlr.shape), _resident(blr.shape),
                  _resident(nw.shape), _resident(g.shape)],
        out_specs=[row(wd) for wd, _ in outs],
        out_shape=[jax.ShapeDtypeStruct((t, wd), dt) for wd, dt in outs],
        compiler_params=_params(("arbitrary",)),
        name="proj",
    )(x, cos, sin, ln, w, wlr, blr, nw, g)


def _gla_kernel(gqkv_ref, la_ref, sog_ref, s0_ref, onw_ref, y_ref, sfin_ref, s_scr, *, chunk, n_chunks):
    j = pl.program_id(1)

    @pl.when(j == 0)
    def _():
        s_scr[...] = s0_ref[0]

    r = lax.broadcasted_iota(jnp.int32, (chunk, chunk), 0)
    c = lax.broadcasted_iota(jnp.int32, (chunk, chunk), 1)
    causal = r >= c
    tri = causal.astype(BF16)
    ones = jnp.ones((chunk, GLA_DV), BF16)
    hk = GLA_HEADS * GLA_DK

    def body(ci, carry):
        r0 = pl.multiple_of(ci * chunk, chunk)
        blk = gqkv_ref[0, pl.ds(r0, chunk), :]
        la_hi, la_lo = _split_bf16(la_ref[0, pl.ds(r0, chunk), :])
        b = _dot(tri, la_hi) + _dot(tri, la_lo)
        blast = b[chunk - 1:chunk, :]
        q = blk[:, :hk].astype(F32) * GLA_DK ** -0.5
        k = blk[:, hk:2 * hk].astype(F32)
        qd = (q * jnp.exp(b)).astype(BF16)
        kd = (k * jnp.exp(-b)).astype(BF16)
        kdec = (k * jnp.exp(blast - b)).astype(BF16)
        outs = []
        for h in range(GLA_HEADS):
            sl = slice(h * GLA_DK, (h + 1) * GLA_DK)
            v_h = blk[:, 2 * hk + h * GLA_DV:2 * hk + (h + 1) * GLA_DV]
            att = jnp.where(causal, _dot_nt(qd[:, sl], kd[:, sl]), 0.0).astype(BF16)
            s_prev = s_scr[h]
            o = _dot(att, v_h) + _dot(qd[:, sl], s_prev.astype(BF16))
            dsum = _dot_tn(la_hi[:, sl], ones) + _dot_tn(la_lo[:, sl], ones)
            s_scr[h] = s_prev * jnp.exp(dsum) + _dot_tn(kdec[:, sl], v_h)
            ms = jnp.mean(o * o, axis=-1, keepdims=True)
            outs.append(o * lax.rsqrt(ms + EPS) * onw_ref[...])
        o_all = jnp.concatenate(outs, axis=1) * sog_ref[0, pl.ds(r0, chunk), :].astype(F32)
        y_ref[0, pl.ds(r0, chunk), :] = o_all.astype(BF16)
        return carry

    lax.fori_loop(0, n_chunks, body, 0, unroll=min(n_chunks, 2))

    @pl.when(j == pl.num_programs(1) - 1)
    def _():
        sfin_ref[0] = s_scr[...]


def _gla(gqkv, la, sog, s0, onw, chunk, block):
    b, l, _ = gqkv.shape
    tok = lambda width: pl.BlockSpec((1, block, width), lambda bi, j: (bi, j, 0))
    st = pl.BlockSpec((1, GLA_HEADS, GLA_DK, GLA_DV), lambda bi, j: (bi, 0, 0, 0))
    return pl.pallas_call(
        functools.partial(_gla_kernel, chunk=chunk, n_chunks=block // chunk),
        grid=(b, l // block),
        in_specs=[tok(2048), tok(512), tok(1024), st, _resident(onw.shape)],
        out_specs=[tok(1024), st],
        out_shape=[jax.ShapeDtypeStruct((b, l, 1024), BF16),
                   jax.ShapeDtypeStruct((b, GLA_HEADS, GLA_DK, GLA_DV), F32)],
        scratch_shapes=[pltpu.VMEM((GLA_HEADS, GLA_DK, GLA_DV), F32)],
        compiler_params=_params(("arbitrary", "arbitrary")),
        name="gla",
    )(gqkv, la, sog, s0, onw)


def _band_heads(q, kw, vw, valid):
    outs, lses = [], []
    for h in range(SWA_HPG):
        sl = slice(h * SWA_HEAD_DIM, (h + 1) * SWA_HEAD_DIM)
        s = _dot_nt(q[:, sl], kw[:, sl]) * SWA_HEAD_DIM ** -0.5
        s = jnp.where(valid, s, NEG)
        m = jnp.max(s, axis=-1, keepdims=True)
        p = jnp.exp(s - m)
        l = jnp.sum(p, axis=-1, keepdims=True)
        outs.append(_dot(p.astype(BF16), vw[:, sl]) / l)
        lses.append(jnp.broadcast_to(m + jnp.log(l), (q.shape[0], SWA_HEAD_DIM)))
    return jnp.concatenate(outs, axis=1), jnp.concatenate(lses, axis=1)


_SWA_TOKENS = 2048


def _swa_kernel(q_ref, k_ref, v_ref, kp_ref, vp_ref, o_ref, lse_ref, *stage, dil):
    blk = SWA_BLOCK
    nsub = q_ref.shape[1] // (blk * dil)
    first = pl.program_id(1) == 0
    qi = lax.broadcasted_iota(jnp.int32, (blk, 2 * blk), 0)
    kj = lax.broadcasted_iota(jnp.int32, (blk, 2 * blk), 1)
    band = (kj >= qi) & (kj <= qi + blk)
    halves = SWA_OUT // LANES

    if dil > 1:
        ins = (q_ref, k_ref, v_ref, kp_ref, vp_ref)
        q_ref, k_ref, v_ref, kp_ref, vp_ref, o_st, lse_st = stage
        for src, dst in zip(ins, stage):
            for hf in range(halves):
                dst[hf] = src[0, :, hf * LANES:(hf + 1) * LANES]

    def rows(ref, start):
        if dil == 1:
            return ref[0, pl.ds(start, blk), :]
        return jnp.concatenate([ref[hf, pl.ds(start, blk, stride=dil), :] for hf in range(halves)], axis=1)

    def unit(u, carry):
        r = u // nsub
        j = u % nsub
        start = r + dil * blk * j
        inside = r + dil * blk * jnp.maximum(j - 1, 0)
        if dil == 1:
            r, start, inside = 0, pl.multiple_of(start, blk), pl.multiple_of(inside, blk)
        head = j == 0
        kprev = jnp.where(head, rows(kp_ref, r), rows(k_ref, inside))
        vprev = jnp.where(head, rows(vp_ref, r), rows(v_ref, inside))
        kw = jnp.concatenate([kprev, rows(k_ref, start)], axis=0).astype(BF16)
        vw = jnp.concatenate([vprev, rows(v_ref, start)], axis=0).astype(BF16)
        valid = band & (kj >= jnp.where(head & first, blk, 0))
        o, lse = _band_heads(rows(q_ref, start).astype(BF16), kw, vw, valid)
        if dil == 1:
            o_ref[0, pl.ds(start, blk), :] = o
            lse_ref[0, pl.ds(start, blk), :] = lse
        else:
            for hf in range(halves):
                o_st[hf, pl.ds(start, blk, stride=dil), :] = o[:, hf * LANES:(hf + 1) * LANES]
                lse_st[hf, pl.ds(start, blk, stride=dil), :] = lse[:, hf * LANES:(hf + 1) * LANES]
        return carry

    lax.fori_loop(0, dil * nsub, unit, 0)
    if dil > 1:
        for hf in range(halves):
            o_ref[0, :, hf * LANES:(hf + 1) * LANES] = o_st[hf]
            lse_ref[0, :, hf * LANES:(hf + 1) * LANES] = lse_st[hf]


def _swa_prompt(q, k, v, g, dil):
    b, s, _ = q.shape
    tb = _SWA_TOKENS
    back = SWA_BLOCK * dil
    assert s % tb == 0 and tb % back == 0
    cur = pl.BlockSpec((1, tb, SWA_OUT), lambda bi, i: (bi, i, g))
    prev = pl.BlockSpec((1, back, SWA_OUT), lambda bi, i: (bi, jnp.maximum(i * (tb // back) - 1, 0), g))
    out = pl.BlockSpec((1, tb, SWA_OUT), lambda bi, i: (bi, i, 0))
    halves = SWA_OUT // LANES
    stage = [pltpu.VMEM((halves, n, LANES), F32) for n in (tb, tb, tb, back, back, tb, tb)] if dil > 1 else []
    o, lse = pl.pallas_call(
        functools.partial(_swa_kernel, dil=dil),
        grid=(b, s // tb),
        in_specs=[cur, cur, cur, prev, prev],
        out_specs=[out, out],
        out_shape=[jax.ShapeDtypeStruct((b, s, SWA_OUT), F32)] * 2,
        scratch_shapes=stage,
        compiler_params=_params(("arbitrary", "arbitrary")),
        name=f"swa_prompt_g{g}",
    )(q, k, v, k, v)
    return o.reshape(b * s, SWA_OUT), lse.reshape(b * s, SWA_OUT)


_SAMPLE_ROWS = 16


def _swa_sample_kernel(q_ref, kn_ref, vn_ref, knt_ref, vnt_ref, k0_ref, v0_ref, k1_ref, v1_ref, k2_ref, v2_ref,
                       ob_ref, ok0_ref, ov0_ref, ok1_ref, ov1_ref, ok2_ref, ov2_ref, *, n_new):
    rows = _SAMPLE_ROWS
    in_refs = ((k0_ref, v0_ref), (k1_ref, v1_ref), (k2_ref, v2_ref))
    out_refs = ((ok0_ref, ov0_ref), (ok1_ref, ov1_ref), (ok2_ref, ov2_ref))
    scale = SWA_HEAD_DIM ** -0.5
    jn = lax.broadcasted_iota(jnp.int32, (rows, rows), 1)
    ln = lax.broadcasted_iota(jnp.int32, (rows, rows), 0)
    tail = lax.broadcasted_iota(jnp.int32, (SWA_HEAD_DIM, LANES), 1) >= LANES - n_new
    o_g, lse_g = [], []
    for g, (win, dil) in enumerate(SWA_GROUPS):
        jc = lax.broadcasted_iota(jnp.int32, (rows, win), 1)
        lc = lax.broadcasted_iota(jnp.int32, (rows, win), 0)
        valid_c = (jc >= lc) & (((jc - lc) & (dil - 1)) == 0)
        valid_n = (jn <= ln) & (((ln - jn) & (dil - 1)) == 0) & (jn < n_new)
        o_h, lse_h = [], []
        for h in range(SWA_HPG):
            col = g * SWA_OUT + h * SWA_HEAD_DIM
            hsl = slice(col, col + SWA_HEAD_DIM)
            qh = q_ref[0, :, hsl].astype(BF16)
            knh = kn_ref[0, :, hsl].astype(BF16)
            vnh = vn_ref[0, :, hsl].astype(BF16)
            for (src, dst, new_t) in ((in_refs[g][0], out_refs[g][0], knt_ref), (in_refs[g][1], out_refs[g][1], vnt_ref)):
                old = src[0, h]
                moved = pltpu.roll(old, win - n_new, 1)
                if win > LANES:
                    dst[0, h, :, 0:win - LANES] = moved[:, 0:win - LANES]
                dst[0, h, :, win - LANES:win] = jnp.where(tail, new_t[0, hsl, :], moved[:, win - LANES:win])
            kt = in_refs[g][0][0, h].astype(BF16)
            vt = in_refs[g][1][0, h].astype(BF16)
            s_c = jnp.where(valid_c, _dot(qh, kt) * scale, NEG)
            s_n = jnp.where(valid_n, _dot_nt(qh, knh) * scale, NEG)
            m = jnp.maximum(jnp.max(s_c, axis=-1, keepdims=True), jnp.max(s_n, axis=-1, keepdims=True))
            p_c = jnp.exp(s_c - m)
            p_n = jnp.exp(s_n - m)
            den = jnp.sum(p_c, axis=-1, keepdims=True) + jnp.sum(p_n, axis=-1, keepdims=True)
            o_h.append((_dot_nt(p_c.astype(BF16), vt) + _dot(p_n.astype(BF16), vnh)) / den)
            lse_h.append(jnp.broadcast_to(m + jnp.log(den), (rows, SWA_HEAD_DIM)))
        o_g.append(jnp.concatenate(o_h, axis=1))
        lse_g.append(jnp.concatenate(lse_h, axis=1))
    lmax = jnp.maximum(jnp.maximum(lse_g[0], lse_g[1]), lse_g[2])
    e = [jnp.exp(x - lmax) for x in lse_g]
    ob_ref[0] = ((e[0] * o_g[0] + e[1] * o_g[1] + e[2] * o_g[2]) / (e[0] + e[1] + e[2])).astype(BF16)


def _swa_sample(q, kn, vn, knt, vnt, caches_t, n_new):
    bd = q.shape[0]
    rows = _SAMPLE_ROWS
    tok = pl.BlockSpec((1, rows, SWA_WIDTH), lambda bi: (bi, 0, 0))
    new_t = pl.BlockSpec((1, SWA_WIDTH, LANES), lambda bi: (bi, 0, 0))
    specs, args = [tok, tok, tok, new_t, new_t], [q, kn, vn, knt, vnt]
    out_specs = [pl.BlockSpec((1, rows, SWA_OUT), lambda bi: (bi, 0, 0))]
    out_shape = [jax.ShapeDtypeStruct((bd, rows, SWA_OUT), BF16)]
    for g, (win, dil) in enumerate(SWA_GROUPS):
        for t in caches_t[g]:
            assert t.shape == (bd, SWA_HPG, SWA_HEAD_DIM, win) and win == SWA_BLOCK * dil and win % LANES == 0
            spec = pl.BlockSpec((1, SWA_HPG, SWA_HEAD_DIM, win), lambda bi: (bi, 0, 0, 0))
            args.append(t)
            specs.append(spec)
            out_specs.append(spec)
            out_shape.append(jax.ShapeDtypeStruct(t.shape, t.dtype))
    return pl.pallas_call(
        functools.partial(_swa_sample_kernel, n_new=n_new),
        grid=(bd,),
        in_specs=specs,
        out_specs=out_specs,
        out_shape=out_shape,
        compiler_params=_params(("arbitrary",)),
        name="swa_sample",
    )(*args)


def _merge_kernel(*refs, combine, n_alias, n_real):
    x1_ref, h2c_ref, lg_ref = refs[-3:]
    refs = refs[:len(refs) - 3 - n_alias]

    @pl.when(pl.program_id(0) >= n_real)
    def _():
        h2c_ref[...] = jnp.zeros_like(h2c_ref)
        lg_ref[...] = jnp.zeros_like(lg_ref)

    pl.when(pl.program_id(0) < n_real)(functools.partial(_merge_tile, refs, x1_ref, h2c_ref, lg_ref, combine))


def _merge_tile(refs, x1_ref, h2c_ref, lg_ref, combine):
    if combine:
        (x_ref, ya_ref, o0, o1, o2, l0, l1, l2, sgab_ref, wa_ref, wb_ref, wo_ref, ln_ref, wr_hi_ref, wr_lo_ref,
         br_ref) = refs
        lmax = jnp.maximum(jnp.maximum(l0[...], l1[...]), l2[...])
        e0, e1, e2 = jnp.exp(l0[...] - lmax), jnp.exp(l1[...] - lmax), jnp.exp(l2[...] - lmax)
        ob = ((e0 * o0[...] + e1 * o1[...] + e2 * o2[...]) / (e0 + e1 + e2)).astype(BF16)
    else:
        (x_ref, ya_ref, ob_ref, sgab_ref, wa_ref, wb_ref, wo_ref, ln_ref, wr_hi_ref, wr_lo_ref, br_ref) = refs
        ob = ob_ref[...]
    ya = _dot(ya_ref[...], wa_ref[...])
    yb = _dot(ob, wb_ref[...])
    sga = sgab_ref[:, :D_MODEL].astype(F32)
    sgb = sgab_ref[:, D_MODEL:].astype(F32)
    x1 = x_ref[...] + _dot((sga * ya + sgb * yb).astype(BF16), wo_ref[...])
    x1_ref[...] = x1
    h2 = x1 * lax.rsqrt(jnp.mean(x1 * x1, axis=-1, keepdims=True) + EPS) * ln_ref[...]
    h_hi, h_lo = _split_bf16(h2)
    lg_ref[...] = _dot(h_hi, wr_hi_ref[...]) + _dot(h_hi, wr_lo_ref[...]) + _dot(h_lo, wr_hi_ref[...]) + br_ref[...]
    h2c_ref[...] = h_hi


def _merge(x, ya_in, swa, sgab, weights, tm, tok_off, t_all, shared=None):
    t = x.shape[0]
    assert t % tm == 0 and tok_off % tm == 0 and MERGE_TILE % tm == 0
    combine = len(swa) > 1
    blk_off = tok_off // tm
    t_buf = -(-t_all // MERGE_TILE) * MERGE_TILE
    nsub = D_MODEL // LANES
    n_real = t // tm
    n_fill = 0 if shared is not None else (t_buf - tok_off - t) // tm
    row = lambda width: pl.BlockSpec((tm, width), lambda i: (jnp.minimum(i, n_real - 1), 0))
    shared_in = [] if shared is None else list(shared)
    n_in = 3 + len(swa) + len(weights)
    return pl.pallas_call(
        functools.partial(_merge_kernel, combine=combine, n_alias=len(shared_in), n_real=n_real),
        grid=(n_real + n_fill,),
        in_specs=[row(D_MODEL), row(D_MODEL)] + [row(SWA_OUT)] * len(swa) + [row(2 * D_MODEL)]
                 + [_resident(w.shape) for w in weights] + [pl.BlockSpec(memory_space=pl.ANY)] * len(shared_in),
        out_specs=[row(D_MODEL),
                   pl.BlockSpec((tm, D_MODEL), lambda i: (i + blk_off, 0)),
                   pl.BlockSpec((tm, LANES), lambda i: (i + blk_off, 0))],
        out_shape=[jax.ShapeDtypeStruct((t, D_MODEL), F32),
                   jax.ShapeDtypeStruct((t_buf, D_MODEL), BF16),
                   jax.ShapeDtypeStruct((t_buf, LANES), F32)],
        input_output_aliases={n_in + k: 1 + k for k in range(len(shared_in))},
        compiler_params=_params(("arbitrary",)),
        name="merge",
    )(x, ya_in, *swa, sgab, *weights, *shared_in)


def _router_kernel(lg_ref, mi_ref, mw_ref, cnt_ref, blk_ref, carry):
    phase = pl.program_id(0)
    i = pl.program_id(1)

    @pl.when((phase == 0) & (i == 0))
    def _():
        carry[...] = jnp.zeros_like(carry)

    @pl.when((phase == 1) & (i == 0))
    def _():
        cnt = carry[...]
        cnt_ref[...] = jnp.broadcast_to(cnt, cnt_ref.shape)
        tiles = jnp.floor((cnt + (EXPERT_TILE - 1)) * (1.0 / EXPERT_TILE))
        r = lax.broadcasted_iota(jnp.int32, (LANES, LANES), 0)
        c = lax.broadcasted_iota(jnp.int32, (LANES, LANES), 1)
        before = _dot(jnp.broadcast_to(tiles, (SUBLANES, LANES)).astype(BF16), (r < c).astype(BF16))
        carry[...] = before[0:1, :] * EXPERT_TILE

    lg = lg_ref[...]
    tr = lg.shape[0]
    lane = lax.broadcasted_iota(jnp.int32, lg.shape, 1)
    big = jnp.int32(LANES)
    gl = jnp.where(lane < N_GROUPS, lg, NEG)
    gmax = jnp.max(gl, axis=-1, keepdims=True)
    g_idx = jnp.min(jnp.where(gl == gmax, lane, big), axis=-1, keepdims=True)
    g_w = 1.0 / jnp.sum(jnp.exp(gl - gmax), axis=-1, keepdims=True)
    e_lane = lane - N_GROUPS
    in_group = (e_lane >= 0) & (e_lane < N_EXPERTS) & (e_lane // EXPERTS_PER_GROUP == g_idx)
    el = jnp.where(in_group, lg, NEG)
    v1 = jnp.max(el, axis=-1, keepdims=True)
    i1 = jnp.min(jnp.where(el == v1, lane, big), axis=-1, keepdims=True)
    el2 = jnp.where(lane == i1, NEG, el)
    v2 = jnp.max(el2, axis=-1, keepdims=True)
    i2 = jnp.min(jnp.where(el2 == v2, lane, big), axis=-1, keepdims=True)
    r21 = jnp.exp(v2 - v1)
    w1 = g_w / (1.0 + r21)
    w2 = g_w * r21 / (1.0 + r21)
    e1 = i1 - N_GROUPS
    e2 = i2 - N_GROUPS
    hot1 = lane == e1
    hot2 = lane == e2
    hot = (hot1 | hot2).astype(BF16)
    r = lax.broadcasted_iota(jnp.int32, (tr, tr), 0)
    c = lax.broadcasted_iota(jnp.int32, (tr, tr), 1)
    within = _dot((r > c).astype(BF16), hot)
    start = carry[...]
    cnt = jnp.sum(hot.astype(F32), axis=0, keepdims=True)
    carry[...] = start + cnt
    lane1 = lax.broadcasted_iota(jnp.int32, (SUBLANES, LANES), 1)
    incl = jnp.broadcast_to(cnt, (SUBLANES, LANES))
    for sh in (1, 2, 4, 8, 16):
        incl = incl + jnp.where(lane1 >= sh, pltpu.roll(incl, sh, 1), 0.0)
    local = incl[0:1, :] - cnt
    pick = lambda hot_k, row: jnp.sum(jnp.where(hot_k, row, 0.0), axis=-1, keepdims=True)
    cols = [e1, e2]
    for hot_k in (hot1, hot2):
        cols.append((pick(hot_k, within) + pick(hot_k, start)).astype(jnp.int32))
    for hot_k in (hot1, hot2):
        cols.append((pick(hot_k, within) + pick(hot_k, local)).astype(jnp.int32))
    mi = cols[-1]
    for k in range(len(cols) - 2, -1, -1):
        mi = jnp.where(lane == k, cols[k], mi)
    mi_ref[...] = mi
    mw_ref[...] = jnp.where(lane == 0, w1, w2)
    sub = lax.broadcasted_iota(jnp.int32, (SUBLANES, LANES), 0)
    tbl = jnp.where(sub == 0, start, jnp.where(sub == 1, cnt, jnp.where(sub == 2, local, 0.0)))
    blk_ref[...] = tbl.astype(jnp.int32)


def _router(logits, t, tr):
    assert t % tr == 0 and tr % SUBLANES == 0
    out_row = pl.BlockSpec((tr, LANES), lambda p, i: (i * p, 0))
    return pl.pallas_call(
        _router_kernel,
        grid=(2, t // tr),
        in_specs=[pl.BlockSpec((tr, LANES), lambda p, i: (i, 0))],
        out_specs=[out_row, out_row, pl.BlockSpec((SUBLANES, LANES), lambda p, i: (0, 0)),
                   pl.BlockSpec((SUBLANES, LANES), lambda p, i: (i * p, 0))],
        out_shape=[jax.ShapeDtypeStruct((t, LANES), jnp.int32), jax.ShapeDtypeStruct((t, LANES), F32),
                   jax.ShapeDtypeStruct((SUBLANES, LANES), F32),
                   jax.ShapeDtypeStruct((t // tr * SUBLANES, LANES), jnp.int32)],
        scratch_shapes=[pltpu.VMEM((1, LANES), F32)],
        compiler_params=_params(("arbitrary", "arbitrary")),
        name="router",
    )(logits)


def _run_copies(src, dst, sem, src_row, dst_row, n_rows, wait=False):
    nsub = D_MODEL // LANES

    def copy(s0, d0, rows):
        dma = pltpu.make_async_copy(src.at[pl.ds(pl.multiple_of(s0 * nsub, nsub), rows * nsub), :],
                                    dst.at[pl.ds(pl.multiple_of(d0 * nsub, nsub), rows * nsub), :], sem)
        dma.wait() if wait else dma.start()

    def chunk(k, carry):
        copy(src_row + k * SUBLANES, dst_row + k * SUBLANES, SUBLANES)
        return carry

    n_chunks = n_rows // SUBLANES
    lax.fori_loop(0, n_chunks, chunk, 0)
    done = n_chunks * SUBLANES
    for rows in (4, 2, 1):
        @pl.when((n_rows & rows) != 0)
        def _(rows=rows, done=done):
            copy(src_row + done, dst_row + done, rows)

        done = done + (n_rows & rows)


def _dispatch_kernel(start_ref, cnt_ref, loc_ref, end_ref, tot_ref, h_ref, mi_ref, xs_hbm, stage, zeros, sem, zsem,
                     *, tm):
    i = pl.program_id(0)
    n = pl.num_programs(0)
    nsub = D_MODEL // LANES
    tb = h_ref.shape[0]
    slot = i % 2

    def wait_stage(sl):
        pltpu.make_async_copy(stage.at[sl], stage.at[sl], sem.at[sl]).wait()

    def pad_rows(wait):
        def one(e, carry):
            tot = tot_ref[e]
            n_pad = (tot + tm - 1) // tm * tm - tot
            _run_copies(zeros, xs_hbm, zsem, 0, end_ref[e] - n_pad, n_pad, wait=wait)
            return carry

        lax.fori_loop(0, end_ref.shape[0], one, 0)

        def spare(j, carry):
            dma = pltpu.make_async_copy(zeros, xs_hbm.at[pl.ds(pl.multiple_of(j * tm * nsub, nsub), tm * nsub), :], zsem)
            dma.wait() if wait else dma.start()
            return carry

        lax.fori_loop(end_ref[end_ref.shape[0] - 1] // tm, xs_hbm.shape[0] // (tm * nsub), spare, 0)

    @pl.when(i == 0)
    def _():
        zeros[...] = jnp.zeros_like(zeros)
        pad_rows(wait=False)

    @pl.when(i >= 2)
    def _():
        wait_stage(slot)

    li = mi_ref[...]
    rows_iota = lax.broadcasted_iota(jnp.int32, (tb, 2 * tb), 1)
    onehot = ((rows_iota == li[:, 4:5]) | (rows_iota == li[:, 5:6])).astype(BF16)
    srt = _dot_tn(onehot, h_ref[...])
    for c in range(nsub):
        stage[slot, pl.ds(c, 2 * tb, stride=nsub), :] = srt[:, c * LANES:(c + 1) * LANES]

    def run(e, carry):
        k = i * N_EXPERTS + e
        _run_copies(stage.at[slot], xs_hbm, sem.at[slot], loc_ref[k], start_ref[k], cnt_ref[k])
        return carry

    lax.fori_loop(0, N_EXPERTS, run, 0)

    @pl.when(i == n - 1)
    def _():
        wait_stage(slot)

        @pl.when(n >= 2)
        def _():
            wait_stage(1 - slot)

        pad_rows(wait=True)


def _dispatch(blk_start, blk_cnt, blk_loc, seg_end, seg_tot, h2, meta_i, n_slots, tb):
    t = meta_i.shape[0]
    nsub = D_MODEL // LANES
    assert t % tb == 0
    grid_spec = pltpu.PrefetchScalarGridSpec(
        num_scalar_prefetch=5,
        grid=(t // tb,),
        in_specs=[pl.BlockSpec((tb, D_MODEL), lambda i, *_: (i, 0)),
                  pl.BlockSpec((tb, LANES), lambda i, *_: (i, 0))],
        out_specs=pl.BlockSpec(memory_space=pl.ANY),
        scratch_shapes=[pltpu.VMEM((2, 2 * tb * nsub, LANES), F32),
                        pltpu.VMEM((EXPERT_TILE * nsub, LANES), F32),
                        pltpu.SemaphoreType.DMA((2,)), pltpu.SemaphoreType.DMA(())],
    )
    return pl.pallas_call(
        functools.partial(_dispatch_kernel, tm=EXPERT_TILE),
        grid_spec=grid_spec,
        out_shape=jax.ShapeDtypeStruct((n_slots * nsub, LANES), F32),
        compiler_params=_params(("arbitrary",)),
        name="dispatch",
    )(blk_start, blk_cnt, blk_loc, seg_end, seg_tot, h2, meta_i)


def _expert_kernel(te_ref, tv_ref, xs_ref, wg_ref, wu_ref, wd_ref, out_ref, wgb, wub, wdb):
    i = pl.program_id(0)
    tm = EXPERT_TILE
    nsub = D_MODEL // LANES
    valid = tv_ref[i] != 0

    @pl.when(jnp.logical_not(valid))
    def _():
        out_ref[...] = jnp.zeros_like(out_ref)

    @pl.when(valid & ((i == 0) | (te_ref[i] != te_ref[jnp.maximum(i - 1, 0)])))
    def _():
        wgb[...] = wg_ref[0, 0].astype(BF16)
        wub[...] = wu_ref[0, 0].astype(BF16)
        wdb[...] = wd_ref[0, 0].astype(BF16)

    @pl.when(valid)
    def _():
        h = jnp.concatenate([xs_ref[pl.ds(c, tm, stride=nsub), :] for c in range(nsub)], axis=1).astype(BF16)
        gate = _dot(h, wgb[...])
        up = _dot(h, wub[...])
        a = (gate * _sigmoid(gate) * up).astype(BF16)
        o = _dot(a, wdb[...])
        for c in range(nsub):
            out_ref[pl.ds(c, tm, stride=nsub), :] = o[:, c * LANES:(c + 1) * LANES]


def _experts(tile_expert, tile_valid, xs, wg, wu, wd):
    n_tiles = tile_expert.shape[0]
    tm = EXPERT_TILE
    nsub = D_MODEL // LANES
    wspec = lambda shape: pl.BlockSpec((1, 1) + shape, lambda i, te, tv: (0, te[i], 0, 0))
    last = lambda i, tv: jnp.where(tv[i] != 0, i, 0)
    grid_spec = pltpu.PrefetchScalarGridSpec(
        num_scalar_prefetch=2,
        grid=(n_tiles,),
        in_specs=[pl.BlockSpec((tm * nsub, LANES), lambda i, te, tv: (last(i, tv), 0)),
                  wspec((D_MODEL, D_EXPERT)), wspec((D_MODEL, D_EXPERT)), wspec((D_EXPERT, D_MODEL))],
        out_specs=pl.BlockSpec((tm * nsub, LANES), lambda i, te, tv: (i, 0)),
        scratch_shapes=[pltpu.VMEM((D_MODEL, D_EXPERT), BF16), pltpu.VMEM((D_MODEL, D_EXPERT), BF16),
                        pltpu.VMEM((D_EXPERT, D_MODEL), BF16)],
    )
    return pl.pallas_call(
        _expert_kernel,
        grid_spec=grid_spec,
        out_shape=jax.ShapeDtypeStruct((n_tiles * tm * nsub, LANES), F32),
        compiler_params=_params(("arbitrary",)),
        name="experts",
    )(tile_expert, tile_valid, xs, wg, wu, wd)


def _combine_kernel(p0_ref, p1_ref, x1_ref, mw_ref, rows_hbm, y_ref, g0, g1, sem, *, tc, blk_off):
    i = pl.program_id(0)
    n = pl.num_programs(0)
    nout = D_MODEL // LANES
    slot = i % 2

    def issue_all(step, sl):
        base = (step + blk_off) * tc

        def issue(j, carry):
            for pos_ref, dst, s in ((p0_ref, g0, 0), (p1_ref, g1, 1)):
                src0 = pl.multiple_of(pos_ref[base + j] * nout, nout)
                pltpu.make_async_copy(rows_hbm.at[pl.ds(src0, nout), :],
                                      dst.at[sl, pl.ds(pl.multiple_of(j * nout, nout), nout), :],
                                      sem.at[sl, s]).start()
            return carry

        lax.fori_loop(0, tc, issue, 0, unroll=4)

    @pl.when(i == 0)
    def _():
        issue_all(i, slot)

    @pl.when(i + 1 < n)
    def _():
        issue_all(i + 1, 1 - slot)

    for dst, s in ((g0, 0), (g1, 1)):
        pltpu.make_async_copy(rows_hbm.at[pl.ds(0, tc * nout), :], dst.at[slot], sem.at[slot, s]).wait()
    w0 = mw_ref[:, 0:1]
    w1 = mw_ref[:, 1:2]
    for c in range(nout):
        sl = slice(c * LANES, (c + 1) * LANES)
        y_ref[:, sl] = (x1_ref[:, sl] + w0 * g0[slot, pl.ds(c, tc, stride=nout), :]
                        + w1 * g1[slot, pl.ds(c, tc, stride=nout), :])


def _combine(pos0, pos1, x1, meta_w, rows, tc, tok_off):
    t = x1.shape[0]
    assert t % tc == 0 and tok_off % tc == 0
    blk_off = tok_off // tc
    nout = D_MODEL // LANES
    row = pl.BlockSpec((tc, D_MODEL), lambda i, p0, p1: (i, 0))
    grid_spec = pltpu.PrefetchScalarGridSpec(
        num_scalar_prefetch=2,
        grid=(t // tc,),
        in_specs=[row, pl.BlockSpec((tc, LANES), lambda i, p0, p1: (i + blk_off, 0)),
                  pl.BlockSpec(memory_space=pl.ANY)],
        out_specs=row,
        scratch_shapes=[pltpu.VMEM((2, tc * nout, LANES), F32), pltpu.VMEM((2, tc * nout, LANES), F32),
                        pltpu.SemaphoreType.DMA((2, 2))],
    )
    return pl.pallas_call(
        functools.partial(_combine_kernel, tc=tc, blk_off=blk_off),
        grid_spec=grid_spec,
        out_shape=jax.ShapeDtypeStruct((t, D_MODEL), F32),
        compiler_params=_params(("arbitrary",)),
        name="combine",
    )(pos0, pos1, x1, meta_w, rows)


def _rope_tables(pos):
    half = SWA_HEAD_DIM // 2
    inv_freq = ROPE_THETA ** (-jnp.arange(half, dtype=F32) / half)
    ang = pos.astype(F32)[:, None] * inv_freq[None, :]
    cos, sin = jnp.cos(ang), jnp.sin(ang)
    reps = LANES // SWA_HEAD_DIM
    return jnp.tile(jnp.concatenate([cos, cos], axis=1), (1, reps)), jnp.tile(jnp.concatenate([-sin, sin], axis=1), (1, reps))


def kernel(x_prompt, x_sample, state_gla, cache_swa_k0, cache_swa_v0, cache_swa_k1, cache_swa_v1, cache_swa_k2, cache_swa_v2, ln1_w, w_in, w_gla_lr, b_gla_lr, gla_onorm_w, q_norm_w, k_norm_w, w_branch_a, w_branch_b, w_out, ln2_w, w_router_group, b_router_group, w_router_expert, b_router_expert, w_exp_gate, w_exp_up, w_exp_down):
    b, s, d = x_prompt.shape
    bd, ls, _ = x_sample.shape
    tp, ts = b * s, bd * ls
    assert w_in.shape[0] == 1 and d == D_MODEL and ts % SUBLANES == 0
    k_caches = (cache_swa_k0, cache_swa_k1, cache_swa_k2)
    v_caches = (cache_swa_v0, cache_swa_v1, cache_swa_v2)

    w = w_in[0]
    cuts = np.cumsum((512, 512, 1024, GLA_RANK, 1024, 768, 768, 768, 1024, 1024))
    sec = lambda a: w[:, (0 if a == 0 else cuts[a - 1]):cuts[a]]
    lr_pad = jnp.pad(sec(3), ((0, 0), (0, LANES - GLA_RANK)))
    w_packed = jnp.concatenate([sec(0), sec(1), sec(2), sec(4), sec(5), sec(6), sec(7), sec(8), sec(9), lr_pad],
                               axis=1).astype(BF16)
    wlr = jnp.pad(w_gla_lr[0], ((0, LANES - GLA_RANK), (0, 0))).astype(BF16)
    blr = b_gla_lr[0][None, :]
    nw = jnp.concatenate([jnp.tile(q_norm_w[0], SWA_WIDTH // SWA_HEAD_DIM), jnp.tile(k_norm_w[0], SWA_WIDTH // SWA_HEAD_DIM)])[None, :]
    gi = np.arange(256) // SWA_HEAD_DIM
    gmat = jnp.asarray((gi[:, None] == gi[None, :]).astype(np.float32) / SWA_HEAD_DIM, dtype=BF16)
    ln1 = ln1_w[0][None, :]
    ln2 = ln2_w[0][None, :]
    onw = gla_onorm_w[0][None, :]
    wa = w_branch_a[0].astype(BF16)
    wb = w_branch_b[0].astype(BF16)
    wo = w_out[0].astype(BF16)
    wr = jnp.pad(jnp.concatenate([w_router_group[0], w_router_expert[0]], axis=1),
                 ((0, 0), (0, LANES - N_GROUPS - N_EXPERTS)))
    wr_hi, wr_lo = _split_bf16(wr)
    br = jnp.pad(jnp.concatenate([b_router_group[0], b_router_expert[0]]), (0, LANES - N_GROUPS - N_EXPERTS))[None, :]

    cos_p, sin_p = _rope_tables(jnp.arange(s, dtype=jnp.int32))
    cos_s, sin_s = _rope_tables(PAST_LEN + jnp.arange(ts, dtype=jnp.int32) % ls)

    tm_p = 512
    proj_p = _proj(x_prompt.reshape(tp, d), cos_p, sin_p, s // tm_p, ln1, w_packed, wlr, blr, nw, gmat, tm_p)
    proj_s = _proj(x_sample.reshape(ts, d), cos_s, sin_s, 1, ln1, w_packed, wlr, blr, nw, gmat, ts)
    gqkv_p, la_p, sog_p, q_p, k_p, v_p, sgab_p = proj_p
    gqkv_s, la_s, sog_s, q_s, k_s, v_s, sgab_s = proj_s

    r3 = lambda t, nb: t.reshape(nb, t.shape[0] // nb, t.shape[1])
    ya_p, st_p = _gla(r3(gqkv_p, b), r3(la_p, b), r3(sog_p, b),
                      jnp.zeros((b, GLA_HEADS, GLA_DK, GLA_DV), F32), onw, GLA_CHUNK, 512)
    pad_s = lambda t: jnp.pad(r3(t, bd), ((0, 0), (0, _SAMPLE_ROWS - ls), (0, 0)))
    ya_s, st_s = _gla(pad_s(gqkv_s), pad_s(la_s), pad_s(sog_s), state_gla[0], onw, _SAMPLE_ROWS, _SAMPLE_ROWS)
    ya_s = ya_s[:, :ls].reshape(ts, d)

    q3, k3, v3 = r3(q_p, b), r3(k_p, b), r3(v_p, b)
    swa_p = [_swa_prompt(q3, k3, v3, g, dil) for g, (_, dil) in enumerate(SWA_GROUPS)]
    to_t = lambda c: jnp.transpose(c[0], (0, 2, 3, 1))
    caches_t = [(to_t(k_caches[g]), to_t(v_caches[g])) for g in range(len(SWA_GROUPS))]
    new_t = lambda t: jnp.pad(jnp.transpose(r3(t, bd), (0, 2, 1)), ((0, 0), (0, 0), (LANES - ls, 0)))
    ob_s, *new_caches = _swa_sample(pad_s(q_s), pad_s(k_s), pad_s(v_s), new_t(k_s), new_t(v_s), caches_t, ls)
    ob_s = ob_s[:, :ls].reshape(ts, SWA_OUT)

    t = tp + ts
    mw = (wa, wb, wo, ln2, wr_hi, wr_lo, br)
    x1_p, h2c, lg = _merge(x_prompt.reshape(tp, d), ya_p.reshape(tp, d),
                           [o for o, _ in swa_p] + [l for _, l in swa_p], sgab_p, mw, 512, 0, t)
    x1_s, h2c, lg = _merge(x_sample.reshape(ts, d), ya_s, [ob_s], sgab_s, mw, ts, tp, t, shared=(h2c, lg))

    tb = DISPATCH_BLOCK
    meta_i, meta_w, cnt, blocks = _router(lg, t, tb)
    blocks = blocks.reshape(t // tb, SUBLANES, LANES)[:, :, :N_EXPERTS]
    blk_start, blk_cnt, blk_loc = (blocks[:, r].reshape(-1) for r in range(3))
    tm = EXPERT_TILE
    n_tiles = (2 * t) // tm + N_EXPERTS
    counts = cnt[0, :N_EXPERTS].astype(jnp.int32)
    ends = jnp.cumsum((counts + tm - 1) // tm * tm)
    pos0, pos1 = meta_i[:, 2], meta_i[:, 3]
    tile_start = jnp.arange(n_tiles, dtype=jnp.int32) * tm
    tile_valid = (tile_start < ends[-1]).astype(jnp.int32)
    last_slot = jnp.minimum(tile_start, ends[-1] - 1)
    tile_expert = jnp.minimum(jnp.sum((last_slot[:, None] >= ends[None, :]).astype(jnp.int32), axis=1), N_EXPERTS - 1)

    xs = _dispatch(blk_start, blk_cnt, blk_loc, ends, counts, h2c, meta_i, n_tiles * tm, tb)
    rows = _experts(tile_expert, tile_valid, xs, w_exp_gate, w_exp_up, w_exp_down)
    y_p = _combine(pos0, pos1, x1_p, meta_w, rows, 256, 0)
    y_s = _combine(pos0, pos1, x1_s, meta_w, rows, ts, tp)

    heads = lambda a: a.reshape(1, a.shape[0], a.shape[1], SWA_HPG, SWA_HEAD_DIM)
    outs = [y_p.reshape(b, s, d), y_s.reshape(bd, ls, d), st_p[None].astype(x_prompt.dtype)]
    for g, (win, _) in enumerate(SWA_GROUPS):
        keep = min(win, s)
        gsl = slice(g * SWA_OUT, (g + 1) * SWA_OUT)
        outs += [heads(k3[:, s - keep:, gsl]), heads(v3[:, s - keep:, gsl])]
    outs.append(st_s[None].astype(state_gla.dtype))
    outs += [jnp.transpose(c, (0, 3, 1, 2))[None] for c in new_caches]
    return tuple(outs)
```

```python
import functools

import numpy as np
import jax
import jax.numpy as jnp
from jax import lax
from jax.experimental import pallas as pl
from jax.experimental.pallas import tpu as pltpu

F32 = jnp.float32
BF16 = jnp.bfloat16

D_MODEL = 1024
PAST_LEN = 16384
GLA_HEADS = 4
GLA_DK = 128
GLA_DV = 256
GLA_RANK = 16
GLA_TAU = 16.0
GLA_CHUNK = 64
SWA_GROUPS = ((128, 1), (512, 4), (2048, 16))
SWA_HPG = 4
SWA_HEAD_DIM = 64
SWA_WIDTH = 768
SWA_OUT = 256
SWA_BLOCK = 128
ROPE_THETA = 10000.0
N_GROUPS = 4
EXPERTS_PER_GROUP = 8
N_EXPERTS = 32
D_EXPERT = 512
EPS = 1e-6

LANES = 128
SUBLANES = 8
VMEM_LIMIT = 56 * 1024 * 1024
NEG = -1e30
MXU_WIDTH = 256
EXPERT_TILE = 256
MERGE_TILE = 512
DISPATCH_BLOCK = 384

_C_GQKV = (0, 2048)
_C_GOG = (2048, 3072)
_C_QK = (3072, 4608)
_C_V = (4608, 5376)
_C_GAB = (5376, 7424)
_C_LR = (7424, 7552)


def _contract(a, b, dims):
    exact = a.dtype == F32 and b.dtype == F32
    return lax.dot_general(a, b, (dims, ((), ())), preferred_element_type=F32,
                           precision=lax.Precision.HIGHEST if exact else None)


def _dot(a, b):
    return _contract(a, b, ((1,), (0,)))


def _dot_nt(a, b):
    return _contract(a, b, ((1,), (1,)))


def _dot_tn(a, b):
    return _contract(a, b, ((0,), (0,)))


def _sigmoid(x):
    return 1.0 / (1.0 + jnp.exp(-x))


def _split(x, dt):
    hi = x.astype(dt)
    lo = (x - hi.astype(F32)).astype(dt)
    return hi, lo


def _split_bf16(x):
    return _split(x, BF16)


def _params(sem):
    return pltpu.CompilerParams(dimension_semantics=sem, vmem_limit_bytes=VMEM_LIMIT)


def _resident(shape):
    nd = len(shape)
    return pl.BlockSpec(shape, lambda *_: (0,) * nd, pipeline_mode=pl.Buffered(1))


def _proj_kernel(x_ref, cos_ref, sin_ref, ln_ref, w_ref, wlr_ref, blr_ref, nw_ref, g_ref,
                 gqkv_ref, la_ref, sog_ref, q_ref, k_ref, v_ref, sgab_ref):
    cdt = w_ref.dtype
    x = x_ref[...]
    h = (x * lax.rsqrt(jnp.mean(x * x, axis=-1, keepdims=True) + EPS) * ln_ref[...]).astype(cdt)
    gqkv_ref[...] = _dot(h, w_ref[:, _C_GQKV[0]:_C_GQKV[1]]).astype(cdt)
    og = _dot(h, w_ref[:, _C_GOG[0]:_C_GOG[1]])
    sog_ref[...] = (og * _sigmoid(og)).astype(cdt)
    lr = _dot(h, w_ref[:, _C_LR[0]:_C_LR[1]]).astype(cdt)
    z = _dot(lr, wlr_ref[...]) + blr_ref[...]
    la_ref[...] = (jnp.minimum(z, 0.0) - jnp.log(1.0 + jnp.exp(-jnp.abs(z)))) / GLA_TAU
    qk = _dot(h, w_ref[:, _C_QK[0]:_C_QK[1]])
    sq = (qk * qk).astype(cdt)
    ms = jnp.concatenate([_dot(sq[:, c * 256:(c + 1) * 256], g_ref[...]) for c in range(6)], axis=1)
    qn = qk * lax.rsqrt(ms + EPS) * nw_ref[...]
    width = 2 * SWA_WIDTH
    cos = jnp.tile(cos_ref[...], (1, width // LANES))
    sin = jnp.tile(sin_ref[...], (1, width // LANES))
    lane = lax.broadcasted_iota(jnp.int32, qn.shape, 1)
    half = SWA_HEAD_DIM // 2
    rot = jnp.where(lane % SWA_HEAD_DIM < half, pltpu.roll(qn, width - half, 1), pltpu.roll(qn, half, 1))
    qr = qn * cos + rot * sin
    q_ref[...] = qr[:, :SWA_WIDTH]
    k_ref[...] = qr[:, SWA_WIDTH:]
    v_ref[...] = _dot(h, w_ref[:, _C_V[0]:_C_V[1]])
    gab = _dot(h, w_ref[:, _C_GAB[0]:_C_GAB[1]])
    sgab_ref[...] = _sigmoid(gab).astype(cdt)


def _proj(x, cos, sin, rope_blocks, ln, w, wlr, blr, nw, g, tm):
    t = x.shape[0]
    assert t % tm == 0 and w.dtype == wlr.dtype == g.dtype
    row = lambda width: pl.BlockSpec((tm, width), lambda i: (i, 0))
    outs = [(2048, w.dtype), (512, F32), (1024, w.dtype), (768, F32), (768, F32), (768, F32), (2048, w.dtype)]
    return pl.pallas_call(
        _proj_kernel,
        grid=(t // tm,),
        in_specs=[row(D_MODEL),
                  pl.BlockSpec((tm, LANES), lambda i: (i % rope_blocks, 0)),
                  pl.BlockSpec((tm, LANES), lambda i: (i % rope_blocks, 0)),
                  _resident(ln.shape), _resident(w.shape), _resident(wlr.shape), _resident(blr.shape),
                  _resident(nw.shape), _resident(g.shape)],
        out_specs=[row(wd) for wd, _ in outs],
        out_shape=[jax.ShapeDtypeStruct((t, wd), dt) for wd, dt in outs],
        compiler_params=_params(("arbitrary",)),
        name="proj",
    )(x, cos, sin, ln, w, wlr, blr, nw, g)


def _gla_kernel(gqkv_ref, la_ref, sog_ref, s0_ref, onw_ref, y_ref, sfin_ref, s_scr, *, chunk, n_chunks):
    j = pl.program_id(1)

    @pl.when(j == 0)
    def _():
        s_scr[...] = s0_ref[0]

    r = lax.broadcasted_iota(jnp.int32, (chunk, chunk), 0)
    c = lax.broadcasted_iota(jnp.int32, (chunk, chunk), 1)
    causal = r >= c
    cdt = gqkv_ref.dtype
    tri = causal.astype(cdt)
    ones = jnp.ones((chunk, GLA_DV), cdt)
    hk = GLA_HEADS * GLA_DK

    def body(ci, carry):
        r0 = pl.multiple_of(ci * chunk, chunk)
        blk = gqkv_ref[0, pl.ds(r0, chunk), :]
        la_hi, la_lo = _split(la_ref[0, pl.ds(r0, chunk), :], cdt)
        b = _dot(tri, la_hi) + _dot(tri, la_lo)
        blast = b[chunk - 1:chunk, :]
        q = blk[:, :hk].astype(F32) * GLA_DK ** -0.5
        k = blk[:, hk:2 * hk].astype(F32)
        qd = (q * jnp.exp(b)).astype(cdt)
        kd = (k * jnp.exp(-b)).astype(cdt)
        kdec = (k * jnp.exp(blast - b)).astype(cdt)
        outs = []
        for h in range(GLA_HEADS):
            sl = slice(h * GLA_DK, (h + 1) * GLA_DK)
            v_h = blk[:, 2 * hk + h * GLA_DV:2 * hk + (h + 1) * GLA_DV]
            att = jnp.where(causal, _dot_nt(qd[:, sl], kd[:, sl]), 0.0).astype(cdt)
            s_prev = s_scr[h]
            o = _dot(att, v_h) + _dot(qd[:, sl], s_prev.astype(cdt))
            dsum = _dot_tn(la_hi[:, sl], ones) + _dot_tn(la_lo[:, sl], ones)
            s_scr[h] = s_prev * jnp.exp(dsum) + _dot_tn(kdec[:, sl], v_h)
            ms = jnp.mean(o * o, axis=-1, keepdims=True)
            outs.append(o * lax.rsqrt(ms + EPS) * onw_ref[...])
        o_all = jnp.concatenate(outs, axis=1) * sog_ref[0, pl.ds(r0, chunk), :].astype(F32)
        y_ref[0, pl.ds(r0, chunk), :] = o_all.astype(cdt)
        return carry

    lax.fori_loop(0, n_chunks, body, 0, unroll=min(n_chunks, 2))

    @pl.when(j == pl.num_programs(1) - 1)
    def _():
        sfin_ref[0] = s_scr[...]


def _gla(gqkv, la, sog, s0, onw, chunk, block):
    b, l, _ = gqkv.shape
    tok = lambda width: pl.BlockSpec((1, block, width), lambda bi, j: (bi, j, 0))
    st = pl.BlockSpec((1, GLA_HEADS, GLA_DK, GLA_DV), lambda bi, j: (bi, 0, 0, 0))
    return pl.pallas_call(
        functools.partial(_gla_kernel, chunk=chunk, n_chunks=block // chunk),
        grid=(b, l // block),
        in_specs=[tok(2048), tok(512), tok(1024), st, _resident(onw.shape)],
        out_specs=[tok(1024), st],
        out_shape=[jax.ShapeDtypeStruct((b, l, 1024), gqkv.dtype),
                   jax.ShapeDtypeStruct((b, GLA_HEADS, GLA_DK, GLA_DV), F32)],
        scratch_shapes=[pltpu.VMEM((GLA_HEADS, GLA_DK, GLA_DV), F32)],
        compiler_params=_params(("arbitrary", "arbitrary")),
        name="gla",
    )(gqkv, la, sog, s0, onw)


def _band_heads(q, kw, vw, valid):
    outs, lses = [], []
    for h in range(SWA_HPG):
        sl = slice(h * SWA_HEAD_DIM, (h + 1) * SWA_HEAD_DIM)
        s = _dot_nt(q[:, sl], kw[:, sl]) * SWA_HEAD_DIM ** -0.5
        s = jnp.where(valid, s, NEG)
        m = jnp.max(s, axis=-1, keepdims=True)
        p = jnp.exp(s - m)
        l = jnp.sum(p, axis=-1, keepdims=True)
        outs.append(_dot(p.astype(BF16), vw[:, sl]) / l)
        lses.append(jnp.broadcast_to(m + jnp.log(l), (q.shape[0], SWA_HEAD_DIM)))
    return jnp.concatenate(outs, axis=1), jnp.concatenate(lses, axis=1)


_SWA_TOKENS = 2048


def _swa_kernel(q_ref, k_ref, v_ref, kp_ref, vp_ref, o_ref, lse_ref, *stage, dil):
    blk = SWA_BLOCK
    nsub = q_ref.shape[1] // (blk * dil)
    first = pl.program_id(1) == 0
    qi = lax.broadcasted_iota(jnp.int32, (blk, 2 * blk), 0)
    kj = lax.broadcasted_iota(jnp.int32, (blk, 2 * blk), 1)
    band = (kj >= qi) & (kj <= qi + blk)
    halves = SWA_OUT // LANES

    if dil > 1:
        ins = (q_ref, k_ref, v_ref, kp_ref, vp_ref)
        q_ref, k_ref, v_ref, kp_ref, vp_ref, o_st, lse_st = stage
        for src, dst in zip(ins, stage):
            for hf in range(halves):
                dst[hf] = src[0, :, hf * LANES:(hf + 1) * LANES]

    def rows(ref, start):
        if dil == 1:
            return ref[0, pl.ds(start, blk), :]
        return jnp.concatenate([ref[hf, pl.ds(start, blk, stride=dil), :] for hf in range(halves)], axis=1)

    def unit(u, carry):
        r = u // nsub
        j = u % nsub
        start = r + dil * blk * j
        inside = r + dil * blk * jnp.maximum(j - 1, 0)
        if dil == 1:
            r, start, inside = 0, pl.multiple_of(start, blk), pl.multiple_of(inside, blk)
        head = j == 0
        kprev = jnp.where(head, rows(kp_ref, r), rows(k_ref, inside))
        vprev = jnp.where(head, rows(vp_ref, r), rows(v_ref, inside))
        kw = jnp.concatenate([kprev, rows(k_ref, start)], axis=0).astype(BF16)
        vw = jnp.concatenate([vprev, rows(v_ref, start)], axis=0).astype(BF16)
        valid = band & (kj >= jnp.where(head & first, blk, 0))
        o, lse = _band_heads(rows(q_ref, start).astype(BF16), kw, vw, valid)
        if dil == 1:
            o_ref[0, pl.ds(start, blk), :] = o
            lse_ref[0, pl.ds(start, blk), :] = lse
        else:
            for hf in range(halves):
                o_st[hf, pl.ds(start, blk, stride=dil), :] = o[:, hf * LANES:(hf + 1) * LANES]
                lse_st[hf, pl.ds(start, blk, stride=dil), :] = lse[:, hf * LANES:(hf + 1) * LANES]
        return carry

    lax.fori_loop(0, dil * nsub, unit, 0)
    if dil > 1:
        for hf in range(halves):
            o_ref[0, :, hf * LANES:(hf + 1) * LANES] = o_st[hf]
            lse_ref[0, :, hf * LANES:(hf + 1) * LANES] = lse_st[hf]


def _swa_prompt(q, k, v, g, dil):
    b, s, _ = q.shape
    tb = _SWA_TOKENS
    back = SWA_BLOCK * dil
    assert s % tb == 0 and tb % back == 0
    cur = pl.BlockSpec((1, tb, SWA_OUT), lambda bi, i: (bi, i, g))
    prev = pl.BlockSpec((1, back, SWA_OUT), lambda bi, i: (bi, jnp.maximum(i * (tb // back) - 1, 0), g))
    out = pl.BlockSpec((1, tb, SWA_OUT), lambda bi, i: (bi, i, 0))
    halves = SWA_OUT // LANES
    stage = [pltpu.VMEM((halves, n, LANES), F32) for n in (tb, tb, tb, back, back, tb, tb)] if dil > 1 else []
    o, lse = pl.pallas_call(
        functools.partial(_swa_kernel, dil=dil),
        grid=(b, s // tb),
        in_specs=[cur, cur, cur, prev, prev],
        out_specs=[out, out],
        out_shape=[jax.ShapeDtypeStruct((b, s, SWA_OUT), F32)] * 2,
        scratch_shapes=stage,
        compiler_params=_params(("arbitrary", "arbitrary")),
        name=f"swa_prompt_g{g}",
    )(q, k, v, k, v)
    return o.reshape(b * s, SWA_OUT), lse.reshape(b * s, SWA_OUT)


_SAMPLE_ROWS = 16


def _swa_sample_kernel(q_ref, kn_ref, vn_ref, knt_ref, vnt_ref, k0_ref, v0_ref, k1_ref, v1_ref, k2_ref, v2_ref,
                       ob_ref, ok0_ref, ov0_ref, ok1_ref, ov1_ref, ok2_ref, ov2_ref, *, n_new):
    rows = _SAMPLE_ROWS
    in_refs = ((k0_ref, v0_ref), (k1_ref, v1_ref), (k2_ref, v2_ref))
    out_refs = ((ok0_ref, ov0_ref), (ok1_ref, ov1_ref), (ok2_ref, ov2_ref))
    scale = SWA_HEAD_DIM ** -0.5
    jn = lax.broadcasted_iota(jnp.int32, (rows, rows), 1)
    ln = lax.broadcasted_iota(jnp.int32, (rows, rows), 0)
    tail = lax.broadcasted_iota(jnp.int32, (SWA_HEAD_DIM, LANES), 1) >= LANES - n_new
    o_g, lse_g = [], []
    for g, (win, dil) in enumerate(SWA_GROUPS):
        jc = lax.broadcasted_iota(jnp.int32, (rows, win), 1)
        lc = lax.broadcasted_iota(jnp.int32, (rows, win), 0)
        valid_c = (jc >= lc) & (((jc - lc) & (dil - 1)) == 0)
        valid_n = (jn <= ln) & (((ln - jn) & (dil - 1)) == 0) & (jn < n_new)
        o_h, lse_h = [], []
        for h in range(SWA_HPG):
            col = g * SWA_OUT + h * SWA_HEAD_DIM
            hsl = slice(col, col + SWA_HEAD_DIM)
            qh = q_ref[0, :, hsl]
            knh = kn_ref[0, :, hsl]
            vnh = vn_ref[0, :, hsl]
            for (src, dst, new_t) in ((in_refs[g][0], out_refs[g][0], knt_ref), (in_refs[g][1], out_refs[g][1], vnt_ref)):
                old = src[0, h]
                moved = pltpu.roll(old, win - n_new, 1)
                if win > LANES:
                    dst[0, h, :, 0:win - LANES] = moved[:, 0:win - LANES]
                dst[0, h, :, win - LANES:win] = jnp.where(tail, new_t[0, hsl, :], moved[:, win - LANES:win])
            kt = in_refs[g][0][0, h]
            vt = in_refs[g][1][0, h]
            s_c = jnp.where(valid_c, _dot(qh, kt) * scale, NEG)
            s_n = jnp.where(valid_n, _dot_nt(qh, knh) * scale, NEG)
            m = jnp.maximum(jnp.max(s_c, axis=-1, keepdims=True), jnp.max(s_n, axis=-1, keepdims=True))
            p_c = jnp.exp(s_c - m)
            p_n = jnp.exp(s_n - m)
            den = jnp.sum(p_c, axis=-1, keepdims=True) + jnp.sum(p_n, axis=-1, keepdims=True)
            o_h.append((_dot_nt(p_c, vt) + _dot(p_n, vnh)) / den)
            lse_h.append(jnp.broadcast_to(m + jnp.log(den), (rows, SWA_HEAD_DIM)))
        o_g.append(jnp.concatenate(o_h, axis=1))
        lse_g.append(jnp.concatenate(lse_h, axis=1))
    lmax = jnp.maximum(jnp.maximum(lse_g[0], lse_g[1]), lse_g[2])
    e = [jnp.exp(x - lmax) for x in lse_g]
    ob_ref[0] = (e[0] * o_g[0] + e[1] * o_g[1] + e[2] * o_g[2]) / (e[0] + e[1] + e[2])


def _swa_sample(q, kn, vn, knt, vnt, caches_t, n_new):
    bd = q.shape[0]
    rows = _SAMPLE_ROWS
    tok = pl.BlockSpec((1, rows, SWA_WIDTH), lambda bi: (bi, 0, 0))
    new_t = pl.BlockSpec((1, SWA_WIDTH, LANES), lambda bi: (bi, 0, 0))
    specs, args = [tok, tok, tok, new_t, new_t], [q, kn, vn, knt, vnt]
    out_specs = [pl.BlockSpec((1, rows, SWA_OUT), lambda bi: (bi, 0, 0))]
    out_shape = [jax.ShapeDtypeStruct((bd, rows, SWA_OUT), F32)]
    for g, (win, dil) in enumerate(SWA_GROUPS):
        for t in caches_t[g]:
            assert t.shape == (bd, SWA_HPG, SWA_HEAD_DIM, win) and win == SWA_BLOCK * dil and win % LANES == 0
            spec = pl.BlockSpec((1, SWA_HPG, SWA_HEAD_DIM, win), lambda bi: (bi, 0, 0, 0))
            args.append(t)
            specs.append(spec)
            out_specs.append(spec)
            out_shape.append(jax.ShapeDtypeStruct(t.shape, t.dtype))
    return pl.pallas_call(
        functools.partial(_swa_sample_kernel, n_new=n_new),
        grid=(bd,),
        in_specs=specs,
        out_specs=out_specs,
        out_shape=out_shape,
        compiler_params=_params(("arbitrary",)),
        name="swa_sample",
    )(*args)


def _merge_kernel(*refs, combine, n_alias, n_real):
    x1_ref, h2c_ref, lg_ref = refs[-3:]
    refs = refs[:len(refs) - 3 - n_alias]

    @pl.when(pl.program_id(0) >= n_real)
    def _():
        h2c_ref[...] = jnp.zeros_like(h2c_ref)
        lg_ref[...] = jnp.zeros_like(lg_ref)

    pl.when(pl.program_id(0) < n_real)(functools.partial(_merge_tile, refs, x1_ref, h2c_ref, lg_ref, combine))


def _merge_tile(refs, x1_ref, h2c_ref, lg_ref, combine):
    if combine:
        (x_ref, ya_ref, o0, o1, o2, l0, l1, l2, sgab_ref, wa_ref, wb_ref, wo_ref, ln_ref, wr_hi_ref, wr_lo_ref,
         br_ref) = refs
        lmax = jnp.maximum(jnp.maximum(l0[...], l1[...]), l2[...])
        e0, e1, e2 = jnp.exp(l0[...] - lmax), jnp.exp(l1[...] - lmax), jnp.exp(l2[...] - lmax)
        ob = ((e0 * o0[...] + e1 * o1[...] + e2 * o2[...]) / (e0 + e1 + e2)).astype(wb_ref.dtype)
    else:
        (x_ref, ya_ref, ob_ref, sgab_ref, wa_ref, wb_ref, wo_ref, ln_ref, wr_hi_ref, wr_lo_ref, br_ref) = refs
        ob = ob_ref[...]
    ya = _dot(ya_ref[...], wa_ref[...])
    yb = _dot(ob, wb_ref[...])
    sga = sgab_ref[:, :D_MODEL].astype(F32)
    sgb = sgab_ref[:, D_MODEL:].astype(F32)
    x1 = x_ref[...] + _dot((sga * ya + sgb * yb).astype(wo_ref.dtype), wo_ref[...])
    x1_ref[...] = x1
    h2 = x1 * lax.rsqrt(jnp.mean(x1 * x1, axis=-1, keepdims=True) + EPS) * ln_ref[...]
    h_hi, h_lo = _split_bf16(h2)
    lg_ref[...] = _dot(h_hi, wr_hi_ref[...]) + _dot(h_hi, wr_lo_ref[...]) + _dot(h_lo, wr_hi_ref[...]) + br_ref[...]
    h2c_ref[...] = h_hi


def _merge(x, ya_in, swa, sgab, weights, tm, tok_off, t_all, shared=None):
    t = x.shape[0]
    assert t % tm == 0 and tok_off % tm == 0 and MERGE_TILE % tm == 0
    combine = len(swa) > 1
    blk_off = tok_off // tm
    t_buf = -(-t_all // MERGE_TILE) * MERGE_TILE
    nsub = D_MODEL // LANES
    n_real = t // tm
    n_fill = 0 if shared is not None else (t_buf - tok_off - t) // tm
    row = lambda width: pl.BlockSpec((tm, width), lambda i: (jnp.minimum(i, n_real - 1), 0))
    shared_in = [] if shared is None else list(shared)
    n_in = 3 + len(swa) + len(weights)
    return pl.pallas_call(
        functools.partial(_merge_kernel, combine=combine, n_alias=len(shared_in), n_real=n_real),
        grid=(n_real + n_fill,),
        in_specs=[row(D_MODEL), row(D_MODEL)] + [row(SWA_OUT)] * len(swa) + [row(2 * D_MODEL)]
                 + [_resident(w.shape) for w in weights] + [pl.BlockSpec(memory_space=pl.ANY)] * len(shared_in),
        out_specs=[row(D_MODEL),
                   pl.BlockSpec((tm, D_MODEL), lambda i: (i + blk_off, 0)),
                   pl.BlockSpec((tm, LANES), lambda i: (i + blk_off, 0))],
        out_shape=[jax.ShapeDtypeStruct((t, D_MODEL), F32),
                   jax.ShapeDtypeStruct((t_buf, D_MODEL), BF16),
                   jax.ShapeDtypeStruct((t_buf, LANES), F32)],
        input_output_aliases={n_in + k: 1 + k for k in range(len(shared_in))},
        compiler_params=_params(("arbitrary",)),
        name="merge",
    )(x, ya_in, *swa, sgab, *weights, *shared_in)


def _router_kernel(lg_ref, mi_ref, mw_ref, cnt_ref, blk_ref, carry):
    phase = pl.program_id(0)
    i = pl.program_id(1)

    @pl.when((phase == 0) & (i == 0))
    def _():
        carry[...] = jnp.zeros_like(carry)

    @pl.when((phase == 1) & (i == 0))
    def _():
        cnt = carry[...]
        cnt_ref[...] = jnp.broadcast_to(cnt, cnt_ref.shape)
        tiles = jnp.floor((cnt + (EXPERT_TILE - 1)) * (1.0 / EXPERT_TILE))
        r = lax.broadcasted_iota(jnp.int32, (LANES, LANES), 0)
        c = lax.broadcasted_iota(jnp.int32, (LANES, LANES), 1)
        before = _dot(jnp.broadcast_to(tiles, (SUBLANES, LANES)).astype(BF16), (r < c).astype(BF16))
        carry[...] = before[0:1, :] * EXPERT_TILE

    lg = lg_ref[...]
    tr = lg.shape[0]
    lane = lax.broadcasted_iota(jnp.int32, lg.shape, 1)
    big = jnp.int32(LANES)
    gl = jnp.where(lane < N_GROUPS, lg, NEG)
    gmax = jnp.max(gl, axis=-1, keepdims=True)
    g_idx = jnp.min(jnp.where(gl == gmax, lane, big), axis=-1, keepdims=True)
    g_w = 1.0 / jnp.sum(jnp.exp(gl - gmax), axis=-1, keepdims=True)
    e_lane = lane - N_GROUPS
    in_group = (e_lane >= 0) & (e_lane < N_EXPERTS) & (e_lane // EXPERTS_PER_GROUP == g_idx)
    el = jnp.where(in_group, lg, NEG)
    v1 = jnp.max(el, axis=-1, keepdims=True)
    i1 = jnp.min(jnp.where(el == v1, lane, big), axis=-1, keepdims=True)
    el2 = jnp.where(lane == i1, NEG, el)
    v2 = jnp.max(el2, axis=-1, keepdims=True)
    i2 = jnp.min(jnp.where(el2 == v2, lane, big), axis=-1, keepdims=True)
    r21 = jnp.exp(v2 - v1)
    w1 = g_w / (1.0 + r21)
    w2 = g_w * r21 / (1.0 + r21)
    e1 = i1 - N_GROUPS
    e2 = i2 - N_GROUPS
    hot1 = lane == e1
    hot2 = lane == e2
    hot = (hot1 | hot2).astype(BF16)
    start = carry[...]
    cnt = jnp.sum(hot.astype(F32), axis=0, keepdims=True)
    carry[...] = start + cnt
    pl.when(phase == 1)(functools.partial(_router_emit, lane, e1, e2, w1, w2, hot1, hot2, hot, start, cnt,
                                          mi_ref, mw_ref, blk_ref))


def _router_emit(lane, e1, e2, w1, w2, hot1, hot2, hot, start, cnt, mi_ref, mw_ref, blk_ref):
    tr = lane.shape[0]
    r = lax.broadcasted_iota(jnp.int32, (tr, tr), 0)
    c = lax.broadcasted_iota(jnp.int32, (tr, tr), 1)
    within = _dot((r > c).astype(BF16), hot)
    lane1 = lax.broadcasted_iota(jnp.int32, (SUBLANES, LANES), 1)
    incl = jnp.broadcast_to(cnt, (SUBLANES, LANES))
    for sh in (1, 2, 4, 8, 16):
        incl = incl + jnp.where(lane1 >= sh, pltpu.roll(incl, sh, 1), 0.0)
    local = incl[0:1, :] - cnt
    pick = lambda hot_k, row: jnp.sum(jnp.where(hot_k, row, 0.0), axis=-1, keepdims=True)
    cols = [e1, e2]
    for hot_k in (hot1, hot2):
        cols.append((pick(hot_k, within) + pick(hot_k, start)).astype(jnp.int32))
    for hot_k in (hot1, hot2):
        cols.append((pick(hot_k, within) + pick(hot_k, local)).astype(jnp.int32))
    mi = cols[-1]
    for k in range(len(cols) - 2, -1, -1):
        mi = jnp.where(lane == k, cols[k], mi)
    mi_ref[...] = mi
    mw_ref[...] = jnp.where(lane == 0, w1, w2)
    sub = lax.broadcasted_iota(jnp.int32, (SUBLANES, LANES), 0)
    tbl = jnp.where(sub == 0, start, jnp.where(sub == 1, cnt, jnp.where(sub == 2, local, 0.0)))
    blk_ref[...] = tbl.astype(jnp.int32)


def _router(logits, t, tr):
    assert t % tr == 0 and tr % SUBLANES == 0
    out_row = pl.BlockSpec((tr, LANES), lambda p, i: (i * p, 0))
    return pl.pallas_call(
        _router_kernel,
        grid=(2, t // tr),
        in_specs=[pl.BlockSpec((tr, LANES), lambda p, i: (i, 0))],
        out_specs=[out_row, out_row, pl.BlockSpec((SUBLANES, LANES), lambda p, i: (0, 0)),
                   pl.BlockSpec((SUBLANES, LANES), lambda p, i: (i * p, 0))],
        out_shape=[jax.ShapeDtypeStruct((t, LANES), jnp.int32), jax.ShapeDtypeStruct((t, LANES), F32),
                   jax.ShapeDtypeStruct((SUBLANES, LANES), F32),
                   jax.ShapeDtypeStruct((t // tr * SUBLANES, LANES), jnp.int32)],
        scratch_shapes=[pltpu.VMEM((1, LANES), F32)],
        compiler_params=_params(("arbitrary", "arbitrary")),
        name="router",
    )(logits)


def _run_copies(src, dst, sem, src_row, dst_row, n_rows, wait=False):
    nsub = D_MODEL // LANES

    def copy(s0, d0, rows):
        dma = pltpu.make_async_copy(src.at[pl.ds(pl.multiple_of(s0 * nsub, nsub), rows * nsub), :],
                                    dst.at[pl.ds(pl.multiple_of(d0 * nsub, nsub), rows * nsub), :], sem)
        dma.wait() if wait else dma.start()

    def chunk(k, carry):
        copy(src_row + k * SUBLANES, dst_row + k * SUBLANES, SUBLANES)
        return carry

    n_chunks = n_rows // SUBLANES
    lax.fori_loop(0, n_chunks, chunk, 0)
    done = n_chunks * SUBLANES
    for rows in (4, 2, 1):
        @pl.when((n_rows & rows) != 0)
        def _(rows=rows, done=done):
            copy(src_row + done, dst_row + done, rows)

        done = done + (n_rows & rows)


def _dispatch_kernel(start_ref, cnt_ref, loc_ref, end_ref, tot_ref, h_ref, mi_ref, xs_hbm, stage, zeros, sem, zsem,
                     *, tm):
    i = pl.program_id(0)
    n = pl.num_programs(0)
    nsub = D_MODEL // LANES
    tb = h_ref.shape[0]
    slot = i % 2

    def wait_stage(sl):
        pltpu.make_async_copy(stage.at[sl], stage.at[sl], sem.at[sl]).wait()

    def pad_rows(wait):
        def one(e, carry):
            tot = tot_ref[e]
            n_pad = (tot + tm - 1) // tm * tm - tot
            _run_copies(zeros, xs_hbm, zsem, 0, end_ref[e] - n_pad, n_pad, wait=wait)
            return carry

        lax.fori_loop(0, end_ref.shape[0], one, 0)

        def spare(j, carry):
            dma = pltpu.make_async_copy(zeros, xs_hbm.at[pl.ds(pl.multiple_of(j * tm * nsub, nsub), tm * nsub), :], zsem)
            dma.wait() if wait else dma.start()
            return carry

        lax.fori_loop(end_ref[end_ref.shape[0] - 1] // tm, xs_hbm.shape[0] // (tm * nsub), spare, 0)

    @pl.when(i == 0)
    def _():
        zeros[...] = jnp.zeros_like(zeros)
        pad_rows(wait=False)

    @pl.when(i >= 2)
    def _():
        wait_stage(slot)

    li = mi_ref[...]
    rows_iota = lax.broadcasted_iota(jnp.int32, (tb, 2 * tb), 1)
    onehot = ((rows_iota == li[:, 4:5]) | (rows_iota == li[:, 5:6])).astype(BF16)
    srt = _dot_tn(onehot, h_ref[...])
    for c in range(nsub):
        stage[slot, pl.ds(c, 2 * tb, stride=nsub), :] = srt[:, c * LANES:(c + 1) * LANES]

    def run(e, carry):
        k = i * N_EXPERTS + e
        _run_copies(stage.at[slot], xs_hbm, sem.at[slot], loc_ref[k], start_ref[k], cnt_ref[k])
        return carry

    lax.fori_loop(0, N_EXPERTS, run, 0)

    @pl.when(i == n - 1)
    def _():
        wait_stage(slot)

        @pl.when(n >= 2)
        def _():
            wait_stage(1 - slot)

        pad_rows(wait=True)


def _dispatch(blk_start, blk_cnt, blk_loc, seg_end, seg_tot, h2, meta_i, n_slots, tb):
    t = meta_i.shape[0]
    nsub = D_MODEL // LANES
    assert t % tb == 0
    grid_spec = pltpu.PrefetchScalarGridSpec(
        num_scalar_prefetch=5,
        grid=(t // tb,),
        in_specs=[pl.BlockSpec((tb, D_MODEL), lambda i, *_: (i, 0)),
                  pl.BlockSpec((tb, LANES), lambda i, *_: (i, 0))],
        out_specs=pl.BlockSpec(memory_space=pl.ANY),
        scratch_shapes=[pltpu.VMEM((2, 2 * tb * nsub, LANES), F32),
                        pltpu.VMEM((EXPERT_TILE * nsub, LANES), F32),
                        pltpu.SemaphoreType.DMA((2,)), pltpu.SemaphoreType.DMA(())],
    )
    return pl.pallas_call(
        functools.partial(_dispatch_kernel, tm=EXPERT_TILE),
        grid_spec=grid_spec,
        out_shape=jax.ShapeDtypeStruct((n_slots * nsub, LANES), F32),
        compiler_params=_params(("arbitrary",)),
        name="dispatch",
    )(blk_start, blk_cnt, blk_loc, seg_end, seg_tot, h2, meta_i)


def _expert_kernel(te_ref, tv_ref, xs_ref, wg_ref, wu_ref, wd_ref, out_ref, wgb, wub, wdb):
    i = pl.program_id(0)
    tm = EXPERT_TILE
    nsub = D_MODEL // LANES
    valid = tv_ref[i] != 0

    @pl.when(jnp.logical_not(valid))
    def _():
        out_ref[...] = jnp.zeros_like(out_ref)

    @pl.when(valid & ((i == 0) | (te_ref[i] != te_ref[jnp.maximum(i - 1, 0)])))
    def _():
        wgb[...] = wg_ref[0, 0].astype(BF16)
        wub[...] = wu_ref[0, 0].astype(BF16)
        wdb[...] = wd_ref[0, 0].astype(BF16)

    @pl.when(valid)
    def _():
        h = jnp.concatenate([xs_ref[pl.ds(c, tm, stride=nsub), :] for c in range(nsub)], axis=1).astype(BF16)
        gate = _dot(h, wgb[...])
        up = _dot(h, wub[...])
        a = (gate * _sigmoid(gate) * up).astype(BF16)
        o = _dot(a, wdb[...])
        for c in range(nsub):
            out_ref[pl.ds(c, tm, stride=nsub), :] = o[:, c * LANES:(c + 1) * LANES]


def _experts(tile_expert, tile_valid, xs, wg, wu, wd):
    n_tiles = tile_expert.shape[0]
    tm = EXPERT_TILE
    nsub = D_MODEL // LANES
    wspec = lambda shape: pl.BlockSpec((1, 1) + shape, lambda i, te, tv: (0, te[i], 0, 0))
    last = lambda i, tv: jnp.where(tv[i] != 0, i, 0)
    grid_spec = pltpu.PrefetchScalarGridSpec(
        num_scalar_prefetch=2,
        grid=(n_tiles,),
        in_specs=[pl.BlockSpec((tm * nsub, LANES), lambda i, te, tv: (last(i, tv), 0)),
                  wspec((D_MODEL, D_EXPERT)), wspec((D_MODEL, D_EXPERT)), wspec((D_EXPERT, D_MODEL))],
        out_specs=pl.BlockSpec((tm * nsub, LANES), lambda i, te, tv: (i, 0)),
        scratch_shapes=[pltpu.VMEM((D_MODEL, D_EXPERT), BF16), pltpu.VMEM((D_MODEL, D_EXPERT), BF16),
                        pltpu.VMEM((D_EXPERT, D_MODEL), BF16)],
    )
    return pl.pallas_call(
        _expert_kernel,
        grid_spec=grid_spec,
        out_shape=jax.ShapeDtypeStruct((n_tiles * tm * nsub, LANES), F32),
        compiler_params=_params(("arbitrary",)),
        name="experts",
    )(tile_expert, tile_valid, xs, wg, wu, wd)


def _combine_kernel(p0_ref, p1_ref, x1_ref, mw_ref, rows_hbm, y_ref, g0, g1, sem, *, tc, blk_off):
    i = pl.program_id(0)
    n = pl.num_programs(0)
    nout = D_MODEL // LANES
    slot = i % 2

    def issue_all(step, sl):
        base = (step + blk_off) * tc

        def issue(j, carry):
            for pos_ref, dst, s in ((p0_ref, g0, 0), (p1_ref, g1, 1)):
                src0 = pl.multiple_of(pos_ref[base + j] * nout, nout)
                pltpu.make_async_copy(rows_hbm.at[pl.ds(src0, nout), :],
                                      dst.at[sl, pl.ds(pl.multiple_of(j * nout, nout), nout), :],
                                      sem.at[sl, s]).start()
            return carry

        lax.fori_loop(0, tc, issue, 0, unroll=4)

    @pl.when(i == 0)
    def _():
        issue_all(i, slot)

    @pl.when(i + 1 < n)
    def _():
        issue_all(i + 1, 1 - slot)

    for dst, s in ((g0, 0), (g1, 1)):
        pltpu.make_async_copy(rows_hbm.at[pl.ds(0, tc * nout), :], dst.at[slot], sem.at[slot, s]).wait()
    w0 = mw_ref[:, 0:1]
    w1 = mw_ref[:, 1:2]
    for c in range(nout):
        sl = slice(c * LANES, (c + 1) * LANES)
        y_ref[:, sl] = (x1_ref[:, sl] + w0 * g0[slot, pl.ds(c, tc, stride=nout), :]
                        + w1 * g1[slot, pl.ds(c, tc, stride=nout), :])


def _combine(pos0, pos1, x1, meta_w, rows, tc, tok_off):
    t = x1.shape[0]
    assert t % tc == 0 and tok_off % tc == 0
    blk_off = tok_off // tc
    nout = D_MODEL // LANES
    row = pl.BlockSpec((tc, D_MODEL), lambda i, p0, p1: (i, 0))
    grid_spec = pltpu.PrefetchScalarGridSpec(
        num_scalar_prefetch=2,
        grid=(t // tc,),
        in_specs=[row, pl.BlockSpec((tc, LANES), lambda i, p0, p1: (i + blk_off, 0)),
                  pl.BlockSpec(memory_space=pl.ANY)],
        out_specs=row,
        scratch_shapes=[pltpu.VMEM((2, tc * nout, LANES), F32), pltpu.VMEM((2, tc * nout, LANES), F32),
                        pltpu.SemaphoreType.DMA((2, 2))],
    )
    return pl.pallas_call(
        functools.partial(_combine_kernel, tc=tc, blk_off=blk_off),
        grid_spec=grid_spec,
        out_shape=jax.ShapeDtypeStruct((t, D_MODEL), F32),
        compiler_params=_params(("arbitrary",)),
        name="combine",
    )(pos0, pos1, x1, meta_w, rows)


def _rope_tables(pos):
    half = SWA_HEAD_DIM // 2
    inv_freq = ROPE_THETA ** (-jnp.arange(half, dtype=F32) / half)
    ang = pos.astype(F32)[:, None] * inv_freq[None, :]
    cos, sin = jnp.cos(ang), jnp.sin(ang)
    reps = LANES // SWA_HEAD_DIM
    return jnp.tile(jnp.concatenate([cos, cos], axis=1), (1, reps)), jnp.tile(jnp.concatenate([-sin, sin], axis=1), (1, reps))


def kernel(x_prompt, x_sample, state_gla, cache_swa_k0, cache_swa_v0, cache_swa_k1, cache_swa_v1, cache_swa_k2, cache_swa_v2, ln1_w, w_in, w_gla_lr, b_gla_lr, gla_onorm_w, q_norm_w, k_norm_w, w_branch_a, w_branch_b, w_out, ln2_w, w_router_group, b_router_group, w_router_expert, b_router_expert, w_exp_gate, w_exp_up, w_exp_down):
    b, s, d = x_prompt.shape
    bd, ls, _ = x_sample.shape
    tp, ts = b * s, bd * ls
    assert w_in.shape[0] == 1 and d == D_MODEL and ts % SUBLANES == 0
    k_caches = (cache_swa_k0, cache_swa_k1, cache_swa_k2)
    v_caches = (cache_swa_v0, cache_swa_v1, cache_swa_v2)

    w = w_in[0]
    cuts = np.cumsum((512, 512, 1024, GLA_RANK, 1024, 768, 768, 768, 1024, 1024))
    sec = lambda a: w[:, (0 if a == 0 else cuts[a - 1]):cuts[a]]
    lr_pad = jnp.pad(sec(3), ((0, 0), (0, LANES - GLA_RANK)))
    w_f32 = jnp.concatenate([sec(0), sec(1), sec(2), sec(4), sec(5), sec(6), sec(7), sec(8), sec(9), lr_pad], axis=1)
    w_packed = w_f32.astype(BF16)
    wlr_f32 = jnp.pad(w_gla_lr[0], ((0, LANES - GLA_RANK), (0, 0)))
    wlr = wlr_f32.astype(BF16)
    blr = b_gla_lr[0][None, :]
    nw = jnp.concatenate([jnp.tile(q_norm_w[0], SWA_WIDTH // SWA_HEAD_DIM), jnp.tile(k_norm_w[0], SWA_WIDTH // SWA_HEAD_DIM)])[None, :]
    gi = np.arange(256) // SWA_HEAD_DIM
    gmat_f32 = jnp.asarray((gi[:, None] == gi[None, :]).astype(np.float32) / SWA_HEAD_DIM)
    gmat = gmat_f32.astype(BF16)
    ln1 = ln1_w[0][None, :]
    ln2 = ln2_w[0][None, :]
    onw = gla_onorm_w[0][None, :]
    wa_f32, wb_f32, wo_f32 = w_branch_a[0], w_branch_b[0], w_out[0]
    wa, wb, wo = wa_f32.astype(BF16), wb_f32.astype(BF16), wo_f32.astype(BF16)
    wr = jnp.pad(jnp.concatenate([w_router_group[0], w_router_expert[0]], axis=1),
                 ((0, 0), (0, LANES - N_GROUPS - N_EXPERTS)))
    wr_hi, wr_lo = _split_bf16(wr)
    br = jnp.pad(jnp.concatenate([b_router_group[0], b_router_expert[0]]), (0, LANES - N_GROUPS - N_EXPERTS))[None, :]

    cos_p, sin_p = _rope_tables(jnp.arange(s, dtype=jnp.int32))
    cos_s, sin_s = _rope_tables(PAST_LEN + jnp.arange(ts, dtype=jnp.int32) % ls)

    tm_p = 512
    proj_p = _proj(x_prompt.reshape(tp, d), cos_p, sin_p, s // tm_p, ln1, w_packed, wlr, blr, nw, gmat, tm_p)
    proj_s = _proj(x_sample.reshape(ts, d), cos_s, sin_s, 1, ln1, w_f32, wlr_f32, blr, nw, gmat_f32, ts)
    gqkv_p, la_p, sog_p, q_p, k_p, v_p, sgab_p = proj_p
    gqkv_s, la_s, sog_s, q_s, k_s, v_s, sgab_s = proj_s

    r3 = lambda t, nb: t.reshape(nb, t.shape[0] // nb, t.shape[1])
    ya_p, st_p = _gla(r3(gqkv_p, b), r3(la_p, b), r3(sog_p, b),
                      jnp.zeros((b, GLA_HEADS, GLA_DK, GLA_DV), F32), onw, GLA_CHUNK, 512)
    pad_s = lambda t: jnp.pad(r3(t, bd), ((0, 0), (0, _SAMPLE_ROWS - ls), (0, 0)))
    ya_s, st_s = _gla(pad_s(gqkv_s), pad_s(la_s), pad_s(sog_s), state_gla[0], onw, _SAMPLE_ROWS, _SAMPLE_ROWS)
    ya_s = ya_s[:, :ls].reshape(ts, d)

    q3, k3, v3 = r3(q_p, b), r3(k_p, b), r3(v_p, b)
    swa_p = [_swa_prompt(q3, k3, v3, g, dil) for g, (_, dil) in enumerate(SWA_GROUPS)]
    to_t = lambda c: jnp.transpose(c[0], (0, 2, 3, 1))
    caches_t = [(to_t(k_caches[g]), to_t(v_caches[g])) for g in range(len(SWA_GROUPS))]
    new_t = lambda t: jnp.pad(jnp.transpose(r3(t, bd), (0, 2, 1)), ((0, 0), (0, 0), (LANES - ls, 0)))
    ob_s, *new_caches = _swa_sample(pad_s(q_s), pad_s(k_s), pad_s(v_s), new_t(k_s), new_t(v_s), caches_t, ls)
    ob_s = ob_s[:, :ls].reshape(ts, SWA_OUT)

    t = tp + ts
    mw = (wa, wb, wo, ln2, wr_hi, wr_lo, br)
    x1_p, h2c, lg = _merge(x_prompt.reshape(tp, d), ya_p.reshape(tp, d),
                           [o for o, _ in swa_p] + [l for _, l in swa_p], sgab_p, mw, 512, 0, t)
    mw_f32 = (wa_f32, wb_f32, wo_f32) + mw[3:]
    x1_s, h2c, lg = _merge(x_sample.reshape(ts, d), ya_s, [ob_s], sgab_s, mw_f32, ts, tp, t, shared=(h2c, lg))

    tb = DISPATCH_BLOCK
    meta_i, meta_w, cnt, blocks = _router(lg, t, tb)
    blocks = blocks.reshape(t // tb, SUBLANES, LANES)[:, :, :N_EXPERTS]
    blk_start, blk_cnt, blk_loc = (blocks[:, r].reshape(-1) for r in range(3))
    tm = EXPERT_TILE
    n_tiles = (2 * t) // tm + N_EXPERTS
    counts = cnt[0, :N_EXPERTS].astype(jnp.int32)
    ends = jnp.cumsum((counts + tm - 1) // tm * tm)
    pos0, pos1 = meta_i[:, 2], meta_i[:, 3]
    tile_start = jnp.arange(n_tiles, dtype=jnp.int32) * tm
    tile_valid = (tile_start < ends[-1]).astype(jnp.int32)
    last_slot = jnp.minimum(tile_start, ends[-1] - 1)
    tile_expert = jnp.minimum(jnp.sum((last_slot[:, None] >= ends[None, :]).astype(jnp.int32), axis=1), N_EXPERTS - 1)

    xs = _dispatch(blk_start, blk_cnt, blk_loc, ends, counts, h2c, meta_i, n_tiles * tm, tb)
    rows = _experts(tile_expert, tile_valid, xs, w_exp_gate, w_exp_up, w_exp_down)
    y_p = _combine(pos0, pos1, x1_p, meta_w, rows, 256, 0)
    y_s = _combine(pos0, pos1, x1_s, meta_w, rows, ts, tp)

    heads = lambda a: a.reshape(1, a.shape[0], a.shape[1], SWA_HPG, SWA_HEAD_DIM)
    outs = [y_p.reshape(b, s, d), y_s.reshape(bd, ls, d), st_p[None].astype(x_prompt.dtype)]
    for g, (win, _) in enumerate(SWA_GROUPS):
        keep = min(win, s)
        gsl = slice(g * SWA_OUT, (g + 1) * SWA_OUT)
        outs += [heads(k3[:, s - keep:, gsl]), heads(v3[:, s - keep:, gsl])]
    outs.append(st_s[None].astype(state_gla.dtype))
    outs += [jnp.transpose(c, (0, 3, 1, 2))[None] for c in new_caches]
    return tuple(outs)
```

```python
import functools

import numpy as np
import jax
import jax.numpy as jnp
from jax import lax
from jax.experimental import pallas as pl
from jax.experimental.pallas import tpu as pltpu

F32 = jnp.float32
BF16 = jnp.bfloat16

D_MODEL = 1024
PAST_LEN = 16384
GLA_HEADS = 4
GLA_DK = 128
GLA_DV = 256
GLA_RANK = 16
GLA_TAU = 16.0
GLA_CHUNK = 64
SWA_GROUPS = ((128, 1), (512, 4), (2048, 16))
SWA_HPG = 4
SWA_HEAD_DIM = 64
SWA_WIDTH = 768
SWA_OUT = 256
SWA_BLOCK = 128
ROPE_THETA = 10000.0
N_GROUPS = 4
EXPERTS_PER_GROUP = 8
N_EXPERTS = 32
D_EXPERT = 512
EPS = 1e-6

LANES = 128
SUBLANES = 8
VMEM_LIMIT = 56 * 1024 * 1024
NEG = -1e30
MXU_WIDTH = 256
EXPERT_TILE = 256
MERGE_TILE = 512
DISPATCH_BLOCK = 384

_C_GQKV = (0, 2048)
_C_GOG = (2048, 3072)
_C_QK = (3072, 4608)
_C_V = (4608, 5376)
_C_GAB = (5376, 7424)
_C_LR = (7424, 7552)


def _contract(a, b, dims):
    dg = lambda x, y: lax.dot_general(x, y, (dims, ((), ())), preferred_element_type=F32)
    if a.dtype == F32 and b.dtype == F32:
        a_hi, a_lo = _split(a, BF16)
        b_hi, b_lo = _split(b, BF16)
        return dg(a_hi, b_hi) + dg(a_hi, b_lo) + dg(a_lo, b_hi)
    return dg(a, b)


def _dot(a, b):
    return _contract(a, b, ((1,), (0,)))


def _dot_nt(a, b):
    return _contract(a, b, ((1,), (1,)))


def _dot_tn(a, b):
    return _contract(a, b, ((0,), (0,)))


def _sigmoid(x):
    return 1.0 / (1.0 + jnp.exp(-x))


def _split(x, dt):
    hi = x.astype(dt)
    lo = (x - hi.astype(F32)).astype(dt)
    return hi, lo


def _split_bf16(x):
    return _split(x, BF16)


def _params(sem):
    return pltpu.CompilerParams(dimension_semantics=sem, vmem_limit_bytes=VMEM_LIMIT)


def _resident(shape):
    nd = len(shape)
    return pl.BlockSpec(shape, lambda *_: (0,) * nd, pipeline_mode=pl.Buffered(1))


def _proj_kernel(x_ref, cos_ref, sin_ref, ln_ref, w_ref, wlr_ref, blr_ref, nw_ref, g_ref,
                 gqkv_ref, la_ref, sog_ref, q_ref, k_ref, v_ref, sgab_ref):
    cdt = w_ref.dtype
    x = x_ref[...]
    h = (x * lax.rsqrt(jnp.mean(x * x, axis=-1, keepdims=True) + EPS) * ln_ref[...]).astype(cdt)
    gqkv_ref[...] = _dot(h, w_ref[:, _C_GQKV[0]:_C_GQKV[1]]).astype(cdt)
    og = _dot(h, w_ref[:, _C_GOG[0]:_C_GOG[1]])
    sog_ref[...] = (og * _sigmoid(og)).astype(cdt)
    lr = _dot(h, w_ref[:, _C_LR[0]:_C_LR[1]]).astype(cdt)
    z = _dot(lr, wlr_ref[...]) + blr_ref[...]
    la_ref[...] = (jnp.minimum(z, 0.0) - jnp.log(1.0 + jnp.exp(-jnp.abs(z)))) / GLA_TAU
    qk = _dot(h, w_ref[:, _C_QK[0]:_C_QK[1]])
    sq = (qk * qk).astype(cdt)
    ms = jnp.concatenate([_dot(sq[:, c * 256:(c + 1) * 256], g_ref[...]) for c in range(6)], axis=1)
    qn = qk * lax.rsqrt(ms + EPS) * nw_ref[...]
    width = 2 * SWA_WIDTH
    cos = jnp.tile(cos_ref[...], (1, width // LANES))
    sin = jnp.tile(sin_ref[...], (1, width // LANES))
    lane = lax.broadcasted_iota(jnp.int32, qn.shape, 1)
    half = SWA_HEAD_DIM // 2
    rot = jnp.where(lane % SWA_HEAD_DIM < half, pltpu.roll(qn, width - half, 1), pltpu.roll(qn, half, 1))
    qr = qn * cos + rot * sin
    q_ref[...] = qr[:, :SWA_WIDTH]
    k_ref[...] = qr[:, SWA_WIDTH:]
    v_ref[...] = _dot(h, w_ref[:, _C_V[0]:_C_V[1]])
    gab = _dot(h, w_ref[:, _C_GAB[0]:_C_GAB[1]])
    sgab_ref[...] = _sigmoid(gab).astype(cdt)


def _proj(x, cos, sin, rope_blocks, ln, w, wlr, blr, nw, g, tm):
    t = x.shape[0]
    assert t % tm == 0 and w.dtype == wlr.dtype == g.dtype
    row = lambda width: pl.BlockSpec((tm, width), lambda i: (i, 0))
    outs = [(2048, w.dtype), (512, F32), (1024, w.dtype), (768, F32), (768, F32), (768, F32), (2048, w.dtype)]
    return pl.pallas_call(
        _proj_kernel,
        grid=(t // tm,),
        in_specs=[row(D_MODEL),
                  pl.BlockSpec((tm, LANES), lambda i: (i % rope_blocks, 0)),
                  pl.BlockSpec((tm, LANES), lambda i: (i % rope_blocks, 0)),
                  _resident(ln.shape), _resident(w.shape), _resident(wlr.shape), _resident(blr.shape),
                  _resident(nw.shape), _resident(g.shape)],
        out_specs=[row(wd) for wd, _ in outs],
        out_shape=[jax.ShapeDtypeStruct((t, wd), dt) for wd, dt in outs],
        compiler_params=_params(("arbitrary",)),
        name="proj",
    )(x, cos, sin, ln, w, wlr, blr, nw, g)


def _gla_kernel(gqkv_ref, la_ref, sog_ref, s0_ref, onw_ref, y_ref, sfin_ref, s_scr, *, chunk, n_chunks):
    j = pl.program_id(1)

    @pl.when(j == 0)
    def _():
        s_scr[...] = s0_ref[0]

    r = lax.broadcasted_iota(jnp.int32, (chunk, chunk), 0)
    c = lax.broadcasted_iota(jnp.int32, (chunk, chunk), 1)
    causal = r >= c
    cdt = gqkv_ref.dtype
    tri = causal.astype(cdt)
    ones = jnp.ones((chunk, GLA_DV), cdt)
    hk = GLA_HEADS * GLA_DK

    def body(ci, carry):
        r0 = pl.multiple_of(ci * chunk, chunk)
        blk = gqkv_ref[0, pl.ds(r0, chunk), :]
        la_hi, la_lo = _split(la_ref[0, pl.ds(r0, chunk), :], cdt)
        b = _dot(tri, la_hi) + _dot(tri, la_lo)
        blast = b[chunk - 1:chunk, :]
        q = blk[:, :hk].astype(F32) * GLA_DK ** -0.5
        k = blk[:, hk:2 * hk].astype(F32)
        qd = (q * jnp.exp(b)).astype(cdt)
        kd = (k * jnp.exp(-b)).astype(cdt)
        kdec = (k * jnp.exp(blast - b)).astype(cdt)
        outs = []
        for h in range(GLA_HEADS):
            sl = slice(h * GLA_DK, (h + 1) * GLA_DK)
            v_h = blk[:, 2 * hk + h * GLA_DV:2 * hk + (h + 1) * GLA_DV]
            att = jnp.where(causal, _dot_nt(qd[:, sl], kd[:, sl]), 0.0).astype(cdt)
            s_prev = s_scr[h]
            o = _dot(att, v_h) + _dot(qd[:, sl], s_prev.astype(cdt))
            dsum = _dot_tn(la_hi[:, sl], ones) + _dot_tn(la_lo[:, sl], ones)
            s_scr[h] = s_prev * jnp.exp(dsum) + _dot_tn(kdec[:, sl], v_h)
            ms = jnp.mean(o * o, axis=-1, keepdims=True)
            outs.append(o * lax.rsqrt(ms + EPS) * onw_ref[...])
        o_all = jnp.concatenate(outs, axis=1) * sog_ref[0, pl.ds(r0, chunk), :].astype(F32)
        y_ref[0, pl.ds(r0, chunk), :] = o_all.astype(cdt)
        return carry

    lax.fori_loop(0, n_chunks, body, 0, unroll=min(n_chunks, 2))

    @pl.when(j == pl.num_programs(1) - 1)
    def _():
        sfin_ref[0] = s_scr[...]


def _gla(gqkv, la, sog, s0, onw, chunk, block):
    b, l, _ = gqkv.shape
    tok = lambda width: pl.BlockSpec((1, block, width), lambda bi, j: (bi, j, 0))
    st = pl.BlockSpec((1, GLA_HEADS, GLA_DK, GLA_DV), lambda bi, j: (bi, 0, 0, 0))
    return pl.pallas_call(
        functools.partial(_gla_kernel, chunk=chunk, n_chunks=block // chunk),
        grid=(b, l // block),
        in_specs=[tok(2048), tok(512), tok(1024), st, _resident(onw.shape)],
        out_specs=[tok(1024), st],
        out_shape=[jax.ShapeDtypeStruct((b, l, 1024), gqkv.dtype),
                   jax.ShapeDtypeStruct((b, GLA_HEADS, GLA_DK, GLA_DV), F32)],
        scratch_shapes=[pltpu.VMEM((GLA_HEADS, GLA_DK, GLA_DV), F32)],
        compiler_params=_params(("arbitrary", "arbitrary")),
        name="gla",
    )(gqkv, la, sog, s0, onw)


def _band_heads(q, kw, vw, valid):
    outs, lses = [], []
    for h in range(SWA_HPG):
        sl = slice(h * SWA_HEAD_DIM, (h + 1) * SWA_HEAD_DIM)
        s = _dot_nt(q[:, sl], kw[:, sl]) * SWA_HEAD_DIM ** -0.5
        s = jnp.where(valid, s, NEG)
        m = jnp.max(s, axis=-1, keepdims=True)
        p = jnp.exp(s - m)
        l = jnp.sum(p, axis=-1, keepdims=True)
        outs.append(_dot(p.astype(BF16), vw[:, sl]) / l)
        lses.append(jnp.broadcast_to(m + jnp.log(l), (q.shape[0], SWA_HEAD_DIM)))
    return jnp.concatenate(outs, axis=1), jnp.concatenate(lses, axis=1)


_SWA_TOKENS = 2048


def _swa_kernel(q_ref, k_ref, v_ref, kp_ref, vp_ref, o_ref, lse_ref, *stage, dil):
    blk = SWA_BLOCK
    nsub = q_ref.shape[1] // (blk * dil)
    first = pl.program_id(1) == 0
    qi = lax.broadcasted_iota(jnp.int32, (blk, 2 * blk), 0)
    kj = lax.broadcasted_iota(jnp.int32, (blk, 2 * blk), 1)
    band = (kj >= qi) & (kj <= qi + blk)
    halves = SWA_OUT // LANES

    if dil > 1:
        ins = (q_ref, k_ref, v_ref, kp_ref, vp_ref)
        q_ref, k_ref, v_ref, kp_ref, vp_ref, o_st, lse_st = stage
        for src, dst in zip(ins, stage):
            for hf in range(halves):
                dst[hf] = src[0, :, hf * LANES:(hf + 1) * LANES]

    def rows(ref, start):
        if dil == 1:
            return ref[0, pl.ds(start, blk), :]
        return jnp.concatenate([ref[hf, pl.ds(start, blk, stride=dil), :] for hf in range(halves)], axis=1)

    def unit(u, carry):
        r = u // nsub
        j = u % nsub
        start = r + dil * blk * j
        inside = r + dil * blk * jnp.maximum(j - 1, 0)
        if dil == 1:
            r, start, inside = 0, pl.multiple_of(start, blk), pl.multiple_of(inside, blk)
        head = j == 0
        kprev = jnp.where(head, rows(kp_ref, r), rows(k_ref, inside))
        vprev = jnp.where(head, rows(vp_ref, r), rows(v_ref, inside))
        kw = jnp.concatenate([kprev, rows(k_ref, start)], axis=0).astype(BF16)
        vw = jnp.concatenate([vprev, rows(v_ref, start)], axis=0).astype(BF16)
        valid = band & (kj >= jnp.where(head & first, blk, 0))
        o, lse = _band_heads(rows(q_ref, start).astype(BF16), kw, vw, valid)
        if dil == 1:
            o_ref[0, pl.ds(start, blk), :] = o
            lse_ref[0, pl.ds(start, blk), :] = lse
        else:
            for hf in range(halves):
                o_st[hf, pl.ds(start, blk, stride=dil), :] = o[:, hf * LANES:(hf + 1) * LANES]
                lse_st[hf, pl.ds(start, blk, stride=dil), :] = lse[:, hf * LANES:(hf + 1) * LANES]
        return carry

    lax.fori_loop(0, dil * nsub, unit, 0)
    if dil > 1:
        for hf in range(halves):
            o_ref[0, :, hf * LANES:(hf + 1) * LANES] = o_st[hf]
            lse_ref[0, :, hf * LANES:(hf + 1) * LANES] = lse_st[hf]


def _swa_prompt(q, k, v, g, dil):
    b, s, _ = q.shape
    tb = _SWA_TOKENS
    back = SWA_BLOCK * dil
    assert s % tb == 0 and tb % back == 0
    cur = pl.BlockSpec((1, tb, SWA_OUT), lambda bi, i: (bi, i, g))
    prev = pl.BlockSpec((1, back, SWA_OUT), lambda bi, i: (bi, jnp.maximum(i * (tb // back) - 1, 0), g))
    out = pl.BlockSpec((1, tb, SWA_OUT), lambda bi, i: (bi, i, 0))
    halves = SWA_OUT // LANES
    stage = [pltpu.VMEM((halves, n, LANES), F32) for n in (tb, tb, tb, back, back, tb, tb)] if dil > 1 else []
    o, lse = pl.pallas_call(
        functools.partial(_swa_kernel, dil=dil),
        grid=(b, s // tb),
        in_specs=[cur, cur, cur, prev, prev],
        out_specs=[out, out],
        out_shape=[jax.ShapeDtypeStruct((b, s, SWA_OUT), F32)] * 2,
        scratch_shapes=stage,
        compiler_params=_params(("arbitrary", "arbitrary")),
        name=f"swa_prompt_g{g}",
    )(q, k, v, k, v)
    return o.reshape(b * s, SWA_OUT), lse.reshape(b * s, SWA_OUT)


_SAMPLE_ROWS = 16


def _swa_sample_kernel(q_ref, kn_ref, vn_ref, knt_ref, vnt_ref, k0_ref, v0_ref, k1_ref, v1_ref, k2_ref, v2_ref,
                       ob_ref, ok0_ref, ov0_ref, ok1_ref, ov1_ref, ok2_ref, ov2_ref, *, n_new):
    rows = _SAMPLE_ROWS
    in_refs = ((k0_ref, v0_ref), (k1_ref, v1_ref), (k2_ref, v2_ref))
    out_refs = ((ok0_ref, ov0_ref), (ok1_ref, ov1_ref), (ok2_ref, ov2_ref))
    scale = SWA_HEAD_DIM ** -0.5
    jn = lax.broadcasted_iota(jnp.int32, (rows, rows), 1)
    ln = lax.broadcasted_iota(jnp.int32, (rows, rows), 0)
    tail = lax.broadcasted_iota(jnp.int32, (SWA_HEAD_DIM, LANES), 1) >= LANES - n_new
    o_g, lse_g = [], []
    for g, (win, dil) in enumerate(SWA_GROUPS):
        jc = lax.broadcasted_iota(jnp.int32, (rows, win), 1)
        lc = lax.broadcasted_iota(jnp.int32, (rows, win), 0)
        valid_c = (jc >= lc) & (((jc - lc) & (dil - 1)) == 0)
        valid_n = (jn <= ln) & (((ln - jn) & (dil - 1)) == 0) & (jn < n_new)
        o_h, lse_h = [], []
        for h in range(SWA_HPG):
            col = g * SWA_OUT + h * SWA_HEAD_DIM
            hsl = slice(col, col + SWA_HEAD_DIM)
            qh = q_ref[0, :, hsl]
            knh = kn_ref[0, :, hsl]
            vnh = vn_ref[0, :, hsl]
            for (src, dst, new_t) in ((in_refs[g][0], out_refs[g][0], knt_ref), (in_refs[g][1], out_refs[g][1], vnt_ref)):
                old = src[0, h]
                moved = pltpu.roll(old, win - n_new, 1)
                if win > LANES:
                    dst[0, h, :, 0:win - LANES] = moved[:, 0:win - LANES]
                dst[0, h, :, win - LANES:win] = jnp.where(tail, new_t[0, hsl, :], moved[:, win - LANES:win])
            kt = in_refs[g][0][0, h]
            vt = in_refs[g][1][0, h]
            s_c = jnp.where(valid_c, _dot(qh, kt) * scale, NEG)
            s_n = jnp.where(valid_n, _dot_nt(qh, knh) * scale, NEG)
            m = jnp.maximum(jnp.max(s_c, axis=-1, keepdims=True), jnp.max(s_n, axis=-1, keepdims=True))
            p_c = jnp.exp(s_c - m)
            p_n = jnp.exp(s_n - m)
            den = jnp.sum(p_c, axis=-1, keepdims=True) + jnp.sum(p_n, axis=-1, keepdims=True)
            o_h.append((_dot_nt(p_c, vt) + _dot(p_n, vnh)) / den)
            lse_h.append(jnp.broadcast_to(m + jnp.log(den), (rows, SWA_HEAD_DIM)))
        o_g.append(jnp.concatenate(o_h, axis=1))
        lse_g.append(jnp.concatenate(lse_h, axis=1))
    lmax = jnp.maximum(jnp.maximum(lse_g[0], lse_g[1]), lse_g[2])
    e = [jnp.exp(x - lmax) for x in lse_g]
    ob_ref[0] = (e[0] * o_g[0] + e[1] * o_g[1] + e[2] * o_g[2]) / (e[0] + e[1] + e[2])


def _swa_sample(q, kn, vn, knt, vnt, caches_t, n_new):
    bd = q.shape[0]
    rows = _SAMPLE_ROWS
    tok = pl.BlockSpec((1, rows, SWA_WIDTH), lambda bi: (bi, 0, 0))
    new_t = pl.BlockSpec((1, SWA_WIDTH, LANES), lambda bi: (bi, 0, 0))
    specs, args = [tok, tok, tok, new_t, new_t], [q, kn, vn, knt, vnt]
    out_specs = [pl.BlockSpec((1, rows, SWA_OUT), lambda bi: (bi, 0, 0))]
    out_shape = [jax.ShapeDtypeStruct((bd, rows, SWA_OUT), F32)]
    for g, (win, dil) in enumerate(SWA_GROUPS):
        for t in caches_t[g]:
            assert t.shape == (bd, SWA_HPG, SWA_HEAD_DIM, win) and win == SWA_BLOCK * dil and win % LANES == 0
            spec = pl.BlockSpec((1, SWA_HPG, SWA_HEAD_DIM, win), lambda bi: (bi, 0, 0, 0))
            args.append(t)
            specs.append(spec)
            out_specs.append(spec)
            out_shape.append(jax.ShapeDtypeStruct(t.shape, t.dtype))
    return pl.pallas_call(
        functools.partial(_swa_sample_kernel, n_new=n_new),
        grid=(bd,),
        in_specs=specs,
        out_specs=out_specs,
        out_shape=out_shape,
        compiler_params=_params(("arbitrary",)),
        name="swa_sample",
    )(*args)


def _merge_kernel(*refs, combine, n_alias, n_real):
    x1_ref, h2c_ref, lg_ref = refs[-3:]
    refs = refs[:len(refs) - 3 - n_alias]

    @pl.when(pl.program_id(0) >= n_real)
    def _():
        h2c_ref[...] = jnp.zeros_like(h2c_ref)
        lg_ref[...] = jnp.zeros_like(lg_ref)

    pl.when(pl.program_id(0) < n_real)(functools.partial(_merge_tile, refs, x1_ref, h2c_ref, lg_ref, combine))


def _merge_tile(refs, x1_ref, h2c_ref, lg_ref, combine):
    if combine:
        (x_ref, ya_ref, o0, o1, o2, l0, l1, l2, sgab_ref, wa_ref, wb_ref, wo_ref, ln_ref, wr_hi_ref, wr_lo_ref,
         br_ref) = refs
        lmax = jnp.maximum(jnp.maximum(l0[...], l1[...]), l2[...])
        e0, e1, e2 = jnp.exp(l0[...] - lmax), jnp.exp(l1[...] - lmax), jnp.exp(l2[...] - lmax)
        ob = ((e0 * o0[...] + e1 * o1[...] + e2 * o2[...]) / (e0 + e1 + e2)).astype(wb_ref.dtype)
    else:
        (x_ref, ya_ref, ob_ref, sgab_ref, wa_ref, wb_ref, wo_ref, ln_ref, wr_hi_ref, wr_lo_ref, br_ref) = refs
        ob = ob_ref[...]
    ya = _dot(ya_ref[...], wa_ref[...])
    yb = _dot(ob, wb_ref[...])
    sga = sgab_ref[:, :D_MODEL].astype(F32)
    sgb = sgab_ref[:, D_MODEL:].astype(F32)
    x1 = x_ref[...] + _dot((sga * ya + sgb * yb).astype(wo_ref.dtype), wo_ref[...])
    x1_ref[...] = x1
    h2 = x1 * lax.rsqrt(jnp.mean(x1 * x1, axis=-1, keepdims=True) + EPS) * ln_ref[...]
    h_hi, h_lo = _split_bf16(h2)
    lg_ref[...] = _dot(h_hi, wr_hi_ref[...]) + _dot(h_hi, wr_lo_ref[...]) + _dot(h_lo, wr_hi_ref[...]) + br_ref[...]
    h2c_ref[...] = h_hi


def _merge(x, ya_in, swa, sgab, weights, tm, tok_off, t_all, shared=None):
    t = x.shape[0]
    assert t % tm == 0 and tok_off % tm == 0 and MERGE_TILE % tm == 0
    combine = len(swa) > 1
    blk_off = tok_off // tm
    t_buf = -(-t_all // MERGE_TILE) * MERGE_TILE
    nsub = D_MODEL // LANES
    n_real = t // tm
    n_fill = 0 if shared is not None else (t_buf - tok_off - t) // tm
    row = lambda width: pl.BlockSpec((tm, width), lambda i: (jnp.minimum(i, n_real - 1), 0))
    shared_in = [] if shared is None else list(shared)
    n_in = 3 + len(swa) + len(weights)
    return pl.pallas_call(
        functools.partial(_merge_kernel, combine=combine, n_alias=len(shared_in), n_real=n_real),
        grid=(n_real + n_fill,),
        in_specs=[row(D_MODEL), row(D_MODEL)] + [row(SWA_OUT)] * len(swa) + [row(2 * D_MODEL)]
                 + [_resident(w.shape) for w in weights] + [pl.BlockSpec(memory_space=pl.ANY)] * len(shared_in),
        out_specs=[row(D_MODEL),
                   pl.BlockSpec((tm, D_MODEL), lambda i: (i + blk_off, 0)),
                   pl.BlockSpec((tm, LANES), lambda i: (i + blk_off, 0))],
        out_shape=[jax.ShapeDtypeStruct((t, D_MODEL), F32),
                   jax.ShapeDtypeStruct((t_buf, D_MODEL), BF16),
                   jax.ShapeDtypeStruct((t_buf, LANES), F32)],
        input_output_aliases={n_in + k: 1 + k for k in range(len(shared_in))},
        compiler_params=_params(("arbitrary",)),
        name="merge",
    )(x, ya_in, *swa, sgab, *weights, *shared_in)


def _router_kernel(lg_ref, mi_ref, mw_ref, cnt_ref, blk_ref, carry):
    phase = pl.program_id(0)
    i = pl.program_id(1)

    @pl.when((phase == 0) & (i == 0))
    def _():
        carry[...] = jnp.zeros_like(carry)

    @pl.when((phase == 1) & (i == 0))
    def _():
        cnt = carry[...]
        cnt_ref[...] = jnp.broadcast_to(cnt, cnt_ref.shape)
        tiles = jnp.floor((cnt + (EXPERT_TILE - 1)) * (1.0 / EXPERT_TILE))
        r = lax.broadcasted_iota(jnp.int32, (LANES, LANES), 0)
        c = lax.broadcasted_iota(jnp.int32, (LANES, LANES), 1)
        before = _dot(jnp.broadcast_to(tiles, (SUBLANES, LANES)).astype(BF16), (r < c).astype(BF16))
        carry[...] = before[0:1, :] * EXPERT_TILE

    lg = lg_ref[...]
    tr = lg.shape[0]
    lane = lax.broadcasted_iota(jnp.int32, lg.shape, 1)
    big = jnp.int32(LANES)
    gl = jnp.where(lane < N_GROUPS, lg, NEG)
    gmax = jnp.max(gl, axis=-1, keepdims=True)
    g_idx = jnp.min(jnp.where(gl == gmax, lane, big), axis=-1, keepdims=True)
    g_w = 1.0 / jnp.sum(jnp.exp(gl - gmax), axis=-1, keepdims=True)
    e_lane = lane - N_GROUPS
    in_group = (e_lane >= 0) & (e_lane < N_EXPERTS) & (e_lane // EXPERTS_PER_GROUP == g_idx)
    el = jnp.where(in_group, lg, NEG)
    v1 = jnp.max(el, axis=-1, keepdims=True)
    i1 = jnp.min(jnp.where(el == v1, lane, big), axis=-1, keepdims=True)
    el2 = jnp.where(lane == i1, NEG, el)
    v2 = jnp.max(el2, axis=-1, keepdims=True)
    i2 = jnp.min(jnp.where(el2 == v2, lane, big), axis=-1, keepdims=True)
    r21 = jnp.exp(v2 - v1)
    w1 = g_w / (1.0 + r21)
    w2 = g_w * r21 / (1.0 + r21)
    e1 = i1 - N_GROUPS
    e2 = i2 - N_GROUPS
    hot1 = lane == e1
    hot2 = lane == e2
    hot = (hot1 | hot2).astype(BF16)
    start = carry[...]
    cnt = jnp.sum(hot.astype(F32), axis=0, keepdims=True)
    carry[...] = start + cnt
    pl.when(phase == 1)(functools.partial(_router_emit, lane, e1, e2, w1, w2, hot1, hot2, hot, start, cnt,
                                          mi_ref, mw_ref, blk_ref))


def _router_emit(lane, e1, e2, w1, w2, hot1, hot2, hot, start, cnt, mi_ref, mw_ref, blk_ref):
    tr = lane.shape[0]
    r = lax.broadcasted_iota(jnp.int32, (tr, tr), 0)
    c = lax.broadcasted_iota(jnp.int32, (tr, tr), 1)
    within = _dot((r > c).astype(BF16), hot)
    lane1 = lax.broadcasted_iota(jnp.int32, (SUBLANES, LANES), 1)
    incl = jnp.broadcast_to(cnt, (SUBLANES, LANES))
    for sh in (1, 2, 4, 8, 16):
        incl = incl + jnp.where(lane1 >= sh, pltpu.roll(incl, sh, 1), 0.0)
    local = incl[0:1, :] - cnt
    pick = lambda hot_k, row: jnp.sum(jnp.where(hot_k, row, 0.0), axis=-1, keepdims=True)
    cols = [e1, e2]
    for hot_k in (hot1, hot2):
        cols.append((pick(hot_k, within) + pick(hot_k, start)).astype(jnp.int32))
    for hot_k in (hot1, hot2):
        cols.append((pick(hot_k, within) + pick(hot_k, local)).astype(jnp.int32))
    mi = cols[-1]
    for k in range(len(cols) - 2, -1, -1):
        mi = jnp.where(lane == k, cols[k], mi)
    mi_ref[...] = mi
    mw_ref[...] = jnp.where(lane == 0, w1, w2)
    sub = lax.broadcasted_iota(jnp.int32, (SUBLANES, LANES), 0)
    tbl = jnp.where(sub == 0, start, jnp.where(sub == 1, cnt, jnp.where(sub == 2, local, 0.0)))
    blk_ref[...] = tbl.astype(jnp.int32)


def _router(logits, t, tr):
    assert t % tr == 0 and tr % SUBLANES == 0
    out_row = pl.BlockSpec((tr, LANES), lambda p, i: (i * p, 0))
    return pl.pallas_call(
        _router_kernel,
        grid=(2, t // tr),
        in_specs=[pl.BlockSpec((tr, LANES), lambda p, i: (i, 0))],
        out_specs=[out_row, out_row, pl.BlockSpec((SUBLANES, LANES), lambda p, i: (0, 0)),
                   pl.BlockSpec((SUBLANES, LANES), lambda p, i: (i * p, 0))],
        out_shape=[jax.ShapeDtypeStruct((t, LANES), jnp.int32), jax.ShapeDtypeStruct((t, LANES), F32),
                   jax.ShapeDtypeStruct((SUBLANES, LANES), F32),
                   jax.ShapeDtypeStruct((t // tr * SUBLANES, LANES), jnp.int32)],
        scratch_shapes=[pltpu.VMEM((1, LANES), F32)],
        compiler_params=_params(("arbitrary", "arbitrary")),
        name="router",
    )(logits)


def _run_copies(src, dst, sem, src_row, dst_row, n_rows, wait=False):
    nsub = D_MODEL // LANES

    def copy(s0, d0, rows):
        dma = pltpu.make_async_copy(src.at[pl.ds(pl.multiple_of(s0 * nsub, nsub), rows * nsub), :],
                                    dst.at[pl.ds(pl.multiple_of(d0 * nsub, nsub), rows * nsub), :], sem)
        dma.wait() if wait else dma.start()

    def chunk(k, carry):
        copy(src_row + k * SUBLANES, dst_row + k * SUBLANES, SUBLANES)
        return carry

    n_chunks = n_rows // SUBLANES
    lax.fori_loop(0, n_chunks, chunk, 0)
    done = n_chunks * SUBLANES
    for rows in (4, 2, 1):
        @pl.when((n_rows & rows) != 0)
        def _(rows=rows, done=done):
            copy(src_row + done, dst_row + done, rows)

        done = done + (n_rows & rows)


def _dispatch_kernel(start_ref, cnt_ref, loc_ref, end_ref, tot_ref, h_ref, mi_ref, xs_hbm, stage, zeros, sem, zsem,
                     *, tm):
    i = pl.program_id(0)
    n = pl.num_programs(0)
    nsub = D_MODEL // LANES
    tb = h_ref.shape[0]
    slot = i % 2

    def wait_stage(sl):
        pltpu.make_async_copy(stage.at[sl], stage.at[sl], sem.at[sl]).wait()

    def pad_rows(wait):
        def one(e, carry):
            tot = tot_ref[e]
            n_pad = (tot + tm - 1) // tm * tm - tot
            _run_copies(zeros, xs_hbm, zsem, 0, end_ref[e] - n_pad, n_pad, wait=wait)
            return carry

        lax.fori_loop(0, end_ref.shape[0], one, 0)

        def spare(j, carry):
            dma = pltpu.make_async_copy(zeros, xs_hbm.at[pl.ds(pl.multiple_of(j * tm * nsub, nsub), tm * nsub), :], zsem)
            dma.wait() if wait else dma.start()
            return carry

        lax.fori_loop(end_ref[end_ref.shape[0] - 1] // tm, xs_hbm.shape[0] // (tm * nsub), spare, 0)

    @pl.when(i == 0)
    def _():
        zeros[...] = jnp.zeros_like(zeros)
        pad_rows(wait=False)

    @pl.when(i >= 2)
    def _():
        wait_stage(slot)

    li = mi_ref[...]
    rows_iota = lax.broadcasted_iota(jnp.int32, (tb, 2 * tb), 1)
    onehot = ((rows_iota == li[:, 4:5]) | (rows_iota == li[:, 5:6])).astype(BF16)
    srt = _dot_tn(onehot, h_ref[...])
    for c in range(nsub):
        stage[slot, pl.ds(c, 2 * tb, stride=nsub), :] = srt[:, c * LANES:(c + 1) * LANES]

    def run(e, carry):
        k = i * N_EXPERTS + e
        _run_copies(stage.at[slot], xs_hbm, sem.at[slot], loc_ref[k], start_ref[k], cnt_ref[k])
        return carry

    lax.fori_loop(0, N_EXPERTS, run, 0)

    @pl.when(i == n - 1)
    def _():
        wait_stage(slot)

        @pl.when(n >= 2)
        def _():
            wait_stage(1 - slot)

        pad_rows(wait=True)


def _dispatch(blk_start, blk_cnt, blk_loc, seg_end, seg_tot, h2, meta_i, n_slots, tb):
    t = meta_i.shape[0]
    nsub = D_MODEL // LANES
    assert t % tb == 0
    grid_spec = pltpu.PrefetchScalarGridSpec(
        num_scalar_prefetch=5,
        grid=(t // tb,),
        in_specs=[pl.BlockSpec((tb, D_MODEL), lambda i, *_: (i, 0)),
                  pl.BlockSpec((tb, LANES), lambda i, *_: (i, 0))],
        out_specs=pl.BlockSpec(memory_space=pl.ANY),
        scratch_shapes=[pltpu.VMEM((2, 2 * tb * nsub, LANES), F32),
                        pltpu.VMEM((EXPERT_TILE * nsub, LANES), F32),
                        pltpu.SemaphoreType.DMA((2,)), pltpu.SemaphoreType.DMA(())],
    )
    return pl.pallas_call(
        functools.partial(_dispatch_kernel, tm=EXPERT_TILE),
        grid_spec=grid_spec,
        out_shape=jax.ShapeDtypeStruct((n_slots * nsub, LANES), F32),
        compiler_params=_params(("arbitrary",)),
        name="dispatch",
    )(blk_start, blk_cnt, blk_loc, seg_end, seg_tot, h2, meta_i)


def _expert_kernel(te_ref, tv_ref, xs_ref, wg_ref, wu_ref, wd_ref, out_ref, wgb, wub, wdb):
    i = pl.program_id(0)
    tm = EXPERT_TILE
    nsub = D_MODEL // LANES
    valid = tv_ref[i] != 0

    @pl.when(jnp.logical_not(valid))
    def _():
        out_ref[...] = jnp.zeros_like(out_ref)

    @pl.when(valid & ((i == 0) | (te_ref[i] != te_ref[jnp.maximum(i - 1, 0)])))
    def _():
        wgb[...] = wg_ref[0, 0].astype(BF16)
        wub[...] = wu_ref[0, 0].astype(BF16)
        wdb[...] = wd_ref[0, 0].astype(BF16)

    @pl.when(valid)
    def _():
        h = jnp.concatenate([xs_ref[pl.ds(c, tm, stride=nsub), :] for c in range(nsub)], axis=1).astype(BF16)
        gate = _dot(h, wgb[...])
        up = _dot(h, wub[...])
        a = (gate * _sigmoid(gate) * up).astype(BF16)
        o = _dot(a, wdb[...])
        for c in range(nsub):
            out_ref[pl.ds(c, tm, stride=nsub), :] = o[:, c * LANES:(c + 1) * LANES]


def _experts(tile_expert, tile_valid, xs, wg, wu, wd):
    n_tiles = tile_expert.shape[0]
    tm = EXPERT_TILE
    nsub = D_MODEL // LANES
    wspec = lambda shape: pl.BlockSpec((1, 1) + shape, lambda i, te, tv: (0, te[i], 0, 0))
    last = lambda i, tv: jnp.where(tv[i] != 0, i, 0)
    grid_spec = pltpu.PrefetchScalarGridSpec(
        num_scalar_prefetch=2,
        grid=(n_tiles,),
        in_specs=[pl.BlockSpec((tm * nsub, LANES), lambda i, te, tv: (last(i, tv), 0)),
                  wspec((D_MODEL, D_EXPERT)), wspec((D_MODEL, D_EXPERT)), wspec((D_EXPERT, D_MODEL))],
        out_specs=pl.BlockSpec((tm * nsub, LANES), lambda i, te, tv: (i, 0)),
        scratch_shapes=[pltpu.VMEM((D_MODEL, D_EXPERT), BF16), pltpu.VMEM((D_MODEL, D_EXPERT), BF16),
                        pltpu.VMEM((D_EXPERT, D_MODEL), BF16)],
    )
    return pl.pallas_call(
        _expert_kernel,
        grid_spec=grid_spec,
        out_shape=jax.ShapeDtypeStruct((n_tiles * tm * nsub, LANES), F32),
        compiler_params=_params(("arbitrary",)),
        name="experts",
    )(tile_expert, tile_valid, xs, wg, wu, wd)


def _combine_kernel(p0_ref, p1_ref, x1_ref, mw_ref, rows_hbm, y_ref, g0, g1, sem, *, tc, blk_off):
    i = pl.program_id(0)
    n = pl.num_programs(0)
    nout = D_MODEL // LANES
    slot = i % 2

    def issue_all(step, sl):
        base = (step + blk_off) * tc

        def issue(j, carry):
            for pos_ref, dst, s in ((p0_ref, g0, 0), (p1_ref, g1, 1)):
                src0 = pl.multiple_of(pos_ref[base + j] * nout, nout)
                pltpu.make_async_copy(rows_hbm.at[pl.ds(src0, nout), :],
                                      dst.at[sl, pl.ds(pl.multiple_of(j * nout, nout), nout), :],
                                      sem.at[sl, s]).start()
            return carry

        lax.fori_loop(0, tc, issue, 0, unroll=4)

    @pl.when(i == 0)
    def _():
        issue_all(i, slot)

    @pl.when(i + 1 < n)
    def _():
        issue_all(i + 1, 1 - slot)

    for dst, s in ((g0, 0), (g1, 1)):
        pltpu.make_async_copy(rows_hbm.at[pl.ds(0, tc * nout), :], dst.at[slot], sem.at[slot, s]).wait()
    w0 = mw_ref[:, 0:1]
    w1 = mw_ref[:, 1:2]
    for c in range(nout):
        sl = slice(c * LANES, (c + 1) * LANES)
        y_ref[:, sl] = (x1_ref[:, sl] + w0 * g0[slot, pl.ds(c, tc, stride=nout), :]
                        + w1 * g1[slot, pl.ds(c, tc, stride=nout), :])


def _combine(pos0, pos1, x1, meta_w, rows, tc, tok_off):
    t = x1.shape[0]
    assert t % tc == 0 and tok_off % tc == 0
    blk_off = tok_off // tc
    nout = D_MODEL // LANES
    row = pl.BlockSpec((tc, D_MODEL), lambda i, p0, p1: (i, 0))
    grid_spec = pltpu.PrefetchScalarGridSpec(
        num_scalar_prefetch=2,
        grid=(t // tc,),
        in_specs=[row, pl.BlockSpec((tc, LANES), lambda i, p0, p1: (i + blk_off, 0)),
                  pl.BlockSpec(memory_space=pl.ANY)],
        out_specs=row,
        scratch_shapes=[pltpu.VMEM((2, tc * nout, LANES), F32), pltpu.VMEM((2, tc * nout, LANES), F32),
                        pltpu.SemaphoreType.DMA((2, 2))],
    )
    return pl.pallas_call(
        functools.partial(_combine_kernel, tc=tc, blk_off=blk_off),
        grid_spec=grid_spec,
        out_shape=jax.ShapeDtypeStruct((t, D_MODEL), F32),
        compiler_params=_params(("arbitrary",)),
        name="combine",
    )(pos0, pos1, x1, meta_w, rows)


def _rope_tables(pos):
    half = SWA_HEAD_DIM // 2
    inv_freq = ROPE_THETA ** (-jnp.arange(half, dtype=F32) / half)
    ang = pos.astype(F32)[:, None] * inv_freq[None, :]
    cos, sin = jnp.cos(ang), jnp.sin(ang)
    reps = LANES // SWA_HEAD_DIM
    return jnp.tile(jnp.concatenate([cos, cos], axis=1), (1, reps)), jnp.tile(jnp.concatenate([-sin, sin], axis=1), (1, reps))


def kernel(x_prompt, x_sample, state_gla, cache_swa_k0, cache_swa_v0, cache_swa_k1, cache_swa_v1, cache_swa_k2, cache_swa_v2, ln1_w, w_in, w_gla_lr, b_gla_lr, gla_onorm_w, q_norm_w, k_norm_w, w_branch_a, w_branch_b, w_out, ln2_w, w_router_group, b_router_group, w_router_expert, b_router_expert, w_exp_gate, w_exp_up, w_exp_down):
    b, s, d = x_prompt.shape
    bd, ls, _ = x_sample.shape
    tp, ts = b * s, bd * ls
    assert w_in.shape[0] == 1 and d == D_MODEL and ts % SUBLANES == 0
    k_caches = (cache_swa_k0, cache_swa_k1, cache_swa_k2)
    v_caches = (cache_swa_v0, cache_swa_v1, cache_swa_v2)

    w = w_in[0]
    cuts = np.cumsum((512, 512, 1024, GLA_RANK, 1024, 768, 768, 768, 1024, 1024))
    sec = lambda a: w[:, (0 if a == 0 else cuts[a - 1]):cuts[a]]
    lr_pad = jnp.pad(sec(3), ((0, 0), (0, LANES - GLA_RANK)))
    w_f32 = jnp.concatenate([sec(0), sec(1), sec(2), sec(4), sec(5), sec(6), sec(7), sec(8), sec(9), lr_pad], axis=1)
    w_packed = w_f32.astype(BF16)
    wlr_f32 = jnp.pad(w_gla_lr[0], ((0, LANES - GLA_RANK), (0, 0)))
    wlr = wlr_f32.astype(BF16)
    blr = b_gla_lr[0][None, :]
    nw = jnp.concatenate([jnp.tile(q_norm_w[0], SWA_WIDTH // SWA_HEAD_DIM), jnp.tile(k_norm_w[0], SWA_WIDTH // SWA_HEAD_DIM)])[None, :]
    gi = np.arange(256) // SWA_HEAD_DIM
    gmat_f32 = jnp.asarray((gi[:, None] == gi[None, :]).astype(np.float32) / SWA_HEAD_DIM)
    gmat = gmat_f32.astype(BF16)
    ln1 = ln1_w[0][None, :]
    ln2 = ln2_w[0][None, :]
    onw = gla_onorm_w[0][None, :]
    wa_f32, wb_f32, wo_f32 = w_branch_a[0], w_branch_b[0], w_out[0]
    wa, wb, wo = wa_f32.astype(BF16), wb_f32.astype(BF16), wo_f32.astype(BF16)
    wr = jnp.pad(jnp.concatenate([w_router_group[0], w_router_expert[0]], axis=1),
                 ((0, 0), (0, LANES - N_GROUPS - N_EXPERTS)))
    wr_hi, wr_lo = _split_bf16(wr)
    br = jnp.pad(jnp.concatenate([b_router_group[0], b_router_expert[0]]), (0, LANES - N_GROUPS - N_EXPERTS))[None, :]

    cos_p, sin_p = _rope_tables(jnp.arange(s, dtype=jnp.int32))
    cos_s, sin_s = _rope_tables(PAST_LEN + jnp.arange(ts, dtype=jnp.int32) % ls)

    tm_p = 512
    proj_p = _proj(x_prompt.reshape(tp, d), cos_p, sin_p, s // tm_p, ln1, w_packed, wlr, blr, nw, gmat, tm_p)
    proj_s = _proj(x_sample.reshape(ts, d), cos_s, sin_s, 1, ln1, w_f32, wlr_f32, blr, nw, gmat_f32, ts)
    gqkv_p, la_p, sog_p, q_p, k_p, v_p, sgab_p = proj_p
    gqkv_s, la_s, sog_s, q_s, k_s, v_s, sgab_s = proj_s

    r3 = lambda t, nb: t.reshape(nb, t.shape[0] // nb, t.shape[1])
    ya_p, st_p = _gla(r3(gqkv_p, b), r3(la_p, b), r3(sog_p, b),
                      jnp.zeros((b, GLA_HEADS, GLA_DK, GLA_DV), F32), onw, GLA_CHUNK, 512)
    pad_s = lambda t: jnp.pad(r3(t, bd), ((0, 0), (0, _SAMPLE_ROWS - ls), (0, 0)))
    ya_s, st_s = _gla(pad_s(gqkv_s), pad_s(la_s), pad_s(sog_s), state_gla[0], onw, _SAMPLE_ROWS, _SAMPLE_ROWS)
    ya_s = ya_s[:, :ls].reshape(ts, d)

    q3, k3, v3 = r3(q_p, b), r3(k_p, b), r3(v_p, b)
    swa_p = [_swa_prompt(q3, k3, v3, g, dil) for g, (_, dil) in enumerate(SWA_GROUPS)]
    to_t = lambda c: jnp.transpose(c[0], (0, 2, 3, 1))
    caches_t = [(to_t(k_caches[g]), to_t(v_caches[g])) for g in range(len(SWA_GROUPS))]
    new_t = lambda t: jnp.pad(jnp.transpose(r3(t, bd), (0, 2, 1)), ((0, 0), (0, 0), (LANES - ls, 0)))
    ob_s, *new_caches = _swa_sample(pad_s(q_s), pad_s(k_s), pad_s(v_s), new_t(k_s), new_t(v_s), caches_t, ls)
    ob_s = ob_s[:, :ls].reshape(ts, SWA_OUT)

    t = tp + ts
    mw = (wa, wb, wo, ln2, wr_hi, wr_lo, br)
    x1_p, h2c, lg = _merge(x_prompt.reshape(tp, d), ya_p.reshape(tp, d),
                           [o for o, _ in swa_p] + [l for _, l in swa_p], sgab_p, mw, 512, 0, t)
    mw_f32 = (wa_f32, wb_f32, wo_f32) + mw[3:]
    x1_s, h2c, lg = _merge(x_sample.reshape(ts, d), ya_s, [ob_s], sgab_s, mw_f32, ts, tp, t, shared=(h2c, lg))

    tb = DISPATCH_BLOCK
    meta_i, meta_w, cnt, blocks = _router(lg, t, tb)
    blocks = blocks.reshape(t // tb, SUBLANES, LANES)[:, :, :N_EXPERTS]
    blk_start, blk_cnt, blk_loc = (blocks[:, r].reshape(-1) for r in range(3))
    tm = EXPERT_TILE
    n_tiles = (2 * t) // tm + N_EXPERTS
    counts = cnt[0, :N_EXPERTS].astype(jnp.int32)
    ends = jnp.cumsum((counts + tm - 1) // tm * tm)
    pos0, pos1 = meta_i[:, 2], meta_i[:, 3]
    tile_start = jnp.arange(n_tiles, dtype=jnp.int32) * tm
    tile_valid = (tile_start < ends[-1]).astype(jnp.int32)
    last_slot = jnp.minimum(tile_start, ends[-1] - 1)
    tile_expert = jnp.minimum(jnp.sum((last_slot[:, None] >= ends[None, :]).astype(jnp.int32), axis=1), N_EXPERTS - 1)

    xs = _dispatch(blk_start, blk_cnt, blk_loc, ends, counts, h2c, meta_i, n_tiles * tm, tb)
    rows = _experts(tile_expert, tile_valid, xs, w_exp_gate, w_exp_up, w_exp_down)
    y_p = _combine(pos0, pos1, x1_p, meta_w, rows, 256, 0)
    y_s = _combine(pos0, pos1, x1_s, meta_w, rows, ts, tp)

    heads = lambda a: a.reshape(1, a.shape[0], a.shape[1], SWA_HPG, SWA_HEAD_DIM)
    outs = [y_p.reshape(b, s, d), y_s.reshape(bd, ls, d), st_p[None].astype(x_prompt.dtype)]
    for g, (win, _) in enumerate(SWA_GROUPS):
        keep = min(win, s)
        gsl = slice(g * SWA_OUT, (g + 1) * SWA_OUT)
        outs += [heads(k3[:, s - keep:, gsl]), heads(v3[:, s - keep:, gsl])]
    outs.append(st_s[None].astype(state_gla.dtype))
    outs += [jnp.transpose(c, (0, 3, 1, 2))[None] for c in new_caches]
    return tuple(outs)
```

```python
import functools

import numpy as np
import jax
import jax.numpy as jnp
from jax import lax
from jax.experimental import pallas as pl
from jax.experimental.pallas import tpu as pltpu

F32 = jnp.float32
BF16 = jnp.bfloat16

D_MODEL = 1024
PAST_LEN = 16384
GLA_HEADS = 4
GLA_DK = 128
GLA_DV = 256
GLA_RANK = 16
GLA_TAU = 16.0
GLA_CHUNK = 64
SWA_GROUPS = ((128, 1), (512, 4), (2048, 16))
SWA_HPG = 4
SWA_HEAD_DIM = 64
SWA_WIDTH = 768
SWA_OUT = 256
SWA_BLOCK = 128
ROPE_THETA = 10000.0
N_GROUPS = 4
EXPERTS_PER_GROUP = 8
N_EXPERTS = 32
D_EXPERT = 512
EPS = 1e-6

LANES = 128
SUBLANES = 8
VMEM_LIMIT = 56 * 1024 * 1024
NEG = -1e30
MXU_WIDTH = 256
EXPERT_TILE = 256
MERGE_TILE = 512
DISPATCH_BLOCK = 384

_C_GQKV = (0, 2048)
_C_GOG = (2048, 3072)
_C_QK = (3072, 4608)
_C_V = (4608, 5376)
_C_GAB = (5376, 7424)
_C_LR = (7424, 7552)


def _contract(a, b, dims):
    dg = lambda x, y: lax.dot_general(x, y, (dims, ((), ())), preferred_element_type=F32)
    if a.dtype == F32 and b.dtype == F32:
        a_hi, a_lo = _split(a, BF16)
        b_hi, b_lo = _split(b, BF16)
        return dg(a_hi, b_hi) + dg(a_hi, b_lo) + dg(a_lo, b_hi)
    return dg(a, b)


def _dot(a, b):
    return _contract(a, b, ((1,), (0,)))


def _dot_nt(a, b):
    return _contract(a, b, ((1,), (1,)))


def _dot_tn(a, b):
    return _contract(a, b, ((0,), (0,)))


def _sigmoid(x):
    return 1.0 / (1.0 + jnp.exp(-x))


def _split(x, dt):
    hi = x.astype(dt)
    lo = (x - hi.astype(F32)).astype(dt)
    return hi, lo


def _split_bf16(x):
    return _split(x, BF16)


def _params(sem):
    return pltpu.CompilerParams(dimension_semantics=sem, vmem_limit_bytes=VMEM_LIMIT)


def _resident(shape):
    nd = len(shape)
    return pl.BlockSpec(shape, lambda *_: (0,) * nd, pipeline_mode=pl.Buffered(1))


def _proj_kernel(x_ref, cos_ref, sin_ref, ln_ref, w_ref, wlr_ref, blr_ref, nw_ref, g_ref,
                 gqkv_ref, la_ref, sog_ref, q_ref, k_ref, v_ref, sgab_ref):
    cdt = w_ref.dtype
    x = x_ref[...]
    h = (x * lax.rsqrt(jnp.mean(x * x, axis=-1, keepdims=True) + EPS) * ln_ref[...]).astype(cdt)
    gqkv_ref[...] = _dot(h, w_ref[:, _C_GQKV[0]:_C_GQKV[1]]).astype(cdt)
    og = _dot(h, w_ref[:, _C_GOG[0]:_C_GOG[1]])
    sog_ref[...] = (og * _sigmoid(og)).astype(cdt)
    lr = _dot(h, w_ref[:, _C_LR[0]:_C_LR[1]]).astype(cdt)
    z = _dot(lr, wlr_ref[...]) + blr_ref[...]
    la_ref[...] = (jnp.minimum(z, 0.0) - jnp.log(1.0 + jnp.exp(-jnp.abs(z)))) / GLA_TAU
    qk = _dot(h, w_ref[:, _C_QK[0]:_C_QK[1]])
    sq = (qk * qk).astype(cdt)
    ms = jnp.concatenate([_dot(sq[:, c * 256:(c + 1) * 256], g_ref[...]) for c in range(6)], axis=1)
    qn = qk * lax.rsqrt(ms + EPS) * nw_ref[...]
    width = 2 * SWA_WIDTH
    cos = jnp.tile(cos_ref[...], (1, width // LANES))
    sin = jnp.tile(sin_ref[...], (1, width // LANES))
    lane = lax.broadcasted_iota(jnp.int32, qn.shape, 1)
    half = SWA_HEAD_DIM // 2
    rot = jnp.where(lane % SWA_HEAD_DIM < half, pltpu.roll(qn, width - half, 1), pltpu.roll(qn, half, 1))
    qr = qn * cos + rot * sin
    q_ref[...] = qr[:, :SWA_WIDTH]
    k_ref[...] = qr[:, SWA_WIDTH:]
    v_ref[...] = _dot(h, w_ref[:, _C_V[0]:_C_V[1]])
    gab = _dot(h, w_ref[:, _C_GAB[0]:_C_GAB[1]])
    sgab_ref[...] = _sigmoid(gab).astype(cdt)


def _proj(x, cos, sin, rope_blocks, ln, w, wlr, blr, nw, g, tm):
    t = x.shape[0]
    assert t % tm == 0 and w.dtype == wlr.dtype == g.dtype
    row = lambda width: pl.BlockSpec((tm, width), lambda i: (i, 0))
    outs = [(2048, w.dtype), (512, F32), (1024, w.dtype), (768, F32), (768, F32), (768, F32), (2048, w.dtype)]
    return pl.pallas_call(
        _proj_kernel,
        grid=(t // tm,),
        in_specs=[row(D_MODEL),
                  pl.BlockSpec((tm, LANES), lambda i: (i % rope_blocks, 0)),
                  pl.BlockSpec((tm, LANES), lambda i: (i % rope_blocks, 0)),
                  _resident(ln.shape), _resident(w.shape), _resident(wlr.shape), _resident(blr.shape),
                  _resident(nw.shape), _resident(g.shape)],
        out_specs=[row(wd) for wd, _ in outs],
        out_shape=[jax.ShapeDtypeStruct((t, wd), dt) for wd, dt in outs],
        compiler_params=_params(("arbitrary",)),
        name="proj",
    )(x, cos, sin, ln, w, wlr, blr, nw, g)


def _gla_kernel(gqkv_ref, la_ref, sog_ref, s0_ref, onw_ref, y_ref, sfin_ref, st_scr, *, chunk, n_chunks):
    j = pl.program_id(1)

    @pl.when(j == 0)
    def _():
        for h in range(GLA_HEADS):
            st_scr[h] = s0_ref[0, h].T

    r = lax.broadcasted_iota(jnp.int32, (chunk, chunk), 0)
    c = lax.broadcasted_iota(jnp.int32, (chunk, chunk), 1)
    causal = r >= c
    cdt = gqkv_ref.dtype
    tri = causal.astype(cdt)
    hk = GLA_HEADS * GLA_DK
    rows = [slice(ci * chunk, (ci + 1) * chunk) for ci in range(n_chunks)]

    la_hi, la_lo = _split(la_ref[0], cdt)
    b_chunks = [_dot(tri, la_hi[rs]) + _dot(tri, la_lo[rs]) for rs in rows]
    last = [bc[chunk - 1:chunk, :] for bc in b_chunks]
    b = jnp.concatenate(b_chunks, axis=0)
    b_end = jnp.concatenate([jnp.broadcast_to(x, (chunk, hk)) for x in last], axis=0)
    q = gqkv_ref[0, :, :hk].astype(F32) * GLA_DK ** -0.5
    k = gqkv_ref[0, :, hk:2 * hk].astype(F32)
    qd = (q * jnp.exp(b)).astype(cdt)
    kd = (k * jnp.exp(-b)).astype(cdt)
    kdec = (k * jnp.exp(b_end - b)).astype(cdt)
    decay = [jnp.exp(x) for x in last]

    state = [st_scr[h] for h in range(GLA_HEADS)]
    for ci, rs in enumerate(rows):
        outs = []
        for h in range(GLA_HEADS):
            sl = slice(h * GLA_DK, (h + 1) * GLA_DK)
            v_h = gqkv_ref[0, rs, 2 * hk + h * GLA_DV:2 * hk + (h + 1) * GLA_DV]
            att = jnp.where(causal, _dot_nt(qd[rs, sl], kd[rs, sl]), 0.0).astype(cdt)
            o = _dot(att, v_h) + _dot_nt(qd[rs, sl], state[h].astype(cdt))
            state[h] = state[h] * decay[ci][:, sl] + _dot_tn(v_h, kdec[rs, sl])
            ms = jnp.mean(o * o, axis=-1, keepdims=True)
            outs.append(o * lax.rsqrt(ms + EPS) * onw_ref[...])
        o_all = jnp.concatenate(outs, axis=1) * sog_ref[0, rs, :].astype(F32)
        y_ref[0, rs, :] = o_all.astype(cdt)
    for h in range(GLA_HEADS):
        st_scr[h] = state[h]

    @pl.when(j == pl.num_programs(1) - 1)
    def _():
        for h in range(GLA_HEADS):
            sfin_ref[0, h] = state[h].T


def _gla(gqkv, la, sog, s0, onw, chunk, block):
    b, l, _ = gqkv.shape
    tok = lambda width: pl.BlockSpec((1, block, width), lambda bi, j: (bi, j, 0))
    st = pl.BlockSpec((1, GLA_HEADS, GLA_DK, GLA_DV), lambda bi, j: (bi, 0, 0, 0))
    return pl.pallas_call(
        functools.partial(_gla_kernel, chunk=chunk, n_chunks=block // chunk),
        grid=(b, l // block),
        in_specs=[tok(2048), tok(512), tok(1024), st, _resident(onw.shape)],
        out_specs=[tok(1024), st],
        out_shape=[jax.ShapeDtypeStruct((b, l, 1024), gqkv.dtype),
                   jax.ShapeDtypeStruct((b, GLA_HEADS, GLA_DK, GLA_DV), F32)],
        scratch_shapes=[pltpu.VMEM((GLA_HEADS, GLA_DV, GLA_DK), F32)],
        compiler_params=_params(("arbitrary", "arbitrary")),
        name="gla",
    )(gqkv, la, sog, s0, onw)


def _band_heads(q, kw, vw, valid):
    n = q.shape[0]
    low = lax.broadcasted_iota(jnp.int32, (n, LANES), 1) < SWA_HEAD_DIM
    ones = jnp.ones((kw.shape[0], LANES), kw.dtype)
    outs, lses = [], []
    for pair in range(SWA_OUT // LANES):
        cols = slice(pair * LANES, (pair + 1) * LANES)
        v_aug = jnp.concatenate([vw[:, cols], ones], axis=1)
        res = []
        for mine in (low, jnp.logical_not(low)):
            qm = jnp.where(mine, q[:, cols], 0.0).astype(kw.dtype)
            s = _dot_nt(qm, kw[:, cols]) * SWA_HEAD_DIM ** -0.5
            s = jnp.where(valid, s, NEG)
            m = jnp.max(s, axis=-1, keepdims=True)
            r = _dot(jnp.exp(s - m).astype(vw.dtype), v_aug)
            den = r[:, LANES:]
            res.append((r[:, :LANES] / den, m + jnp.log(den)))
        outs.append(jnp.where(low, res[0][0], res[1][0]))
        lses.append(jnp.where(low, res[0][1], res[1][1]))
    return jnp.concatenate(outs, axis=1), jnp.concatenate(lses, axis=1)


_SWA_TOKENS = 2048


def _swa_kernel(q_ref, k_ref, v_ref, kp_ref, vp_ref, o_ref, lse_ref, *stage, dil):
    blk = SWA_BLOCK
    nsub = q_ref.shape[1] // (blk * dil)
    first = pl.program_id(1) == 0
    qi = lax.broadcasted_iota(jnp.int32, (blk, 2 * blk), 0)
    kj = lax.broadcasted_iota(jnp.int32, (blk, 2 * blk), 1)
    band = (kj >= qi) & (kj <= qi + blk)
    halves = SWA_OUT // LANES

    if dil > 1:
        ins = (q_ref, k_ref, v_ref, kp_ref, vp_ref)
        q_ref, k_ref, v_ref, kp_ref, vp_ref, o_st, lse_st = stage
        for src, dst in zip(ins, stage):
            for hf in range(halves):
                dst[hf] = src[0, :, hf * LANES:(hf + 1) * LANES]

    def rows(ref, start):
        if dil == 1:
            return ref[0, pl.ds(start, blk), :]
        return jnp.concatenate([ref[hf, pl.ds(start, blk, stride=dil), :] for hf in range(halves)], axis=1)

    def unit(u, carry):
        r = u // nsub
        j = u % nsub
        start = r + dil * blk * j
        inside = r + dil * blk * jnp.maximum(j - 1, 0)
        if dil == 1:
            r, start, inside = 0, pl.multiple_of(start, blk), pl.multiple_of(inside, blk)
        head = j == 0
        kprev = jnp.where(head, rows(kp_ref, r), rows(k_ref, inside))
        vprev = jnp.where(head, rows(vp_ref, r), rows(v_ref, inside))
        kw = jnp.concatenate([kprev, rows(k_ref, start)], axis=0).astype(BF16)
        vw = jnp.concatenate([vprev, rows(v_ref, start)], axis=0).astype(BF16)
        valid = band & (kj >= jnp.where(head & first, blk, 0))
        o, lse = _band_heads(rows(q_ref, start), kw, vw, valid)
        if dil == 1:
            o_ref[0, pl.ds(start, blk), :] = o
            lse_ref[0, pl.ds(start, blk), :] = lse
        else:
            for hf in range(halves):
                o_st[hf, pl.ds(start, blk, stride=dil), :] = o[:, hf * LANES:(hf + 1) * LANES]
                lse_st[hf, pl.ds(start, blk, stride=dil), :] = lse[:, hf * LANES:(hf + 1) * LANES]
        return carry

    lax.fori_loop(0, dil * nsub, unit, 0)
    if dil > 1:
        for hf in range(halves):
            o_ref[0, :, hf * LANES:(hf + 1) * LANES] = o_st[hf]
            lse_ref[0, :, hf * LANES:(hf + 1) * LANES] = lse_st[hf]


def _swa_prompt(q, k, v, g, dil):
    b, s, _ = q.shape
    tb = _SWA_TOKENS
    back = SWA_BLOCK * dil
    assert s % tb == 0 and tb % back == 0
    cur = pl.BlockSpec((1, tb, SWA_OUT), lambda bi, i: (bi, i, g))
    prev = pl.BlockSpec((1, back, SWA_OUT), lambda bi, i: (bi, jnp.maximum(i * (tb // back) - 1, 0), g))
    out = pl.BlockSpec((1, tb, SWA_OUT), lambda bi, i: (bi, i, 0))
    halves = SWA_OUT // LANES
    stage = [pltpu.VMEM((halves, n, LANES), F32) for n in (tb, tb, tb, back, back, tb, tb)] if dil > 1 else []
    o, lse = pl.pallas_call(
        functools.partial(_swa_kernel, dil=dil),
        grid=(b, s // tb),
        in_specs=[cur, cur, cur, prev, prev],
        out_specs=[out, out],
        out_shape=[jax.ShapeDtypeStruct((b, s, SWA_OUT), F32)] * 2,
        scratch_shapes=stage,
        compiler_params=_params(("arbitrary", "arbitrary")),
        name=f"swa_prompt_g{g}",
    )(q, k, v, k, v)
    return o.reshape(b * s, SWA_OUT), lse.reshape(b * s, SWA_OUT)


_SAMPLE_ROWS = 16


def _swa_sample_kernel(q_ref, kn_ref, vn_ref, knt_ref, vnt_ref, k0_ref, v0_ref, k1_ref, v1_ref, k2_ref, v2_ref,
                       ob_ref, ok0_ref, ov0_ref, ok1_ref, ov1_ref, ok2_ref, ov2_ref, *, n_new):
    rows = _SAMPLE_ROWS
    in_refs = ((k0_ref, v0_ref), (k1_ref, v1_ref), (k2_ref, v2_ref))
    out_refs = ((ok0_ref, ov0_ref), (ok1_ref, ov1_ref), (ok2_ref, ov2_ref))
    scale = SWA_HEAD_DIM ** -0.5
    jn = lax.broadcasted_iota(jnp.int32, (rows, rows), 1)
    ln = lax.broadcasted_iota(jnp.int32, (rows, rows), 0)
    tail = lax.broadcasted_iota(jnp.int32, (SWA_HEAD_DIM, LANES), 1) >= LANES - n_new
    o_g, lse_g = [], []
    for g, (win, dil) in enumerate(SWA_GROUPS):
        jc = lax.broadcasted_iota(jnp.int32, (rows, win), 1)
        lc = lax.broadcasted_iota(jnp.int32, (rows, win), 0)
        valid_c = (jc >= lc) & (((jc - lc) & (dil - 1)) == 0)
        valid_n = (jn <= ln) & (((ln - jn) & (dil - 1)) == 0) & (jn < n_new)
        o_h, lse_h = [], []
        for h in range(SWA_HPG):
            col = g * SWA_OUT + h * SWA_HEAD_DIM
            hsl = slice(col, col + SWA_HEAD_DIM)
            qh = q_ref[0, :, hsl]
            knh = kn_ref[0, :, hsl]
            vnh = vn_ref[0, :, hsl]
            for (src, dst, new_t) in ((in_refs[g][0], out_refs[g][0], knt_ref), (in_refs[g][1], out_refs[g][1], vnt_ref)):
                old = src[0, h]
                moved = pltpu.roll(old, win - n_new, 1)
                if win > LANES:
                    dst[0, h, :, 0:win - LANES] = moved[:, 0:win - LANES]
                dst[0, h, :, win - LANES:win] = jnp.where(tail, new_t[0, hsl, :], moved[:, win - LANES:win])
            kt = in_refs[g][0][0, h]
            vt = in_refs[g][1][0, h]
            s_c = jnp.where(valid_c, _dot(qh, kt) * scale, NEG)
            s_n = jnp.where(valid_n, _dot_nt(qh, knh) * scale, NEG)
            m = jnp.maximum(jnp.max(s_c, axis=-1, keepdims=True), jnp.max(s_n, axis=-1, keepdims=True))
            p_c = jnp.exp(s_c - m)
            p_n = jnp.exp(s_n - m)
            den = jnp.sum(p_c, axis=-1, keepdims=True) + jnp.sum(p_n, axis=-1, keepdims=True)
            o_h.append((_dot_nt(p_c, vt) + _dot(p_n, vnh)) / den)
            lse_h.append(jnp.broadcast_to(m + jnp.log(den), (rows, SWA_HEAD_DIM)))
        o_g.append(jnp.concatenate(o_h, axis=1))
        lse_g.append(jnp.concatenate(lse_h, axis=1))
    lmax = jnp.maximum(jnp.maximum(lse_g[0], lse_g[1]), lse_g[2])
    e = [jnp.exp(x - lmax) for x in lse_g]
    ob_ref[0] = (e[0] * o_g[0] + e[1] * o_g[1] + e[2] * o_g[2]) / (e[0] + e[1] + e[2])


def _swa_sample(q, kn, vn, knt, vnt, caches_t, n_new):
    bd = q.shape[0]
    rows = _SAMPLE_ROWS
    tok = pl.BlockSpec((1, rows, SWA_WIDTH), lambda bi: (bi, 0, 0))
    new_t = pl.BlockSpec((1, SWA_WIDTH, LANES), lambda bi: (bi, 0, 0))
    specs, args = [tok, tok, tok, new_t, new_t], [q, kn, vn, knt, vnt]
    out_specs = [pl.BlockSpec((1, rows, SWA_OUT), lambda bi: (bi, 0, 0))]
    out_shape = [jax.ShapeDtypeStruct((bd, rows, SWA_OUT), F32)]
    for g, (win, dil) in enumerate(SWA_GROUPS):
        for t in caches_t[g]:
            assert t.shape == (bd, SWA_HPG, SWA_HEAD_DIM, win) and win == SWA_BLOCK * dil and win % LANES == 0
            spec = pl.BlockSpec((1, SWA_HPG, SWA_HEAD_DIM, win), lambda bi: (bi, 0, 0, 0))
            args.append(t)
            specs.append(spec)
            out_specs.append(spec)
            out_shape.append(jax.ShapeDtypeStruct(t.shape, t.dtype))
    return pl.pallas_call(
        functools.partial(_swa_sample_kernel, n_new=n_new),
        grid=(bd,),
        in_specs=specs,
        out_specs=out_specs,
        out_shape=out_shape,
        compiler_params=_params(("arbitrary",)),
        name="swa_sample",
    )(*args)


def _merge_kernel(*refs, combine, n_alias, n_real):
    x1_ref, h2c_ref, lg_ref = refs[-3:]
    refs = refs[:len(refs) - 3 - n_alias]

    @pl.when(pl.program_id(0) >= n_real)
    def _():
        h2c_ref[...] = jnp.zeros_like(h2c_ref)
        lg_ref[...] = jnp.zeros_like(lg_ref)

    pl.when(pl.program_id(0) < n_real)(functools.partial(_merge_tile, refs, x1_ref, h2c_ref, lg_ref, combine))


def _merge_tile(refs, x1_ref, h2c_ref, lg_ref, combine):
    if combine:
        (x_ref, ya_ref, o0, o1, o2, l0, l1, l2, sgab_ref, wa_ref, wb_ref, wo_ref, ln_ref, wr_hi_ref, wr_lo_ref,
         br_ref) = refs
        lmax = jnp.maximum(jnp.maximum(l0[...], l1[...]), l2[...])
        e0, e1, e2 = jnp.exp(l0[...] - lmax), jnp.exp(l1[...] - lmax), jnp.exp(l2[...] - lmax)
        ob = ((e0 * o0[...] + e1 * o1[...] + e2 * o2[...]) / (e0 + e1 + e2)).astype(wb_ref.dtype)
    else:
        (x_ref, ya_ref, ob_ref, sgab_ref, wa_ref, wb_ref, wo_ref, ln_ref, wr_hi_ref, wr_lo_ref, br_ref) = refs
        ob = ob_ref[...]
    ya = _dot(ya_ref[...], wa_ref[...])
    yb = _dot(ob, wb_ref[...])
    sga = sgab_ref[:, :D_MODEL].astype(F32)
    sgb = sgab_ref[:, D_MODEL:].astype(F32)
    x1 = x_ref[...] + _dot((sga * ya + sgb * yb).astype(wo_ref.dtype), wo_ref[...])
    x1_ref[...] = x1
    h2 = x1 * lax.rsqrt(jnp.mean(x1 * x1, axis=-1, keepdims=True) + EPS) * ln_ref[...]
    h_hi, h_lo = _split_bf16(h2)
    lg_ref[...] = _dot(h_hi, wr_hi_ref[...]) + _dot(h_hi, wr_lo_ref[...]) + _dot(h_lo, wr_hi_ref[...]) + br_ref[...]
    h2c_ref[...] = h_hi


def _merge(x, ya_in, swa, sgab, weights, tm, tok_off, t_all, shared=None):
    t = x.shape[0]
    assert t % tm == 0 and tok_off % tm == 0 and MERGE_TILE % tm == 0
    combine = len(swa) > 1
    blk_off = tok_off // tm
    t_buf = -(-t_all // MERGE_TILE) * MERGE_TILE
    nsub = D_MODEL // LANES
    n_real = t // tm
    n_fill = 0 if shared is not None else (t_buf - tok_off - t) // tm
    row = lambda width: pl.BlockSpec((tm, width), lambda i: (jnp.minimum(i, n_real - 1), 0))
    shared_in = [] if shared is None else list(shared)
    n_in = 3 + len(swa) + len(weights)
    return pl.pallas_call(
        functools.partial(_merge_kernel, combine=combine, n_alias=len(shared_in), n_real=n_real),
        grid=(n_real + n_fill,),
        in_specs=[row(D_MODEL), row(D_MODEL)] + [row(SWA_OUT)] * len(swa) + [row(2 * D_MODEL)]
                 + [_resident(w.shape) for w in weights] + [pl.BlockSpec(memory_space=pl.ANY)] * len(shared_in),
        out_specs=[row(D_MODEL),
                   pl.BlockSpec((tm, D_MODEL), lambda i: (i + blk_off, 0)),
                   pl.BlockSpec((tm, LANES), lambda i: (i + blk_off, 0))],
        out_shape=[jax.ShapeDtypeStruct((t, D_MODEL), F32),
                   jax.ShapeDtypeStruct((t_buf, D_MODEL), BF16),
                   jax.ShapeDtypeStruct((t_buf, LANES), F32)],
        input_output_aliases={n_in + k: 1 + k for k in range(len(shared_in))},
        compiler_params=_params(("arbitrary",)),
        name="merge",
    )(x, ya_in, *swa, sgab, *weights, *shared_in)


def _router_kernel(lg_ref, mi_ref, mw_ref, cnt_ref, blk_ref, carry):
    phase = pl.program_id(0)
    i = pl.program_id(1)

    @pl.when((phase == 0) & (i == 0))
    def _():
        carry[...] = jnp.zeros_like(carry)

    @pl.when((phase == 1) & (i == 0))
    def _():
        cnt = carry[...]
        cnt_ref[...] = jnp.broadcast_to(cnt, cnt_ref.shape)
        tiles = jnp.floor((cnt + (EXPERT_TILE - 1)) * (1.0 / EXPERT_TILE))
        r = lax.broadcasted_iota(jnp.int32, (LANES, LANES), 0)
        c = lax.broadcasted_iota(jnp.int32, (LANES, LANES), 1)
        before = _dot(jnp.broadcast_to(tiles, (SUBLANES, LANES)).astype(BF16), (r < c).astype(BF16))
        carry[...] = before[0:1, :] * EXPERT_TILE

    lg = lg_ref[...]
    tr = lg.shape[0]
    lane = lax.broadcasted_iota(jnp.int32, lg.shape, 1)
    big = jnp.int32(LANES)
    gl = jnp.where(lane < N_GROUPS, lg, NEG)
    gmax = jnp.max(gl, axis=-1, keepdims=True)
    g_idx = jnp.min(jnp.where(gl == gmax, lane, big), axis=-1, keepdims=True)
    g_w = 1.0 / jnp.sum(jnp.exp(gl - gmax), axis=-1, keepdims=True)
    e_lane = lane - N_GROUPS
    in_group = (e_lane >= 0) & (e_lane < N_EXPERTS) & (e_lane // EXPERTS_PER_GROUP == g_idx)
    el = jnp.where(in_group, lg, NEG)
    v1 = jnp.max(el, axis=-1, keepdims=True)
    i1 = jnp.min(jnp.where(el == v1, lane, big), axis=-1, keepdims=True)
    el2 = jnp.where(lane == i1, NEG, el)
    v2 = jnp.max(el2, axis=-1, keepdims=True)
    i2 = jnp.min(jnp.where(el2 == v2, lane, big), axis=-1, keepdims=True)
    r21 = jnp.exp(v2 - v1)
    w1 = g_w / (1.0 + r21)
    w2 = g_w * r21 / (1.0 + r21)
    e1 = i1 - N_GROUPS
    e2 = i2 - N_GROUPS
    hot1 = lane == e1
    hot2 = lane == e2
    hot = (hot1 | hot2).astype(BF16)
    start = carry[...]
    cnt = jnp.sum(hot.astype(F32), axis=0, keepdims=True)
    carry[...] = start + cnt
    pl.when(phase == 1)(functools.partial(_router_emit, lane, e1, e2, w1, w2, hot1, hot2, hot, start, cnt,
                                          mi_ref, mw_ref, blk_ref))


def _router_emit(lane, e1, e2, w1, w2, hot1, hot2, hot, start, cnt, mi_ref, mw_ref, blk_ref):
    tr = lane.shape[0]
    r = lax.broadcasted_iota(jnp.int32, (tr, tr), 0)
    c = lax.broadcasted_iota(jnp.int32, (tr, tr), 1)
    within = _dot((r > c).astype(BF16), hot)
    lane1 = lax.broadcasted_iota(jnp.int32, (SUBLANES, LANES), 1)
    incl = jnp.broadcast_to(cnt, (SUBLANES, LANES))
    for sh in (1, 2, 4, 8, 16):
        incl = incl + jnp.where(lane1 >= sh, pltpu.roll(incl, sh, 1), 0.0)
    local = incl[0:1, :] - cnt
    pick = lambda hot_k, row: jnp.sum(jnp.where(hot_k, row, 0.0), axis=-1, keepdims=True)
    cols = [e1, e2]
    for hot_k in (hot1, hot2):
        cols.append((pick(hot_k, within) + pick(hot_k, start)).astype(jnp.int32))
    for hot_k in (hot1, hot2):
        cols.append((pick(hot_k, within) + pick(hot_k, local)).astype(jnp.int32))
    mi = cols[-1]
    for k in range(len(cols) - 2, -1, -1):
        mi = jnp.where(lane == k, cols[k], mi)
    mi_ref[...] = mi
    mw_ref[...] = jnp.where(lane == 0, w1, w2)
    sub = lax.broadcasted_iota(jnp.int32, (SUBLANES, LANES), 0)
    tbl = jnp.where(sub == 0, start, jnp.where(sub == 1, cnt, jnp.where(sub == 2, local, 0.0)))
    blk_ref[...] = tbl.astype(jnp.int32)


def _router(logits, t, tr):
    assert t % tr == 0 and tr % SUBLANES == 0
    out_row = pl.BlockSpec((tr, LANES), lambda p, i: (i * p, 0))
    return pl.pallas_call(
        _router_kernel,
        grid=(2, t // tr),
        in_specs=[pl.BlockSpec((tr, LANES), lambda p, i: (i, 0))],
        out_specs=[out_row, out_row, pl.BlockSpec((SUBLANES, LANES), lambda p, i: (0, 0)),
                   pl.BlockSpec((SUBLANES, LANES), lambda p, i: (i * p, 0))],
        out_shape=[jax.ShapeDtypeStruct((t, LANES), jnp.int32), jax.ShapeDtypeStruct((t, LANES), F32),
                   jax.ShapeDtypeStruct((SUBLANES, LANES), F32),
                   jax.ShapeDtypeStruct((t // tr * SUBLANES, LANES), jnp.int32)],
        scratch_shapes=[pltpu.VMEM((1, LANES), F32)],
        compiler_params=_params(("arbitrary", "arbitrary")),
        name="router",
    )(logits)


def _run_copies(src, dst, sem, src_row, dst_row, n_rows, wait=False):
    nsub = D_MODEL // LANES

    def copy(s0, d0, rows):
        dma = pltpu.make_async_copy(src.at[pl.ds(pl.multiple_of(s0 * nsub, nsub), rows * nsub), :],
                                    dst.at[pl.ds(pl.multiple_of(d0 * nsub, nsub), rows * nsub), :], sem)
        dma.wait() if wait else dma.start()

    def chunk(k, carry):
        copy(src_row + k * SUBLANES, dst_row + k * SUBLANES, SUBLANES)
        return carry

    n_chunks = n_rows // SUBLANES
    lax.fori_loop(0, n_chunks, chunk, 0)
    done = n_chunks * SUBLANES
    for rows in (4, 2, 1):
        @pl.when((n_rows & rows) != 0)
        def _(rows=rows, done=done):
            copy(src_row + done, dst_row + done, rows)

        done = done + (n_rows & rows)


def _dispatch_kernel(start_ref, cnt_ref, loc_ref, end_ref, tot_ref, h_ref, mi_ref, xs_hbm, stage, zeros, sem, zsem,
                     *, tm):
    i = pl.program_id(0)
    n = pl.num_programs(0)
    nsub = D_MODEL // LANES
    tb = h_ref.shape[0]
    slot = i % 2

    def wait_stage(sl):
        pltpu.make_async_copy(stage.at[sl], stage.at[sl], sem.at[sl]).wait()

    def pad_rows(wait):
        def one(e, carry):
            tot = tot_ref[e]
            n_pad = (tot + tm - 1) // tm * tm - tot
            _run_copies(zeros, xs_hbm, zsem, 0, end_ref[e] - n_pad, n_pad, wait=wait)
            return carry

        lax.fori_loop(0, end_ref.shape[0], one, 0)

        def spare(j, carry):
            dma = pltpu.make_async_copy(zeros, xs_hbm.at[pl.ds(pl.multiple_of(j * tm * nsub, nsub), tm * nsub), :], zsem)
            dma.wait() if wait else dma.start()
            return carry

        lax.fori_loop(end_ref[end_ref.shape[0] - 1] // tm, xs_hbm.shape[0] // (tm * nsub), spare, 0)

    @pl.when(i == 0)
    def _():
        zeros[...] = jnp.zeros_like(zeros)
        pad_rows(wait=False)

    @pl.when(i >= 2)
    def _():
        wait_stage(slot)

    li = mi_ref[...]
    rows_iota = lax.broadcasted_iota(jnp.int32, (tb, 2 * tb), 1)
    onehot = ((rows_iota == li[:, 4:5]) | (rows_iota == li[:, 5:6])).astype(BF16)
    srt = _dot_tn(onehot, h_ref[...])
    for c in range(nsub):
        stage[slot, pl.ds(c, 2 * tb, stride=nsub), :] = srt[:, c * LANES:(c + 1) * LANES]

    def run(e, carry):
        k = i * N_EXPERTS + e
        _run_copies(stage.at[slot], xs_hbm, sem.at[slot], loc_ref[k], start_ref[k], cnt_ref[k])
        return carry

    lax.fori_loop(0, N_EXPERTS, run, 0)

    @pl.when(i == n - 1)
    def _():
        wait_stage(slot)

        @pl.when(n >= 2)
        def _():
            wait_stage(1 - slot)

        pad_rows(wait=True)


def _dispatch(blk_start, blk_cnt, blk_loc, seg_end, seg_tot, h2, meta_i, n_slots, tb):
    t = meta_i.shape[0]
    nsub = D_MODEL // LANES
    assert t % tb == 0
    grid_spec = pltpu.PrefetchScalarGridSpec(
        num_scalar_prefetch=5,
        grid=(t // tb,),
        in_specs=[pl.BlockSpec((tb, D_MODEL), lambda i, *_: (i, 0)),
                  pl.BlockSpec((tb, LANES), lambda i, *_: (i, 0))],
        out_specs=pl.BlockSpec(memory_space=pl.ANY),
        scratch_shapes=[pltpu.VMEM((2, 2 * tb * nsub, LANES), F32),
                        pltpu.VMEM((EXPERT_TILE * nsub, LANES), F32),
                        pltpu.SemaphoreType.DMA((2,)), pltpu.SemaphoreType.DMA(())],
    )
    return pl.pallas_call(
        functools.partial(_dispatch_kernel, tm=EXPERT_TILE),
        grid_spec=grid_spec,
        out_shape=jax.ShapeDtypeStruct((n_slots * nsub, LANES), F32),
        compiler_params=_params(("arbitrary",)),
        name="dispatch",
    )(blk_start, blk_cnt, blk_loc, seg_end, seg_tot, h2, meta_i)


def _expert_kernel(te_ref, tv_ref, nxt_ref, xs_ref, wg_hbm, wu_hbm, wd_hbm, out_ref,
                   wg_st, wu_st, wd_st, wgb, wub, wdb, n_switch, sem):
    i = pl.program_id(0)
    tm = EXPERT_TILE
    nsub = D_MODEL // LANES
    valid = tv_ref[i] != 0
    e = te_ref[i]

    def weight_copies(expert, sl):
        return [pltpu.make_async_copy(hbm.at[0, expert], st.at[sl], sem.at[sl, k])
                for k, (hbm, st) in enumerate(((wg_hbm, wg_st), (wu_hbm, wu_st), (wd_hbm, wd_st)))]

    @pl.when(i == 0)
    def _():
        n_switch[0] = 0
        for dma in weight_copies(e, 0):
            dma.start()

    @pl.when(jnp.logical_not(valid))
    def _():
        out_ref[...] = jnp.zeros_like(out_ref)

    @pl.when(valid & ((i == 0) | (e != te_ref[jnp.maximum(i - 1, 0)])))
    def _():
        sl = n_switch[0] % 2
        n_switch[0] = n_switch[0] + 1
        for dma in weight_copies(e, sl):
            dma.wait()
        wgb[...] = wg_st[sl].astype(BF16)
        wub[...] = wu_st[sl].astype(BF16)
        wdb[...] = wd_st[sl].astype(BF16)

        @pl.when(nxt_ref[e] != e)
        def _():
            for dma in weight_copies(nxt_ref[e], 1 - sl):
                dma.start()

    @pl.when(valid)
    def _():
        h = jnp.concatenate([xs_ref[pl.ds(c, tm, stride=nsub), :] for c in range(nsub)], axis=1).astype(BF16)
        gate = _dot(h, wgb[...])
        up = _dot(h, wub[...])
        a = (gate * _sigmoid(gate) * up).astype(BF16)
        o = _dot(a, wdb[...])
        for c in range(nsub):
            out_ref[pl.ds(c, tm, stride=nsub), :] = o[:, c * LANES:(c + 1) * LANES]


def _experts(tile_expert, tile_valid, next_expert, xs, wg, wu, wd):
    n_tiles = tile_expert.shape[0]
    tm = EXPERT_TILE
    nsub = D_MODEL // LANES
    used = lambda i, tv: jnp.where(tv[i] != 0, i, 0)
    anywhere = pl.BlockSpec(memory_space=pl.ANY)
    up_shape, down_shape = (D_MODEL, D_EXPERT), (D_EXPERT, D_MODEL)
    grid_spec = pltpu.PrefetchScalarGridSpec(
        num_scalar_prefetch=3,
        grid=(n_tiles,),
        in_specs=[pl.BlockSpec((tm * nsub, LANES), lambda i, te, tv, nx: (used(i, tv), 0)),
                  anywhere, anywhere, anywhere],
        out_specs=pl.BlockSpec((tm * nsub, LANES), lambda i, te, tv, nx: (i, 0)),
        scratch_shapes=[pltpu.VMEM((2,) + up_shape, F32), pltpu.VMEM((2,) + up_shape, F32),
                        pltpu.VMEM((2,) + down_shape, F32),
                        pltpu.VMEM(up_shape, BF16), pltpu.VMEM(up_shape, BF16), pltpu.VMEM(down_shape, BF16),
                        pltpu.SMEM((1,), jnp.int32), pltpu.SemaphoreType.DMA((2, 3))],
    )
    return pl.pallas_call(
        _expert_kernel,
        grid_spec=grid_spec,
        out_shape=jax.ShapeDtypeStruct((n_tiles * tm * nsub, LANES), F32),
        compiler_params=_params(("arbitrary",)),
        name="experts",
    )(tile_expert, tile_valid, next_expert, xs, wg, wu, wd)


def _combine_kernel(p0_ref, p1_ref, x1_ref, mw_ref, rows_hbm, y_ref, g0, g1, sem, *, tc, blk_off):
    i = pl.program_id(0)
    n = pl.num_programs(0)
    nout = D_MODEL // LANES
    slot = i % 2

    def issue_all(step, sl):
        base = (step + blk_off) * tc

        def issue(j, carry):
            for pos_ref, dst, s in ((p0_ref, g0, 0), (p1_ref, g1, 1)):
                src0 = pl.multiple_of(pos_ref[base + j] * nout, nout)
                pltpu.make_async_copy(rows_hbm.at[pl.ds(src0, nout), :],
                                      dst.at[sl, pl.ds(pl.multiple_of(j * nout, nout), nout), :],
                                      sem.at[sl, s]).start()
            return carry

        lax.fori_loop(0, tc, issue, 0, unroll=4)

    @pl.when(i == 0)
    def _():
        issue_all(i, slot)

    @pl.when(i + 1 < n)
    def _():
        issue_all(i + 1, 1 - slot)

    for dst, s in ((g0, 0), (g1, 1)):
        pltpu.make_async_copy(rows_hbm.at[pl.ds(0, tc * nout), :], dst.at[slot], sem.at[slot, s]).wait()
    w0 = mw_ref[:, 0:1]
    w1 = mw_ref[:, 1:2]
    for c in range(nout):
        sl = slice(c * LANES, (c + 1) * LANES)
        y_ref[:, sl] = (x1_ref[:, sl] + w0 * g0[slot, pl.ds(c, tc, stride=nout), :]
                        + w1 * g1[slot, pl.ds(c, tc, stride=nout), :])


def _combine(pos0, pos1, x1, meta_w, rows, tc, tok_off):
    t = x1.shape[0]
    assert t % tc == 0 and tok_off % tc == 0
    blk_off = tok_off // tc
    nout = D_MODEL // LANES
    row = pl.BlockSpec((tc, D_MODEL), lambda i, p0, p1: (i, 0))
    grid_spec = pltpu.PrefetchScalarGridSpec(
        num_scalar_prefetch=2,
        grid=(t // tc,),
        in_specs=[row, pl.BlockSpec((tc, LANES), lambda i, p0, p1: (i + blk_off, 0)),
                  pl.BlockSpec(memory_space=pl.ANY)],
        out_specs=row,
        scratch_shapes=[pltpu.VMEM((2, tc * nout, LANES), F32), pltpu.VMEM((2, tc * nout, LANES), F32),
                        pltpu.SemaphoreType.DMA((2, 2))],
    )
    return pl.pallas_call(
        functools.partial(_combine_kernel, tc=tc, blk_off=blk_off),
        grid_spec=grid_spec,
        out_shape=jax.ShapeDtypeStruct((t, D_MODEL), F32),
        compiler_params=_params(("arbitrary",)),
        name="combine",
    )(pos0, pos1, x1, meta_w, rows)


def _rope_tables(pos):
    half = SWA_HEAD_DIM // 2
    inv_freq = ROPE_THETA ** (-jnp.arange(half, dtype=F32) / half)
    ang = pos.astype(F32)[:, None] * inv_freq[None, :]
    cos, sin = jnp.cos(ang), jnp.sin(ang)
    reps = LANES // SWA_HEAD_DIM
    return jnp.tile(jnp.concatenate([cos, cos], axis=1), (1, reps)), jnp.tile(jnp.concatenate([-sin, sin], axis=1), (1, reps))


def kernel(x_prompt, x_sample, state_gla, cache_swa_k0, cache_swa_v0, cache_swa_k1, cache_swa_v1, cache_swa_k2, cache_swa_v2, ln1_w, w_in, w_gla_lr, b_gla_lr, gla_onorm_w, q_norm_w, k_norm_w, w_branch_a, w_branch_b, w_out, ln2_w, w_router_group, b_router_group, w_router_expert, b_router_expert, w_exp_gate, w_exp_up, w_exp_down):
    b, s, d = x_prompt.shape
    bd, ls, _ = x_sample.shape
    tp, ts = b * s, bd * ls
    assert w_in.shape[0] == 1 and d == D_MODEL and ts % SUBLANES == 0
    k_caches = (cache_swa_k0, cache_swa_k1, cache_swa_k2)
    v_caches = (cache_swa_v0, cache_swa_v1, cache_swa_v2)

    w = w_in[0]
    cuts = np.cumsum((512, 512, 1024, GLA_RANK, 1024, 768, 768, 768, 1024, 1024))
    sec = lambda a: w[:, (0 if a == 0 else cuts[a - 1]):cuts[a]]
    lr_pad = jnp.pad(sec(3), ((0, 0), (0, LANES - GLA_RANK)))
    w_f32 = jnp.concatenate([sec(0), sec(1), sec(2), sec(4), sec(5), sec(6), sec(7), sec(8), sec(9), lr_pad], axis=1)
    w_packed = w_f32.astype(BF16)
    wlr_f32 = jnp.pad(w_gla_lr[0], ((0, LANES - GLA_RANK), (0, 0)))
    wlr = wlr_f32.astype(BF16)
    blr = b_gla_lr[0][None, :]
    nw = jnp.concatenate([jnp.tile(q_norm_w[0], SWA_WIDTH // SWA_HEAD_DIM), jnp.tile(k_norm_w[0], SWA_WIDTH // SWA_HEAD_DIM)])[None, :]
    gi = np.arange(256) // SWA_HEAD_DIM
    gmat_f32 = jnp.asarray((gi[:, None] == gi[None, :]).astype(np.float32) / SWA_HEAD_DIM)
    gmat = gmat_f32.astype(BF16)
    ln1 = ln1_w[0][None, :]
    ln2 = ln2_w[0][None, :]
    onw = gla_onorm_w[0][None, :]
    wa_f32, wb_f32, wo_f32 = w_branch_a[0], w_branch_b[0], w_out[0]
    wa, wb, wo = wa_f32.astype(BF16), wb_f32.astype(BF16), wo_f32.astype(BF16)
    wr = jnp.pad(jnp.concatenate([w_router_group[0], w_router_expert[0]], axis=1),
                 ((0, 0), (0, LANES - N_GROUPS - N_EXPERTS)))
    wr_hi, wr_lo = _split_bf16(wr)
    br = jnp.pad(jnp.concatenate([b_router_group[0], b_router_expert[0]]), (0, LANES - N_GROUPS - N_EXPERTS))[None, :]

    cos_p, sin_p = _rope_tables(jnp.arange(s, dtype=jnp.int32))
    cos_s, sin_s = _rope_tables(PAST_LEN + jnp.arange(ts, dtype=jnp.int32) % ls)

    tm_p = 512
    proj_p = _proj(x_prompt.reshape(tp, d), cos_p, sin_p, s // tm_p, ln1, w_packed, wlr, blr, nw, gmat, tm_p)
    proj_s = _proj(x_sample.reshape(ts, d), cos_s, sin_s, 1, ln1, w_f32, wlr_f32, blr, nw, gmat_f32, ts)
    gqkv_p, la_p, sog_p, q_p, k_p, v_p, sgab_p = proj_p
    gqkv_s, la_s, sog_s, q_s, k_s, v_s, sgab_s = proj_s

    r3 = lambda t, nb: t.reshape(nb, t.shape[0] // nb, t.shape[1])
    ya_p, st_p = _gla(r3(gqkv_p, b), r3(la_p, b), r3(sog_p, b),
                      jnp.zeros((b, GLA_HEADS, GLA_DK, GLA_DV), F32), onw, GLA_CHUNK, 512)
    pad_s = lambda t: jnp.pad(r3(t, bd), ((0, 0), (0, _SAMPLE_ROWS - ls), (0, 0)))
    ya_s, st_s = _gla(pad_s(gqkv_s), pad_s(la_s), pad_s(sog_s), state_gla[0], onw, _SAMPLE_ROWS, _SAMPLE_ROWS)
    ya_s = ya_s[:, :ls].reshape(ts, d)

    q3, k3, v3 = r3(q_p, b), r3(k_p, b), r3(v_p, b)
    swa_p = [_swa_prompt(q3, k3, v3, g, dil) for g, (_, dil) in enumerate(SWA_GROUPS)]
    to_t = lambda c: jnp.transpose(c[0], (0, 2, 3, 1))
    caches_t = [(to_t(k_caches[g]), to_t(v_caches[g])) for g in range(len(SWA_GROUPS))]
    new_t = lambda t: jnp.pad(jnp.transpose(r3(t, bd), (0, 2, 1)), ((0, 0), (0, 0), (LANES - ls, 0)))
    ob_s, *new_caches = _swa_sample(pad_s(q_s), pad_s(k_s), pad_s(v_s), new_t(k_s), new_t(v_s), caches_t, ls)
    ob_s = ob_s[:, :ls].reshape(ts, SWA_OUT)

    t = tp + ts
    mw = (wa, wb, wo, ln2, wr_hi, wr_lo, br)
    x1_p, h2c, lg = _merge(x_prompt.reshape(tp, d), ya_p.reshape(tp, d),
                           [o for o, _ in swa_p] + [l for _, l in swa_p], sgab_p, mw, 512, 0, t)
    mw_f32 = (wa_f32, wb_f32, wo_f32) + mw[3:]
    x1_s, h2c, lg = _merge(x_sample.reshape(ts, d), ya_s, [ob_s], sgab_s, mw_f32, ts, tp, t, shared=(h2c, lg))

    tb = DISPATCH_BLOCK
    meta_i, meta_w, cnt, blocks = _router(lg, t, tb)
    blocks = blocks.reshape(t // tb, SUBLANES, LANES)[:, :, :N_EXPERTS]
    blk_start, blk_cnt, blk_loc = (blocks[:, r].reshape(-1) for r in range(3))
    tm = EXPERT_TILE
    n_tiles = (2 * t) // tm + N_EXPERTS
    counts = cnt[0, :N_EXPERTS].astype(jnp.int32)
    ends = jnp.cumsum((counts + tm - 1) // tm * tm)
    pos0, pos1 = meta_i[:, 2], meta_i[:, 3]
    tile_start = jnp.arange(n_tiles, dtype=jnp.int32) * tm
    tile_valid = (tile_start < ends[-1]).astype(jnp.int32)
    last_slot = jnp.minimum(tile_start, ends[-1] - 1)
    tile_expert = jnp.minimum(jnp.sum((last_slot[:, None] >= ends[None, :]).astype(jnp.int32), axis=1), N_EXPERTS - 1)

    xs = _dispatch(blk_start, blk_cnt, blk_loc, ends, counts, h2c, meta_i, n_tiles * tm, tb)
    ids = jnp.arange(N_EXPERTS, dtype=jnp.int32)
    later_used = (counts[None, :] > 0) & (ids[None, :] > ids[:, None])
    next_expert = jnp.min(jnp.where(later_used, ids[None, :], N_EXPERTS), axis=1)
    next_expert = jnp.where(next_expert == N_EXPERTS, ids, next_expert)
    rows = _experts(tile_expert, tile_valid, next_expert, xs, w_exp_gate, w_exp_up, w_exp_down)
    y_p = _combine(pos0, pos1, x1_p, meta_w, rows, 256, 0)
    y_s = _combine(pos0, pos1, x1_s, meta_w, rows, ts, tp)

    heads = lambda a: a.reshape(1, a.shape[0], a.shape[1], SWA_HPG, SWA_HEAD_DIM)
    outs = [y_p.reshape(b, s, d), y_s.reshape(bd, ls, d), st_p[None].astype(x_prompt.dtype)]
    for g, (win, _) in enumerate(SWA_GROUPS):
        keep = min(win, s)
        gsl = slice(g * SWA_OUT, (g + 1) * SWA_OUT)
        outs += [heads(k3[:, s - keep:, gsl]), heads(v3[:, s - keep:, gsl])]
    outs.append(st_s[None].astype(state_gla.dtype))
    outs += [jnp.transpose(c, (0, 3, 1, 2))[None] for c in new_caches]
    return tuple(outs)
```

```python
import functools

import numpy as np
import jax
import jax.numpy as jnp
from jax import lax
from jax.experimental import pallas as pl
from jax.experimental.pallas import tpu as pltpu

F32 = jnp.float32
BF16 = jnp.bfloat16

D_MODEL = 1024
PAST_LEN = 16384
GLA_HEADS = 4
GLA_DK = 128
GLA_DV = 256
GLA_RANK = 16
GLA_TAU = 16.0
GLA_CHUNK = 64
SWA_GROUPS = ((128, 1), (512, 4), (2048, 16))
SWA_HPG = 4
SWA_HEAD_DIM = 64
SWA_WIDTH = 768
SWA_OUT = 256
SWA_BLOCK = 128
ROPE_THETA = 10000.0
N_GROUPS = 4
EXPERTS_PER_GROUP = 8
N_EXPERTS = 32
D_EXPERT = 512
EPS = 1e-6

LANES = 128
SUBLANES = 8
VMEM_LIMIT = 56 * 1024 * 1024
NEG = -1e30
MXU_WIDTH = 256
EXPERT_TILE = 256
MERGE_TILE = 512
DISPATCH_BLOCK = 384

_C_GQKV = (0, 2048)
_C_GOG = (2048, 3072)
_C_QK = (3072, 4608)
_C_V = (4608, 5376)
_C_GAB = (5376, 7424)
_C_LR = (7424, 7552)


def _contract(a, b, dims):
    dg = lambda x, y: lax.dot_general(x, y, (dims, ((), ())), preferred_element_type=F32)
    if a.dtype == F32 and b.dtype == F32:
        a_hi, a_lo = _split(a, BF16)
        b_hi, b_lo = _split(b, BF16)
        return dg(a_hi, b_hi) + dg(a_hi, b_lo) + dg(a_lo, b_hi)
    return dg(a, b)


def _dot(a, b):
    return _contract(a, b, ((1,), (0,)))


def _dot_nt(a, b):
    return _contract(a, b, ((1,), (1,)))


def _dot_tn(a, b):
    return _contract(a, b, ((0,), (0,)))


def _sigmoid(x):
    return 1.0 / (1.0 + jnp.exp(-x))


def _split(x, dt):
    hi = x.astype(dt)
    lo = (x - hi.astype(F32)).astype(dt)
    return hi, lo


def _split_bf16(x):
    return _split(x, BF16)


def _params(sem):
    return pltpu.CompilerParams(dimension_semantics=sem, vmem_limit_bytes=VMEM_LIMIT)


def _resident(shape):
    nd = len(shape)
    return pl.BlockSpec(shape, lambda *_: (0,) * nd, pipeline_mode=pl.Buffered(1))


def _proj_kernel(x_ref, cos_ref, sin_ref, ln_ref, w_ref, wlr_ref, blr_ref, nw_ref, g_ref,
                 gqkv_ref, la_ref, sog_ref, q_ref, k_ref, v_ref, sgab_ref):
    cdt = w_ref.dtype
    x = x_ref[...]
    h = (x * lax.rsqrt(jnp.mean(x * x, axis=-1, keepdims=True) + EPS) * ln_ref[...]).astype(cdt)
    gqkv_ref[...] = _dot(h, w_ref[:, _C_GQKV[0]:_C_GQKV[1]]).astype(cdt)
    og = _dot(h, w_ref[:, _C_GOG[0]:_C_GOG[1]])
    sog_ref[...] = (og * _sigmoid(og)).astype(cdt)
    lr = _dot(h, w_ref[:, _C_LR[0]:_C_LR[1]]).astype(cdt)
    z = _dot(lr, wlr_ref[...]) + blr_ref[...]
    la_ref[...] = (jnp.minimum(z, 0.0) - jnp.log(1.0 + jnp.exp(-jnp.abs(z)))) / GLA_TAU
    qk = _dot(h, w_ref[:, _C_QK[0]:_C_QK[1]])
    sq = (qk * qk).astype(cdt)
    ms = jnp.concatenate([_dot(sq[:, c * 256:(c + 1) * 256], g_ref[...]) for c in range(6)], axis=1)
    qn = qk * lax.rsqrt(ms + EPS) * nw_ref[...]
    width = 2 * SWA_WIDTH
    cos = jnp.tile(cos_ref[...], (1, width // LANES))
    sin = jnp.tile(sin_ref[...], (1, width // LANES))
    lane = lax.broadcasted_iota(jnp.int32, qn.shape, 1)
    half = SWA_HEAD_DIM // 2
    rot = jnp.where(lane % SWA_HEAD_DIM < half, pltpu.roll(qn, width - half, 1), pltpu.roll(qn, half, 1))
    qr = qn * cos + rot * sin
    q_ref[...] = qr[:, :SWA_WIDTH]
    k_ref[...] = qr[:, SWA_WIDTH:]
    v_ref[...] = _dot(h, w_ref[:, _C_V[0]:_C_V[1]])
    gab = _dot(h, w_ref[:, _C_GAB[0]:_C_GAB[1]])
    sgab_ref[...] = _sigmoid(gab).astype(cdt)


def _proj(x, cos, sin, rope_blocks, ln, w, wlr, blr, nw, g, tm):
    t = x.shape[0]
    assert t % tm == 0 and w.dtype == wlr.dtype == g.dtype
    row = lambda width: pl.BlockSpec((tm, width), lambda i: (i, 0))
    outs = [(2048, w.dtype), (512, F32), (1024, w.dtype), (768, F32), (768, F32), (768, F32), (2048, w.dtype)]
    return pl.pallas_call(
        _proj_kernel,
        grid=(t // tm,),
        in_specs=[row(D_MODEL),
                  pl.BlockSpec((tm, LANES), lambda i: (i % rope_blocks, 0)),
                  pl.BlockSpec((tm, LANES), lambda i: (i % rope_blocks, 0)),
                  _resident(ln.shape), _resident(w.shape), _resident(wlr.shape), _resident(blr.shape),
                  _resident(nw.shape), _resident(g.shape)],
        out_specs=[row(wd) for wd, _ in outs],
        out_shape=[jax.ShapeDtypeStruct((t, wd), dt) for wd, dt in outs],
        compiler_params=_params(("arbitrary",)),
        name="proj",
    )(x, cos, sin, ln, w, wlr, blr, nw, g)


def _gla_kernel(gqkv_ref, la_ref, sog_ref, s0_ref, onw_ref, y_ref, sfin_ref, st_scr, *, chunk, n_chunks):
    j = pl.program_id(1)

    @pl.when(j == 0)
    def _():
        for h in range(GLA_HEADS):
            st_scr[h] = s0_ref[0, h].T

    r = lax.broadcasted_iota(jnp.int32, (chunk, chunk), 0)
    c = lax.broadcasted_iota(jnp.int32, (chunk, chunk), 1)
    causal = r >= c
    cdt = gqkv_ref.dtype
    tri = causal.astype(cdt)
    hk = GLA_HEADS * GLA_DK
    rows = [slice(ci * chunk, (ci + 1) * chunk) for ci in range(n_chunks)]

    la_hi, la_lo = _split(la_ref[0], cdt)
    b_chunks = [_dot(tri, la_hi[rs]) + _dot(tri, la_lo[rs]) for rs in rows]
    last = [bc[chunk - 1:chunk, :] for bc in b_chunks]
    b = jnp.concatenate(b_chunks, axis=0)
    b_end = jnp.concatenate([jnp.broadcast_to(x, (chunk, hk)) for x in last], axis=0)
    q = gqkv_ref[0, :, :hk].astype(F32) * GLA_DK ** -0.5
    k = gqkv_ref[0, :, hk:2 * hk].astype(F32)
    qd = (q * jnp.exp(b)).astype(cdt)
    kd = (k * jnp.exp(-b)).astype(cdt)
    kdec = (k * jnp.exp(b_end - b)).astype(cdt)
    decay = [jnp.exp(x) for x in last]

    state = [st_scr[h] for h in range(GLA_HEADS)]
    for ci, rs in enumerate(rows):
        outs = []
        for h in range(GLA_HEADS):
            sl = slice(h * GLA_DK, (h + 1) * GLA_DK)
            v_h = gqkv_ref[0, rs, 2 * hk + h * GLA_DV:2 * hk + (h + 1) * GLA_DV]
            att = jnp.where(causal, _dot_nt(qd[rs, sl], kd[rs, sl]), 0.0).astype(cdt)
            o = _dot(att, v_h) + _dot_nt(qd[rs, sl], state[h].astype(cdt))
            state[h] = state[h] * decay[ci][:, sl] + _dot_tn(v_h, kdec[rs, sl])
            ms = jnp.mean(o * o, axis=-1, keepdims=True)
            outs.append(o * lax.rsqrt(ms + EPS) * onw_ref[...])
        o_all = jnp.concatenate(outs, axis=1) * sog_ref[0, rs, :].astype(F32)
        y_ref[0, rs, :] = o_all.astype(cdt)
    for h in range(GLA_HEADS):
        st_scr[h] = state[h]

    @pl.when(j == pl.num_programs(1) - 1)
    def _():
        for h in range(GLA_HEADS):
            sfin_ref[0, h] = state[h].T


def _gla(gqkv, la, sog, s0, onw, chunk, block):
    b, l, _ = gqkv.shape
    tok = lambda width: pl.BlockSpec((1, block, width), lambda bi, j: (bi, j, 0))
    st = pl.BlockSpec((1, GLA_HEADS, GLA_DK, GLA_DV), lambda bi, j: (bi, 0, 0, 0))
    return pl.pallas_call(
        functools.partial(_gla_kernel, chunk=chunk, n_chunks=block // chunk),
        grid=(b, l // block),
        in_specs=[tok(2048), tok(512), tok(1024), st, _resident(onw.shape)],
        out_specs=[tok(1024), st],
        out_shape=[jax.ShapeDtypeStruct((b, l, 1024), gqkv.dtype),
                   jax.ShapeDtypeStruct((b, GLA_HEADS, GLA_DK, GLA_DV), F32)],
        scratch_shapes=[pltpu.VMEM((GLA_HEADS, GLA_DV, GLA_DK), F32)],
        compiler_params=_params(("arbitrary", "arbitrary")),
        name="gla",
    )(gqkv, la, sog, s0, onw)


def _band_heads(q, kw, vw, valid):
    n = q.shape[0]
    low = lax.broadcasted_iota(jnp.int32, (n, LANES), 1) < SWA_HEAD_DIM
    ones = jnp.ones((kw.shape[0], LANES), kw.dtype)
    outs, lses = [], []
    for pair in range(SWA_OUT // LANES):
        cols = slice(pair * LANES, (pair + 1) * LANES)
        v_aug = jnp.concatenate([vw[:, cols], ones], axis=1)
        res = []
        for mine in (low, jnp.logical_not(low)):
            qm = jnp.where(mine, q[:, cols], 0.0).astype(kw.dtype)
            s = _dot_nt(qm, kw[:, cols]) * SWA_HEAD_DIM ** -0.5
            s = jnp.where(valid, s, NEG)
            m = jnp.max(s, axis=-1, keepdims=True)
            r = _dot(jnp.exp(s - m).astype(vw.dtype), v_aug)
            den = r[:, LANES:]
            res.append((r[:, :LANES] / den, m + jnp.log(den)))
        outs.append(jnp.where(low, res[0][0], res[1][0]))
        lses.append(jnp.where(low, res[0][1], res[1][1]))
    return jnp.concatenate(outs, axis=1), jnp.concatenate(lses, axis=1)


_SWA_TOKENS = 2048


def _swa_kernel(q_ref, k_ref, v_ref, kp_ref, vp_ref, o_ref, lse_ref, *stage, dil):
    blk = SWA_BLOCK
    nsub = q_ref.shape[1] // (blk * dil)
    first = pl.program_id(1) == 0
    qi = lax.broadcasted_iota(jnp.int32, (blk, 2 * blk), 0)
    kj = lax.broadcasted_iota(jnp.int32, (blk, 2 * blk), 1)
    band = (kj >= qi) & (kj <= qi + blk)
    halves = SWA_OUT // LANES

    if dil > 1:
        ins = (q_ref, k_ref, v_ref, kp_ref, vp_ref)
        q_ref, k_ref, v_ref, kp_ref, vp_ref, o_st, lse_st = stage
        for src, dst in zip(ins, stage):
            for hf in range(halves):
                dst[hf] = src[0, :, hf * LANES:(hf + 1) * LANES]

    def rows(ref, start):
        if dil == 1:
            return ref[0, pl.ds(start, blk), :]
        return jnp.concatenate([ref[hf, pl.ds(start, blk, stride=dil), :] for hf in range(halves)], axis=1)

    def unit(u, carry):
        r = u // nsub
        j = u % nsub
        start = r + dil * blk * j
        inside = r + dil * blk * jnp.maximum(j - 1, 0)
        if dil == 1:
            r, start, inside = 0, pl.multiple_of(start, blk), pl.multiple_of(inside, blk)
        head = j == 0
        kprev = jnp.where(head, rows(kp_ref, r), rows(k_ref, inside))
        vprev = jnp.where(head, rows(vp_ref, r), rows(v_ref, inside))
        kw = jnp.concatenate([kprev, rows(k_ref, start)], axis=0).astype(BF16)
        vw = jnp.concatenate([vprev, rows(v_ref, start)], axis=0).astype(BF16)
        valid = band & (kj >= jnp.where(head & first, blk, 0))
        o, lse = _band_heads(rows(q_ref, start), kw, vw, valid)
        if dil == 1:
            o_ref[0, pl.ds(start, blk), :] = o
            lse_ref[0, pl.ds(start, blk), :] = lse
        else:
            for hf in range(halves):
                o_st[hf, pl.ds(start, blk, stride=dil), :] = o[:, hf * LANES:(hf + 1) * LANES]
                lse_st[hf, pl.ds(start, blk, stride=dil), :] = lse[:, hf * LANES:(hf + 1) * LANES]
        return carry

    lax.fori_loop(0, dil * nsub, unit, 0)
    if dil > 1:
        for hf in range(halves):
            o_ref[0, :, hf * LANES:(hf + 1) * LANES] = o_st[hf]
            lse_ref[0, :, hf * LANES:(hf + 1) * LANES] = lse_st[hf]


def _swa_prompt(q, k, v, g, dil):
    b, s, _ = q.shape
    tb = _SWA_TOKENS
    back = SWA_BLOCK * dil
    assert s % tb == 0 and tb % back == 0
    cur = pl.BlockSpec((1, tb, SWA_OUT), lambda bi, i: (bi, i, g))
    prev = pl.BlockSpec((1, back, SWA_OUT), lambda bi, i: (bi, jnp.maximum(i * (tb // back) - 1, 0), g))
    out = pl.BlockSpec((1, tb, SWA_OUT), lambda bi, i: (bi, i, 0))
    halves = SWA_OUT // LANES
    stage = [pltpu.VMEM((halves, n, LANES), F32) for n in (tb, tb, tb, back, back, tb, tb)] if dil > 1 else []
    o, lse = pl.pallas_call(
        functools.partial(_swa_kernel, dil=dil),
        grid=(b, s // tb),
        in_specs=[cur, cur, cur, prev, prev],
        out_specs=[out, out],
        out_shape=[jax.ShapeDtypeStruct((b, s, SWA_OUT), F32)] * 2,
        scratch_shapes=stage,
        compiler_params=_params(("arbitrary", "arbitrary")),
        name=f"swa_prompt_g{g}",
    )(q, k, v, k, v)
    return o.reshape(b * s, SWA_OUT), lse.reshape(b * s, SWA_OUT)


_SAMPLE_ROWS = 16


def _swa_sample_kernel(q_ref, kn_ref, vn_ref, knt_ref, vnt_ref, k0_ref, v0_ref, k1_ref, v1_ref, k2_ref, v2_ref,
                       ob_ref, ok0_ref, ov0_ref, ok1_ref, ov1_ref, ok2_ref, ov2_ref, *, n_new):
    rows = _SAMPLE_ROWS
    in_refs = ((k0_ref, v0_ref), (k1_ref, v1_ref), (k2_ref, v2_ref))
    out_refs = ((ok0_ref, ov0_ref), (ok1_ref, ov1_ref), (ok2_ref, ov2_ref))
    scale = SWA_HEAD_DIM ** -0.5
    jn = lax.broadcasted_iota(jnp.int32, (rows, rows), 1)
    ln = lax.broadcasted_iota(jnp.int32, (rows, rows), 0)
    tail = lax.broadcasted_iota(jnp.int32, (SWA_HEAD_DIM, LANES), 1) >= LANES - n_new
    o_g, lse_g = [], []
    for g, (win, dil) in enumerate(SWA_GROUPS):
        jc = lax.broadcasted_iota(jnp.int32, (rows, win), 1)
        lc = lax.broadcasted_iota(jnp.int32, (rows, win), 0)
        valid_c = (jc >= lc) & (((jc - lc) & (dil - 1)) == 0)
        valid_n = (jn <= ln) & (((ln - jn) & (dil - 1)) == 0) & (jn < n_new)
        o_h, lse_h = [], []
        for h in range(SWA_HPG):
            col = g * SWA_OUT + h * SWA_HEAD_DIM
            hsl = slice(col, col + SWA_HEAD_DIM)
            qh = q_ref[0, :, hsl]
            knh = kn_ref[0, :, hsl]
            vnh = vn_ref[0, :, hsl]
            for (src, dst, new_t) in ((in_refs[g][0], out_refs[g][0], knt_ref), (in_refs[g][1], out_refs[g][1], vnt_ref)):
                old = src[0, h]
                moved = pltpu.roll(old, win - n_new, 1)
                if win > LANES:
                    dst[0, h, :, 0:win - LANES] = moved[:, 0:win - LANES]
                dst[0, h, :, win - LANES:win] = jnp.where(tail, new_t[0, hsl, :], moved[:, win - LANES:win])
            kt = in_refs[g][0][0, h]
            vt = in_refs[g][1][0, h]
            s_c = jnp.where(valid_c, _dot(qh, kt) * scale, NEG)
            s_n = jnp.where(valid_n, _dot_nt(qh, knh) * scale, NEG)
            m = jnp.maximum(jnp.max(s_c, axis=-1, keepdims=True), jnp.max(s_n, axis=-1, keepdims=True))
            p_c = jnp.exp(s_c - m)
            p_n = jnp.exp(s_n - m)
            den = jnp.sum(p_c, axis=-1, keepdims=True) + jnp.sum(p_n, axis=-1, keepdims=True)
            o_h.append((_dot_nt(p_c, vt) + _dot(p_n, vnh)) / den)
            lse_h.append(jnp.broadcast_to(m + jnp.log(den), (rows, SWA_HEAD_DIM)))
        o_g.append(jnp.concatenate(o_h, axis=1))
        lse_g.append(jnp.concatenate(lse_h, axis=1))
    lmax = jnp.maximum(jnp.maximum(lse_g[0], lse_g[1]), lse_g[2])
    e = [jnp.exp(x - lmax) for x in lse_g]
    ob_ref[0] = (e[0] * o_g[0] + e[1] * o_g[1] + e[2] * o_g[2]) / (e[0] + e[1] + e[2])


def _swa_sample(q, kn, vn, knt, vnt, caches_t, n_new):
    bd = q.shape[0]
    rows = _SAMPLE_ROWS
    tok = pl.BlockSpec((1, rows, SWA_WIDTH), lambda bi: (bi, 0, 0))
    new_t = pl.BlockSpec((1, SWA_WIDTH, LANES), lambda bi: (bi, 0, 0))
    specs, args = [tok, tok, tok, new_t, new_t], [q, kn, vn, knt, vnt]
    out_specs = [pl.BlockSpec((1, rows, SWA_OUT), lambda bi: (bi, 0, 0))]
    out_shape = [jax.ShapeDtypeStruct((bd, rows, SWA_OUT), F32)]
    for g, (win, dil) in enumerate(SWA_GROUPS):
        for t in caches_t[g]:
            assert t.shape == (bd, SWA_HPG, SWA_HEAD_DIM, win) and win == SWA_BLOCK * dil and win % LANES == 0
            spec = pl.BlockSpec((1, SWA_HPG, SWA_HEAD_DIM, win), lambda bi: (bi, 0, 0, 0))
            args.append(t)
            specs.append(spec)
            out_specs.append(spec)
            out_shape.append(jax.ShapeDtypeStruct(t.shape, t.dtype))
    return pl.pallas_call(
        functools.partial(_swa_sample_kernel, n_new=n_new),
        grid=(bd,),
        in_specs=specs,
        out_specs=out_specs,
        out_shape=out_shape,
        compiler_params=_params(("arbitrary",)),
        name="swa_sample",
    )(*args)


def _merge_kernel(*refs, combine, n_alias, n_real):
    x1_ref, h2c_ref, lg_ref, cnt_ref = refs[-4:]
    refs = refs[:len(refs) - 4 - n_alias]

    @pl.when(pl.program_id(0) >= n_real)
    def _():
        h2c_ref[...] = jnp.zeros_like(h2c_ref)
        lg_ref[...] = jnp.zeros_like(lg_ref)
        cnt_ref[...] = jnp.zeros_like(cnt_ref)

    pl.when(pl.program_id(0) < n_real)(
        functools.partial(_merge_tile, refs, x1_ref, h2c_ref, lg_ref, cnt_ref, combine))


def _merge_tile(refs, x1_ref, h2c_ref, lg_ref, cnt_ref, combine):
    if combine:
        (x_ref, ya_ref, o0, o1, o2, l0, l1, l2, sgab_ref, wa_ref, wb_ref, wo_ref, ln_ref, wr_cat_ref, wr_hi_ref,
         br_ref) = refs
        lmax = jnp.maximum(jnp.maximum(l0[...], l1[...]), l2[...])
        e0, e1, e2 = jnp.exp(l0[...] - lmax), jnp.exp(l1[...] - lmax), jnp.exp(l2[...] - lmax)
        ob = ((e0 * o0[...] + e1 * o1[...] + e2 * o2[...]) / (e0 + e1 + e2)).astype(wb_ref.dtype)
    else:
        (x_ref, ya_ref, ob_ref, sgab_ref, wa_ref, wb_ref, wo_ref, ln_ref, wr_cat_ref, wr_hi_ref, br_ref) = refs
        ob = ob_ref[...]
    ya = _dot(ya_ref[...], wa_ref[...])
    yb = _dot(ob, wb_ref[...])
    sga = sgab_ref[:, :D_MODEL].astype(F32)
    sgb = sgab_ref[:, D_MODEL:].astype(F32)
    x1 = x_ref[...] + _dot((sga * ya + sgb * yb).astype(wo_ref.dtype), wo_ref[...])
    x1_ref[...] = x1
    h2 = x1 * lax.rsqrt(jnp.mean(x1 * x1, axis=-1, keepdims=True) + EPS) * ln_ref[...]
    h_hi, h_lo = _split_bf16(h2)
    both = _dot(h_hi, wr_cat_ref[...])
    lg = both[:, :LANES] + both[:, LANES:] + _dot(h_lo, wr_hi_ref[...]) + br_ref[...]
    lg_ref[...] = lg
    h2c_ref[...] = h_hi
    _, _, _, _, _, hot1, hot2 = _route_select(lg)
    cnt = jnp.sum((hot1 | hot2).astype(F32), axis=0, keepdims=True)
    cnt_ref[...] = jnp.broadcast_to(cnt, cnt_ref.shape)


def _merge(x, ya_in, swa, sgab, weights, tm, tok_off, t_all, shared=None):
    t = x.shape[0]
    assert t % tm == 0 and tok_off % tm == 0 and MERGE_TILE % tm == 0
    combine = len(swa) > 1
    blk_off = tok_off // tm
    t_buf = -(-t_all // MERGE_TILE) * MERGE_TILE
    nsub = D_MODEL // LANES
    n_real = t // tm
    n_fill = 0 if shared is not None else (t_buf - tok_off - t) // tm
    row = lambda width: pl.BlockSpec((tm, width), lambda i: (jnp.minimum(i, n_real - 1), 0))
    shared_in = [] if shared is None else list(shared)
    n_in = 3 + len(swa) + len(weights)
    return pl.pallas_call(
        functools.partial(_merge_kernel, combine=combine, n_alias=len(shared_in), n_real=n_real),
        grid=(n_real + n_fill,),
        in_specs=[row(D_MODEL), row(D_MODEL)] + [row(SWA_OUT)] * len(swa) + [row(2 * D_MODEL)]
                 + [_resident(w.shape) for w in weights] + [pl.BlockSpec(memory_space=pl.ANY)] * len(shared_in),
        out_specs=[row(D_MODEL),
                   pl.BlockSpec((tm, D_MODEL), lambda i: (i + blk_off, 0)),
                   pl.BlockSpec((tm, LANES), lambda i: (i + blk_off, 0)),
                   pl.BlockSpec((SUBLANES, LANES), lambda i: (i, 0))],
        out_shape=[jax.ShapeDtypeStruct((t, D_MODEL), F32),
                   jax.ShapeDtypeStruct((t_buf, D_MODEL), BF16),
                   jax.ShapeDtypeStruct((t_buf, LANES), F32),
                   jax.ShapeDtypeStruct(((n_real + n_fill) * SUBLANES, LANES), F32)],
        input_output_aliases={n_in + k: 1 + k for k in range(len(shared_in))},
        compiler_params=_params(("arbitrary",)),
        name="merge",
    )(x, ya_in, *swa, sgab, *weights, *shared_in)


def _route_select(lg):
    lane = lax.broadcasted_iota(jnp.int32, lg.shape, 1)
    big = jnp.int32(LANES)
    gl = jnp.where(lane < N_GROUPS, lg, NEG)
    gmax = jnp.max(gl, axis=-1, keepdims=True)
    g_idx = jnp.min(jnp.where(gl == gmax, lane, big), axis=-1, keepdims=True)
    g_w = 1.0 / jnp.sum(jnp.exp(gl - gmax), axis=-1, keepdims=True)
    e_lane = lane - N_GROUPS
    in_group = (e_lane >= 0) & (e_lane < N_EXPERTS) & (e_lane // EXPERTS_PER_GROUP == g_idx)
    el = jnp.where(in_group, lg, NEG)
    v1 = jnp.max(el, axis=-1, keepdims=True)
    i1 = jnp.min(jnp.where(el == v1, lane, big), axis=-1, keepdims=True)
    el2 = jnp.where(lane == i1, NEG, el)
    v2 = jnp.max(el2, axis=-1, keepdims=True)
    i2 = jnp.min(jnp.where(el2 == v2, lane, big), axis=-1, keepdims=True)
    r21 = jnp.exp(v2 - v1)
    w1 = g_w / (1.0 + r21)
    w2 = g_w * r21 / (1.0 + r21)
    e1 = i1 - N_GROUPS
    e2 = i2 - N_GROUPS
    return lane, e1, e2, w1, w2, lane == e1, lane == e2


def _router_kernel(lg_ref, offs_ref, mi_ref, mw_ref, blk_ref, carry):
    @pl.when(pl.program_id(0) == 0)
    def _():
        carry[...] = offs_ref[0:1, :]

    lane, e1, e2, w1, w2, hot1, hot2 = _route_select(lg_ref[...])
    tr = lane.shape[0]
    hot = (hot1 | hot2).astype(BF16)
    start = carry[...]
    cnt = jnp.sum(hot.astype(F32), axis=0, keepdims=True)
    carry[...] = start + cnt
    r = lax.broadcasted_iota(jnp.int32, (tr, tr), 0)
    c = lax.broadcasted_iota(jnp.int32, (tr, tr), 1)
    within = _dot((r > c).astype(BF16), hot)
    lane1 = lax.broadcasted_iota(jnp.int32, (SUBLANES, LANES), 1)
    incl = jnp.broadcast_to(cnt, (SUBLANES, LANES))
    for sh in (1, 2, 4, 8, 16):
        incl = incl + jnp.where(lane1 >= sh, pltpu.roll(incl, sh, 1), 0.0)
    local = incl[0:1, :] - cnt
    pick = lambda hot_k, row: jnp.sum(jnp.where(hot_k, row, 0.0), axis=-1, keepdims=True)
    cols = [e1, e2]
    for hot_k in (hot1, hot2):
        cols.append((pick(hot_k, within) + pick(hot_k, start)).astype(jnp.int32))
    for hot_k in (hot1, hot2):
        cols.append((pick(hot_k, within) + pick(hot_k, local)).astype(jnp.int32))
    mi = cols[-1]
    for k in range(len(cols) - 2, -1, -1):
        mi = jnp.where(lane == k, cols[k], mi)
    mi_ref[...] = mi
    mw_ref[...] = jnp.where(lane == 0, w1, w2)
    sub = lax.broadcasted_iota(jnp.int32, (SUBLANES, LANES), 0)
    tbl = jnp.where(sub == 0, start, jnp.where(sub == 1, cnt, jnp.where(sub == 2, local, 0.0)))
    blk_ref[...] = tbl.astype(jnp.int32)


def _router(logits, seg_start, t, tr):
    assert t % tr == 0 and tr % SUBLANES == 0
    row = pl.BlockSpec((tr, LANES), lambda i: (i, 0))
    small = pl.BlockSpec((SUBLANES, LANES), lambda i: (i, 0))
    return pl.pallas_call(
        _router_kernel,
        grid=(t // tr,),
        in_specs=[row, pl.BlockSpec((SUBLANES, LANES), lambda i: (0, 0))],
        out_specs=[row, row, small],
        out_shape=[jax.ShapeDtypeStruct((t, LANES), jnp.int32), jax.ShapeDtypeStruct((t, LANES), F32),
                   jax.ShapeDtypeStruct((t // tr * SUBLANES, LANES), jnp.int32)],
        scratch_shapes=[pltpu.VMEM((1, LANES), F32)],
        compiler_params=_params(("arbitrary",)),
        name="router",
    )(logits, seg_start)


def _run_copies(src, dst, sem, src_row, dst_row, n_rows, wait=False):
    nsub = D_MODEL // LANES

    def copy(s0, d0, rows):
        dma = pltpu.make_async_copy(src.at[pl.ds(pl.multiple_of(s0 * nsub, nsub), rows * nsub), :],
                                    dst.at[pl.ds(pl.multiple_of(d0 * nsub, nsub), rows * nsub), :], sem)
        dma.wait() if wait else dma.start()

    def chunk(k, carry):
        copy(src_row + k * SUBLANES, dst_row + k * SUBLANES, SUBLANES)
        return carry

    n_chunks = n_rows // SUBLANES
    lax.fori_loop(0, n_chunks, chunk, 0)
    done = n_chunks * SUBLANES
    for rows in (4, 2, 1):
        @pl.when((n_rows & rows) != 0)
        def _(rows=rows, done=done):
            copy(src_row + done, dst_row + done, rows)

        done = done + (n_rows & rows)


def _dispatch_kernel(start_ref, cnt_ref, loc_ref, end_ref, tot_ref, h_ref, mi_ref, xs_hbm, stage, zeros, sem, zsem,
                     *, tm):
    i = pl.program_id(0)
    n = pl.num_programs(0)
    nsub = D_MODEL // LANES
    tb = h_ref.shape[0]
    slot = i % 2

    def wait_stage(sl):
        pltpu.make_async_copy(stage.at[sl], stage.at[sl], sem.at[sl]).wait()

    def pad_rows(wait):
        def one(e, carry):
            tot = tot_ref[e]
            n_pad = (tot + tm - 1) // tm * tm - tot
            _run_copies(zeros, xs_hbm, zsem, 0, end_ref[e] - n_pad, n_pad, wait=wait)
            return carry

        lax.fori_loop(0, end_ref.shape[0], one, 0)

        def spare(j, carry):
            dma = pltpu.make_async_copy(zeros, xs_hbm.at[pl.ds(pl.multiple_of(j * tm * nsub, nsub), tm * nsub), :], zsem)
            dma.wait() if wait else dma.start()
            return carry

        lax.fori_loop(end_ref[end_ref.shape[0] - 1] // tm, xs_hbm.shape[0] // (tm * nsub), spare, 0)

    @pl.when(i == 0)
    def _():
        zeros[...] = jnp.zeros_like(zeros)
        pad_rows(wait=False)

    @pl.when(i >= 2)
    def _():
        wait_stage(slot)

    li = mi_ref[...]
    rows_iota = lax.broadcasted_iota(jnp.int32, (tb, 2 * tb), 1)
    onehot = ((rows_iota == li[:, 4:5]) | (rows_iota == li[:, 5:6])).astype(BF16)
    srt = _dot_tn(onehot, h_ref[...])
    for c in range(nsub):
        stage[slot, pl.ds(c, 2 * tb, stride=nsub), :] = srt[:, c * LANES:(c + 1) * LANES]

    def run(e, carry):
        k = i * N_EXPERTS + e
        _run_copies(stage.at[slot], xs_hbm, sem.at[slot], loc_ref[k], start_ref[k], cnt_ref[k])
        return carry

    lax.fori_loop(0, N_EXPERTS, run, 0)

    @pl.when(i == n - 1)
    def _():
        wait_stage(slot)

        @pl.when(n >= 2)
        def _():
            wait_stage(1 - slot)

        pad_rows(wait=True)


def _dispatch(blk_start, blk_cnt, blk_loc, seg_end, seg_tot, h2, meta_i, n_slots, tb):
    t = meta_i.shape[0]
    nsub = D_MODEL // LANES
    assert t % tb == 0
    grid_spec = pltpu.PrefetchScalarGridSpec(
        num_scalar_prefetch=5,
        grid=(t // tb,),
        in_specs=[pl.BlockSpec((tb, D_MODEL), lambda i, *_: (i, 0)),
                  pl.BlockSpec((tb, LANES), lambda i, *_: (i, 0))],
        out_specs=pl.BlockSpec(memory_space=pl.ANY),
        scratch_shapes=[pltpu.VMEM((2, 2 * tb * nsub, LANES), F32),
                        pltpu.VMEM((EXPERT_TILE * nsub, LANES), F32),
                        pltpu.SemaphoreType.DMA((2,)), pltpu.SemaphoreType.DMA(())],
    )
    return pl.pallas_call(
        functools.partial(_dispatch_kernel, tm=EXPERT_TILE),
        grid_spec=grid_spec,
        out_shape=jax.ShapeDtypeStruct((n_slots * nsub, LANES), F32),
        compiler_params=_params(("arbitrary",)),
        name="dispatch",
    )(blk_start, blk_cnt, blk_loc, seg_end, seg_tot, h2, meta_i)


def _expert_kernel(te_ref, tv_ref, nxt_ref, xs_ref, wg_hbm, wu_hbm, wd_hbm, out_ref,
                   wg_st, wu_st, wd_st, wgb, wub, wdb, n_switch, sem):
    i = pl.program_id(0)
    tm = EXPERT_TILE
    nsub = D_MODEL // LANES
    valid = tv_ref[i] != 0
    e = te_ref[i]

    def weight_copies(expert, sl):
        return [pltpu.make_async_copy(hbm.at[0, expert], st.at[sl], sem.at[sl, k])
                for k, (hbm, st) in enumerate(((wg_hbm, wg_st), (wu_hbm, wu_st), (wd_hbm, wd_st)))]

    @pl.when(i == 0)
    def _():
        n_switch[0] = 0
        for dma in weight_copies(e, 0):
            dma.start()

    @pl.when(jnp.logical_not(valid))
    def _():
        out_ref[...] = jnp.zeros_like(out_ref)

    @pl.when(valid & ((i == 0) | (e != te_ref[jnp.maximum(i - 1, 0)])))
    def _():
        sl = n_switch[0] % 2
        n_switch[0] = n_switch[0] + 1
        for dma in weight_copies(e, sl):
            dma.wait()
        wgb[...] = wg_st[sl].astype(BF16)
        wub[...] = wu_st[sl].astype(BF16)
        wdb[...] = wd_st[sl].astype(BF16)

        @pl.when(nxt_ref[e] != e)
        def _():
            for dma in weight_copies(nxt_ref[e], 1 - sl):
                dma.start()

    @pl.when(valid)
    def _():
        h = jnp.concatenate([xs_ref[pl.ds(c, tm, stride=nsub), :] for c in range(nsub)], axis=1).astype(BF16)
        gate = _dot(h, wgb[...])
        up = _dot(h, wub[...])
        a = (gate * _sigmoid(gate) * up).astype(BF16)
        o = _dot(a, wdb[...])
        for c in range(nsub):
            out_ref[pl.ds(c, tm, stride=nsub), :] = o[:, c * LANES:(c + 1) * LANES]


def _experts(tile_expert, tile_valid, next_expert, xs, wg, wu, wd):
    n_tiles = tile_expert.shape[0]
    tm = EXPERT_TILE
    nsub = D_MODEL // LANES
    used = lambda i, tv: jnp.where(tv[i] != 0, i, 0)
    anywhere = pl.BlockSpec(memory_space=pl.ANY)
    up_shape, down_shape = (D_MODEL, D_EXPERT), (D_EXPERT, D_MODEL)
    grid_spec = pltpu.PrefetchScalarGridSpec(
        num_scalar_prefetch=3,
        grid=(n_tiles,),
        in_specs=[pl.BlockSpec((tm * nsub, LANES), lambda i, te, tv, nx: (used(i, tv), 0)),
                  anywhere, anywhere, anywhere],
        out_specs=pl.BlockSpec((tm * nsub, LANES), lambda i, te, tv, nx: (i, 0)),
        scratch_shapes=[pltpu.VMEM((2,) + up_shape, F32), pltpu.VMEM((2,) + up_shape, F32),
                        pltpu.VMEM((2,) + down_shape, F32),
                        pltpu.VMEM(up_shape, BF16), pltpu.VMEM(up_shape, BF16), pltpu.VMEM(down_shape, BF16),
                        pltpu.SMEM((1,), jnp.int32), pltpu.SemaphoreType.DMA((2, 3))],
    )
    return pl.pallas_call(
        _expert_kernel,
        grid_spec=grid_spec,
        out_shape=jax.ShapeDtypeStruct((n_tiles * tm * nsub, LANES), F32),
        compiler_params=_params(("arbitrary",)),
        name="experts",
    )(tile_expert, tile_valid, next_expert, xs, wg, wu, wd)


def _combine_kernel(p0_ref, p1_ref, x1_ref, mw_ref, rows_hbm, y_ref, g0, g1, sem, *, tc, blk_off):
    i = pl.program_id(0)
    n = pl.num_programs(0)
    nout = D_MODEL // LANES
    slot = i % 2

    def issue_all(step, sl):
        base = (step + blk_off) * tc

        def issue(j, carry):
            for pos_ref, dst, s in ((p0_ref, g0, 0), (p1_ref, g1, 1)):
                src0 = pl.multiple_of(pos_ref[base + j] * nout, nout)
                pltpu.make_async_copy(rows_hbm.at[pl.ds(src0, nout), :],
                                      dst.at[sl, pl.ds(pl.multiple_of(j * nout, nout), nout), :],
                                      sem.at[sl, s]).start()
            return carry

        lax.fori_loop(0, tc, issue, 0, unroll=4)

    @pl.when(i == 0)
    def _():
        issue_all(i, slot)

    @pl.when(i + 1 < n)
    def _():
        issue_all(i + 1, 1 - slot)

    for dst, s in ((g0, 0), (g1, 1)):
        pltpu.make_async_copy(rows_hbm.at[pl.ds(0, tc * nout), :], dst.at[slot], sem.at[slot, s]).wait()
    w0 = mw_ref[:, 0:1]
    w1 = mw_ref[:, 1:2]
    for c in range(nout):
        sl = slice(c * LANES, (c + 1) * LANES)
        y_ref[:, sl] = (x1_ref[:, sl] + w0 * g0[slot, pl.ds(c, tc, stride=nout), :]
                        + w1 * g1[slot, pl.ds(c, tc, stride=nout), :])


def _combine(pos0, pos1, x1, meta_w, rows, tc, tok_off):
    t = x1.shape[0]
    assert t % tc == 0 and tok_off % tc == 0
    blk_off = tok_off // tc
    nout = D_MODEL // LANES
    row = pl.BlockSpec((tc, D_MODEL), lambda i, p0, p1: (i, 0))
    grid_spec = pltpu.PrefetchScalarGridSpec(
        num_scalar_prefetch=2,
        grid=(t // tc,),
        in_specs=[row, pl.BlockSpec((tc, LANES), lambda i, p0, p1: (i + blk_off, 0)),
                  pl.BlockSpec(memory_space=pl.ANY)],
        out_specs=row,
        scratch_shapes=[pltpu.VMEM((2, tc * nout, LANES), F32), pltpu.VMEM((2, tc * nout, LANES), F32),
                        pltpu.SemaphoreType.DMA((2, 2))],
    )
    return pl.pallas_call(
        functools.partial(_combine_kernel, tc=tc, blk_off=blk_off),
        grid_spec=grid_spec,
        out_shape=jax.ShapeDtypeStruct((t, D_MODEL), F32),
        compiler_params=_params(("arbitrary",)),
        name="combine",
    )(pos0, pos1, x1, meta_w, rows)


def _rope_tables(pos):
    half = SWA_HEAD_DIM // 2
    inv_freq = ROPE_THETA ** (-jnp.arange(half, dtype=F32) / half)
    ang = pos.astype(F32)[:, None] * inv_freq[None, :]
    cos, sin = jnp.cos(ang), jnp.sin(ang)
    reps = LANES // SWA_HEAD_DIM
    return jnp.tile(jnp.concatenate([cos, cos], axis=1), (1, reps)), jnp.tile(jnp.concatenate([-sin, sin], axis=1), (1, reps))


def kernel(x_prompt, x_sample, state_gla, cache_swa_k0, cache_swa_v0, cache_swa_k1, cache_swa_v1, cache_swa_k2, cache_swa_v2, ln1_w, w_in, w_gla_lr, b_gla_lr, gla_onorm_w, q_norm_w, k_norm_w, w_branch_a, w_branch_b, w_out, ln2_w, w_router_group, b_router_group, w_router_expert, b_router_expert, w_exp_gate, w_exp_up, w_exp_down):
    b, s, d = x_prompt.shape
    bd, ls, _ = x_sample.shape
    tp, ts = b * s, bd * ls
    assert w_in.shape[0] == 1 and d == D_MODEL and ts % SUBLANES == 0
    k_caches = (cache_swa_k0, cache_swa_k1, cache_swa_k2)
    v_caches = (cache_swa_v0, cache_swa_v1, cache_swa_v2)

    w = w_in[0]
    cuts = np.cumsum((512, 512, 1024, GLA_RANK, 1024, 768, 768, 768, 1024, 1024))
    sec = lambda a: w[:, (0 if a == 0 else cuts[a - 1]):cuts[a]]
    lr_pad = jnp.pad(sec(3), ((0, 0), (0, LANES - GLA_RANK)))
    w_f32 = jnp.concatenate([sec(0), sec(1), sec(2), sec(4), sec(5), sec(6), sec(7), sec(8), sec(9), lr_pad], axis=1)
    w_packed = w_f32.astype(BF16)
    wlr_f32 = jnp.pad(w_gla_lr[0], ((0, LANES - GLA_RANK), (0, 0)))
    wlr = wlr_f32.astype(BF16)
    blr = b_gla_lr[0][None, :]
    nw = jnp.concatenate([jnp.tile(q_norm_w[0], SWA_WIDTH // SWA_HEAD_DIM), jnp.tile(k_norm_w[0], SWA_WIDTH // SWA_HEAD_DIM)])[None, :]
    gi = np.arange(256) // SWA_HEAD_DIM
    gmat_f32 = jnp.asarray((gi[:, None] == gi[None, :]).astype(np.float32) / SWA_HEAD_DIM)
    gmat = gmat_f32.astype(BF16)
    ln1 = ln1_w[0][None, :]
    ln2 = ln2_w[0][None, :]
    onw = gla_onorm_w[0][None, :]
    wa_f32, wb_f32, wo_f32 = w_branch_a[0], w_branch_b[0], w_out[0]
    wa, wb, wo = wa_f32.astype(BF16), wb_f32.astype(BF16), wo_f32.astype(BF16)
    wr = jnp.pad(jnp.concatenate([w_router_group[0], w_router_expert[0]], axis=1),
                 ((0, 0), (0, LANES - N_GROUPS - N_EXPERTS)))
    wr_hi, wr_lo = _split_bf16(wr)
    br = jnp.pad(jnp.concatenate([b_router_group[0], b_router_expert[0]]), (0, LANES - N_GROUPS - N_EXPERTS))[None, :]

    cos_p, sin_p = _rope_tables(jnp.arange(s, dtype=jnp.int32))
    cos_s, sin_s = _rope_tables(PAST_LEN + jnp.arange(ts, dtype=jnp.int32) % ls)

    tm_p = 512
    proj_p = _proj(x_prompt.reshape(tp, d), cos_p, sin_p, s // tm_p, ln1, w_packed, wlr, blr, nw, gmat, tm_p)
    proj_s = _proj(x_sample.reshape(ts, d), cos_s, sin_s, 1, ln1, w_f32, wlr_f32, blr, nw, gmat_f32, ts)
    gqkv_p, la_p, sog_p, q_p, k_p, v_p, sgab_p = proj_p
    gqkv_s, la_s, sog_s, q_s, k_s, v_s, sgab_s = proj_s

    r3 = lambda t, nb: t.reshape(nb, t.shape[0] // nb, t.shape[1])
    ya_p, st_p = _gla(r3(gqkv_p, b), r3(la_p, b), r3(sog_p, b),
                      jnp.zeros((b, GLA_HEADS, GLA_DK, GLA_DV), F32), onw, GLA_CHUNK, 512)
    pad_s = lambda t: jnp.pad(r3(t, bd), ((0, 0), (0, _SAMPLE_ROWS - ls), (0, 0)))
    ya_s, st_s = _gla(pad_s(gqkv_s), pad_s(la_s), pad_s(sog_s), state_gla[0], onw, _SAMPLE_ROWS, _SAMPLE_ROWS)
    ya_s = ya_s[:, :ls].reshape(ts, d)

    q3, k3, v3 = r3(q_p, b), r3(k_p, b), r3(v_p, b)
    swa_p = [_swa_prompt(q3, k3, v3, g, dil) for g, (_, dil) in enumerate(SWA_GROUPS)]
    to_t = lambda c: jnp.transpose(c[0], (0, 2, 3, 1))
    caches_t = [(to_t(k_caches[g]), to_t(v_caches[g])) for g in range(len(SWA_GROUPS))]
    new_t = lambda t: jnp.pad(jnp.transpose(r3(t, bd), (0, 2, 1)), ((0, 0), (0, 0), (LANES - ls, 0)))
    ob_s, *new_caches = _swa_sample(pad_s(q_s), pad_s(k_s), pad_s(v_s), new_t(k_s), new_t(v_s), caches_t, ls)
    ob_s = ob_s[:, :ls].reshape(ts, SWA_OUT)

    t = tp + ts
    mw = (wa, wb, wo, ln2, jnp.concatenate([wr_hi, wr_lo], axis=1), wr_hi, br)
    x1_p, h2c, lg, cnt_p = _merge(x_prompt.reshape(tp, d), ya_p.reshape(tp, d),
                                  [o for o, _ in swa_p] + [l for _, l in swa_p], sgab_p, mw, MERGE_TILE, 0, t)
    mw_f32 = (wa_f32, wb_f32, wo_f32) + mw[3:]
    x1_s, h2c, lg, cnt_s = _merge(x_sample.reshape(ts, d), ya_s, [ob_s], sgab_s, mw_f32, ts, tp, t,
                                  shared=(h2c, lg))

    tm = EXPERT_TILE
    n_tiles = (2 * t) // tm + N_EXPERTS
    counts = (jnp.sum(cnt_p[::SUBLANES], axis=0) + jnp.sum(cnt_s[::SUBLANES], axis=0)).astype(jnp.int32)
    ends_all = jnp.cumsum((counts + tm - 1) // tm * tm)
    seg_start = jnp.broadcast_to((ends_all - (counts + tm - 1) // tm * tm).astype(F32)[None, :], (SUBLANES, LANES))
    counts, ends = counts[:N_EXPERTS], ends_all[:N_EXPERTS]
    tb = DISPATCH_BLOCK
    meta_i, meta_w, blocks = _router(lg, seg_start, t, tb)
    blocks = blocks.reshape(t // tb, SUBLANES, LANES)[:, :, :N_EXPERTS]
    blk_start, blk_cnt, blk_loc = (blocks[:, r].reshape(-1) for r in range(3))
    pos0, pos1 = meta_i[:, 2], meta_i[:, 3]
    tile_start = jnp.arange(n_tiles, dtype=jnp.int32) * tm
    tile_valid = (tile_start < ends[-1]).astype(jnp.int32)
    last_slot = jnp.minimum(tile_start, ends[-1] - 1)
    tile_expert = jnp.minimum(jnp.sum((last_slot[:, None] >= ends[None, :]).astype(jnp.int32), axis=1), N_EXPERTS - 1)

    xs = _dispatch(blk_start, blk_cnt, blk_loc, ends, counts, h2c, meta_i, n_tiles * tm, tb)
    ids = jnp.arange(N_EXPERTS, dtype=jnp.int32)
    later_used = (counts[None, :] > 0) & (ids[None, :] > ids[:, None])
    next_expert = jnp.min(jnp.where(later_used, ids[None, :], N_EXPERTS), axis=1)
    next_expert = jnp.where(next_expert == N_EXPERTS, ids, next_expert)
    rows = _experts(tile_expert, tile_valid, next_expert, xs, w_exp_gate, w_exp_up, w_exp_down)
    y_p = _combine(pos0, pos1, x1_p, meta_w, rows, 256, 0)
    y_s = _combine(pos0, pos1, x1_s, meta_w, rows, ts, tp)

    heads = lambda a: a.reshape(1, a.shape[0], a.shape[1], SWA_HPG, SWA_HEAD_DIM)
    outs = [y_p.reshape(b, s, d), y_s.reshape(bd, ls, d), st_p[None].astype(x_prompt.dtype)]
    for g, (win, _) in enumerate(SWA_GROUPS):
        keep = min(win, s)
        gsl = slice(g * SWA_OUT, (g + 1) * SWA_OUT)
        outs += [heads(k3[:, s - keep:, gsl]), heads(v3[:, s - keep:, gsl])]
    outs.append(st_s[None].astype(state_gla.dtype))
    outs += [jnp.transpose(c, (0, 3, 1, 2))[None] for c in new_caches]
    return tuple(outs)
```

```python
import functools

import numpy as np
import jax
import jax.numpy as jnp
from jax import lax
from jax.experimental import pallas as pl
from jax.experimental.pallas import tpu as pltpu

F32 = jnp.float32
BF16 = jnp.bfloat16

D_MODEL = 1024
PAST_LEN = 16384
GLA_HEADS = 4
GLA_DK = 128
GLA_DV = 256
GLA_RANK = 16
GLA_TAU = 16.0
GLA_CHUNK = 64
SWA_GROUPS = ((128, 1), (512, 4), (2048, 16))
SWA_HPG = 4
SWA_HEAD_DIM = 64
SWA_WIDTH = 768
SWA_OUT = 256
SWA_BLOCK = 128
ROPE_THETA = 10000.0
N_GROUPS = 4
EXPERTS_PER_GROUP = 8
N_EXPERTS = 32
D_EXPERT = 512
EPS = 1e-6

LANES = 128
SUBLANES = 8
VMEM_LIMIT = 56 * 1024 * 1024
NEG = -1e30
MXU_WIDTH = 256
EXPERT_TILE = 256
MERGE_TILE = 512
DISPATCH_BLOCK = 384

_C_GQKV = (0, 2048)
_C_GOG = (2048, 3072)
_C_QK = (3072, 4608)
_C_V = (4608, 5376)
_C_GAB = (5376, 7424)
_C_LR = (7424, 7552)


def _contract(a, b, dims):
    dg = lambda x, y: lax.dot_general(x, y, (dims, ((), ())), preferred_element_type=F32)
    if a.dtype == F32 and b.dtype == F32:
        a_hi, a_lo = _split(a, BF16)
        b_hi, b_lo = _split(b, BF16)
        return dg(a_hi, b_hi) + dg(a_hi, b_lo) + dg(a_lo, b_hi)
    return dg(a, b)


def _dot(a, b):
    return _contract(a, b, ((1,), (0,)))


def _dot_nt(a, b):
    return _contract(a, b, ((1,), (1,)))


def _dot_tn(a, b):
    return _contract(a, b, ((0,), (0,)))


def _sigmoid(x):
    return 1.0 / (1.0 + jnp.exp(-x))


def _split(x, dt):
    hi = x.astype(dt)
    lo = (x - hi.astype(F32)).astype(dt)
    return hi, lo


def _split_bf16(x):
    return _split(x, BF16)


def _params(sem):
    return pltpu.CompilerParams(dimension_semantics=sem, vmem_limit_bytes=VMEM_LIMIT)


def _resident(shape):
    nd = len(shape)
    return pl.BlockSpec(shape, lambda *_: (0,) * nd, pipeline_mode=pl.Buffered(1))


def _proj_kernel(x_ref, cos_ref, sin_ref, ln_ref, w_ref, wlr_ref, blr_ref, nw_ref, g_ref,
                 gqkv_ref, la_ref, sog_ref, q_ref, k_ref, v_ref, sgab_ref):
    cdt = w_ref.dtype
    x = x_ref[...]
    h = (x * lax.rsqrt(jnp.mean(x * x, axis=-1, keepdims=True) + EPS) * ln_ref[...]).astype(cdt)
    gqkv_ref[...] = _dot(h, w_ref[:, _C_GQKV[0]:_C_GQKV[1]]).astype(cdt)
    og = _dot(h, w_ref[:, _C_GOG[0]:_C_GOG[1]])
    sog_ref[...] = (og * _sigmoid(og)).astype(cdt)
    lr = _dot(h, w_ref[:, _C_LR[0]:_C_LR[1]]).astype(cdt)
    z = _dot(lr, wlr_ref[...]) + blr_ref[...]
    la_ref[...] = (jnp.minimum(z, 0.0) - jnp.log(1.0 + jnp.exp(-jnp.abs(z)))) / GLA_TAU
    qk = _dot(h, w_ref[:, _C_QK[0]:_C_QK[1]])
    sq = (qk * qk).astype(cdt)
    ms = jnp.concatenate([_dot(sq[:, c * 256:(c + 1) * 256], g_ref[...]) for c in range(6)], axis=1)
    qn = qk * lax.rsqrt(ms + EPS) * nw_ref[...]
    width = 2 * SWA_WIDTH
    cos = jnp.tile(cos_ref[...], (1, width // LANES))
    sin = jnp.tile(sin_ref[...], (1, width // LANES))
    lane = lax.broadcasted_iota(jnp.int32, qn.shape, 1)
    half = SWA_HEAD_DIM // 2
    rot = jnp.where(lane % SWA_HEAD_DIM < half, pltpu.roll(qn, width - half, 1), pltpu.roll(qn, half, 1))
    qr = qn * cos + rot * sin
    q_ref[...] = qr[:, :SWA_WIDTH]
    k_ref[...] = qr[:, SWA_WIDTH:]
    v_ref[...] = _dot(h, w_ref[:, _C_V[0]:_C_V[1]])
    gab = _dot(h, w_ref[:, _C_GAB[0]:_C_GAB[1]])
    sgab_ref[...] = _sigmoid(gab).astype(cdt)


def _proj(x, cos, sin, rope_blocks, ln, w, wlr, blr, nw, g, tm):
    t = x.shape[0]
    assert t % tm == 0 and w.dtype == wlr.dtype == g.dtype
    row = lambda width: pl.BlockSpec((tm, width), lambda i: (i, 0))
    outs = [(2048, w.dtype), (512, F32), (1024, w.dtype), (768, F32), (768, F32), (768, F32), (2048, w.dtype)]
    return pl.pallas_call(
        _proj_kernel,
        grid=(t // tm,),
        in_specs=[row(D_MODEL),
                  pl.BlockSpec((tm, LANES), lambda i: (i % rope_blocks, 0)),
                  pl.BlockSpec((tm, LANES), lambda i: (i % rope_blocks, 0)),
                  _resident(ln.shape), _resident(w.shape), _resident(wlr.shape), _resident(blr.shape),
                  _resident(nw.shape), _resident(g.shape)],
        out_specs=[row(wd) for wd, _ in outs],
        out_shape=[jax.ShapeDtypeStruct((t, wd), dt) for wd, dt in outs],
        compiler_params=_params(("arbitrary",)),
        name="proj",
    )(x, cos, sin, ln, w, wlr, blr, nw, g)


def _gla_kernel(gqkv_ref, la_ref, sog_ref, s0_ref, onw_ref, y_ref, sfin_ref, st_scr, *, chunk, n_chunks):
    j = pl.program_id(1)

    @pl.when(j == 0)
    def _():
        for h in range(GLA_HEADS):
            st_scr[h] = s0_ref[0, h].T

    r = lax.broadcasted_iota(jnp.int32, (chunk, chunk), 0)
    c = lax.broadcasted_iota(jnp.int32, (chunk, chunk), 1)
    causal = r >= c
    cdt = gqkv_ref.dtype
    tri = causal.astype(cdt)
    hk = GLA_HEADS * GLA_DK
    rows = [slice(ci * chunk, (ci + 1) * chunk) for ci in range(n_chunks)]

    la_hi, la_lo = _split(la_ref[0], cdt)
    b_chunks = [_dot(tri, la_hi[rs]) + _dot(tri, la_lo[rs]) for rs in rows]
    last = [bc[chunk - 1:chunk, :] for bc in b_chunks]
    b = jnp.concatenate(b_chunks, axis=0)
    b_end = jnp.concatenate([jnp.broadcast_to(x, (chunk, hk)) for x in last], axis=0)
    q = gqkv_ref[0, :, :hk].astype(F32) * GLA_DK ** -0.5
    k = gqkv_ref[0, :, hk:2 * hk].astype(F32)
    qd = (q * jnp.exp(b)).astype(cdt)
    kd = (k * jnp.exp(-b)).astype(cdt)
    kdec = (k * jnp.exp(b_end - b)).astype(cdt)
    decay = [jnp.exp(x) for x in last]

    state = [st_scr[h] for h in range(GLA_HEADS)]
    for ci, rs in enumerate(rows):
        outs = []
        for h in range(GLA_HEADS):
            sl = slice(h * GLA_DK, (h + 1) * GLA_DK)
            v_h = gqkv_ref[0, rs, 2 * hk + h * GLA_DV:2 * hk + (h + 1) * GLA_DV]
            att = jnp.where(causal, _dot_nt(qd[rs, sl], kd[rs, sl]), 0.0).astype(cdt)
            o = _dot(att, v_h) + _dot_nt(qd[rs, sl], state[h].astype(cdt))
            state[h] = state[h] * decay[ci][:, sl] + _dot_tn(v_h, kdec[rs, sl])
            ms = jnp.mean(o * o, axis=-1, keepdims=True)
            outs.append(o * lax.rsqrt(ms + EPS) * onw_ref[...])
        o_all = jnp.concatenate(outs, axis=1) * sog_ref[0, rs, :].astype(F32)
        y_ref[0, rs, :] = o_all.astype(cdt)
    for h in range(GLA_HEADS):
        st_scr[h] = state[h]

    @pl.when(j == pl.num_programs(1) - 1)
    def _():
        for h in range(GLA_HEADS):
            sfin_ref[0, h] = state[h].T


def _gla(gqkv, la, sog, s0, onw, chunk, block):
    b, l, _ = gqkv.shape
    tok = lambda width: pl.BlockSpec((1, block, width), lambda bi, j: (bi, j, 0))
    st = pl.BlockSpec((1, GLA_HEADS, GLA_DK, GLA_DV), lambda bi, j: (bi, 0, 0, 0))
    return pl.pallas_call(
        functools.partial(_gla_kernel, chunk=chunk, n_chunks=block // chunk),
        grid=(b, l // block),
        in_specs=[tok(2048), tok(512), tok(1024), st, _resident(onw.shape)],
        out_specs=[tok(1024), st],
        out_shape=[jax.ShapeDtypeStruct((b, l, 1024), gqkv.dtype),
                   jax.ShapeDtypeStruct((b, GLA_HEADS, GLA_DK, GLA_DV), F32)],
        scratch_shapes=[pltpu.VMEM((GLA_HEADS, GLA_DV, GLA_DK), F32)],
        compiler_params=_params(("arbitrary", "arbitrary")),
        name="gla",
    )(gqkv, la, sog, s0, onw)


def _band_heads(q, kw, vw, valid):
    n = q.shape[0]
    low = lax.broadcasted_iota(jnp.int32, (n, LANES), 1) < SWA_HEAD_DIM
    ones = jnp.ones((kw.shape[0], LANES), kw.dtype)
    outs, lses = [], []
    for pair in range(SWA_OUT // LANES):
        cols = slice(pair * LANES, (pair + 1) * LANES)
        v_aug = jnp.concatenate([vw[:, cols], ones], axis=1)
        res = []
        for mine in (low, jnp.logical_not(low)):
            qm = jnp.where(mine, q[:, cols], 0.0).astype(kw.dtype)
            s = _dot_nt(qm, kw[:, cols]) * SWA_HEAD_DIM ** -0.5
            s = jnp.where(valid, s, NEG)
            m = jnp.max(s, axis=-1, keepdims=True)
            r = _dot(jnp.exp(s - m).astype(vw.dtype), v_aug)
            den = r[:, LANES:]
            res.append((r[:, :LANES] / den, m + jnp.log(den)))
        outs.append(jnp.where(low, res[0][0], res[1][0]))
        lses.append(jnp.where(low, res[0][1], res[1][1]))
    return jnp.concatenate(outs, axis=1), jnp.concatenate(lses, axis=1)


_SWA_TOKENS = 2048


def _swa_kernel(q_ref, k_ref, v_ref, kp_ref, vp_ref, o_ref, lse_ref, *stage, dil):
    blk = SWA_BLOCK
    nsub = q_ref.shape[1] // (blk * dil)
    first = pl.program_id(1) == 0
    qi = lax.broadcasted_iota(jnp.int32, (blk, 2 * blk), 0)
    kj = lax.broadcasted_iota(jnp.int32, (blk, 2 * blk), 1)
    band = (kj >= qi) & (kj <= qi + blk)
    halves = SWA_OUT // LANES

    if dil > 1:
        ins = (q_ref, k_ref, v_ref, kp_ref, vp_ref)
        q_ref, k_ref, v_ref, kp_ref, vp_ref, o_st, lse_st = stage
        for src, dst in zip(ins, stage):
            for hf in range(halves):
                dst[hf] = src[0, :, hf * LANES:(hf + 1) * LANES]

    def rows(ref, start):
        if dil == 1:
            return ref[0, pl.ds(start, blk), :]
        return jnp.concatenate([ref[hf, pl.ds(start, blk, stride=dil), :] for hf in range(halves)], axis=1)

    def unit(u, carry):
        r = u // nsub
        j = u % nsub
        start = r + dil * blk * j
        inside = r + dil * blk * jnp.maximum(j - 1, 0)
        if dil == 1:
            r, start, inside = 0, pl.multiple_of(start, blk), pl.multiple_of(inside, blk)
        head = j == 0
        kprev = jnp.where(head, rows(kp_ref, r), rows(k_ref, inside))
        vprev = jnp.where(head, rows(vp_ref, r), rows(v_ref, inside))
        kw = jnp.concatenate([kprev, rows(k_ref, start)], axis=0).astype(BF16)
        vw = jnp.concatenate([vprev, rows(v_ref, start)], axis=0).astype(BF16)
        valid = band & (kj >= jnp.where(head & first, blk, 0))
        o, lse = _band_heads(rows(q_ref, start), kw, vw, valid)
        if dil == 1:
            o_ref[0, pl.ds(start, blk), :] = o
            lse_ref[0, pl.ds(start, blk), :] = lse
        else:
            for hf in range(halves):
                o_st[hf, pl.ds(start, blk, stride=dil), :] = o[:, hf * LANES:(hf + 1) * LANES]
                lse_st[hf, pl.ds(start, blk, stride=dil), :] = lse[:, hf * LANES:(hf + 1) * LANES]
        return carry

    lax.fori_loop(0, dil * nsub, unit, 0)
    if dil > 1:
        for hf in range(halves):
            o_ref[0, :, hf * LANES:(hf + 1) * LANES] = o_st[hf]
            lse_ref[0, :, hf * LANES:(hf + 1) * LANES] = lse_st[hf]


def _swa_prompt(q, k, v, g, dil):
    b, s, _ = q.shape
    tb = _SWA_TOKENS
    back = SWA_BLOCK * dil
    assert s % tb == 0 and tb % back == 0
    cur = pl.BlockSpec((1, tb, SWA_OUT), lambda bi, i: (bi, i, g))
    prev = pl.BlockSpec((1, back, SWA_OUT), lambda bi, i: (bi, jnp.maximum(i * (tb // back) - 1, 0), g))
    out = pl.BlockSpec((1, tb, SWA_OUT), lambda bi, i: (bi, i, 0))
    halves = SWA_OUT // LANES
    stage = [pltpu.VMEM((halves, n, LANES), F32) for n in (tb, tb, tb, back, back, tb, tb)] if dil > 1 else []
    o, lse = pl.pallas_call(
        functools.partial(_swa_kernel, dil=dil),
        grid=(b, s // tb),
        in_specs=[cur, cur, cur, prev, prev],
        out_specs=[out, out],
        out_shape=[jax.ShapeDtypeStruct((b, s, SWA_OUT), F32)] * 2,
        scratch_shapes=stage,
        compiler_params=_params(("arbitrary", "arbitrary")),
        name=f"swa_prompt_g{g}",
    )(q, k, v, k, v)
    return o.reshape(b * s, SWA_OUT), lse.reshape(b * s, SWA_OUT)


_SAMPLE_ROWS = 16


def _swa_sample_kernel(q_ref, kn_ref, vn_ref, k0_ref, v0_ref, k1_ref, v1_ref, k2_ref, v2_ref,
                       ob_ref, ok0_ref, ov0_ref, ok1_ref, ov1_ref, ok2_ref, ov2_ref, *, n_new):
    rows = _SAMPLE_ROWS
    in_refs = ((k0_ref, v0_ref), (k1_ref, v1_ref), (k2_ref, v2_ref))
    out_refs = ((ok0_ref, ov0_ref), (ok1_ref, ov1_ref), (ok2_ref, ov2_ref))
    scale = SWA_HEAD_DIM ** -0.5
    jn = lax.broadcasted_iota(jnp.int32, (rows, rows), 1)
    ln = lax.broadcasted_iota(jnp.int32, (rows, rows), 0)
    tail = lax.broadcasted_iota(jnp.int32, (SWA_HEAD_DIM, LANES), 1) >= LANES - n_new
    sel_row = lax.broadcasted_iota(jnp.int32, (rows, LANES), 0)
    sel_lane = lax.broadcasted_iota(jnp.int32, (rows, LANES), 1)
    selector = ((sel_lane == sel_row + (LANES - n_new)) & (sel_row < n_new)).astype(F32)
    knt = _dot_tn(kn_ref[0], selector)
    vnt = _dot_tn(vn_ref[0], selector)
    o_g, lse_g = [], []
    for g, (win, dil) in enumerate(SWA_GROUPS):
        jc = lax.broadcasted_iota(jnp.int32, (rows, win), 1)
        lc = lax.broadcasted_iota(jnp.int32, (rows, win), 0)
        valid_c = (jc >= lc) & (((jc - lc) & (dil - 1)) == 0)
        valid_n = (jn <= ln) & (((ln - jn) & (dil - 1)) == 0) & (jn < n_new)
        o_h, lse_h = [], []
        for h in range(SWA_HPG):
            col = g * SWA_OUT + h * SWA_HEAD_DIM
            hsl = slice(col, col + SWA_HEAD_DIM)
            qh = q_ref[0, :, hsl]
            knh = kn_ref[0, :, hsl]
            vnh = vn_ref[0, :, hsl]
            for (src, dst, new_t) in ((in_refs[g][0], out_refs[g][0], knt), (in_refs[g][1], out_refs[g][1], vnt)):
                old = src[0, h]
                moved = pltpu.roll(old, win - n_new, 1)
                if win > LANES:
                    dst[0, h, :, 0:win - LANES] = moved[:, 0:win - LANES]
                dst[0, h, :, win - LANES:win] = jnp.where(tail, new_t[hsl, :], moved[:, win - LANES:win])
            kt = in_refs[g][0][0, h]
            vt = in_refs[g][1][0, h]
            s_c = jnp.where(valid_c, _dot(qh, kt) * scale, NEG)
            s_n = jnp.where(valid_n, _dot_nt(qh, knh) * scale, NEG)
            m = jnp.maximum(jnp.max(s_c, axis=-1, keepdims=True), jnp.max(s_n, axis=-1, keepdims=True))
            p_c = jnp.exp(s_c - m)
            p_n = jnp.exp(s_n - m)
            den = jnp.sum(p_c, axis=-1, keepdims=True) + jnp.sum(p_n, axis=-1, keepdims=True)
            o_h.append((_dot_nt(p_c, vt) + _dot(p_n, vnh)) / den)
            lse_h.append(jnp.broadcast_to(m + jnp.log(den), (rows, SWA_HEAD_DIM)))
        o_g.append(jnp.concatenate(o_h, axis=1))
        lse_g.append(jnp.concatenate(lse_h, axis=1))
    lmax = jnp.maximum(jnp.maximum(lse_g[0], lse_g[1]), lse_g[2])
    e = [jnp.exp(x - lmax) for x in lse_g]
    ob_ref[0] = (e[0] * o_g[0] + e[1] * o_g[1] + e[2] * o_g[2]) / (e[0] + e[1] + e[2])


def _swa_sample(q, kn, vn, caches_t, n_new):
    bd = q.shape[0]
    rows = _SAMPLE_ROWS
    tok = pl.BlockSpec((1, rows, SWA_WIDTH), lambda bi: (bi, 0, 0))
    specs, args = [tok, tok, tok], [q, kn, vn]
    out_specs = [pl.BlockSpec((1, rows, SWA_OUT), lambda bi: (bi, 0, 0))]
    out_shape = [jax.ShapeDtypeStruct((bd, rows, SWA_OUT), F32)]
    for g, (win, dil) in enumerate(SWA_GROUPS):
        for t in caches_t[g]:
            assert t.shape == (bd, SWA_HPG, SWA_HEAD_DIM, win) and win == SWA_BLOCK * dil and win % LANES == 0
            spec = pl.BlockSpec((1, SWA_HPG, SWA_HEAD_DIM, win), lambda bi: (bi, 0, 0, 0))
            args.append(t)
            specs.append(spec)
            out_specs.append(spec)
            out_shape.append(jax.ShapeDtypeStruct(t.shape, t.dtype))
    return pl.pallas_call(
        functools.partial(_swa_sample_kernel, n_new=n_new),
        grid=(bd,),
        in_specs=specs,
        out_specs=out_specs,
        out_shape=out_shape,
        compiler_params=_params(("arbitrary",)),
        name="swa_sample",
    )(*args)


def _merge_kernel(*refs, combine, n_alias, n_real):
    x1_ref, h2c_ref, lg_ref, cnt_ref = refs[-4:]
    refs = refs[:len(refs) - 4 - n_alias]

    @pl.when(pl.program_id(0) >= n_real)
    def _():
        h2c_ref[...] = jnp.zeros_like(h2c_ref)
        lg_ref[...] = jnp.zeros_like(lg_ref)
        cnt_ref[...] = jnp.zeros_like(cnt_ref)

    pl.when(pl.program_id(0) < n_real)(
        functools.partial(_merge_tile, refs, x1_ref, h2c_ref, lg_ref, cnt_ref, combine))


def _merge_tile(refs, x1_ref, h2c_ref, lg_ref, cnt_ref, combine):
    if combine:
        (x_ref, ya_ref, o0, o1, o2, l0, l1, l2, sgab_ref, wa_ref, wb_ref, wo_ref, ln_ref, wr_cat_ref, wr_hi_ref,
         br_ref) = refs
        lmax = jnp.maximum(jnp.maximum(l0[...], l1[...]), l2[...])
        e0, e1, e2 = jnp.exp(l0[...] - lmax), jnp.exp(l1[...] - lmax), jnp.exp(l2[...] - lmax)
        ob = ((e0 * o0[...] + e1 * o1[...] + e2 * o2[...]) / (e0 + e1 + e2)).astype(wb_ref.dtype)
    else:
        (x_ref, ya_ref, ob_ref, sgab_ref, wa_ref, wb_ref, wo_ref, ln_ref, wr_cat_ref, wr_hi_ref, br_ref) = refs
        ob = ob_ref[...]
    ya = _dot(ya_ref[...], wa_ref[...])
    yb = _dot(ob, wb_ref[...])
    sga = sgab_ref[:, :D_MODEL].astype(F32)
    sgb = sgab_ref[:, D_MODEL:].astype(F32)
    x1 = x_ref[...] + _dot((sga * ya + sgb * yb).astype(wo_ref.dtype), wo_ref[...])
    x1_ref[...] = x1
    h2 = x1 * lax.rsqrt(jnp.mean(x1 * x1, axis=-1, keepdims=True) + EPS) * ln_ref[...]
    h_hi, h_lo = _split_bf16(h2)
    both = _dot(h_hi, wr_cat_ref[...])
    lg = both[:, :LANES] + both[:, LANES:] + _dot(h_lo, wr_hi_ref[...]) + br_ref[...]
    lg_ref[...] = lg
    h2c_ref[...] = h_hi
    _, _, _, _, _, hot1, hot2 = _route_select(lg)
    cnt = jnp.sum((hot1 | hot2).astype(F32), axis=0, keepdims=True)
    cnt_ref[...] = jnp.broadcast_to(cnt, cnt_ref.shape)


def _merge(x, ya_in, swa, sgab, weights, tm, tok_off, t_all, shared=None):
    t = x.shape[0]
    assert t % tm == 0 and tok_off % tm == 0 and MERGE_TILE % tm == 0
    combine = len(swa) > 1
    blk_off = tok_off // tm
    t_buf = -(-t_all // MERGE_TILE) * MERGE_TILE
    nsub = D_MODEL // LANES
    n_real = t // tm
    n_fill = 0 if shared is not None else (t_buf - tok_off - t) // tm
    row = lambda width: pl.BlockSpec((tm, width), lambda i: (jnp.minimum(i, n_real - 1), 0))
    shared_in = [] if shared is None else list(shared)
    n_in = 3 + len(swa) + len(weights)
    return pl.pallas_call(
        functools.partial(_merge_kernel, combine=combine, n_alias=len(shared_in), n_real=n_real),
        grid=(n_real + n_fill,),
        in_specs=[row(D_MODEL), row(D_MODEL)] + [row(SWA_OUT)] * len(swa) + [row(2 * D_MODEL)]
                 + [_resident(w.shape) for w in weights] + [pl.BlockSpec(memory_space=pl.ANY)] * len(shared_in),
        out_specs=[row(D_MODEL),
                   pl.BlockSpec((tm, D_MODEL), lambda i: (i + blk_off, 0)),
                   pl.BlockSpec((tm, LANES), lambda i: (i + blk_off, 0)),
                   pl.BlockSpec((SUBLANES, LANES), lambda i: (i, 0))],
        out_shape=[jax.ShapeDtypeStruct((t, D_MODEL), F32),
                   jax.ShapeDtypeStruct((t_buf, D_MODEL), BF16),
                   jax.ShapeDtypeStruct((t_buf, LANES), F32),
                   jax.ShapeDtypeStruct(((n_real + n_fill) * SUBLANES, LANES), F32)],
        input_output_aliases={n_in + k: 1 + k for k in range(len(shared_in))},
        compiler_params=_params(("arbitrary",)),
        name="merge",
    )(x, ya_in, *swa, sgab, *weights, *shared_in)


def _route_select(lg):
    lane = lax.broadcasted_iota(jnp.int32, lg.shape, 1)
    big = jnp.int32(LANES)
    gl = jnp.where(lane < N_GROUPS, lg, NEG)
    gmax = jnp.max(gl, axis=-1, keepdims=True)
    g_idx = jnp.min(jnp.where(gl == gmax, lane, big), axis=-1, keepdims=True)
    g_w = 1.0 / jnp.sum(jnp.exp(gl - gmax), axis=-1, keepdims=True)
    e_lane = lane - N_GROUPS
    in_group = (e_lane >= 0) & (e_lane < N_EXPERTS) & (e_lane // EXPERTS_PER_GROUP == g_idx)
    el = jnp.where(in_group, lg, NEG)
    v1 = jnp.max(el, axis=-1, keepdims=True)
    i1 = jnp.min(jnp.where(el == v1, lane, big), axis=-1, keepdims=True)
    el2 = jnp.where(lane == i1, NEG, el)
    v2 = jnp.max(el2, axis=-1, keepdims=True)
    i2 = jnp.min(jnp.where(el2 == v2, lane, big), axis=-1, keepdims=True)
    r21 = jnp.exp(v2 - v1)
    w1 = g_w / (1.0 + r21)
    w2 = g_w * r21 / (1.0 + r21)
    e1 = i1 - N_GROUPS
    e2 = i2 - N_GROUPS
    return lane, e1, e2, w1, w2, lane == e1, lane == e2


def _router_kernel(lg_ref, offs_ref, mi_ref, mw_ref, blk_ref, carry):
    @pl.when(pl.program_id(0) == 0)
    def _():
        carry[...] = offs_ref[0:1, :]

    lane, e1, e2, w1, w2, hot1, hot2 = _route_select(lg_ref[...])
    tr = lane.shape[0]
    hot = (hot1 | hot2).astype(BF16)
    start = carry[...]
    cnt = jnp.sum(hot.astype(F32), axis=0, keepdims=True)
    carry[...] = start + cnt
    r = lax.broadcasted_iota(jnp.int32, (tr, tr), 0)
    c = lax.broadcasted_iota(jnp.int32, (tr, tr), 1)
    within = _dot((r > c).astype(BF16), hot)
    lane1 = lax.broadcasted_iota(jnp.int32, (SUBLANES, LANES), 1)
    incl = jnp.broadcast_to(cnt, (SUBLANES, LANES))
    for sh in (1, 2, 4, 8, 16):
        incl = incl + jnp.where(lane1 >= sh, pltpu.roll(incl, sh, 1), 0.0)
    local = incl[0:1, :] - cnt
    pick = lambda hot_k, row: jnp.sum(jnp.where(hot_k, row, 0.0), axis=-1, keepdims=True)
    cols = [e1, e2]
    for hot_k in (hot1, hot2):
        cols.append((pick(hot_k, within) + pick(hot_k, start)).astype(jnp.int32))
    for hot_k in (hot1, hot2):
        cols.append((pick(hot_k, within) + pick(hot_k, local)).astype(jnp.int32))
    mi = cols[-1]
    for k in range(len(cols) - 2, -1, -1):
        mi = jnp.where(lane == k, cols[k], mi)
    mi_ref[...] = mi
    mw_ref[...] = jnp.where(lane == 0, w1, w2)
    sub = lax.broadcasted_iota(jnp.int32, (SUBLANES, LANES), 0)
    tbl = jnp.where(sub == 0, start, jnp.where(sub == 1, cnt, jnp.where(sub == 2, local, 0.0)))
    blk_ref[...] = tbl.astype(jnp.int32)


def _router(logits, seg_start, t, tr):
    assert t % tr == 0 and tr % SUBLANES == 0
    row = pl.BlockSpec((tr, LANES), lambda i: (i, 0))
    small = pl.BlockSpec((SUBLANES, LANES), lambda i: (i, 0))
    return pl.pallas_call(
        _router_kernel,
        grid=(t // tr,),
        in_specs=[row, pl.BlockSpec((SUBLANES, LANES), lambda i: (0, 0))],
        out_specs=[row, row, small],
        out_shape=[jax.ShapeDtypeStruct((t, LANES), jnp.int32), jax.ShapeDtypeStruct((t, LANES), F32),
                   jax.ShapeDtypeStruct((t // tr * SUBLANES, LANES), jnp.int32)],
        scratch_shapes=[pltpu.VMEM((1, LANES), F32)],
        compiler_params=_params(("arbitrary",)),
        name="router",
    )(logits, seg_start)


def _run_copies(src, dst, sem, src_row, dst_row, n_rows, wait=False):
    nsub = D_MODEL // LANES

    def copy(s0, d0, rows):
        dma = pltpu.make_async_copy(src.at[pl.ds(pl.multiple_of(s0 * nsub, nsub), rows * nsub), :],
                                    dst.at[pl.ds(pl.multiple_of(d0 * nsub, nsub), rows * nsub), :], sem)
        dma.wait() if wait else dma.start()

    def chunk(k, carry):
        copy(src_row + k * SUBLANES, dst_row + k * SUBLANES, SUBLANES)
        return carry

    n_chunks = n_rows // SUBLANES
    lax.fori_loop(0, n_chunks, chunk, 0)
    done = n_chunks * SUBLANES
    for rows in (4, 2, 1):
        @pl.when((n_rows & rows) != 0)
        def _(rows=rows, done=done):
            copy(src_row + done, dst_row + done, rows)

        done = done + (n_rows & rows)


def _dispatch_kernel(start_ref, cnt_ref, loc_ref, end_ref, tot_ref, h_ref, mi_ref, xs_hbm, stage, zeros, sem, zsem,
                     *, tm):
    i = pl.program_id(0)
    n = pl.num_programs(0)
    nsub = D_MODEL // LANES
    tb = h_ref.shape[0]
    slot = i % 2

    def wait_stage(sl):
        pltpu.make_async_copy(stage.at[sl], stage.at[sl], sem.at[sl]).wait()

    def pad_rows(wait):
        def one(e, carry):
            tot = tot_ref[e]
            n_pad = (tot + tm - 1) // tm * tm - tot
            _run_copies(zeros, xs_hbm, zsem, 0, end_ref[e] - n_pad, n_pad, wait=wait)
            return carry

        lax.fori_loop(0, end_ref.shape[0], one, 0)

        def spare(j, carry):
            dma = pltpu.make_async_copy(zeros, xs_hbm.at[pl.ds(pl.multiple_of(j * tm * nsub, nsub), tm * nsub), :], zsem)
            dma.wait() if wait else dma.start()
            return carry

        lax.fori_loop(end_ref[end_ref.shape[0] - 1] // tm, xs_hbm.shape[0] // (tm * nsub), spare, 0)

    @pl.when(i == 0)
    def _():
        zeros[...] = jnp.zeros_like(zeros)
        pad_rows(wait=False)

    @pl.when(i >= 2)
    def _():
        wait_stage(slot)

    li = mi_ref[...]
    rows_iota = lax.broadcasted_iota(jnp.int32, (tb, 2 * tb), 1)
    onehot = ((rows_iota == li[:, 4:5]) | (rows_iota == li[:, 5:6])).astype(BF16)
    srt = _dot_tn(onehot, h_ref[...])
    for c in range(nsub):
        stage[slot, pl.ds(c, 2 * tb, stride=nsub), :] = srt[:, c * LANES:(c + 1) * LANES]

    def run(e, carry):
        k = i * N_EXPERTS + e
        _run_copies(stage.at[slot], xs_hbm, sem.at[slot], loc_ref[k], start_ref[k], cnt_ref[k])
        return carry

    lax.fori_loop(0, N_EXPERTS, run, 0)

    @pl.when(i == n - 1)
    def _():
        wait_stage(slot)

        @pl.when(n >= 2)
        def _():
            wait_stage(1 - slot)

        pad_rows(wait=True)


def _dispatch(blk_start, blk_cnt, blk_loc, seg_end, seg_tot, h2, meta_i, n_slots, tb):
    t = meta_i.shape[0]
    nsub = D_MODEL // LANES
    assert t % tb == 0
    grid_spec = pltpu.PrefetchScalarGridSpec(
        num_scalar_prefetch=5,
        grid=(t // tb,),
        in_specs=[pl.BlockSpec((tb, D_MODEL), lambda i, *_: (i, 0)),
                  pl.BlockSpec((tb, LANES), lambda i, *_: (i, 0))],
        out_specs=pl.BlockSpec(memory_space=pl.ANY),
        scratch_shapes=[pltpu.VMEM((2, 2 * tb * nsub, LANES), F32),
                        pltpu.VMEM((EXPERT_TILE * nsub, LANES), F32),
                        pltpu.SemaphoreType.DMA((2,)), pltpu.SemaphoreType.DMA(())],
    )
    return pl.pallas_call(
        functools.partial(_dispatch_kernel, tm=EXPERT_TILE),
        grid_spec=grid_spec,
        out_shape=jax.ShapeDtypeStruct((n_slots * nsub, LANES), F32),
        compiler_params=_params(("arbitrary",)),
        name="dispatch",
    )(blk_start, blk_cnt, blk_loc, seg_end, seg_tot, h2, meta_i)


def _expert_kernel(te_ref, tv_ref, nxt_ref, xs_ref, wg_hbm, wu_hbm, wd_hbm, out_ref,
                   wg_st, wu_st, wd_st, wgb, wub, wdb, n_switch, sem):
    i = pl.program_id(0)
    tm = EXPERT_TILE
    nsub = D_MODEL // LANES
    valid = tv_ref[i] != 0
    e = te_ref[i]

    def weight_copies(expert, sl):
        return [pltpu.make_async_copy(hbm.at[0, expert], st.at[sl], sem.at[sl, k])
                for k, (hbm, st) in enumerate(((wg_hbm, wg_st), (wu_hbm, wu_st), (wd_hbm, wd_st)))]

    @pl.when(i == 0)
    def _():
        n_switch[0] = 0
        for dma in weight_copies(e, 0):
            dma.start()

    @pl.when(jnp.logical_not(valid))
    def _():
        out_ref[...] = jnp.zeros_like(out_ref)

    @pl.when(valid & ((i == 0) | (e != te_ref[jnp.maximum(i - 1, 0)])))
    def _():
        sl = n_switch[0] % 2
        n_switch[0] = n_switch[0] + 1
        for dma in weight_copies(e, sl):
            dma.wait()
        wgb[...] = wg_st[sl].astype(BF16)
        wub[...] = wu_st[sl].astype(BF16)
        wdb[...] = wd_st[sl].astype(BF16)

        @pl.when(nxt_ref[e] != e)
        def _():
            for dma in weight_copies(nxt_ref[e], 1 - sl):
                dma.start()

    @pl.when(valid)
    def _():
        h = jnp.concatenate([xs_ref[pl.ds(c, tm, stride=nsub), :] for c in range(nsub)], axis=1).astype(BF16)
        gate = _dot(h, wgb[...])
        up = _dot(h, wub[...])
        a = (gate * _sigmoid(gate) * up).astype(BF16)
        o = _dot(a, wdb[...])
        for c in range(nsub):
            out_ref[pl.ds(c, tm, stride=nsub), :] = o[:, c * LANES:(c + 1) * LANES]


def _experts(tile_expert, tile_valid, next_expert, xs, wg, wu, wd):
    n_tiles = tile_expert.shape[0]
    tm = EXPERT_TILE
    nsub = D_MODEL // LANES
    used = lambda i, tv: jnp.where(tv[i] != 0, i, 0)
    anywhere = pl.BlockSpec(memory_space=pl.ANY)
    up_shape, down_shape = (D_MODEL, D_EXPERT), (D_EXPERT, D_MODEL)
    grid_spec = pltpu.PrefetchScalarGridSpec(
        num_scalar_prefetch=3,
        grid=(n_tiles,),
        in_specs=[pl.BlockSpec((tm * nsub, LANES), lambda i, te, tv, nx: (used(i, tv), 0)),
                  anywhere, anywhere, anywhere],
        out_specs=pl.BlockSpec((tm * nsub, LANES), lambda i, te, tv, nx: (i, 0)),
        scratch_shapes=[pltpu.VMEM((2,) + up_shape, F32), pltpu.VMEM((2,) + up_shape, F32),
                        pltpu.VMEM((2,) + down_shape, F32),
                        pltpu.VMEM(up_shape, BF16), pltpu.VMEM(up_shape, BF16), pltpu.VMEM(down_shape, BF16),
                        pltpu.SMEM((1,), jnp.int32), pltpu.SemaphoreType.DMA((2, 3))],
    )
    return pl.pallas_call(
        _expert_kernel,
        grid_spec=grid_spec,
        out_shape=jax.ShapeDtypeStruct((n_tiles * tm * nsub, LANES), F32),
        compiler_params=_params(("arbitrary",)),
        name="experts",
    )(tile_expert, tile_valid, next_expert, xs, wg, wu, wd)


def _combine_kernel(p0_ref, p1_ref, x1_ref, mw_ref, rows_hbm, y_ref, g0, g1, sem, *, tc, blk_off):
    i = pl.program_id(0)
    n = pl.num_programs(0)
    nout = D_MODEL // LANES
    slot = i % 2

    def issue_all(step, sl):
        base = (step + blk_off) * tc

        def issue(j, carry):
            for pos_ref, dst, s in ((p0_ref, g0, 0), (p1_ref, g1, 1)):
                src0 = pl.multiple_of(pos_ref[base + j] * nout, nout)
                pltpu.make_async_copy(rows_hbm.at[pl.ds(src0, nout), :],
                                      dst.at[sl, pl.ds(pl.multiple_of(j * nout, nout), nout), :],
                                      sem.at[sl, s]).start()
            return carry

        lax.fori_loop(0, tc, issue, 0, unroll=4)

    @pl.when(i == 0)
    def _():
        issue_all(i, slot)

    @pl.when(i + 1 < n)
    def _():
        issue_all(i + 1, 1 - slot)

    for dst, s in ((g0, 0), (g1, 1)):
        pltpu.make_async_copy(rows_hbm.at[pl.ds(0, tc * nout), :], dst.at[slot], sem.at[slot, s]).wait()
    w0 = mw_ref[:, 0:1]
    w1 = mw_ref[:, 1:2]
    for c in range(nout):
        sl = slice(c * LANES, (c + 1) * LANES)
        y_ref[:, sl] = (x1_ref[:, sl] + w0 * g0[slot, pl.ds(c, tc, stride=nout), :]
                        + w1 * g1[slot, pl.ds(c, tc, stride=nout), :])


def _combine(pos0, pos1, x1, meta_w, rows, tc, tok_off):
    t = x1.shape[0]
    assert t % tc == 0 and tok_off % tc == 0
    blk_off = tok_off // tc
    nout = D_MODEL // LANES
    row = pl.BlockSpec((tc, D_MODEL), lambda i, p0, p1: (i, 0))
    grid_spec = pltpu.PrefetchScalarGridSpec(
        num_scalar_prefetch=2,
        grid=(t // tc,),
        in_specs=[row, pl.BlockSpec((tc, LANES), lambda i, p0, p1: (i + blk_off, 0)),
                  pl.BlockSpec(memory_space=pl.ANY)],
        out_specs=row,
        scratch_shapes=[pltpu.VMEM((2, tc * nout, LANES), F32), pltpu.VMEM((2, tc * nout, LANES), F32),
                        pltpu.SemaphoreType.DMA((2, 2))],
    )
    return pl.pallas_call(
        functools.partial(_combine_kernel, tc=tc, blk_off=blk_off),
        grid_spec=grid_spec,
        out_shape=jax.ShapeDtypeStruct((t, D_MODEL), F32),
        compiler_params=_params(("arbitrary",)),
        name="combine",
    )(pos0, pos1, x1, meta_w, rows)


def _rope_tables(pos):
    half = SWA_HEAD_DIM // 2
    inv_freq = ROPE_THETA ** (-np.arange(half, dtype=np.float64) / half)
    ang = pos.astype(np.float64)[:, None] * inv_freq[None, :]
    cos, sin = np.cos(ang), np.sin(ang)
    reps = LANES // SWA_HEAD_DIM
    table = lambda a, b: jnp.asarray(np.tile(np.concatenate([a, b], axis=1), (1, reps)).astype(np.float32))
    return table(cos, cos), table(-sin, sin)


def kernel(x_prompt, x_sample, state_gla, cache_swa_k0, cache_swa_v0, cache_swa_k1, cache_swa_v1, cache_swa_k2, cache_swa_v2, ln1_w, w_in, w_gla_lr, b_gla_lr, gla_onorm_w, q_norm_w, k_norm_w, w_branch_a, w_branch_b, w_out, ln2_w, w_router_group, b_router_group, w_router_expert, b_router_expert, w_exp_gate, w_exp_up, w_exp_down):
    b, s, d = x_prompt.shape
    bd, ls, _ = x_sample.shape
    tp, ts = b * s, bd * ls
    assert w_in.shape[0] == 1 and d == D_MODEL and ts % SUBLANES == 0
    k_caches = (cache_swa_k0, cache_swa_k1, cache_swa_k2)
    v_caches = (cache_swa_v0, cache_swa_v1, cache_swa_v2)

    w = w_in[0]
    cuts = np.cumsum((512, 512, 1024, GLA_RANK, 1024, 768, 768, 768, 1024, 1024))
    sec = lambda a: w[:, (0 if a == 0 else cuts[a - 1]):cuts[a]]
    lr_pad = jnp.pad(sec(3), ((0, 0), (0, LANES - GLA_RANK)))
    w_f32 = jnp.concatenate([sec(0), sec(1), sec(2), sec(4), sec(5), sec(6), sec(7), sec(8), sec(9), lr_pad], axis=1)
    w_packed = w_f32.astype(BF16)
    wlr_f32 = jnp.pad(w_gla_lr[0], ((0, LANES - GLA_RANK), (0, 0)))
    wlr = wlr_f32.astype(BF16)
    blr = b_gla_lr[0][None, :]
    nw = jnp.concatenate([jnp.tile(q_norm_w[0], SWA_WIDTH // SWA_HEAD_DIM), jnp.tile(k_norm_w[0], SWA_WIDTH // SWA_HEAD_DIM)])[None, :]
    gi = np.arange(256) // SWA_HEAD_DIM
    gmat_f32 = jnp.asarray((gi[:, None] == gi[None, :]).astype(np.float32) / SWA_HEAD_DIM)
    gmat = gmat_f32.astype(BF16)
    ln1 = ln1_w[0][None, :]
    ln2 = ln2_w[0][None, :]
    onw = gla_onorm_w[0][None, :]
    wa_f32, wb_f32, wo_f32 = w_branch_a[0], w_branch_b[0], w_out[0]
    wa, wb, wo = wa_f32.astype(BF16), wb_f32.astype(BF16), wo_f32.astype(BF16)
    wr = jnp.pad(jnp.concatenate([w_router_group[0], w_router_expert[0]], axis=1),
                 ((0, 0), (0, LANES - N_GROUPS - N_EXPERTS)))
    wr_hi, wr_lo = _split_bf16(wr)
    br = jnp.pad(jnp.concatenate([b_router_group[0], b_router_expert[0]]), (0, LANES - N_GROUPS - N_EXPERTS))[None, :]

    cos_p, sin_p = _rope_tables(np.arange(s))
    cos_s, sin_s = _rope_tables(PAST_LEN + np.arange(ts) % ls)

    tm_p = 512
    proj_p = _proj(x_prompt.reshape(tp, d), cos_p, sin_p, s // tm_p, ln1, w_packed, wlr, blr, nw, gmat, tm_p)
    proj_s = _proj(x_sample.reshape(ts, d), cos_s, sin_s, 1, ln1, w_f32, wlr_f32, blr, nw, gmat_f32, ts)
    gqkv_p, la_p, sog_p, q_p, k_p, v_p, sgab_p = proj_p
    gqkv_s, la_s, sog_s, q_s, k_s, v_s, sgab_s = proj_s

    r3 = lambda t, nb: t.reshape(nb, t.shape[0] // nb, t.shape[1])
    ya_p, st_p = _gla(r3(gqkv_p, b), r3(la_p, b), r3(sog_p, b),
                      jnp.zeros((b, GLA_HEADS, GLA_DK, GLA_DV), F32), onw, GLA_CHUNK, 512)
    pad_s = lambda t: jnp.pad(r3(t, bd), ((0, 0), (0, _SAMPLE_ROWS - ls), (0, 0)))
    ya_s, st_s = _gla(pad_s(gqkv_s), pad_s(la_s), pad_s(sog_s), state_gla[0], onw, _SAMPLE_ROWS, _SAMPLE_ROWS)
    ya_s = ya_s[:, :ls].reshape(ts, d)

    q3, k3, v3 = r3(q_p, b), r3(k_p, b), r3(v_p, b)
    swa_p = [_swa_prompt(q3, k3, v3, g, dil) for g, (_, dil) in enumerate(SWA_GROUPS)]
    to_t = lambda c: jnp.transpose(c[0], (0, 2, 3, 1))
    caches_t = [(to_t(k_caches[g]), to_t(v_caches[g])) for g in range(len(SWA_GROUPS))]
    ob_s, *new_caches = _swa_sample(pad_s(q_s), pad_s(k_s), pad_s(v_s), caches_t, ls)
    ob_s = ob_s[:, :ls].reshape(ts, SWA_OUT)

    t = tp + ts
    mw = (wa, wb, wo, ln2, jnp.concatenate([wr_hi, wr_lo], axis=1), wr_hi, br)
    x1_p, h2c, lg, cnt_p = _merge(x_prompt.reshape(tp, d), ya_p.reshape(tp, d),
                                  [o for o, _ in swa_p] + [l for _, l in swa_p], sgab_p, mw, MERGE_TILE, 0, t)
    mw_f32 = (wa_f32, wb_f32, wo_f32) + mw[3:]
    x1_s, h2c, lg, cnt_s = _merge(x_sample.reshape(ts, d), ya_s, [ob_s], sgab_s, mw_f32, ts, tp, t,
                                  shared=(h2c, lg))

    tm = EXPERT_TILE
    n_tiles = (2 * t) // tm + N_EXPERTS
    counts = ((jnp.sum(cnt_p, axis=0) + jnp.sum(cnt_s, axis=0)) * (1.0 / SUBLANES)).astype(jnp.int32)
    ends_all = jnp.cumsum((counts + tm - 1) // tm * tm)
    seg_start = jnp.broadcast_to((ends_all - (counts + tm - 1) // tm * tm).astype(F32)[None, :], (SUBLANES, LANES))
    counts, ends = counts[:N_EXPERTS], ends_all[:N_EXPERTS]
    tb = DISPATCH_BLOCK
    meta_i, meta_w, blocks = _router(lg, seg_start, t, tb)
    blocks = blocks.reshape(t // tb, SUBLANES, LANES)[:, :, :N_EXPERTS]
    blk_start, blk_cnt, blk_loc = (blocks[:, r].reshape(-1) for r in range(3))
    pos0, pos1 = meta_i[:, 2], meta_i[:, 3]
    tile_start = jnp.arange(n_tiles, dtype=jnp.int32) * tm
    tile_valid = (tile_start < ends[-1]).astype(jnp.int32)
    last_slot = jnp.minimum(tile_start, ends[-1] - 1)
    tile_expert = jnp.minimum(jnp.sum((last_slot[:, None] >= ends[None, :]).astype(jnp.int32), axis=1), N_EXPERTS - 1)

    xs = _dispatch(blk_start, blk_cnt, blk_loc, ends, counts, h2c, meta_i, n_tiles * tm, tb)
    ids = jnp.arange(N_EXPERTS, dtype=jnp.int32)
    later_used = (counts[None, :] > 0) & (ids[None, :] > ids[:, None])
    next_expert = jnp.min(jnp.where(later_used, ids[None, :], N_EXPERTS), axis=1)
    next_expert = jnp.where(next_expert == N_EXPERTS, ids, next_expert)
    rows = _experts(tile_expert, tile_valid, next_expert, xs, w_exp_gate, w_exp_up, w_exp_down)
    y_p = _combine(pos0, pos1, x1_p, meta_w, rows, 256, 0)
    y_s = _combine(pos0, pos1, x1_s, meta_w, rows, ts, tp)

    heads = lambda a: a.reshape(1, a.shape[0], a.shape[1], SWA_HPG, SWA_HEAD_DIM)
    outs = [y_p.reshape(b, s, d), y_s.reshape(bd, ls, d), st_p[None].astype(x_prompt.dtype)]
    for g, (win, _) in enumerate(SWA_GROUPS):
        keep = min(win, s)
        gsl = slice(g * SWA_OUT, (g + 1) * SWA_OUT)
        outs += [heads(k3[:, s - keep:, gsl]), heads(v3[:, s - keep:, gsl])]
    outs.append(st_s[None].astype(state_gla.dtype))
    outs += [jnp.transpose(c, (0, 3, 1, 2))[None] for c in new_caches]
    return tuple(outs)
```

```python
import functools

import numpy as np
import jax
import jax.numpy as jnp
from jax import lax
from jax.experimental import pallas as pl
from jax.experimental.pallas import tpu as pltpu

F32 = jnp.float32
BF16 = jnp.bfloat16

D_MODEL = 1024
PAST_LEN = 16384
GLA_HEADS = 4
GLA_DK = 128
GLA_DV = 256
GLA_RANK = 16
GLA_TAU = 16.0
GLA_CHUNK = 64
SWA_GROUPS = ((128, 1), (512, 4), (2048, 16))
SWA_HPG = 4
SWA_HEAD_DIM = 64
SWA_WIDTH = 768
SWA_OUT = 256
SWA_BLOCK = 128
ROPE_THETA = 10000.0
N_GROUPS = 4
EXPERTS_PER_GROUP = 8
N_EXPERTS = 32
D_EXPERT = 512
EPS = 1e-6

LANES = 128
SUBLANES = 8
VMEM_LIMIT = 56 * 1024 * 1024
NEG = -1e30
EXPERT_TILE = 256
MERGE_TILE = 512
DISPATCH_BLOCK = 384

_C_GQKV = (0, 2048)
_C_GOG = (2048, 3072)
_C_QK = (3072, 4608)
_C_V = (4608, 5376)
_C_GAB = (5376, 7424)
_C_LR = (7424, 7552)


def _contract(a, b, dims):
    dg = lambda x, y: lax.dot_general(x, y, (dims, ((), ())), preferred_element_type=F32)
    if a.dtype == F32 and b.dtype == F32:
        a_hi, a_lo = _split(a, BF16)
        b_hi, b_lo = _split(b, BF16)
        return dg(a_hi, b_hi) + dg(a_hi, b_lo) + dg(a_lo, b_hi)
    return dg(a, b)


def _dot(a, b):
    return _contract(a, b, ((1,), (0,)))


def _dot_nt(a, b):
    return _contract(a, b, ((1,), (1,)))


def _dot_tn(a, b):
    return _contract(a, b, ((0,), (0,)))


def _sigmoid(x):
    return 1.0 / (1.0 + jnp.exp(-x))


def _split(x, dt):
    hi = x.astype(dt)
    lo = (x - hi.astype(F32)).astype(dt)
    return hi, lo


def _split_bf16(x):
    return _split(x, BF16)


def _params(sem):
    return pltpu.CompilerParams(dimension_semantics=sem, vmem_limit_bytes=VMEM_LIMIT)


def _resident(shape):
    nd = len(shape)
    return pl.BlockSpec(shape, lambda *_: (0,) * nd, pipeline_mode=pl.Buffered(1))


def _proj_kernel(x_ref, cos_ref, sin_ref, ln_ref, w_ref, wlr_ref, blr_ref, nw_ref, g_ref,
                 gqkv_ref, la_ref, sog_ref, q_ref, k_ref, v_ref, sgab_ref):
    cdt = w_ref.dtype
    x = x_ref[...]
    h = (x * lax.rsqrt(jnp.mean(x * x, axis=-1, keepdims=True) + EPS) * ln_ref[...]).astype(cdt)
    gqkv_ref[...] = _dot(h, w_ref[:, _C_GQKV[0]:_C_GQKV[1]]).astype(cdt)
    og = _dot(h, w_ref[:, _C_GOG[0]:_C_GOG[1]])
    sog_ref[...] = (og * _sigmoid(og)).astype(cdt)
    lr = _dot(h, w_ref[:, _C_LR[0]:_C_LR[1]]).astype(cdt)
    z = _dot(lr, wlr_ref[...]) + blr_ref[...]
    la_ref[...] = (jnp.minimum(z, 0.0) - jnp.log(1.0 + jnp.exp(-jnp.abs(z)))) / GLA_TAU
    qk = _dot(h, w_ref[:, _C_QK[0]:_C_QK[1]])
    sq = (qk * qk).astype(cdt)
    ms = jnp.concatenate([_dot(sq[:, c * 256:(c + 1) * 256], g_ref[...]) for c in range(6)], axis=1)
    qn = qk * lax.rsqrt(ms + EPS) * nw_ref[...]
    width = 2 * SWA_WIDTH
    cos = jnp.tile(cos_ref[...], (1, width // LANES))
    sin = jnp.tile(sin_ref[...], (1, width // LANES))
    lane = lax.broadcasted_iota(jnp.int32, qn.shape, 1)
    half = SWA_HEAD_DIM // 2
    rot = jnp.where(lane % SWA_HEAD_DIM < half, pltpu.roll(qn, width - half, 1), pltpu.roll(qn, half, 1))
    qr = qn * cos + rot * sin
    q_ref[...] = qr[:, :SWA_WIDTH]
    k_ref[...] = qr[:, SWA_WIDTH:]
    v_ref[...] = _dot(h, w_ref[:, _C_V[0]:_C_V[1]])
    gab = _dot(h, w_ref[:, _C_GAB[0]:_C_GAB[1]])
    sgab_ref[...] = _sigmoid(gab).astype(cdt)


def _proj(x, cos, sin, rope_blocks, ln, w, wlr, blr, nw, g, tm):
    t = x.shape[0]
    assert t % tm == 0 and w.dtype == wlr.dtype == g.dtype
    row = lambda width: pl.BlockSpec((tm, width), lambda i: (i, 0))
    outs = [(2048, w.dtype), (512, F32), (1024, w.dtype), (768, F32), (768, F32), (768, F32), (2048, w.dtype)]
    return pl.pallas_call(
        _proj_kernel,
        grid=(t // tm,),
        in_specs=[row(D_MODEL),
                  pl.BlockSpec((tm, LANES), lambda i: (i % rope_blocks, 0)),
                  pl.BlockSpec((tm, LANES), lambda i: (i % rope_blocks, 0)),
                  _resident(ln.shape), _resident(w.shape), _resident(wlr.shape), _resident(blr.shape),
                  _resident(nw.shape), _resident(g.shape)],
        out_specs=[row(wd) for wd, _ in outs],
        out_shape=[jax.ShapeDtypeStruct((t, wd), dt) for wd, dt in outs],
        compiler_params=_params(("arbitrary",)),
        name="proj",
    )(x, cos, sin, ln, w, wlr, blr, nw, g)


def _gla_kernel(gqkv_ref, la_ref, sog_ref, s0_ref, onw_ref, y_ref, sfin_ref, st_scr, *, chunk, n_chunks):
    j = pl.program_id(1)

    @pl.when(j == 0)
    def _():
        for h in range(GLA_HEADS):
            st_scr[h] = s0_ref[0, h].T

    r = lax.broadcasted_iota(jnp.int32, (chunk, chunk), 0)
    c = lax.broadcasted_iota(jnp.int32, (chunk, chunk), 1)
    causal = r >= c
    cdt = gqkv_ref.dtype
    tri = causal.astype(cdt)
    hk = GLA_HEADS * GLA_DK
    rows = [slice(ci * chunk, (ci + 1) * chunk) for ci in range(n_chunks)]

    la_hi, la_lo = _split(la_ref[0], cdt)
    b_chunks = [_dot(tri, la_hi[rs]) + _dot(tri, la_lo[rs]) for rs in rows]
    last = [bc[chunk - 1:chunk, :] for bc in b_chunks]
    b = jnp.concatenate(b_chunks, axis=0)
    b_end = jnp.concatenate([jnp.broadcast_to(x, (chunk, hk)) for x in last], axis=0)
    q = gqkv_ref[0, :, :hk].astype(F32) * GLA_DK ** -0.5
    k = gqkv_ref[0, :, hk:2 * hk].astype(F32)
    qd = (q * jnp.exp(b)).astype(cdt)
    kd = (k * jnp.exp(-b)).astype(cdt)
    kdec = (k * jnp.exp(b_end - b)).astype(cdt)
    decay = [jnp.exp(x) for x in last]

    state = [st_scr[h] for h in range(GLA_HEADS)]
    for ci, rs in enumerate(rows):
        outs = []
        for h in range(GLA_HEADS):
            sl = slice(h * GLA_DK, (h + 1) * GLA_DK)
            v_h = gqkv_ref[0, rs, 2 * hk + h * GLA_DV:2 * hk + (h + 1) * GLA_DV]
            att = jnp.where(causal, _dot_nt(qd[rs, sl], kd[rs, sl]), 0.0).astype(cdt)
            o = _dot(att, v_h) + _dot_nt(qd[rs, sl], state[h].astype(cdt))
            state[h] = state[h] * decay[ci][:, sl] + _dot_tn(v_h, kdec[rs, sl])
            ms = jnp.mean(o * o, axis=-1, keepdims=True)
            outs.append(o * lax.rsqrt(ms + EPS) * onw_ref[...])
        o_all = jnp.concatenate(outs, axis=1) * sog_ref[0, rs, :].astype(F32)
        y_ref[0, rs, :] = o_all.astype(cdt)
    for h in range(GLA_HEADS):
        st_scr[h] = state[h]

    @pl.when(j == pl.num_programs(1) - 1)
    def _():
        for h in range(GLA_HEADS):
            sfin_ref[0, h] = state[h].T


def _gla(gqkv, la, sog, s0, onw, chunk, block):
    b, l, _ = gqkv.shape
    tok = lambda width: pl.BlockSpec((1, block, width), lambda bi, j: (bi, j, 0))
    st = pl.BlockSpec((1, GLA_HEADS, GLA_DK, GLA_DV), lambda bi, j: (bi, 0, 0, 0))
    return pl.pallas_call(
        functools.partial(_gla_kernel, chunk=chunk, n_chunks=block // chunk),
        grid=(b, l // block),
        in_specs=[tok(2048), tok(512), tok(1024), st, _resident(onw.shape)],
        out_specs=[tok(1024), st],
        out_shape=[jax.ShapeDtypeStruct((b, l, 1024), gqkv.dtype),
                   jax.ShapeDtypeStruct((b, GLA_HEADS, GLA_DK, GLA_DV), F32)],
        scratch_shapes=[pltpu.VMEM((GLA_HEADS, GLA_DV, GLA_DK), F32)],
        compiler_params=_params(("arbitrary", "arbitrary")),
        name="gla",
    )(gqkv, la, sog, s0, onw)


def _band_heads(q, kw, vw, valid):
    n = q.shape[0]
    low = lax.broadcasted_iota(jnp.int32, (n, LANES), 1) < SWA_HEAD_DIM
    ones = jnp.ones((kw.shape[0], LANES), kw.dtype)
    outs, lses = [], []
    for pair in range(SWA_OUT // LANES):
        cols = slice(pair * LANES, (pair + 1) * LANES)
        v_aug = jnp.concatenate([vw[:, cols], ones], axis=1)
        res = []
        for mine in (low, jnp.logical_not(low)):
            qm = jnp.where(mine, q[:, cols], 0.0).astype(kw.dtype)
            s = _dot_nt(qm, kw[:, cols]) * SWA_HEAD_DIM ** -0.5
            s = jnp.where(valid, s, NEG)
            m = jnp.max(s, axis=-1, keepdims=True)
            r = _dot(jnp.exp(s - m).astype(vw.dtype), v_aug)
            den = r[:, LANES:]
            res.append((r[:, :LANES] / den, m + jnp.log(den)))
        outs.append(jnp.where(low, res[0][0], res[1][0]))
        lses.append(jnp.where(low, res[0][1], res[1][1]))
    return jnp.concatenate(outs, axis=1), jnp.concatenate(lses, axis=1)


_SWA_TOKENS = 2048


def _swa_kernel(q_ref, k_ref, v_ref, kp_ref, vp_ref, o_ref, lse_ref, *stage, dil):
    blk = SWA_BLOCK
    nsub = q_ref.shape[1] // (blk * dil)
    first = pl.program_id(1) == 0
    qi = lax.broadcasted_iota(jnp.int32, (blk, 2 * blk), 0)
    kj = lax.broadcasted_iota(jnp.int32, (blk, 2 * blk), 1)
    band = (kj >= qi) & (kj <= qi + blk)
    halves = SWA_OUT // LANES

    if dil > 1:
        ins = (q_ref, k_ref, v_ref, kp_ref, vp_ref)
        q_ref, k_ref, v_ref, kp_ref, vp_ref, o_st, lse_st = stage
        for src, dst in zip(ins, stage):
            for hf in range(halves):
                dst[hf] = src[0, :, hf * LANES:(hf + 1) * LANES]

    def rows(ref, start):
        if dil == 1:
            return ref[0, pl.ds(start, blk), :]
        return jnp.concatenate([ref[hf, pl.ds(start, blk, stride=dil), :] for hf in range(halves)], axis=1)

    def unit(u, carry):
        r = u // nsub
        j = u % nsub
        start = r + dil * blk * j
        inside = r + dil * blk * jnp.maximum(j - 1, 0)
        if dil == 1:
            r, start, inside = 0, pl.multiple_of(start, blk), pl.multiple_of(inside, blk)
        head = j == 0
        kprev = jnp.where(head, rows(kp_ref, r), rows(k_ref, inside))
        vprev = jnp.where(head, rows(vp_ref, r), rows(v_ref, inside))
        kw = jnp.concatenate([kprev, rows(k_ref, start)], axis=0).astype(BF16)
        vw = jnp.concatenate([vprev, rows(v_ref, start)], axis=0).astype(BF16)
        valid = band & (kj >= jnp.where(head & first, blk, 0))
        o, lse = _band_heads(rows(q_ref, start), kw, vw, valid)
        if dil == 1:
            o_ref[0, pl.ds(start, blk), :] = o
            lse_ref[0, pl.ds(start, blk), :] = lse
        else:
            for hf in range(halves):
                o_st[hf, pl.ds(start, blk, stride=dil), :] = o[:, hf * LANES:(hf + 1) * LANES]
                lse_st[hf, pl.ds(start, blk, stride=dil), :] = lse[:, hf * LANES:(hf + 1) * LANES]
        return carry

    lax.fori_loop(0, dil * nsub, unit, 0)
    if dil > 1:
        for hf in range(halves):
            o_ref[0, :, hf * LANES:(hf + 1) * LANES] = o_st[hf]
            lse_ref[0, :, hf * LANES:(hf + 1) * LANES] = lse_st[hf]


def _swa_prompt(q, k, v, g, dil):
    b, s, _ = q.shape
    tb = _SWA_TOKENS
    back = SWA_BLOCK * dil
    assert s % tb == 0 and tb % back == 0
    cur = pl.BlockSpec((1, tb, SWA_OUT), lambda bi, i: (bi, i, g))
    prev = pl.BlockSpec((1, back, SWA_OUT), lambda bi, i: (bi, jnp.maximum(i * (tb // back) - 1, 0), g))
    out = pl.BlockSpec((1, tb, SWA_OUT), lambda bi, i: (bi, i, 0))
    halves = SWA_OUT // LANES
    stage = [pltpu.VMEM((halves, n, LANES), F32) for n in (tb, tb, tb, back, back, tb, tb)] if dil > 1 else []
    o, lse = pl.pallas_call(
        functools.partial(_swa_kernel, dil=dil),
        grid=(b, s // tb),
        in_specs=[cur, cur, cur, prev, prev],
        out_specs=[out, out],
        out_shape=[jax.ShapeDtypeStruct((b, s, SWA_OUT), F32)] * 2,
        scratch_shapes=stage,
        compiler_params=_params(("arbitrary", "arbitrary")),
        name=f"swa_prompt_g{g}",
    )(q, k, v, k, v)
    return o.reshape(b * s, SWA_OUT), lse.reshape(b * s, SWA_OUT)


_SAMPLE_ROWS = 16


def _swa_sample_kernel(q_ref, kn_ref, vn_ref, k0_ref, v0_ref, k1_ref, v1_ref, k2_ref, v2_ref,
                       ob_ref, ok0_ref, ov0_ref, ok1_ref, ov1_ref, ok2_ref, ov2_ref, *, n_new):
    rows = _SAMPLE_ROWS
    in_refs = ((k0_ref, v0_ref), (k1_ref, v1_ref), (k2_ref, v2_ref))
    out_refs = ((ok0_ref, ov0_ref), (ok1_ref, ov1_ref), (ok2_ref, ov2_ref))
    scale = SWA_HEAD_DIM ** -0.5
    jn = lax.broadcasted_iota(jnp.int32, (rows, rows), 1)
    ln = lax.broadcasted_iota(jnp.int32, (rows, rows), 0)
    tail = lax.broadcasted_iota(jnp.int32, (SWA_HEAD_DIM, LANES), 1) >= LANES - n_new
    sel_row = lax.broadcasted_iota(jnp.int32, (rows, LANES), 0)
    sel_lane = lax.broadcasted_iota(jnp.int32, (rows, LANES), 1)
    selector = ((sel_lane == sel_row + (LANES - n_new)) & (sel_row < n_new)).astype(F32)
    knt = _dot_tn(kn_ref[0], selector)
    vnt = _dot_tn(vn_ref[0], selector)
    o_g, lse_g = [], []
    for g, (win, dil) in enumerate(SWA_GROUPS):
        jc = lax.broadcasted_iota(jnp.int32, (rows, win), 1)
        lc = lax.broadcasted_iota(jnp.int32, (rows, win), 0)
        valid_c = (jc >= lc) & (((jc - lc) & (dil - 1)) == 0)
        valid_n = (jn <= ln) & (((ln - jn) & (dil - 1)) == 0) & (jn < n_new)
        o_h, lse_h = [], []
        for h in range(SWA_HPG):
            col = g * SWA_OUT + h * SWA_HEAD_DIM
            hsl = slice(col, col + SWA_HEAD_DIM)
            qh = q_ref[0, :, hsl]
            knh = kn_ref[0, :, hsl]
            vnh = vn_ref[0, :, hsl]
            for (src, dst, new_t) in ((in_refs[g][0], out_refs[g][0], knt), (in_refs[g][1], out_refs[g][1], vnt)):
                old = src[0, h]
                moved = pltpu.roll(old, win - n_new, 1)
                if win > LANES:
                    dst[0, h, :, 0:win - LANES] = moved[:, 0:win - LANES]
                dst[0, h, :, win - LANES:win] = jnp.where(tail, new_t[hsl, :], moved[:, win - LANES:win])
            kt = in_refs[g][0][0, h]
            vt = in_refs[g][1][0, h]
            s_c = jnp.where(valid_c, _dot(qh, kt) * scale, NEG)
            s_n = jnp.where(valid_n, _dot_nt(qh, knh) * scale, NEG)
            m = jnp.maximum(jnp.max(s_c, axis=-1, keepdims=True), jnp.max(s_n, axis=-1, keepdims=True))
            p_c = jnp.exp(s_c - m)
            p_n = jnp.exp(s_n - m)
            den = jnp.sum(p_c, axis=-1, keepdims=True) + jnp.sum(p_n, axis=-1, keepdims=True)
            o_h.append((_dot_nt(p_c, vt) + _dot(p_n, vnh)) / den)
            lse_h.append(jnp.broadcast_to(m + jnp.log(den), (rows, SWA_HEAD_DIM)))
        o_g.append(jnp.concatenate(o_h, axis=1))
        lse_g.append(jnp.concatenate(lse_h, axis=1))
    lmax = jnp.maximum(jnp.maximum(lse_g[0], lse_g[1]), lse_g[2])
    e = [jnp.exp(x - lmax) for x in lse_g]
    ob_ref[0] = (e[0] * o_g[0] + e[1] * o_g[1] + e[2] * o_g[2]) / (e[0] + e[1] + e[2])


def _swa_sample(q, kn, vn, caches_t, n_new):
    bd = q.shape[0]
    rows = _SAMPLE_ROWS
    tok = pl.BlockSpec((1, rows, SWA_WIDTH), lambda bi: (bi, 0, 0))
    specs, args = [tok, tok, tok], [q, kn, vn]
    out_specs = [pl.BlockSpec((1, rows, SWA_OUT), lambda bi: (bi, 0, 0))]
    out_shape = [jax.ShapeDtypeStruct((bd, rows, SWA_OUT), F32)]
    for g, (win, dil) in enumerate(SWA_GROUPS):
        for t in caches_t[g]:
            assert t.shape == (bd, SWA_HPG, SWA_HEAD_DIM, win) and win == SWA_BLOCK * dil and win % LANES == 0
            spec = pl.BlockSpec((1, SWA_HPG, SWA_HEAD_DIM, win), lambda bi: (bi, 0, 0, 0))
            args.append(t)
            specs.append(spec)
            out_specs.append(spec)
            out_shape.append(jax.ShapeDtypeStruct(t.shape, t.dtype))
    return pl.pallas_call(
        functools.partial(_swa_sample_kernel, n_new=n_new),
        grid=(bd,),
        in_specs=specs,
        out_specs=out_specs,
        out_shape=out_shape,
        compiler_params=_params(("arbitrary",)),
        name="swa_sample",
    )(*args)


def _merge_kernel(*refs, combine, n_alias, n_real):
    x1_ref, h2c_ref, lg_ref, cnt_ref = refs[-4:]
    refs = refs[:len(refs) - 4 - n_alias]

    @pl.when(pl.program_id(0) >= n_real)
    def _():
        h2c_ref[...] = jnp.zeros_like(h2c_ref)
        lg_ref[...] = jnp.zeros_like(lg_ref)
        cnt_ref[...] = jnp.zeros_like(cnt_ref)

    pl.when(pl.program_id(0) < n_real)(
        functools.partial(_merge_tile, refs, x1_ref, h2c_ref, lg_ref, cnt_ref, combine))


def _merge_tile(refs, x1_ref, h2c_ref, lg_ref, cnt_ref, combine):
    if combine:
        (x_ref, ya_ref, o0, o1, o2, l0, l1, l2, sgab_ref, wa_ref, wb_ref, wo_ref, ln_ref, wr_cat_ref, wr_hi_ref,
         br_ref) = refs
        lmax = jnp.maximum(jnp.maximum(l0[...], l1[...]), l2[...])
        e0, e1, e2 = jnp.exp(l0[...] - lmax), jnp.exp(l1[...] - lmax), jnp.exp(l2[...] - lmax)
        ob = ((e0 * o0[...] + e1 * o1[...] + e2 * o2[...]) / (e0 + e1 + e2)).astype(wb_ref.dtype)
    else:
        (x_ref, ya_ref, ob_ref, sgab_ref, wa_ref, wb_ref, wo_ref, ln_ref, wr_cat_ref, wr_hi_ref, br_ref) = refs
        ob = ob_ref[...]
    ya = _dot(ya_ref[...], wa_ref[...])
    yb = _dot(ob, wb_ref[...])
    sga = sgab_ref[:, :D_MODEL].astype(F32)
    sgb = sgab_ref[:, D_MODEL:].astype(F32)
    x1 = x_ref[...] + _dot((sga * ya + sgb * yb).astype(wo_ref.dtype), wo_ref[...])
    x1_ref[...] = x1
    h2 = x1 * lax.rsqrt(jnp.mean(x1 * x1, axis=-1, keepdims=True) + EPS) * ln_ref[...]
    h_hi, h_lo = _split_bf16(h2)
    both = _dot(h_hi, wr_cat_ref[...])
    lg = both[:, :LANES] + both[:, LANES:] + _dot(h_lo, wr_hi_ref[...]) + br_ref[...]
    lg_ref[...] = lg
    h2c_ref[...] = h_hi
    _, _, _, _, _, hot1, hot2 = _route_select(lg)
    cnt = jnp.sum((hot1 | hot2).astype(F32), axis=0, keepdims=True)
    cnt_ref[...] = jnp.broadcast_to(cnt, cnt_ref.shape)


def _merge(x, ya_in, swa, sgab, weights, tm, tok_off, t_all, shared=None):
    t = x.shape[0]
    assert t % tm == 0 and tok_off % tm == 0 and MERGE_TILE % tm == 0
    combine = len(swa) > 1
    blk_off = tok_off // tm
    t_buf = -(-t_all // MERGE_TILE) * MERGE_TILE
    n_real = t // tm
    n_fill = 0 if shared is not None else (t_buf - tok_off - t) // tm
    row = lambda width: pl.BlockSpec((tm, width), lambda i: (jnp.minimum(i, n_real - 1), 0))
    shared_in = [] if shared is None else list(shared)
    n_in = 3 + len(swa) + len(weights)
    return pl.pallas_call(
        functools.partial(_merge_kernel, combine=combine, n_alias=len(shared_in), n_real=n_real),
        grid=(n_real + n_fill,),
        in_specs=[row(D_MODEL), row(D_MODEL)] + [row(SWA_OUT)] * len(swa) + [row(2 * D_MODEL)]
                 + [_resident(w.shape) for w in weights] + [pl.BlockSpec(memory_space=pl.ANY)] * len(shared_in),
        out_specs=[row(D_MODEL),
                   pl.BlockSpec((tm, D_MODEL), lambda i: (i + blk_off, 0)),
                   pl.BlockSpec((tm, LANES), lambda i: (i + blk_off, 0)),
                   pl.BlockSpec((SUBLANES, LANES), lambda i: (i, 0))],
        out_shape=[jax.ShapeDtypeStruct((t, D_MODEL), F32),
                   jax.ShapeDtypeStruct((t_buf, D_MODEL), BF16),
                   jax.ShapeDtypeStruct((t_buf, LANES), F32),
                   jax.ShapeDtypeStruct(((n_real + n_fill) * SUBLANES, LANES), F32)],
        input_output_aliases={n_in + k: 1 + k for k in range(len(shared_in))},
        compiler_params=_params(("arbitrary",)),
        name="merge",
    )(x, ya_in, *swa, sgab, *weights, *shared_in)


def _route_select(lg):
    lane = lax.broadcasted_iota(jnp.int32, lg.shape, 1)
    big = jnp.int32(LANES)
    gl = jnp.where(lane < N_GROUPS, lg, NEG)
    gmax = jnp.max(gl, axis=-1, keepdims=True)
    g_idx = jnp.min(jnp.where(gl == gmax, lane, big), axis=-1, keepdims=True)
    g_w = 1.0 / jnp.sum(jnp.exp(gl - gmax), axis=-1, keepdims=True)
    e_lane = lane - N_GROUPS
    in_group = (e_lane >= 0) & (e_lane < N_EXPERTS) & (e_lane // EXPERTS_PER_GROUP == g_idx)
    el = jnp.where(in_group, lg, NEG)
    v1 = jnp.max(el, axis=-1, keepdims=True)
    i1 = jnp.min(jnp.where(el == v1, lane, big), axis=-1, keepdims=True)
    el2 = jnp.where(lane == i1, NEG, el)
    v2 = jnp.max(el2, axis=-1, keepdims=True)
    i2 = jnp.min(jnp.where(el2 == v2, lane, big), axis=-1, keepdims=True)
    r21 = jnp.exp(v2 - v1)
    w1 = g_w / (1.0 + r21)
    w2 = g_w * r21 / (1.0 + r21)
    e1 = i1 - N_GROUPS
    e2 = i2 - N_GROUPS
    return lane, e1, e2, w1, w2, lane == e1, lane == e2


def _router_kernel(lg_ref, offs_ref, mi_ref, mw_ref, blk_ref, carry):
    @pl.when(pl.program_id(0) == 0)
    def _():
        carry[...] = offs_ref[0:1, :]

    lane, e1, e2, w1, w2, hot1, hot2 = _route_select(lg_ref[...])
    tr = lane.shape[0]
    hot = (hot1 | hot2).astype(BF16)
    start = carry[...]
    cnt = jnp.sum(hot.astype(F32), axis=0, keepdims=True)
    carry[...] = start + cnt
    r = lax.broadcasted_iota(jnp.int32, (tr, tr), 0)
    c = lax.broadcasted_iota(jnp.int32, (tr, tr), 1)
    within = _dot((r > c).astype(BF16), hot)
    lane1 = lax.broadcasted_iota(jnp.int32, (SUBLANES, LANES), 1)
    incl = jnp.broadcast_to(cnt, (SUBLANES, LANES))
    for sh in (1, 2, 4, 8, 16):
        incl = incl + jnp.where(lane1 >= sh, pltpu.roll(incl, sh, 1), 0.0)
    local = incl[0:1, :] - cnt
    pick = lambda hot_k, row: jnp.sum(jnp.where(hot_k, row, 0.0), axis=-1, keepdims=True)
    cols = [e1, e2]
    for hot_k in (hot1, hot2):
        cols.append((pick(hot_k, within) + pick(hot_k, start)).astype(jnp.int32))
    for hot_k in (hot1, hot2):
        cols.append((pick(hot_k, within) + pick(hot_k, local)).astype(jnp.int32))
    mi = cols[-1]
    for k in range(len(cols) - 2, -1, -1):
        mi = jnp.where(lane == k, cols[k], mi)
    mi_ref[...] = mi
    mw_ref[...] = jnp.where(lane == 0, w1, w2)
    sub = lax.broadcasted_iota(jnp.int32, (SUBLANES, LANES), 0)
    tbl = jnp.where(sub == 0, start, jnp.where(sub == 1, cnt, jnp.where(sub == 2, local, 0.0)))
    blk_ref[...] = tbl.astype(jnp.int32)


def _router(logits, seg_start, t, tr):
    assert t % tr == 0 and tr % SUBLANES == 0
    row = pl.BlockSpec((tr, LANES), lambda i: (i, 0))
    small = pl.BlockSpec((SUBLANES, LANES), lambda i: (i, 0))
    return pl.pallas_call(
        _router_kernel,
        grid=(t // tr,),
        in_specs=[row, pl.BlockSpec((SUBLANES, LANES), lambda i: (0, 0))],
        out_specs=[row, row, small],
        out_shape=[jax.ShapeDtypeStruct((t, LANES), jnp.int32), jax.ShapeDtypeStruct((t, LANES), F32),
                   jax.ShapeDtypeStruct((t // tr * SUBLANES, LANES), jnp.int32)],
        scratch_shapes=[pltpu.VMEM((1, LANES), F32)],
        compiler_params=_params(("arbitrary",)),
        name="router",
    )(logits, seg_start)


def _run_copies(src, dst, sem, src_row, dst_row, n_rows, wait=False):
    nsub = D_MODEL // LANES

    def copy(s0, d0, rows):
        dma = pltpu.make_async_copy(src.at[pl.ds(pl.multiple_of(s0 * nsub, nsub), rows * nsub), :],
                                    dst.at[pl.ds(pl.multiple_of(d0 * nsub, nsub), rows * nsub), :], sem)
        dma.wait() if wait else dma.start()

    def chunk(k, carry):
        copy(src_row + k * SUBLANES, dst_row + k * SUBLANES, SUBLANES)
        return carry

    n_chunks = n_rows // SUBLANES
    lax.fori_loop(0, n_chunks, chunk, 0)
    done = n_chunks * SUBLANES
    for rows in (4, 2, 1):
        @pl.when((n_rows & rows) != 0)
        def _(rows=rows, done=done):
            copy(src_row + done, dst_row + done, rows)

        done = done + (n_rows & rows)


def _dispatch_kernel(start_ref, cnt_ref, loc_ref, end_ref, tot_ref, h_ref, mi_ref, xs_hbm, stage, zeros, sem, zsem,
                     *, tm):
    i = pl.program_id(0)
    n = pl.num_programs(0)
    nsub = D_MODEL // LANES
    tb = h_ref.shape[0]
    slot = i % 2

    def wait_stage(sl):
        pltpu.make_async_copy(stage.at[sl], stage.at[sl], sem.at[sl]).wait()

    def pad_rows(wait):
        def one(e, carry):
            tot = tot_ref[e]
            n_pad = (tot + tm - 1) // tm * tm - tot
            _run_copies(zeros, xs_hbm, zsem, 0, end_ref[e] - n_pad, n_pad, wait=wait)
            return carry

        lax.fori_loop(0, end_ref.shape[0], one, 0)

        def spare(j, carry):
            dma = pltpu.make_async_copy(zeros, xs_hbm.at[pl.ds(pl.multiple_of(j * tm * nsub, nsub), tm * nsub), :], zsem)
            dma.wait() if wait else dma.start()
            return carry

        lax.fori_loop(end_ref[end_ref.shape[0] - 1] // tm, xs_hbm.shape[0] // (tm * nsub), spare, 0)

    @pl.when(i == 0)
    def _():
        zeros[...] = jnp.zeros_like(zeros)
        pad_rows(wait=False)

    @pl.when(i >= 2)
    def _():
        wait_stage(slot)

    li = mi_ref[...]
    rows_iota = lax.broadcasted_iota(jnp.int32, (tb, 2 * tb), 1)
    onehot = ((rows_iota == li[:, 4:5]) | (rows_iota == li[:, 5:6])).astype(BF16)
    srt = _dot_tn(onehot, h_ref[...])
    for c in range(nsub):
        stage[slot, pl.ds(c, 2 * tb, stride=nsub), :] = srt[:, c * LANES:(c + 1) * LANES]

    def run(e, carry):
        k = i * N_EXPERTS + e
        _run_copies(stage.at[slot], xs_hbm, sem.at[slot], loc_ref[k], start_ref[k], cnt_ref[k])
        return carry

    lax.fori_loop(0, N_EXPERTS, run, 0)

    @pl.when(i == n - 1)
    def _():
        wait_stage(slot)

        @pl.when(n >= 2)
        def _():
            wait_stage(1 - slot)

        pad_rows(wait=True)


def _dispatch(blk_start, blk_cnt, blk_loc, seg_end, seg_tot, h2, meta_i, n_slots, tb):
    t = meta_i.shape[0]
    nsub = D_MODEL // LANES
    assert t % tb == 0
    grid_spec = pltpu.PrefetchScalarGridSpec(
        num_scalar_prefetch=5,
        grid=(t // tb,),
        in_specs=[pl.BlockSpec((tb, D_MODEL), lambda i, *_: (i, 0)),
                  pl.BlockSpec((tb, LANES), lambda i, *_: (i, 0))],
        out_specs=pl.BlockSpec(memory_space=pl.ANY),
        scratch_shapes=[pltpu.VMEM((2, 2 * tb * nsub, LANES), F32),
                        pltpu.VMEM((EXPERT_TILE * nsub, LANES), F32),
                        pltpu.SemaphoreType.DMA((2,)), pltpu.SemaphoreType.DMA(())],
    )
    return pl.pallas_call(
        functools.partial(_dispatch_kernel, tm=EXPERT_TILE),
        grid_spec=grid_spec,
        out_shape=jax.ShapeDtypeStruct((n_slots * nsub, LANES), F32),
        compiler_params=_params(("arbitrary",)),
        name="dispatch",
    )(blk_start, blk_cnt, blk_loc, seg_end, seg_tot, h2, meta_i)


def _expert_kernel(te_ref, tv_ref, nxt_ref, xs_ref, wg_hbm, wu_hbm, wd_hbm, out_ref,
                   wg_st, wu_st, wd_st, wgb, wub, wdb, n_switch, sem):
    i = pl.program_id(0)
    tm = EXPERT_TILE
    nsub = D_MODEL // LANES
    valid = tv_ref[i] != 0
    e = te_ref[i]

    def weight_copies(expert, sl):
        return [pltpu.make_async_copy(hbm.at[0, expert], st.at[sl], sem.at[sl, k])
                for k, (hbm, st) in enumerate(((wg_hbm, wg_st), (wu_hbm, wu_st), (wd_hbm, wd_st)))]

    @pl.when(i == 0)
    def _():
        n_switch[0] = 0
        for dma in weight_copies(e, 0):
            dma.start()

    @pl.when(jnp.logical_not(valid))
    def _():
        out_ref[...] = jnp.zeros_like(out_ref)

    @pl.when(valid & ((i == 0) | (e != te_ref[jnp.maximum(i - 1, 0)])))
    def _():
        sl = n_switch[0] % 2
        n_switch[0] = n_switch[0] + 1
        for dma in weight_copies(e, sl):
            dma.wait()
        wgb[...] = wg_st[sl].astype(BF16)
        wub[...] = wu_st[sl].astype(BF16)
        wdb[...] = wd_st[sl].astype(BF16)

        @pl.when(nxt_ref[e] != e)
        def _():
            for dma in weight_copies(nxt_ref[e], 1 - sl):
                dma.start()

    @pl.when(valid)
    def _():
        h = jnp.concatenate([xs_ref[pl.ds(c, tm, stride=nsub), :] for c in range(nsub)], axis=1).astype(BF16)
        gate = _dot(h, wgb[...])
        up = _dot(h, wub[...])
        a = (gate * _sigmoid(gate) * up).astype(BF16)
        o = _dot(a, wdb[...])
        for c in range(nsub):
            out_ref[pl.ds(c, tm, stride=nsub), :] = o[:, c * LANES:(c + 1) * LANES]


def _experts(tile_expert, tile_valid, next_expert, xs, wg, wu, wd):
    n_tiles = tile_expert.shape[0]
    tm = EXPERT_TILE
    nsub = D_MODEL // LANES
    used = lambda i, tv: jnp.where(tv[i] != 0, i, 0)
    anywhere = pl.BlockSpec(memory_space=pl.ANY)
    up_shape, down_shape = (D_MODEL, D_EXPERT), (D_EXPERT, D_MODEL)
    grid_spec = pltpu.PrefetchScalarGridSpec(
        num_scalar_prefetch=3,
        grid=(n_tiles,),
        in_specs=[pl.BlockSpec((tm * nsub, LANES), lambda i, te, tv, nx: (used(i, tv), 0)),
                  anywhere, anywhere, anywhere],
        out_specs=pl.BlockSpec((tm * nsub, LANES), lambda i, te, tv, nx: (i, 0)),
        scratch_shapes=[pltpu.VMEM((2,) + up_shape, F32), pltpu.VMEM((2,) + up_shape, F32),
                        pltpu.VMEM((2,) + down_shape, F32),
                        pltpu.VMEM(up_shape, BF16), pltpu.VMEM(up_shape, BF16), pltpu.VMEM(down_shape, BF16),
                        pltpu.SMEM((1,), jnp.int32), pltpu.SemaphoreType.DMA((2, 3))],
    )
    return pl.pallas_call(
        _expert_kernel,
        grid_spec=grid_spec,
        out_shape=jax.ShapeDtypeStruct((n_tiles * tm * nsub, LANES), F32),
        compiler_params=_params(("arbitrary",)),
        name="experts",
    )(tile_expert, tile_valid, next_expert, xs, wg, wu, wd)


def _combine_kernel(p0_ref, p1_ref, x1_ref, mw_ref, rows_hbm, y_ref, g0, g1, sem, *, tc, blk_off):
    i = pl.program_id(0)
    n = pl.num_programs(0)
    nout = D_MODEL // LANES
    slot = i % 2

    def issue_all(step, sl):
        base = (step + blk_off) * tc

        def issue(j, carry):
            for pos_ref, dst, s in ((p0_ref, g0, 0), (p1_ref, g1, 1)):
                src0 = pl.multiple_of(pos_ref[base + j] * nout, nout)
                pltpu.make_async_copy(rows_hbm.at[pl.ds(src0, nout), :],
                                      dst.at[sl, pl.ds(pl.multiple_of(j * nout, nout), nout), :],
                                      sem.at[sl, s]).start()
            return carry

        lax.fori_loop(0, tc, issue, 0, unroll=4)

    @pl.when(i == 0)
    def _():
        issue_all(i, slot)

    @pl.when(i + 1 < n)
    def _():
        issue_all(i + 1, 1 - slot)

    for dst, s in ((g0, 0), (g1, 1)):
        pltpu.make_async_copy(rows_hbm.at[pl.ds(0, tc * nout), :], dst.at[slot], sem.at[slot, s]).wait()
    w0 = mw_ref[:, 0:1]
    w1 = mw_ref[:, 1:2]
    for c in range(nout):
        sl = slice(c * LANES, (c + 1) * LANES)
        y_ref[:, sl] = (x1_ref[:, sl] + w0 * g0[slot, pl.ds(c, tc, stride=nout), :]
                        + w1 * g1[slot, pl.ds(c, tc, stride=nout), :])


def _combine(pos0, pos1, x1, meta_w, rows, tc, tok_off):
    t = x1.shape[0]
    assert t % tc == 0 and tok_off % tc == 0
    blk_off = tok_off // tc
    nout = D_MODEL // LANES
    row = pl.BlockSpec((tc, D_MODEL), lambda i, p0, p1: (i, 0))
    grid_spec = pltpu.PrefetchScalarGridSpec(
        num_scalar_prefetch=2,
        grid=(t // tc,),
        in_specs=[row, pl.BlockSpec((tc, LANES), lambda i, p0, p1: (i + blk_off, 0)),
                  pl.BlockSpec(memory_space=pl.ANY)],
        out_specs=row,
        scratch_shapes=[pltpu.VMEM((2, tc * nout, LANES), F32), pltpu.VMEM((2, tc * nout, LANES), F32),
                        pltpu.SemaphoreType.DMA((2, 2))],
    )
    return pl.pallas_call(
        functools.partial(_combine_kernel, tc=tc, blk_off=blk_off),
        grid_spec=grid_spec,
        out_shape=jax.ShapeDtypeStruct((t, D_MODEL), F32),
        compiler_params=_params(("arbitrary",)),
        name="combine",
    )(pos0, pos1, x1, meta_w, rows)


def _rope_tables(pos):
    half = SWA_HEAD_DIM // 2
    inv_freq = ROPE_THETA ** (-np.arange(half, dtype=np.float64) / half)
    ang = pos.astype(np.float64)[:, None] * inv_freq[None, :]
    cos, sin = np.cos(ang), np.sin(ang)
    reps = LANES // SWA_HEAD_DIM
    table = lambda a, b: jnp.asarray(np.tile(np.concatenate([a, b], axis=1), (1, reps)).astype(np.float32))
    return table(cos, cos), table(-sin, sin)


def kernel(x_prompt, x_sample, state_gla, cache_swa_k0, cache_swa_v0, cache_swa_k1, cache_swa_v1, cache_swa_k2, cache_swa_v2, ln1_w, w_in, w_gla_lr, b_gla_lr, gla_onorm_w, q_norm_w, k_norm_w, w_branch_a, w_branch_b, w_out, ln2_w, w_router_group, b_router_group, w_router_expert, b_router_expert, w_exp_gate, w_exp_up, w_exp_down):
    b, s, d = x_prompt.shape
    bd, ls, _ = x_sample.shape
    tp, ts = b * s, bd * ls
    assert w_in.shape[0] == 1 and d == D_MODEL and ts % SUBLANES == 0
    k_caches = (cache_swa_k0, cache_swa_k1, cache_swa_k2)
    v_caches = (cache_swa_v0, cache_swa_v1, cache_swa_v2)

    w = w_in[0]
    cuts = np.cumsum((512, 512, 1024, GLA_RANK, 1024, 768, 768, 768, 1024, 1024))
    sec = lambda a: w[:, (0 if a == 0 else cuts[a - 1]):cuts[a]]
    lr_pad = jnp.pad(sec(3), ((0, 0), (0, LANES - GLA_RANK)))
    w_f32 = jnp.concatenate([sec(0), sec(1), sec(2), sec(4), sec(5), sec(6), sec(7), sec(8), sec(9), lr_pad], axis=1)
    w_packed = w_f32.astype(BF16)
    wlr_f32 = jnp.pad(w_gla_lr[0], ((0, LANES - GLA_RANK), (0, 0)))
    wlr = wlr_f32.astype(BF16)
    blr = b_gla_lr[0][None, :]
    nw = jnp.concatenate([jnp.tile(q_norm_w[0], SWA_WIDTH // SWA_HEAD_DIM), jnp.tile(k_norm_w[0], SWA_WIDTH // SWA_HEAD_DIM)])[None, :]
    gi = np.arange(256) // SWA_HEAD_DIM
    gmat_f32 = jnp.asarray((gi[:, None] == gi[None, :]).astype(np.float32) / SWA_HEAD_DIM)
    gmat = gmat_f32.astype(BF16)
    ln1 = ln1_w[0][None, :]
    ln2 = ln2_w[0][None, :]
    onw = gla_onorm_w[0][None, :]
    wa_f32, wb_f32, wo_f32 = w_branch_a[0], w_branch_b[0], w_out[0]
    wa, wb, wo = wa_f32.astype(BF16), wb_f32.astype(BF16), wo_f32.astype(BF16)
    wr = jnp.pad(jnp.concatenate([w_router_group[0], w_router_expert[0]], axis=1),
                 ((0, 0), (0, LANES - N_GROUPS - N_EXPERTS)))
    wr_hi, wr_lo = _split_bf16(wr)
    br = jnp.pad(jnp.concatenate([b_router_group[0], b_router_expert[0]]), (0, LANES - N_GROUPS - N_EXPERTS))[None, :]

    cos_p, sin_p = _rope_tables(np.arange(s))
    cos_s, sin_s = _rope_tables(PAST_LEN + np.arange(ts) % ls)

    tm_p = 512
    proj_p = _proj(x_prompt.reshape(tp, d), cos_p, sin_p, s // tm_p, ln1, w_packed, wlr, blr, nw, gmat, tm_p)
    proj_s = _proj(x_sample.reshape(ts, d), cos_s, sin_s, 1, ln1, w_f32, wlr_f32, blr, nw, gmat_f32, ts)
    gqkv_p, la_p, sog_p, q_p, k_p, v_p, sgab_p = proj_p
    gqkv_s, la_s, sog_s, q_s, k_s, v_s, sgab_s = proj_s

    r3 = lambda t, nb: t.reshape(nb, t.shape[0] // nb, t.shape[1])
    ya_p, st_p = _gla(r3(gqkv_p, b), r3(la_p, b), r3(sog_p, b),
                      jnp.zeros((b, GLA_HEADS, GLA_DK, GLA_DV), F32), onw, GLA_CHUNK, 512)
    pad_s = lambda t: jnp.pad(r3(t, bd), ((0, 0), (0, _SAMPLE_ROWS - ls), (0, 0)))
    ya_s, st_s = _gla(pad_s(gqkv_s), pad_s(la_s), pad_s(sog_s), state_gla[0], onw, _SAMPLE_ROWS, _SAMPLE_ROWS)
    ya_s = ya_s[:, :ls].reshape(ts, d)

    q3, k3, v3 = r3(q_p, b), r3(k_p, b), r3(v_p, b)
    swa_p = [_swa_prompt(q3, k3, v3, g, dil) for g, (_, dil) in enumerate(SWA_GROUPS)]
    to_t = lambda c: jnp.transpose(c[0], (0, 2, 3, 1))
    caches_t = [(to_t(k_caches[g]), to_t(v_caches[g])) for g in range(len(SWA_GROUPS))]
    ob_s, *new_caches = _swa_sample(pad_s(q_s), pad_s(k_s), pad_s(v_s), caches_t, ls)
    ob_s = ob_s[:, :ls].reshape(ts, SWA_OUT)

    t = tp + ts
    mw = (wa, wb, wo, ln2, jnp.concatenate([wr_hi, wr_lo], axis=1), wr_hi, br)
    x1_p, h2c, lg, cnt_p = _merge(x_prompt.reshape(tp, d), ya_p.reshape(tp, d),
                                  [o for o, _ in swa_p] + [l for _, l in swa_p], sgab_p, mw, MERGE_TILE, 0, t)
    mw_f32 = (wa_f32, wb_f32, wo_f32) + mw[3:]
    x1_s, h2c, lg, cnt_s = _merge(x_sample.reshape(ts, d), ya_s, [ob_s], sgab_s, mw_f32, ts, tp, t,
                                  shared=(h2c, lg))

    tm = EXPERT_TILE
    n_tiles = (2 * t) // tm + N_EXPERTS
    counts = ((jnp.sum(cnt_p, axis=0) + jnp.sum(cnt_s, axis=0)) * (1.0 / SUBLANES)).astype(jnp.int32)
    ends_all = jnp.cumsum((counts + tm - 1) // tm * tm)
    seg_start = jnp.broadcast_to((ends_all - (counts + tm - 1) // tm * tm).astype(F32)[None, :], (SUBLANES, LANES))
    counts, ends = counts[:N_EXPERTS], ends_all[:N_EXPERTS]
    tb = DISPATCH_BLOCK
    meta_i, meta_w, blocks = _router(lg, seg_start, t, tb)
    blocks = blocks.reshape(t // tb, SUBLANES, LANES)[:, :, :N_EXPERTS]
    blk_start, blk_cnt, blk_loc = (blocks[:, r].reshape(-1) for r in range(3))
    pos0, pos1 = meta_i[:, 2], meta_i[:, 3]
    tile_start = jnp.arange(n_tiles, dtype=jnp.int32) * tm
    tile_valid = (tile_start < ends[-1]).astype(jnp.int32)
    last_slot = jnp.minimum(tile_start, ends[-1] - 1)
    tile_expert = jnp.minimum(jnp.sum((last_slot[:, None] >= ends[None, :]).astype(jnp.int32), axis=1), N_EXPERTS - 1)

    xs = _dispatch(blk_start, blk_cnt, blk_loc, ends, counts, h2c, meta_i, n_tiles * tm, tb)
    ids = jnp.arange(N_EXPERTS, dtype=jnp.int32)
    later_used = (counts[None, :] > 0) & (ids[None, :] > ids[:, None])
    next_expert = jnp.min(jnp.where(later_used, ids[None, :], N_EXPERTS), axis=1)
    next_expert = jnp.where(next_expert == N_EXPERTS, ids, next_expert)
    rows = _experts(tile_expert, tile_valid, next_expert, xs, w_exp_gate, w_exp_up, w_exp_down)
    y_p = _combine(pos0, pos1, x1_p, meta_w, rows, 256, 0)
    y_s = _combine(pos0, pos1, x1_s, meta_w, rows, ts, tp)

    heads = lambda a: a.reshape(1, a.shape[0], a.shape[1], SWA_HPG, SWA_HEAD_DIM)
    outs = [y_p.reshape(b, s, d), y_s.reshape(bd, ls, d), st_p[None].astype(x_prompt.dtype)]
    for g, (win, _) in enumerate(SWA_GROUPS):
        keep = min(win, s)
        gsl = slice(g * SWA_OUT, (g + 1) * SWA_OUT)
        outs += [heads(k3[:, s - keep:, gsl]), heads(v3[:, s - keep:, gsl])]
    outs.append(st_s[None].astype(state_gla.dtype))
    outs += [jnp.transpose(c, (0, 3, 1, 2))[None] for c in new_caches]
    return tuple(outs)
```

```python
import functools

import numpy as np
import jax
import jax.numpy as jnp
from jax import lax
from jax.experimental import pallas as pl
from jax.experimental.pallas import tpu as pltpu

F32 = jnp.float32
BF16 = jnp.bfloat16

D_MODEL = 1024
PAST_LEN = 16384
GLA_HEADS = 4
GLA_DK = 128
GLA_DV = 256
GLA_RANK = 16
GLA_TAU = 16.0
GLA_CHUNK = 64
SWA_GROUPS = ((128, 1), (512, 4), (2048, 16))
SWA_HPG = 4
SWA_HEAD_DIM = 64
SWA_WIDTH = 768
SWA_OUT = 256
SWA_BLOCK = 128
ROPE_THETA = 10000.0
N_GROUPS = 4
EXPERTS_PER_GROUP = 8
N_EXPERTS = 32
D_EXPERT = 512
EPS = 1e-6

LANES = 128
SUBLANES = 8
VMEM_LIMIT = 56 * 1024 * 1024
NEG = -1e30
EXPERT_TILE = 256
MERGE_TILE = 512
DISPATCH_BLOCK = 384

_C_GQKV = (0, 2048)
_C_GOG = (2048, 3072)
_C_QK = (3072, 4608)
_C_V = (4608, 5376)
_C_GAB = (5376, 7424)
_C_LR = (7424, 7552)


def _contract(a, b, dims):
    dg = lambda x, y: lax.dot_general(x, y, (dims, ((), ())), preferred_element_type=F32)
    if a.dtype == F32 and b.dtype == F32:
        a_hi, a_lo = _split(a, BF16)
        b_hi, b_lo = _split(b, BF16)
        return dg(a_hi, b_hi) + dg(a_hi, b_lo) + dg(a_lo, b_hi)
    return dg(a, b)


def _dot(a, b):
    return _contract(a, b, ((1,), (0,)))


def _dot_nt(a, b):
    return _contract(a, b, ((1,), (1,)))


def _dot_tn(a, b):
    return _contract(a, b, ((0,), (0,)))


def _sigmoid(x):
    return 1.0 / (1.0 + jnp.exp(-x))


def _split(x, dt):
    hi = x.astype(dt)
    lo = (x - hi.astype(F32)).astype(dt)
    return hi, lo


def _split_bf16(x):
    return _split(x, BF16)


def _params(sem):
    return pltpu.CompilerParams(dimension_semantics=sem, vmem_limit_bytes=VMEM_LIMIT)


def _resident(shape):
    nd = len(shape)
    return pl.BlockSpec(shape, lambda *_: (0,) * nd, pipeline_mode=pl.Buffered(1))


def _proj_kernel(x_ref, cos_ref, sin_ref, ln_ref, w_ref, wlr_ref, blr_ref, nw_ref, g_ref,
                 gqkv_ref, la_ref, sog_ref, q_ref, k_ref, v_ref, sgab_ref):
    cdt = w_ref.dtype
    x = x_ref[...]
    h = (x * lax.rsqrt(jnp.mean(x * x, axis=-1, keepdims=True) + EPS) * ln_ref[...]).astype(cdt)
    gqkv_ref[...] = _dot(h, w_ref[:, _C_GQKV[0]:_C_GQKV[1]]).astype(cdt)
    og = _dot(h, w_ref[:, _C_GOG[0]:_C_GOG[1]])
    sog_ref[...] = (og * _sigmoid(og)).astype(cdt)
    lr = _dot(h, w_ref[:, _C_LR[0]:_C_LR[1]]).astype(cdt)
    z = _dot(lr, wlr_ref[...]) + blr_ref[...]
    la_ref[...] = (jnp.minimum(z, 0.0) - jnp.log(1.0 + jnp.exp(-jnp.abs(z)))) / GLA_TAU
    qk = _dot(h, w_ref[:, _C_QK[0]:_C_QK[1]])
    sq = (qk * qk).astype(cdt)
    ms = jnp.concatenate([_dot(sq[:, c * 256:(c + 1) * 256], g_ref[...]) for c in range(6)], axis=1)
    qn = qk * lax.rsqrt(ms + EPS) * nw_ref[...]
    width = 2 * SWA_WIDTH
    cos = jnp.tile(cos_ref[...], (1, width // LANES))
    sin = jnp.tile(sin_ref[...], (1, width // LANES))
    lane = lax.broadcasted_iota(jnp.int32, qn.shape, 1)
    half = SWA_HEAD_DIM // 2
    rot = jnp.where(lane % SWA_HEAD_DIM < half, pltpu.roll(qn, width - half, 1), pltpu.roll(qn, half, 1))
    qr = qn * cos + rot * sin
    q_ref[...] = qr[:, :SWA_WIDTH]
    k_ref[...] = qr[:, SWA_WIDTH:]
    v_ref[...] = _dot(h, w_ref[:, _C_V[0]:_C_V[1]])
    gab = _dot(h, w_ref[:, _C_GAB[0]:_C_GAB[1]])
    sgab_ref[...] = _sigmoid(gab).astype(cdt)


def _proj(x, cos, sin, rope_blocks, ln, w, wlr, blr, nw, g, tm):
    t = x.shape[0]
    assert t % tm == 0 and w.dtype == wlr.dtype == g.dtype
    row = lambda width: pl.BlockSpec((tm, width), lambda i: (i, 0))
    outs = [(2048, w.dtype), (512, F32), (1024, w.dtype), (768, F32), (768, F32), (768, F32), (2048, w.dtype)]
    return pl.pallas_call(
        _proj_kernel,
        grid=(t // tm,),
        in_specs=[row(D_MODEL),
                  pl.BlockSpec((tm, LANES), lambda i: (i % rope_blocks, 0)),
                  pl.BlockSpec((tm, LANES), lambda i: (i % rope_blocks, 0)),
                  _resident(ln.shape), _resident(w.shape), _resident(wlr.shape), _resident(blr.shape),
                  _resident(nw.shape), _resident(g.shape)],
        out_specs=[row(wd) for wd, _ in outs],
        out_shape=[jax.ShapeDtypeStruct((t, wd), dt) for wd, dt in outs],
        compiler_params=_params(("arbitrary",)),
        name="proj",
    )(x, cos, sin, ln, w, wlr, blr, nw, g)


def _gla_kernel(gqkv_ref, la_ref, sog_ref, s0_ref, onw_ref, y_ref, sfin_ref, st_scr, *, chunk, n_chunks):
    j = pl.program_id(1)
    nb = gqkv_ref.shape[0]

    @pl.when(j == 0)
    def _():
        for bi in range(nb):
            for h in range(GLA_HEADS):
                st_scr[bi * GLA_HEADS + h] = s0_ref[bi, h].T

    finals = [_gla_block(bi, gqkv_ref, la_ref, sog_ref, onw_ref, y_ref, st_scr, chunk, n_chunks) for bi in range(nb)]

    @pl.when(j == pl.num_programs(1) - 1)
    def _():
        for bi in range(nb):
            for h in range(GLA_HEADS):
                sfin_ref[bi, h] = finals[bi][h].T


def _gla_block(bi, gqkv_ref, la_ref, sog_ref, onw_ref, y_ref, st_scr, chunk, n_chunks):
    r = lax.broadcasted_iota(jnp.int32, (chunk, chunk), 0)
    c = lax.broadcasted_iota(jnp.int32, (chunk, chunk), 1)
    causal = r >= c
    cdt = gqkv_ref.dtype
    tri = causal.astype(cdt)
    hk = GLA_HEADS * GLA_DK
    rows = [slice(ci * chunk, (ci + 1) * chunk) for ci in range(n_chunks)]

    la_hi, la_lo = _split(la_ref[bi], cdt)
    b_chunks = [_dot(tri, la_hi[rs]) + _dot(tri, la_lo[rs]) for rs in rows]
    last = [bc[chunk - 1:chunk, :] for bc in b_chunks]
    b = jnp.concatenate(b_chunks, axis=0)
    b_end = jnp.concatenate([jnp.broadcast_to(x, (chunk, hk)) for x in last], axis=0)
    q = gqkv_ref[bi, :, :hk].astype(F32) * GLA_DK ** -0.5
    k = gqkv_ref[bi, :, hk:2 * hk].astype(F32)
    qd = (q * jnp.exp(b)).astype(cdt)
    kd = (k * jnp.exp(-b)).astype(cdt)
    kdec = (k * jnp.exp(b_end - b)).astype(cdt)
    decay = [jnp.exp(x) for x in last]

    state = [st_scr[bi * GLA_HEADS + h] for h in range(GLA_HEADS)]
    for ci, rs in enumerate(rows):
        outs = []
        for h in range(GLA_HEADS):
            sl = slice(h * GLA_DK, (h + 1) * GLA_DK)
            v_h = gqkv_ref[bi, rs, 2 * hk + h * GLA_DV:2 * hk + (h + 1) * GLA_DV]
            att = jnp.where(causal, _dot_nt(qd[rs, sl], kd[rs, sl]), 0.0).astype(cdt)
            o = _dot(att, v_h) + _dot_nt(qd[rs, sl], state[h].astype(cdt))
            state[h] = state[h] * decay[ci][:, sl] + _dot_tn(v_h, kdec[rs, sl])
            ms = jnp.mean(o * o, axis=-1, keepdims=True)
            outs.append(o * lax.rsqrt(ms + EPS) * onw_ref[...])
        o_all = jnp.concatenate(outs, axis=1) * sog_ref[bi, rs, :].astype(F32)
        y_ref[bi, rs, :] = o_all.astype(cdt)
    for h in range(GLA_HEADS):
        st_scr[bi * GLA_HEADS + h] = state[h]
    return state


def _gla(gqkv, la, sog, s0, onw, chunk, block, nb):
    b, l, _ = gqkv.shape
    assert b % nb == 0 and l % block == 0 and block % chunk == 0
    tok = lambda width: pl.BlockSpec((nb, block, width), lambda bi, j: (bi, j, 0))
    st = pl.BlockSpec((nb, GLA_HEADS, GLA_DK, GLA_DV), lambda bi, j: (bi, 0, 0, 0))
    return pl.pallas_call(
        functools.partial(_gla_kernel, chunk=chunk, n_chunks=block // chunk),
        grid=(b // nb, l // block),
        in_specs=[tok(2048), tok(512), tok(1024), st, _resident(onw.shape)],
        out_specs=[tok(1024), st],
        out_shape=[jax.ShapeDtypeStruct((b, l, 1024), gqkv.dtype),
                   jax.ShapeDtypeStruct((b, GLA_HEADS, GLA_DK, GLA_DV), F32)],
        scratch_shapes=[pltpu.VMEM((nb * GLA_HEADS, GLA_DV, GLA_DK), F32)],
        compiler_params=_params(("arbitrary", "arbitrary")),
        name="gla",
    )(gqkv, la, sog, s0, onw)


def _band_heads(q, kw, vw, valid):
    n = q.shape[0]
    low = lax.broadcasted_iota(jnp.int32, (n, LANES), 1) < SWA_HEAD_DIM
    ones = jnp.ones((kw.shape[0], LANES), kw.dtype)
    outs, lses = [], []
    for pair in range(SWA_OUT // LANES):
        cols = slice(pair * LANES, (pair + 1) * LANES)
        v_aug = jnp.concatenate([vw[:, cols], ones], axis=1)
        res = []
        for mine in (low, jnp.logical_not(low)):
            qm = jnp.where(mine, q[:, cols], 0.0).astype(kw.dtype)
            s = _dot_nt(qm, kw[:, cols]) * SWA_HEAD_DIM ** -0.5
            s = jnp.where(valid, s, NEG)
            m = jnp.max(s, axis=-1, keepdims=True)
            r = _dot(jnp.exp(s - m).astype(vw.dtype), v_aug)
            den = r[:, LANES:]
            res.append((r[:, :LANES] / den, m + jnp.log(den)))
        outs.append(jnp.where(low, res[0][0], res[1][0]))
        lses.append(jnp.where(low, res[0][1], res[1][1]))
    return jnp.concatenate(outs, axis=1), jnp.concatenate(lses, axis=1)


_SWA_TOKENS = 2048


def _swa_kernel(q_ref, k_ref, v_ref, kp_ref, vp_ref, o_ref, lse_ref, *stage, dil):
    blk = SWA_BLOCK
    nsub = q_ref.shape[1] // (blk * dil)
    first = pl.program_id(1) == 0
    qi = lax.broadcasted_iota(jnp.int32, (blk, 2 * blk), 0)
    kj = lax.broadcasted_iota(jnp.int32, (blk, 2 * blk), 1)
    band = (kj >= qi) & (kj <= qi + blk)
    halves = SWA_OUT // LANES

    if dil > 1:
        ins = (q_ref, k_ref, v_ref, kp_ref, vp_ref)
        q_ref, k_ref, v_ref, kp_ref, vp_ref, o_st, lse_st = stage
        for src, dst in zip(ins, stage):
            for hf in range(halves):
                dst[hf] = src[0, :, hf * LANES:(hf + 1) * LANES]

    def rows(ref, start):
        if dil == 1:
            return ref[0, pl.ds(start, blk), :]
        return jnp.concatenate([ref[hf, pl.ds(start, blk, stride=dil), :] for hf in range(halves)], axis=1)

    def unit(u, carry):
        r = u // nsub
        j = u % nsub
        start = r + dil * blk * j
        inside = r + dil * blk * jnp.maximum(j - 1, 0)
        if dil == 1:
            r, start, inside = 0, pl.multiple_of(start, blk), pl.multiple_of(inside, blk)
        head = j == 0
        kprev = jnp.where(head, rows(kp_ref, r), rows(k_ref, inside))
        vprev = jnp.where(head, rows(vp_ref, r), rows(v_ref, inside))
        kw = jnp.concatenate([kprev, rows(k_ref, start)], axis=0).astype(BF16)
        vw = jnp.concatenate([vprev, rows(v_ref, start)], axis=0).astype(BF16)
        valid = band & (kj >= jnp.where(head & first, blk, 0))
        o, lse = _band_heads(rows(q_ref, start), kw, vw, valid)
        if dil == 1:
            o_ref[0, pl.ds(start, blk), :] = o
            lse_ref[0, pl.ds(start, blk), :] = lse
        else:
            for hf in range(halves):
                o_st[hf, pl.ds(start, blk, stride=dil), :] = o[:, hf * LANES:(hf + 1) * LANES]
                lse_st[hf, pl.ds(start, blk, stride=dil), :] = lse[:, hf * LANES:(hf + 1) * LANES]
        return carry

    lax.fori_loop(0, dil * nsub, unit, 0)
    if dil > 1:
        for hf in range(halves):
            o_ref[0, :, hf * LANES:(hf + 1) * LANES] = o_st[hf]
            lse_ref[0, :, hf * LANES:(hf + 1) * LANES] = lse_st[hf]


def _swa_prompt(q, k, v, g, dil):
    b, s, _ = q.shape
    tb = _SWA_TOKENS
    back = SWA_BLOCK * dil
    assert s % tb == 0 and tb % back == 0
    cur = pl.BlockSpec((1, tb, SWA_OUT), lambda bi, i: (bi, i, g))
    prev = pl.BlockSpec((1, back, SWA_OUT), lambda bi, i: (bi, jnp.maximum(i * (tb // back) - 1, 0), g))
    out = pl.BlockSpec((1, tb, SWA_OUT), lambda bi, i: (bi, i, 0))
    halves = SWA_OUT // LANES
    stage = [pltpu.VMEM((halves, n, LANES), F32) for n in (tb, tb, tb, back, back, tb, tb)] if dil > 1 else []
    o, lse = pl.pallas_call(
        functools.partial(_swa_kernel, dil=dil),
        grid=(b, s // tb),
        in_specs=[cur, cur, cur, prev, prev],
        out_specs=[out, out],
        out_shape=[jax.ShapeDtypeStruct((b, s, SWA_OUT), F32)] * 2,
        scratch_shapes=stage,
        compiler_params=_params(("arbitrary", "arbitrary")),
        name=f"swa_prompt_g{g}",
    )(q, k, v, k, v)
    return o.reshape(b * s, SWA_OUT), lse.reshape(b * s, SWA_OUT)


_SAMPLE_ROWS = 16


def _swa_sample_kernel(q_ref, kn_ref, vn_ref, k0_ref, v0_ref, k1_ref, v1_ref, k2_ref, v2_ref,
                       ob_ref, ok0_ref, ov0_ref, ok1_ref, ov1_ref, ok2_ref, ov2_ref, *, n_new):
    rows = _SAMPLE_ROWS
    in_refs = ((k0_ref, v0_ref), (k1_ref, v1_ref), (k2_ref, v2_ref))
    out_refs = ((ok0_ref, ov0_ref), (ok1_ref, ov1_ref), (ok2_ref, ov2_ref))
    scale = SWA_HEAD_DIM ** -0.5
    jn = lax.broadcasted_iota(jnp.int32, (rows, rows), 1)
    ln = lax.broadcasted_iota(jnp.int32, (rows, rows), 0)
    tail = lax.broadcasted_iota(jnp.int32, (SWA_HEAD_DIM, LANES), 1) >= LANES - n_new
    sel_row = lax.broadcasted_iota(jnp.int32, (rows, LANES), 0)
    sel_lane = lax.broadcasted_iota(jnp.int32, (rows, LANES), 1)
    selector = ((sel_lane == sel_row + (LANES - n_new)) & (sel_row < n_new)).astype(F32)
    knt = _dot_tn(kn_ref[0], selector)
    vnt = _dot_tn(vn_ref[0], selector)
    o_g, lse_g = [], []
    for g, (win, dil) in enumerate(SWA_GROUPS):
        jc = lax.broadcasted_iota(jnp.int32, (rows, win), 1)
        lc = lax.broadcasted_iota(jnp.int32, (rows, win), 0)
        valid_c = (jc >= lc) & (((jc - lc) & (dil - 1)) == 0)
        valid_n = (jn <= ln) & (((ln - jn) & (dil - 1)) == 0) & (jn < n_new)
        o_h, lse_h = [], []
        for h in range(SWA_HPG):
            col = g * SWA_OUT + h * SWA_HEAD_DIM
            hsl = slice(col, col + SWA_HEAD_DIM)
            qh = q_ref[0, :, hsl]
            knh = kn_ref[0, :, hsl]
            vnh = vn_ref[0, :, hsl]
            for (src, dst, new_t) in ((in_refs[g][0], out_refs[g][0], knt), (in_refs[g][1], out_refs[g][1], vnt)):
                old = src[0, h]
                moved = pltpu.roll(old, win - n_new, 1)
                if win > LANES:
                    dst[0, h, :, 0:win - LANES] = moved[:, 0:win - LANES]
                dst[0, h, :, win - LANES:win] = jnp.where(tail, new_t[hsl, :], moved[:, win - LANES:win])
            kt = in_refs[g][0][0, h]
            vt = in_refs[g][1][0, h]
            s_c = jnp.where(valid_c, _dot(qh, kt) * scale, NEG)
            s_n = jnp.where(valid_n, _dot_nt(qh, knh) * scale, NEG)
            m = jnp.maximum(jnp.max(s_c, axis=-1, keepdims=True), jnp.max(s_n, axis=-1, keepdims=True))
            p_c = jnp.exp(s_c - m)
            p_n = jnp.exp(s_n - m)
            den = jnp.sum(p_c, axis=-1, keepdims=True) + jnp.sum(p_n, axis=-1, keepdims=True)
            o_h.append((_dot_nt(p_c, vt) + _dot(p_n, vnh)) / den)
            lse_h.append(jnp.broadcast_to(m + jnp.log(den), (rows, SWA_HEAD_DIM)))
        o_g.append(jnp.concatenate(o_h, axis=1))
        lse_g.append(jnp.concatenate(lse_h, axis=1))
    lmax = jnp.maximum(jnp.maximum(lse_g[0], lse_g[1]), lse_g[2])
    e = [jnp.exp(x - lmax) for x in lse_g]
    ob_ref[0] = (e[0] * o_g[0] + e[1] * o_g[1] + e[2] * o_g[2]) / (e[0] + e[1] + e[2])


def _swa_sample(q, kn, vn, caches_t, n_new):
    bd = q.shape[0]
    rows = _SAMPLE_ROWS
    tok = pl.BlockSpec((1, rows, SWA_WIDTH), lambda bi: (bi, 0, 0))
    specs, args = [tok, tok, tok], [q, kn, vn]
    out_specs = [pl.BlockSpec((1, rows, SWA_OUT), lambda bi: (bi, 0, 0))]
    out_shape = [jax.ShapeDtypeStruct((bd, rows, SWA_OUT), F32)]
    for g, (win, dil) in enumerate(SWA_GROUPS):
        for t in caches_t[g]:
            assert t.shape == (bd, SWA_HPG, SWA_HEAD_DIM, win) and win == SWA_BLOCK * dil and win % LANES == 0
            spec = pl.BlockSpec((1, SWA_HPG, SWA_HEAD_DIM, win), lambda bi: (bi, 0, 0, 0))
            args.append(t)
            specs.append(spec)
            out_specs.append(spec)
            out_shape.append(jax.ShapeDtypeStruct(t.shape, t.dtype))
    return pl.pallas_call(
        functools.partial(_swa_sample_kernel, n_new=n_new),
        grid=(bd,),
        in_specs=specs,
        out_specs=out_specs,
        out_shape=out_shape,
        compiler_params=_params(("arbitrary",)),
        name="swa_sample",
    )(*args)


def _merge_kernel(*refs, combine, n_alias, n_real):
    x1_ref, h2c_ref, lg_ref, cnt_ref = refs[-4:]
    refs = refs[:len(refs) - 4 - n_alias]

    @pl.when(pl.program_id(0) >= n_real)
    def _():
        h2c_ref[...] = jnp.zeros_like(h2c_ref)
        lg_ref[...] = jnp.zeros_like(lg_ref)
        cnt_ref[...] = jnp.zeros_like(cnt_ref)

    pl.when(pl.program_id(0) < n_real)(
        functools.partial(_merge_tile, refs, x1_ref, h2c_ref, lg_ref, cnt_ref, combine))


def _merge_tile(refs, x1_ref, h2c_ref, lg_ref, cnt_ref, combine):
    if combine:
        (x_ref, ya_ref, o0, o1, o2, l0, l1, l2, sgab_ref, wa_ref, wb_ref, wo_ref, ln_ref, wr_cat_ref, wr_hi_ref,
         br_ref) = refs
        lmax = jnp.maximum(jnp.maximum(l0[...], l1[...]), l2[...])
        e0, e1, e2 = jnp.exp(l0[...] - lmax), jnp.exp(l1[...] - lmax), jnp.exp(l2[...] - lmax)
        ob = ((e0 * o0[...] + e1 * o1[...] + e2 * o2[...]) / (e0 + e1 + e2)).astype(wb_ref.dtype)
    else:
        (x_ref, ya_ref, ob_ref, sgab_ref, wa_ref, wb_ref, wo_ref, ln_ref, wr_cat_ref, wr_hi_ref, br_ref) = refs
        ob = ob_ref[...]
    ya = _dot(ya_ref[...], wa_ref[...])
    yb = _dot(ob, wb_ref[...])
    sga = sgab_ref[:, :D_MODEL].astype(F32)
    sgb = sgab_ref[:, D_MODEL:].astype(F32)
    x1 = x_ref[...] + _dot((sga * ya + sgb * yb).astype(wo_ref.dtype), wo_ref[...])
    x1_ref[...] = x1
    h2 = x1 * lax.rsqrt(jnp.mean(x1 * x1, axis=-1, keepdims=True) + EPS) * ln_ref[...]
    h_hi, h_lo = _split_bf16(h2)
    both = _dot(h_hi, wr_cat_ref[...])
    lg = both[:, :LANES] + both[:, LANES:] + _dot(h_lo, wr_hi_ref[...]) + br_ref[...]
    lg_ref[...] = lg
    h2c_ref[...] = h_hi
    _, _, _, _, _, hot1, hot2 = _route_select(lg)
    cnt = jnp.sum((hot1 | hot2).astype(F32), axis=0, keepdims=True)
    cnt_ref[...] = jnp.broadcast_to(cnt, cnt_ref.shape)


def _merge(x, ya_in, swa, sgab, weights, tm, tok_off, t_all, shared=None):
    t = x.shape[0]
    assert t % tm == 0 and tok_off % tm == 0 and MERGE_TILE % tm == 0
    combine = len(swa) > 1
    blk_off = tok_off // tm
    t_buf = -(-t_all // MERGE_TILE) * MERGE_TILE
    n_real = t // tm
    n_fill = 0 if shared is not None else (t_buf - tok_off - t) // tm
    row = lambda width: pl.BlockSpec((tm, width), lambda i: (jnp.minimum(i, n_real - 1), 0))
    shared_in = [] if shared is None else list(shared)
    n_in = 3 + len(swa) + len(weights)
    return pl.pallas_call(
        functools.partial(_merge_kernel, combine=combine, n_alias=len(shared_in), n_real=n_real),
        grid=(n_real + n_fill,),
        in_specs=[row(D_MODEL), row(D_MODEL)] + [row(SWA_OUT)] * len(swa) + [row(2 * D_MODEL)]
                 + [_resident(w.shape) for w in weights] + [pl.BlockSpec(memory_space=pl.ANY)] * len(shared_in),
        out_specs=[row(D_MODEL),
                   pl.BlockSpec((tm, D_MODEL), lambda i: (i + blk_off, 0)),
                   pl.BlockSpec((tm, LANES), lambda i: (i + blk_off, 0)),
                   pl.BlockSpec((SUBLANES, LANES), lambda i: (i, 0))],
        out_shape=[jax.ShapeDtypeStruct((t, D_MODEL), F32),
                   jax.ShapeDtypeStruct((t_buf, D_MODEL), BF16),
                   jax.ShapeDtypeStruct((t_buf, LANES), F32),
                   jax.ShapeDtypeStruct(((n_real + n_fill) * SUBLANES, LANES), F32)],
        input_output_aliases={n_in + k: 1 + k for k in range(len(shared_in))},
        compiler_params=_params(("arbitrary",)),
        name="merge",
    )(x, ya_in, *swa, sgab, *weights, *shared_in)


def _route_select(lg):
    lane = lax.broadcasted_iota(jnp.int32, lg.shape, 1)
    big = jnp.int32(LANES)
    gl = jnp.where(lane < N_GROUPS, lg, NEG)
    gmax = jnp.max(gl, axis=-1, keepdims=True)
    g_idx = jnp.min(jnp.where(gl == gmax, lane, big), axis=-1, keepdims=True)
    g_w = 1.0 / jnp.sum(jnp.exp(gl - gmax), axis=-1, keepdims=True)
    e_lane = lane - N_GROUPS
    in_group = (e_lane >= 0) & (e_lane < N_EXPERTS) & (e_lane // EXPERTS_PER_GROUP == g_idx)
    el = jnp.where(in_group, lg, NEG)
    v1 = jnp.max(el, axis=-1, keepdims=True)
    i1 = jnp.min(jnp.where(el == v1, lane, big), axis=-1, keepdims=True)
    el2 = jnp.where(lane == i1, NEG, el)
    v2 = jnp.max(el2, axis=-1, keepdims=True)
    i2 = jnp.min(jnp.where(el2 == v2, lane, big), axis=-1, keepdims=True)
    r21 = jnp.exp(v2 - v1)
    w1 = g_w / (1.0 + r21)
    w2 = g_w * r21 / (1.0 + r21)
    e1 = i1 - N_GROUPS
    e2 = i2 - N_GROUPS
    return lane, e1, e2, w1, w2, lane == e1, lane == e2


def _router_kernel(lg_ref, offs_ref, mi_ref, mw_ref, blk_ref, carry):
    @pl.when(pl.program_id(0) == 0)
    def _():
        carry[...] = offs_ref[0:1, :]

    lane, e1, e2, w1, w2, hot1, hot2 = _route_select(lg_ref[...])
    tr = lane.shape[0]
    hot = (hot1 | hot2).astype(BF16)
    start = carry[...]
    cnt = jnp.sum(hot.astype(F32), axis=0, keepdims=True)
    carry[...] = start + cnt
    r = lax.broadcasted_iota(jnp.int32, (tr, tr), 0)
    c = lax.broadcasted_iota(jnp.int32, (tr, tr), 1)
    within = _dot((r > c).astype(BF16), hot)
    lane1 = lax.broadcasted_iota(jnp.int32, (SUBLANES, LANES), 1)
    incl = jnp.broadcast_to(cnt, (SUBLANES, LANES))
    for sh in (1, 2, 4, 8, 16):
        incl = incl + jnp.where(lane1 >= sh, pltpu.roll(incl, sh, 1), 0.0)
    local = incl[0:1, :] - cnt
    pick = lambda hot_k, row: jnp.sum(jnp.where(hot_k, row, 0.0), axis=-1, keepdims=True)
    cols = [e1, e2]
    for hot_k in (hot1, hot2):
        cols.append((pick(hot_k, within) + pick(hot_k, start)).astype(jnp.int32))
    for hot_k in (hot1, hot2):
        cols.append((pick(hot_k, within) + pick(hot_k, local)).astype(jnp.int32))
    mi = cols[-1]
    for k in range(len(cols) - 2, -1, -1):
        mi = jnp.where(lane == k, cols[k], mi)
    mi_ref[...] = mi
    mw_ref[...] = jnp.where(lane == 0, w1, w2)
    sub = lax.broadcasted_iota(jnp.int32, (SUBLANES, LANES), 0)
    tbl = jnp.where(sub == 0, start, jnp.where(sub == 1, cnt, jnp.where(sub == 2, local, 0.0)))
    blk_ref[...] = tbl.astype(jnp.int32)


def _router(logits, seg_start, t, tr):
    assert t % tr == 0 and tr % SUBLANES == 0
    row = pl.BlockSpec((tr, LANES), lambda i: (i, 0))
    small = pl.BlockSpec((SUBLANES, LANES), lambda i: (i, 0))
    return pl.pallas_call(
        _router_kernel,
        grid=(t // tr,),
        in_specs=[row, pl.BlockSpec((SUBLANES, LANES), lambda i: (0, 0))],
        out_specs=[row, row, small],
        out_shape=[jax.ShapeDtypeStruct((t, LANES), jnp.int32), jax.ShapeDtypeStruct((t, LANES), F32),
                   jax.ShapeDtypeStruct((t // tr * SUBLANES, LANES), jnp.int32)],
        scratch_shapes=[pltpu.VMEM((1, LANES), F32)],
        compiler_params=_params(("arbitrary",)),
        name="router",
    )(logits, seg_start)


def _run_copies(src, dst, sem, src_row, dst_row, n_rows, wait=False):
    nsub = D_MODEL // LANES

    def copy(s0, d0, rows):
        dma = pltpu.make_async_copy(src.at[pl.ds(pl.multiple_of(s0 * nsub, nsub), rows * nsub), :],
                                    dst.at[pl.ds(pl.multiple_of(d0 * nsub, nsub), rows * nsub), :], sem)
        dma.wait() if wait else dma.start()

    def chunk(k, carry):
        copy(src_row + k * SUBLANES, dst_row + k * SUBLANES, SUBLANES)
        return carry

    n_chunks = n_rows // SUBLANES
    lax.fori_loop(0, n_chunks, chunk, 0)
    done = n_chunks * SUBLANES
    for rows in (4, 2, 1):
        @pl.when((n_rows & rows) != 0)
        def _(rows=rows, done=done):
            copy(src_row + done, dst_row + done, rows)

        done = done + (n_rows & rows)


def _dispatch_kernel(start_ref, cnt_ref, loc_ref, end_ref, tot_ref, h_ref, mi_ref, xs_hbm, stage, zeros, sem, zsem,
                     *, tm):
    i = pl.program_id(0)
    n = pl.num_programs(0)
    nsub = D_MODEL // LANES
    tb = h_ref.shape[0]
    slot = i % 2

    def wait_stage(sl):
        pltpu.make_async_copy(stage.at[sl], stage.at[sl], sem.at[sl]).wait()

    def pad_rows(wait):
        def one(e, carry):
            tot = tot_ref[e]
            n_pad = (tot + tm - 1) // tm * tm - tot
            _run_copies(zeros, xs_hbm, zsem, 0, end_ref[e] - n_pad, n_pad, wait=wait)
            return carry

        lax.fori_loop(0, end_ref.shape[0], one, 0)

        def spare(j, carry):
            dma = pltpu.make_async_copy(zeros, xs_hbm.at[pl.ds(pl.multiple_of(j * tm * nsub, nsub), tm * nsub), :], zsem)
            dma.wait() if wait else dma.start()
            return carry

        lax.fori_loop(end_ref[end_ref.shape[0] - 1] // tm, xs_hbm.shape[0] // (tm * nsub), spare, 0)

    @pl.when(i == 0)
    def _():
        zeros[...] = jnp.zeros_like(zeros)
        pad_rows(wait=False)

    @pl.when(i >= 2)
    def _():
        wait_stage(slot)

    li = mi_ref[...]
    rows_iota = lax.broadcasted_iota(jnp.int32, (tb, 2 * tb), 1)
    onehot = ((rows_iota == li[:, 4:5]) | (rows_iota == li[:, 5:6])).astype(BF16)
    srt = _dot_tn(onehot, h_ref[...])
    for c in range(nsub):
        stage[slot, pl.ds(c, 2 * tb, stride=nsub), :] = srt[:, c * LANES:(c + 1) * LANES]

    def run(e, carry):
        k = i * N_EXPERTS + e
        _run_copies(stage.at[slot], xs_hbm, sem.at[slot], loc_ref[k], start_ref[k], cnt_ref[k])
        return carry

    lax.fori_loop(0, N_EXPERTS, run, 0)

    @pl.when(i == n - 1)
    def _():
        wait_stage(slot)

        @pl.when(n >= 2)
        def _():
            wait_stage(1 - slot)

        pad_rows(wait=True)


def _dispatch(blk_start, blk_cnt, blk_loc, seg_end, seg_tot, h2, meta_i, n_slots, tb):
    t = meta_i.shape[0]
    nsub = D_MODEL // LANES
    assert t % tb == 0
    grid_spec = pltpu.PrefetchScalarGridSpec(
        num_scalar_prefetch=5,
        grid=(t // tb,),
        in_specs=[pl.BlockSpec((tb, D_MODEL), lambda i, *_: (i, 0)),
                  pl.BlockSpec((tb, LANES), lambda i, *_: (i, 0))],
        out_specs=pl.BlockSpec(memory_space=pl.ANY),
        scratch_shapes=[pltpu.VMEM((2, 2 * tb * nsub, LANES), F32),
                        pltpu.VMEM((EXPERT_TILE * nsub, LANES), F32),
                        pltpu.SemaphoreType.DMA((2,)), pltpu.SemaphoreType.DMA(())],
    )
    return pl.pallas_call(
        functools.partial(_dispatch_kernel, tm=EXPERT_TILE),
        grid_spec=grid_spec,
        out_shape=jax.ShapeDtypeStruct((n_slots * nsub, LANES), F32),
        compiler_params=_params(("arbitrary",)),
        name="dispatch",
    )(blk_start, blk_cnt, blk_loc, seg_end, seg_tot, h2, meta_i)


def _expert_kernel(te_ref, tv_ref, nxt_ref, xs_ref, wg_hbm, wu_hbm, wd_hbm, out_ref,
                   wg_st, wu_st, wd_st, wgb, wub, wdb, n_switch, sem):
    i = pl.program_id(0)
    tm = EXPERT_TILE
    nsub = D_MODEL // LANES
    valid = tv_ref[i] != 0
    e = te_ref[i]

    def weight_copies(expert, sl):
        return [pltpu.make_async_copy(hbm.at[0, expert], st.at[sl], sem.at[sl, k])
                for k, (hbm, st) in enumerate(((wg_hbm, wg_st), (wu_hbm, wu_st), (wd_hbm, wd_st)))]

    @pl.when(i == 0)
    def _():
        n_switch[0] = 0
        for dma in weight_copies(e, 0):
            dma.start()

    @pl.when(jnp.logical_not(valid))
    def _():
        out_ref[...] = jnp.zeros_like(out_ref)

    @pl.when(valid & ((i == 0) | (e != te_ref[jnp.maximum(i - 1, 0)])))
    def _():
        sl = n_switch[0] % 2
        n_switch[0] = n_switch[0] + 1
        for dma in weight_copies(e, sl):
            dma.wait()
        wgb[...] = wg_st[sl].astype(BF16)
        wub[...] = wu_st[sl].astype(BF16)
        wdb[...] = wd_st[sl].astype(BF16)

        @pl.when(nxt_ref[e] != e)
        def _():
            for dma in weight_copies(nxt_ref[e], 1 - sl):
                dma.start()

    @pl.when(valid)
    def _():
        h = jnp.concatenate([xs_ref[pl.ds(c, tm, stride=nsub), :] for c in range(nsub)], axis=1).astype(BF16)
        gate = _dot(h, wgb[...])
        up = _dot(h, wub[...])
        a = (gate * _sigmoid(gate) * up).astype(BF16)
        o = _dot(a, wdb[...])
        for c in range(nsub):
            out_ref[pl.ds(c, tm, stride=nsub), :] = o[:, c * LANES:(c + 1) * LANES]


def _experts(tile_expert, tile_valid, next_expert, xs, wg, wu, wd):
    n_tiles = tile_expert.shape[0]
    tm = EXPERT_TILE
    nsub = D_MODEL // LANES
    used = lambda i, tv: jnp.where(tv[i] != 0, i, 0)
    anywhere = pl.BlockSpec(memory_space=pl.ANY)
    up_shape, down_shape = (D_MODEL, D_EXPERT), (D_EXPERT, D_MODEL)
    grid_spec = pltpu.PrefetchScalarGridSpec(
        num_scalar_prefetch=3,
        grid=(n_tiles,),
        in_specs=[pl.BlockSpec((tm * nsub, LANES), lambda i, te, tv, nx: (used(i, tv), 0)),
                  anywhere, anywhere, anywhere],
        out_specs=pl.BlockSpec((tm * nsub, LANES), lambda i, te, tv, nx: (i, 0)),
        scratch_shapes=[pltpu.VMEM((2,) + up_shape, F32), pltpu.VMEM((2,) + up_shape, F32),
                        pltpu.VMEM((2,) + down_shape, F32),
                        pltpu.VMEM(up_shape, BF16), pltpu.VMEM(up_shape, BF16), pltpu.VMEM(down_shape, BF16),
                        pltpu.SMEM((1,), jnp.int32), pltpu.SemaphoreType.DMA((2, 3))],
    )
    return pl.pallas_call(
        _expert_kernel,
        grid_spec=grid_spec,
        out_shape=jax.ShapeDtypeStruct((n_tiles * tm * nsub, LANES), F32),
        compiler_params=_params(("arbitrary",)),
        name="experts",
    )(tile_expert, tile_valid, next_expert, xs, wg, wu, wd)


def _combine_kernel(p0_ref, p1_ref, x1_ref, mw_ref, rows_hbm, y_ref, g0, g1, sem, *, tc, blk_off):
    i = pl.program_id(0)
    n = pl.num_programs(0)
    nout = D_MODEL // LANES
    slot = i % 2

    def issue_all(step, sl):
        base = (step + blk_off) * tc

        def issue(j, carry):
            for pos_ref, dst, s in ((p0_ref, g0, 0), (p1_ref, g1, 1)):
                src0 = pl.multiple_of(pos_ref[base + j] * nout, nout)
                pltpu.make_async_copy(rows_hbm.at[pl.ds(src0, nout), :],
                                      dst.at[sl, pl.ds(pl.multiple_of(j * nout, nout), nout), :],
                                      sem.at[sl, s]).start()
            return carry

        lax.fori_loop(0, tc, issue, 0, unroll=4)

    @pl.when(i == 0)
    def _():
        issue_all(i, slot)

    @pl.when(i + 1 < n)
    def _():
        issue_all(i + 1, 1 - slot)

    for dst, s in ((g0, 0), (g1, 1)):
        pltpu.make_async_copy(rows_hbm.at[pl.ds(0, tc * nout), :], dst.at[slot], sem.at[slot, s]).wait()
    w0 = mw_ref[:, 0:1]
    w1 = mw_ref[:, 1:2]
    for c in range(nout):
        sl = slice(c * LANES, (c + 1) * LANES)
        y_ref[:, sl] = (x1_ref[:, sl] + w0 * g0[slot, pl.ds(c, tc, stride=nout), :]
                        + w1 * g1[slot, pl.ds(c, tc, stride=nout), :])


def _combine(pos0, pos1, x1, meta_w, rows, tc, tok_off):
    t = x1.shape[0]
    assert t % tc == 0 and tok_off % tc == 0
    blk_off = tok_off // tc
    nout = D_MODEL // LANES
    row = pl.BlockSpec((tc, D_MODEL), lambda i, p0, p1: (i, 0))
    grid_spec = pltpu.PrefetchScalarGridSpec(
        num_scalar_prefetch=2,
        grid=(t // tc,),
        in_specs=[row, pl.BlockSpec((tc, LANES), lambda i, p0, p1: (i + blk_off, 0)),
                  pl.BlockSpec(memory_space=pl.ANY)],
        out_specs=row,
        scratch_shapes=[pltpu.VMEM((2, tc * nout, LANES), F32), pltpu.VMEM((2, tc * nout, LANES), F32),
                        pltpu.SemaphoreType.DMA((2, 2))],
    )
    return pl.pallas_call(
        functools.partial(_combine_kernel, tc=tc, blk_off=blk_off),
        grid_spec=grid_spec,
        out_shape=jax.ShapeDtypeStruct((t, D_MODEL), F32),
        compiler_params=_params(("arbitrary",)),
        name="combine",
    )(pos0, pos1, x1, meta_w, rows)


def _rope_tables(pos):
    half = SWA_HEAD_DIM // 2
    inv_freq = ROPE_THETA ** (-np.arange(half, dtype=np.float64) / half)
    ang = pos.astype(np.float64)[:, None] * inv_freq[None, :]
    cos, sin = np.cos(ang), np.sin(ang)
    reps = LANES // SWA_HEAD_DIM
    table = lambda a, b: jnp.asarray(np.tile(np.concatenate([a, b], axis=1), (1, reps)).astype(np.float32))
    return table(cos, cos), table(-sin, sin)


def kernel(x_prompt, x_sample, state_gla, cache_swa_k0, cache_swa_v0, cache_swa_k1, cache_swa_v1, cache_swa_k2, cache_swa_v2, ln1_w, w_in, w_gla_lr, b_gla_lr, gla_onorm_w, q_norm_w, k_norm_w, w_branch_a, w_branch_b, w_out, ln2_w, w_router_group, b_router_group, w_router_expert, b_router_expert, w_exp_gate, w_exp_up, w_exp_down):
    b, s, d = x_prompt.shape
    bd, ls, _ = x_sample.shape
    tp, ts = b * s, bd * ls
    assert w_in.shape[0] == 1 and d == D_MODEL and ts % SUBLANES == 0
    k_caches = (cache_swa_k0, cache_swa_k1, cache_swa_k2)
    v_caches = (cache_swa_v0, cache_swa_v1, cache_swa_v2)

    w = w_in[0]
    cuts = np.cumsum((512, 512, 1024, GLA_RANK, 1024, 768, 768, 768, 1024, 1024))
    sec = lambda a: w[:, (0 if a == 0 else cuts[a - 1]):cuts[a]]
    lr_pad = jnp.pad(sec(3), ((0, 0), (0, LANES - GLA_RANK)))
    w_f32 = jnp.concatenate([sec(0), sec(1), sec(2), sec(4), sec(5), sec(6), sec(7), sec(8), sec(9), lr_pad], axis=1)
    w_packed = w_f32.astype(BF16)
    wlr_f32 = jnp.pad(w_gla_lr[0], ((0, LANES - GLA_RANK), (0, 0)))
    wlr = wlr_f32.astype(BF16)
    blr = b_gla_lr[0][None, :]
    nw = jnp.concatenate([jnp.tile(q_norm_w[0], SWA_WIDTH // SWA_HEAD_DIM), jnp.tile(k_norm_w[0], SWA_WIDTH // SWA_HEAD_DIM)])[None, :]
    gi = np.arange(256) // SWA_HEAD_DIM
    gmat_f32 = jnp.asarray((gi[:, None] == gi[None, :]).astype(np.float32) / SWA_HEAD_DIM)
    gmat = gmat_f32.astype(BF16)
    ln1 = ln1_w[0][None, :]
    ln2 = ln2_w[0][None, :]
    onw = gla_onorm_w[0][None, :]
    wa_f32, wb_f32, wo_f32 = w_branch_a[0], w_branch_b[0], w_out[0]
    wa, wb, wo = wa_f32.astype(BF16), wb_f32.astype(BF16), wo_f32.astype(BF16)
    wr = jnp.pad(jnp.concatenate([w_router_group[0], w_router_expert[0]], axis=1),
                 ((0, 0), (0, LANES - N_GROUPS - N_EXPERTS)))
    wr_hi, wr_lo = _split_bf16(wr)
    br = jnp.pad(jnp.concatenate([b_router_group[0], b_router_expert[0]]), (0, LANES - N_GROUPS - N_EXPERTS))[None, :]

    cos_p, sin_p = _rope_tables(np.arange(s))
    cos_s, sin_s = _rope_tables(PAST_LEN + np.arange(ts) % ls)

    tm_p = 512
    proj_p = _proj(x_prompt.reshape(tp, d), cos_p, sin_p, s // tm_p, ln1, w_packed, wlr, blr, nw, gmat, tm_p)
    proj_s = _proj(x_sample.reshape(ts, d), cos_s, sin_s, 1, ln1, w_f32, wlr_f32, blr, nw, gmat_f32, ts)
    gqkv_p, la_p, sog_p, q_p, k_p, v_p, sgab_p = proj_p
    gqkv_s, la_s, sog_s, q_s, k_s, v_s, sgab_s = proj_s

    r3 = lambda t, nb: t.reshape(nb, t.shape[0] // nb, t.shape[1])
    ya_p, st_p = _gla(r3(gqkv_p, b), r3(la_p, b), r3(sog_p, b),
                      jnp.zeros((b, GLA_HEADS, GLA_DK, GLA_DV), F32), onw, GLA_CHUNK, 512, b)
    pad_s = lambda t: jnp.pad(r3(t, bd), ((0, 0), (0, _SAMPLE_ROWS - ls), (0, 0)))
    ya_s, st_s = _gla(pad_s(gqkv_s), pad_s(la_s), pad_s(sog_s), state_gla[0], onw, _SAMPLE_ROWS, _SAMPLE_ROWS, 4)
    ya_s = ya_s[:, :ls].reshape(ts, d)

    q3, k3, v3 = r3(q_p, b), r3(k_p, b), r3(v_p, b)
    swa_p = [_swa_prompt(q3, k3, v3, g, dil) for g, (_, dil) in enumerate(SWA_GROUPS)]
    to_t = lambda c: jnp.transpose(c[0], (0, 2, 3, 1))
    caches_t = [(to_t(k_caches[g]), to_t(v_caches[g])) for g in range(len(SWA_GROUPS))]
    ob_s, *new_caches = _swa_sample(pad_s(q_s), pad_s(k_s), pad_s(v_s), caches_t, ls)
    ob_s = ob_s[:, :ls].reshape(ts, SWA_OUT)

    t = tp + ts
    mw = (wa, wb, wo, ln2, jnp.concatenate([wr_hi, wr_lo], axis=1), wr_hi, br)
    x1_p, h2c, lg, cnt_p = _merge(x_prompt.reshape(tp, d), ya_p.reshape(tp, d),
                                  [o for o, _ in swa_p] + [l for _, l in swa_p], sgab_p, mw, MERGE_TILE, 0, t)
    mw_f32 = (wa_f32, wb_f32, wo_f32) + mw[3:]
    x1_s, h2c, lg, cnt_s = _merge(x_sample.reshape(ts, d), ya_s, [ob_s], sgab_s, mw_f32, ts, tp, t,
                                  shared=(h2c, lg))

    tm = EXPERT_TILE
    n_tiles = (2 * t) // tm + N_EXPERTS
    counts = ((jnp.sum(cnt_p, axis=0) + jnp.sum(cnt_s, axis=0)) * (1.0 / SUBLANES)).astype(jnp.int32)
    ends_all = jnp.cumsum((counts + tm - 1) // tm * tm)
    seg_start = jnp.broadcast_to((ends_all - (counts + tm - 1) // tm * tm).astype(F32)[None, :], (SUBLANES, LANES))
    counts, ends = counts[:N_EXPERTS], ends_all[:N_EXPERTS]
    tb = DISPATCH_BLOCK
    meta_i, meta_w, blocks = _router(lg, seg_start, t, tb)
    blocks = blocks.reshape(t // tb, SUBLANES, LANES)[:, :, :N_EXPERTS]
    blk_start, blk_cnt, blk_loc = (blocks[:, r].reshape(-1) for r in range(3))
    pos0, pos1 = meta_i[:, 2], meta_i[:, 3]
    tile_start = jnp.arange(n_tiles, dtype=jnp.int32) * tm
    tile_valid = (tile_start < ends[-1]).astype(jnp.int32)
    last_slot = jnp.minimum(tile_start, ends[-1] - 1)
    tile_expert = jnp.minimum(jnp.sum((last_slot[:, None] >= ends[None, :]).astype(jnp.int32), axis=1), N_EXPERTS - 1)

    xs = _dispatch(blk_start, blk_cnt, blk_loc, ends, counts, h2c, meta_i, n_tiles * tm, tb)
    ids = jnp.arange(N_EXPERTS, dtype=jnp.int32)
    later_used = (counts[None, :] > 0) & (ids[None, :] > ids[:, None])
    next_expert = jnp.min(jnp.where(later_used, ids[None, :], N_EXPERTS), axis=1)
    next_expert = jnp.where(next_expert == N_EXPERTS, ids, next_expert)
    rows = _experts(tile_expert, tile_valid, next_expert, xs, w_exp_gate, w_exp_up, w_exp_down)
    y_p = _combine(pos0, pos1, x1_p, meta_w, rows, 256, 0)
    y_s = _combine(pos0, pos1, x1_s, meta_w, rows, ts, tp)

    heads = lambda a: a.reshape(1, a.shape[0], a.shape[1], SWA_HPG, SWA_HEAD_DIM)
    outs = [y_p.reshape(b, s, d), y_s.reshape(bd, ls, d), st_p[None].astype(x_prompt.dtype)]
    for g, (win, _) in enumerate(SWA_GROUPS):
        keep = min(win, s)
        gsl = slice(g * SWA_OUT, (g + 1) * SWA_OUT)
        outs += [heads(k3[:, s - keep:, gsl]), heads(v3[:, s - keep:, gsl])]
    outs.append(st_s[None].astype(state_gla.dtype))
    outs += [jnp.transpose(c, (0, 3, 1, 2))[None] for c in new_caches]
    return tuple(outs)
```

```python
import functools

import numpy as np
import jax
import jax.numpy as jnp
from jax import lax
from jax.experimental import pallas as pl
from jax.experimental.pallas import tpu as pltpu

F32 = jnp.float32
BF16 = jnp.bfloat16

D_MODEL = 1024
PAST_LEN = 16384
GLA_HEADS = 4
GLA_DK = 128
GLA_DV = 256
GLA_RANK = 16
GLA_TAU = 16.0
GLA_CHUNK = 64
SWA_GROUPS = ((128, 1), (512, 4), (2048, 16))
SWA_HPG = 4
SWA_HEAD_DIM = 64
SWA_WIDTH = 768
SWA_OUT = 256
SWA_BLOCK = 128
ROPE_THETA = 10000.0
N_GROUPS = 4
EXPERTS_PER_GROUP = 8
N_EXPERTS = 32
D_EXPERT = 512
EPS = 1e-6

LANES = 128
SUBLANES = 8
VMEM_LIMIT = 56 * 1024 * 1024
NEG = -1e30
EXPERT_TILE = 256
MERGE_TILE = 512
DISPATCH_BLOCK = 384

_C_GQKV = (0, 2048)
_C_GOG = (2048, 3072)
_C_QK = (3072, 4608)
_C_V = (4608, 5376)
_C_GAB = (5376, 7424)
_C_LR = (7424, 7552)


def _contract(a, b, dims):
    dg = lambda x, y: lax.dot_general(x, y, (dims, ((), ())), preferred_element_type=F32)
    if a.dtype == F32 and b.dtype == F32:
        a_hi, a_lo = _split(a, BF16)
        b_hi, b_lo = _split(b, BF16)
        return dg(a_hi, b_hi) + dg(a_hi, b_lo) + dg(a_lo, b_hi)
    return dg(a, b)


def _dot(a, b):
    return _contract(a, b, ((1,), (0,)))


def _dot_nt(a, b):
    return _contract(a, b, ((1,), (1,)))


def _dot_tn(a, b):
    return _contract(a, b, ((0,), (0,)))


def _sigmoid(x):
    return 1.0 / (1.0 + jnp.exp(-x))


def _split(x, dt):
    hi = x.astype(dt)
    lo = (x - hi.astype(F32)).astype(dt)
    return hi, lo


def _split_bf16(x):
    return _split(x, BF16)


def _params(sem):
    return pltpu.CompilerParams(dimension_semantics=sem, vmem_limit_bytes=VMEM_LIMIT)


def _resident(shape):
    nd = len(shape)
    return pl.BlockSpec(shape, lambda *_: (0,) * nd, pipeline_mode=pl.Buffered(1))


def _proj_kernel(x_ref, cos_ref, sin_ref, ln_ref, w_ref, wlr_ref, blr_ref, nw_ref, g_ref,
                 gqkv_ref, la_ref, sog_ref, q_ref, k_ref, v_ref, sgab_ref):
    cdt = w_ref.dtype
    x = x_ref[...]
    h = (x * lax.rsqrt(jnp.mean(x * x, axis=-1, keepdims=True) + EPS) * ln_ref[...]).astype(cdt)
    gqkv_ref[...] = _dot(h, w_ref[:, _C_GQKV[0]:_C_GQKV[1]]).astype(cdt)
    og = _dot(h, w_ref[:, _C_GOG[0]:_C_GOG[1]])
    sog_ref[...] = (og * _sigmoid(og)).astype(cdt)
    lr = _dot(h, w_ref[:, _C_LR[0]:_C_LR[1]]).astype(cdt)
    z = _dot(lr, wlr_ref[...]) + blr_ref[...]
    la_ref[...] = (jnp.minimum(z, 0.0) - jnp.log(1.0 + jnp.exp(-jnp.abs(z)))) / GLA_TAU
    qk = _dot(h, w_ref[:, _C_QK[0]:_C_QK[1]])
    sq = (qk * qk).astype(cdt)
    ms = jnp.concatenate([_dot(sq[:, c * 256:(c + 1) * 256], g_ref[...]) for c in range(6)], axis=1)
    qn = qk * lax.rsqrt(ms + EPS) * nw_ref[...]
    width = 2 * SWA_WIDTH
    cos = jnp.tile(cos_ref[...], (1, width // LANES))
    sin = jnp.tile(sin_ref[...], (1, width // LANES))
    lane = lax.broadcasted_iota(jnp.int32, qn.shape, 1)
    half = SWA_HEAD_DIM // 2
    rot = jnp.where(lane % SWA_HEAD_DIM < half, pltpu.roll(qn, width - half, 1), pltpu.roll(qn, half, 1))
    qr = qn * cos + rot * sin
    q_ref[...] = qr[:, :SWA_WIDTH]
    k_ref[...] = qr[:, SWA_WIDTH:]
    v_ref[...] = _dot(h, w_ref[:, _C_V[0]:_C_V[1]])
    gab = _dot(h, w_ref[:, _C_GAB[0]:_C_GAB[1]])
    sgab_ref[...] = _sigmoid(gab).astype(cdt)


def _proj(x, cos, sin, rope_blocks, ln, w, wlr, blr, nw, g, tm):
    t = x.shape[0]
    assert t % tm == 0 and w.dtype == wlr.dtype == g.dtype
    row = lambda width: pl.BlockSpec((tm, width), lambda i: (i, 0))
    outs = [(2048, w.dtype), (512, F32), (1024, w.dtype), (768, F32), (768, F32), (768, F32), (2048, w.dtype)]
    return pl.pallas_call(
        _proj_kernel,
        grid=(t // tm,),
        in_specs=[row(D_MODEL),
                  pl.BlockSpec((tm, LANES), lambda i: (i % rope_blocks, 0)),
                  pl.BlockSpec((tm, LANES), lambda i: (i % rope_blocks, 0)),
                  _resident(ln.shape), _resident(w.shape), _resident(wlr.shape), _resident(blr.shape),
                  _resident(nw.shape), _resident(g.shape)],
        out_specs=[row(wd) for wd, _ in outs],
        out_shape=[jax.ShapeDtypeStruct((t, wd), dt) for wd, dt in outs],
        compiler_params=_params(("arbitrary",)),
        name="proj",
    )(x, cos, sin, ln, w, wlr, blr, nw, g)


def _gla_kernel(gqkv_ref, la_ref, sog_ref, s0_ref, onw_ref, y_ref, sfin_ref, st_scr, *, chunk, n_chunks):
    j = pl.program_id(1)
    nb = gqkv_ref.shape[0]

    @pl.when(j == 0)
    def _():
        for bi in range(nb):
            for h in range(GLA_HEADS):
                st_scr[bi * GLA_HEADS + h] = s0_ref[bi, h].T

    finals = [_gla_block(bi, gqkv_ref, la_ref, sog_ref, onw_ref, y_ref, st_scr, chunk, n_chunks) for bi in range(nb)]

    @pl.when(j == pl.num_programs(1) - 1)
    def _():
        for bi in range(nb):
            for h in range(GLA_HEADS):
                sfin_ref[bi, h] = finals[bi][h].T


def _gla_block(bi, gqkv_ref, la_ref, sog_ref, onw_ref, y_ref, st_scr, chunk, n_chunks):
    r = lax.broadcasted_iota(jnp.int32, (chunk, chunk), 0)
    c = lax.broadcasted_iota(jnp.int32, (chunk, chunk), 1)
    causal = r >= c
    cdt = gqkv_ref.dtype
    tri = causal.astype(cdt)
    hk = GLA_HEADS * GLA_DK
    rows = [slice(ci * chunk, (ci + 1) * chunk) for ci in range(n_chunks)]

    la_hi, la_lo = _split(la_ref[bi], cdt)
    b_chunks = [_dot(tri, la_hi[rs]) + _dot(tri, la_lo[rs]) for rs in rows]
    last = [bc[chunk - 1:chunk, :] for bc in b_chunks]
    b = jnp.concatenate(b_chunks, axis=0)
    b_end = jnp.concatenate([jnp.broadcast_to(x, (chunk, hk)) for x in last], axis=0)
    q = gqkv_ref[bi, :, :hk].astype(F32) * GLA_DK ** -0.5
    k = gqkv_ref[bi, :, hk:2 * hk].astype(F32)
    qd = (q * jnp.exp(b)).astype(cdt)
    kd = (k * jnp.exp(-b)).astype(cdt)
    kdec = (k * jnp.exp(b_end - b)).astype(cdt)
    decay = [jnp.exp(x) for x in last]

    state = [st_scr[bi * GLA_HEADS + h] for h in range(GLA_HEADS)]
    for ci, rs in enumerate(rows):
        outs = []
        for h in range(GLA_HEADS):
            sl = slice(h * GLA_DK, (h + 1) * GLA_DK)
            v_h = gqkv_ref[bi, rs, 2 * hk + h * GLA_DV:2 * hk + (h + 1) * GLA_DV]
            att = jnp.where(causal, _dot_nt(qd[rs, sl], kd[rs, sl]), 0.0).astype(cdt)
            o = _dot(att, v_h) + _dot_nt(qd[rs, sl], state[h].astype(cdt))
            state[h] = state[h] * decay[ci][:, sl] + _dot_tn(v_h, kdec[rs, sl])
            ms = jnp.mean(o * o, axis=-1, keepdims=True)
            outs.append(o * lax.rsqrt(ms + EPS) * onw_ref[...])
        o_all = jnp.concatenate(outs, axis=1) * sog_ref[bi, rs, :].astype(F32)
        y_ref[bi, rs, :] = o_all.astype(cdt)
    for h in range(GLA_HEADS):
        st_scr[bi * GLA_HEADS + h] = state[h]
    return state


def _gla(gqkv, la, sog, s0, onw, chunk, block, nb):
    b, l, _ = gqkv.shape
    assert b % nb == 0 and l % block == 0 and block % chunk == 0
    tok = lambda width: pl.BlockSpec((nb, block, width), lambda bi, j: (bi, j, 0))
    st = pl.BlockSpec((nb, GLA_HEADS, GLA_DK, GLA_DV), lambda bi, j: (bi, 0, 0, 0))
    return pl.pallas_call(
        functools.partial(_gla_kernel, chunk=chunk, n_chunks=block // chunk),
        grid=(b // nb, l // block),
        in_specs=[tok(2048), tok(512), tok(1024), st, _resident(onw.shape)],
        out_specs=[tok(1024), st],
        out_shape=[jax.ShapeDtypeStruct((b, l, 1024), gqkv.dtype),
                   jax.ShapeDtypeStruct((b, GLA_HEADS, GLA_DK, GLA_DV), F32)],
        scratch_shapes=[pltpu.VMEM((nb * GLA_HEADS, GLA_DV, GLA_DK), F32)],
        compiler_params=_params(("arbitrary", "arbitrary")),
        name="gla",
    )(gqkv, la, sog, s0, onw)


def _band_heads(q, kw, vw, valid):
    n = q.shape[0]
    low = lax.broadcasted_iota(jnp.int32, (n, LANES), 1) < SWA_HEAD_DIM
    ones = jnp.ones((kw.shape[0], LANES), kw.dtype)
    outs, lses = [], []
    for pair in range(SWA_OUT // LANES):
        cols = slice(pair * LANES, (pair + 1) * LANES)
        v_aug = jnp.concatenate([vw[:, cols], ones], axis=1)
        res = []
        for mine in (low, jnp.logical_not(low)):
            qm = jnp.where(mine, q[:, cols], 0.0).astype(kw.dtype)
            s = _dot_nt(qm, kw[:, cols]) * SWA_HEAD_DIM ** -0.5
            s = jnp.where(valid, s, NEG)
            m = jnp.max(s, axis=-1, keepdims=True)
            r = _dot(jnp.exp(s - m).astype(vw.dtype), v_aug)
            den = r[:, LANES:]
            res.append((r[:, :LANES] / den, m + jnp.log(den)))
        outs.append(jnp.where(low, res[0][0], res[1][0]))
        lses.append(jnp.where(low, res[0][1], res[1][1]))
    return jnp.concatenate(outs, axis=1), jnp.concatenate(lses, axis=1)


_SWA_TOKENS = 2048


def _swa_kernel(q_ref, k_ref, v_ref, kp_ref, vp_ref, o_ref, lse_ref, *stage, dil):
    blk = SWA_BLOCK
    nsub = q_ref.shape[1] // (blk * dil)
    first = pl.program_id(1) == 0
    qi = lax.broadcasted_iota(jnp.int32, (blk, 2 * blk), 0)
    kj = lax.broadcasted_iota(jnp.int32, (blk, 2 * blk), 1)
    band = (kj >= qi) & (kj <= qi + blk)
    halves = SWA_OUT // LANES

    if dil > 1:
        ins = (q_ref, k_ref, v_ref, kp_ref, vp_ref)
        q_ref, k_ref, v_ref, kp_ref, vp_ref, o_st, lse_st = stage
        for src, dst in zip(ins, stage):
            for hf in range(halves):
                dst[hf] = src[0, :, hf * LANES:(hf + 1) * LANES]

    def rows(ref, start):
        if dil == 1:
            return ref[0, pl.ds(start, blk), :]
        return jnp.concatenate([ref[hf, pl.ds(start, blk, stride=dil), :] for hf in range(halves)], axis=1)

    def unit(u, carry):
        r = u // nsub
        j = u % nsub
        start = r + dil * blk * j
        inside = r + dil * blk * jnp.maximum(j - 1, 0)
        if dil == 1:
            r, start, inside = 0, pl.multiple_of(start, blk), pl.multiple_of(inside, blk)
        head = j == 0
        kprev = jnp.where(head, rows(kp_ref, r), rows(k_ref, inside))
        vprev = jnp.where(head, rows(vp_ref, r), rows(v_ref, inside))
        kw = jnp.concatenate([kprev, rows(k_ref, start)], axis=0).astype(BF16)
        vw = jnp.concatenate([vprev, rows(v_ref, start)], axis=0).astype(BF16)
        valid = band & (kj >= jnp.where(head & first, blk, 0))
        o, lse = _band_heads(rows(q_ref, start), kw, vw, valid)
        if dil == 1:
            o_ref[0, pl.ds(start, blk), :] = o
            lse_ref[0, pl.ds(start, blk), :] = lse
        else:
            for hf in range(halves):
                o_st[hf, pl.ds(start, blk, stride=dil), :] = o[:, hf * LANES:(hf + 1) * LANES]
                lse_st[hf, pl.ds(start, blk, stride=dil), :] = lse[:, hf * LANES:(hf + 1) * LANES]
        return carry

    lax.fori_loop(0, dil * nsub, unit, 0)
    if dil > 1:
        for hf in range(halves):
            o_ref[0, :, hf * LANES:(hf + 1) * LANES] = o_st[hf]
            lse_ref[0, :, hf * LANES:(hf + 1) * LANES] = lse_st[hf]


def _swa_prompt(q, k, v, g, dil):
    b, s, _ = q.shape
    tb = _SWA_TOKENS
    back = SWA_BLOCK * dil
    assert s % tb == 0 and tb % back == 0
    cur = pl.BlockSpec((1, tb, SWA_OUT), lambda bi, i: (bi, i, g))
    prev = pl.BlockSpec((1, back, SWA_OUT), lambda bi, i: (bi, jnp.maximum(i * (tb // back) - 1, 0), g))
    out = pl.BlockSpec((1, tb, SWA_OUT), lambda bi, i: (bi, i, 0))
    halves = SWA_OUT // LANES
    stage = [pltpu.VMEM((halves, n, LANES), F32) for n in (tb, tb, tb, back, back, tb, tb)] if dil > 1 else []
    o, lse = pl.pallas_call(
        functools.partial(_swa_kernel, dil=dil),
        grid=(b, s // tb),
        in_specs=[cur, cur, cur, prev, prev],
        out_specs=[out, out],
        out_shape=[jax.ShapeDtypeStruct((b, s, SWA_OUT), F32)] * 2,
        scratch_shapes=stage,
        compiler_params=_params(("arbitrary", "arbitrary")),
        name=f"swa_prompt_g{g}",
    )(q, k, v, k, v)
    return o.reshape(b * s, SWA_OUT), lse.reshape(b * s, SWA_OUT)


_SAMPLE_ROWS = 16


def _swa_sample_kernel(q_ref, kn_ref, vn_ref, k0_ref, v0_ref, k1_ref, v1_ref, k2_ref, v2_ref,
                       ob_ref, ok0_ref, ov0_ref, ok1_ref, ov1_ref, ok2_ref, ov2_ref, *, n_new):
    rows = _SAMPLE_ROWS
    in_refs = ((k0_ref, v0_ref), (k1_ref, v1_ref), (k2_ref, v2_ref))
    out_refs = ((ok0_ref, ov0_ref), (ok1_ref, ov1_ref), (ok2_ref, ov2_ref))
    scale = SWA_HEAD_DIM ** -0.5
    jn = lax.broadcasted_iota(jnp.int32, (rows, rows), 1)
    ln = lax.broadcasted_iota(jnp.int32, (rows, rows), 0)
    tail = lax.broadcasted_iota(jnp.int32, (SWA_HEAD_DIM, LANES), 1) >= LANES - n_new
    sel_row = lax.broadcasted_iota(jnp.int32, (rows, LANES), 0)
    sel_lane = lax.broadcasted_iota(jnp.int32, (rows, LANES), 1)
    selector = ((sel_lane == sel_row + (LANES - n_new)) & (sel_row < n_new)).astype(F32)
    knt = _dot_tn(kn_ref[0], selector)
    vnt = _dot_tn(vn_ref[0], selector)
    o_g, lse_g = [], []
    for g, (win, dil) in enumerate(SWA_GROUPS):
        jc = lax.broadcasted_iota(jnp.int32, (rows, win), 1)
        lc = lax.broadcasted_iota(jnp.int32, (rows, win), 0)
        valid_c = (jc >= lc) & (((jc - lc) & (dil - 1)) == 0)
        valid_n = (jn <= ln) & (((ln - jn) & (dil - 1)) == 0) & (jn < n_new)
        o_h, lse_h = [], []
        for h in range(SWA_HPG):
            col = g * SWA_OUT + h * SWA_HEAD_DIM
            hsl = slice(col, col + SWA_HEAD_DIM)
            qh = q_ref[0, :, hsl]
            knh = kn_ref[0, :, hsl]
            vnh = vn_ref[0, :, hsl]
            for (src, dst, new_t) in ((in_refs[g][0], out_refs[g][0], knt), (in_refs[g][1], out_refs[g][1], vnt)):
                old = src[0, h]
                moved = pltpu.roll(old, win - n_new, 1)
                if win > LANES:
                    dst[0, h, :, 0:win - LANES] = moved[:, 0:win - LANES]
                dst[0, h, :, win - LANES:win] = jnp.where(tail, new_t[hsl, :], moved[:, win - LANES:win])
            kt = in_refs[g][0][0, h]
            vt = in_refs[g][1][0, h]
            s_c = jnp.where(valid_c, _dot(qh, kt) * scale, NEG)
            s_n = jnp.where(valid_n, _dot_nt(qh, knh) * scale, NEG)
            m = jnp.maximum(jnp.max(s_c, axis=-1, keepdims=True), jnp.max(s_n, axis=-1, keepdims=True))
            p_c = jnp.exp(s_c - m)
            p_n = jnp.exp(s_n - m)
            den = jnp.sum(p_c, axis=-1, keepdims=True) + jnp.sum(p_n, axis=-1, keepdims=True)
            o_h.append((_dot_nt(p_c, vt) + _dot(p_n, vnh)) / den)
            lse_h.append(jnp.broadcast_to(m + jnp.log(den), (rows, SWA_HEAD_DIM)))
        o_g.append(jnp.concatenate(o_h, axis=1))
        lse_g.append(jnp.concatenate(lse_h, axis=1))
    lmax = jnp.maximum(jnp.maximum(lse_g[0], lse_g[1]), lse_g[2])
    e = [jnp.exp(x - lmax) for x in lse_g]
    ob_ref[0] = (e[0] * o_g[0] + e[1] * o_g[1] + e[2] * o_g[2]) / (e[0] + e[1] + e[2])


def _swa_sample(q, kn, vn, caches_t, n_new):
    bd = q.shape[0]
    rows = _SAMPLE_ROWS
    tok = pl.BlockSpec((1, rows, SWA_WIDTH), lambda bi: (bi, 0, 0))
    specs, args = [tok, tok, tok], [q, kn, vn]
    out_specs = [pl.BlockSpec((1, rows, SWA_OUT), lambda bi: (bi, 0, 0))]
    out_shape = [jax.ShapeDtypeStruct((bd, rows, SWA_OUT), F32)]
    for g, (win, dil) in enumerate(SWA_GROUPS):
        for t in caches_t[g]:
            assert t.shape == (bd, SWA_HPG, SWA_HEAD_DIM, win) and win == SWA_BLOCK * dil and win % LANES == 0
            spec = pl.BlockSpec((1, SWA_HPG, SWA_HEAD_DIM, win), lambda bi: (bi, 0, 0, 0))
            args.append(t)
            specs.append(spec)
            out_specs.append(spec)
            out_shape.append(jax.ShapeDtypeStruct(t.shape, t.dtype))
    return pl.pallas_call(
        functools.partial(_swa_sample_kernel, n_new=n_new),
        grid=(bd,),
        in_specs=specs,
        out_specs=out_specs,
        out_shape=out_shape,
        compiler_params=_params(("arbitrary",)),
        name="swa_sample",
    )(*args)


def _merge_kernel(*refs, combine, n_alias, n_real):
    x1_ref, h2c_ref, lg_ref, cnt_ref = refs[-4:]
    refs = refs[:len(refs) - 4 - n_alias]

    @pl.when(pl.program_id(0) >= n_real)
    def _():
        h2c_ref[...] = jnp.zeros_like(h2c_ref)
        lg_ref[...] = jnp.zeros_like(lg_ref)
        cnt_ref[...] = jnp.zeros_like(cnt_ref)

    pl.when(pl.program_id(0) < n_real)(
        functools.partial(_merge_tile, refs, x1_ref, h2c_ref, lg_ref, cnt_ref, combine))


def _merge_tile(refs, x1_ref, h2c_ref, lg_ref, cnt_ref, combine):
    if combine:
        (x_ref, ya_ref, o0, o1, o2, l0, l1, l2, sgab_ref, wa_ref, wb_ref, wo_ref, ln_ref, wr_cat_ref, wr_hi_ref,
         br_ref) = refs
        lmax = jnp.maximum(jnp.maximum(l0[...], l1[...]), l2[...])
        e0, e1, e2 = jnp.exp(l0[...] - lmax), jnp.exp(l1[...] - lmax), jnp.exp(l2[...] - lmax)
        ob = ((e0 * o0[...] + e1 * o1[...] + e2 * o2[...]) / (e0 + e1 + e2)).astype(wb_ref.dtype)
    else:
        (x_ref, ya_ref, ob_ref, sgab_ref, wa_ref, wb_ref, wo_ref, ln_ref, wr_cat_ref, wr_hi_ref, br_ref) = refs
        ob = ob_ref[...]
    ya = _dot(ya_ref[...], wa_ref[...])
    yb = _dot(ob, wb_ref[...])
    sga = sgab_ref[:, :D_MODEL].astype(F32)
    sgb = sgab_ref[:, D_MODEL:].astype(F32)
    x1 = x_ref[...] + _dot((sga * ya + sgb * yb).astype(wo_ref.dtype), wo_ref[...])
    x1_ref[...] = x1
    h2 = x1 * lax.rsqrt(jnp.mean(x1 * x1, axis=-1, keepdims=True) + EPS) * ln_ref[...]
    h_hi, h_lo = _split_bf16(h2)
    both = _dot(h_hi, wr_cat_ref[...])
    lg = both[:, :LANES] + both[:, LANES:] + _dot(h_lo, wr_hi_ref[...]) + br_ref[...]
    lg_ref[...] = lg
    h2c_ref[...] = h_hi
    _, _, _, _, _, hot1, hot2 = _route_select(lg)
    cnt = jnp.sum((hot1 | hot2).astype(F32), axis=0, keepdims=True)
    cnt_ref[...] = jnp.broadcast_to(cnt, cnt_ref.shape)


def _merge(x, ya_in, swa, sgab, weights, tm, tok_off, t_all, shared=None):
    t = x.shape[0]
    assert t % tm == 0 and tok_off % tm == 0 and MERGE_TILE % tm == 0
    combine = len(swa) > 1
    blk_off = tok_off // tm
    t_buf = -(-t_all // MERGE_TILE) * MERGE_TILE
    n_real = t // tm
    n_fill = 0 if shared is not None else (t_buf - tok_off - t) // tm
    row = lambda width: pl.BlockSpec((tm, width), lambda i: (jnp.minimum(i, n_real - 1), 0))
    shared_in = [] if shared is None else list(shared)
    n_in = 3 + len(swa) + len(weights)
    return pl.pallas_call(
        functools.partial(_merge_kernel, combine=combine, n_alias=len(shared_in), n_real=n_real),
        grid=(n_real + n_fill,),
        in_specs=[row(D_MODEL), row(D_MODEL)] + [row(SWA_OUT)] * len(swa) + [row(2 * D_MODEL)]
                 + [_resident(w.shape) for w in weights] + [pl.BlockSpec(memory_space=pl.ANY)] * len(shared_in),
        out_specs=[row(D_MODEL),
                   pl.BlockSpec((tm, D_MODEL), lambda i: (i + blk_off, 0)),
                   pl.BlockSpec((tm, LANES), lambda i: (i + blk_off, 0)),
                   pl.BlockSpec((SUBLANES, LANES), lambda i: (i, 0))],
        out_shape=[jax.ShapeDtypeStruct((t, D_MODEL), F32),
                   jax.ShapeDtypeStruct((t_buf, D_MODEL), BF16),
                   jax.ShapeDtypeStruct((t_buf, LANES), F32),
                   jax.ShapeDtypeStruct(((n_real + n_fill) * SUBLANES, LANES), F32)],
        input_output_aliases={n_in + k: 1 + k for k in range(len(shared_in))},
        compiler_params=_params(("arbitrary",)),
        name="merge",
    )(x, ya_in, *swa, sgab, *weights, *shared_in)


def _route_select(lg):
    lane = lax.broadcasted_iota(jnp.int32, lg.shape, 1)
    big = jnp.int32(LANES)
    gl = jnp.where(lane < N_GROUPS, lg, NEG)
    gmax = jnp.max(gl, axis=-1, keepdims=True)
    g_idx = jnp.min(jnp.where(gl == gmax, lane, big), axis=-1, keepdims=True)
    g_w = 1.0 / jnp.sum(jnp.exp(gl - gmax), axis=-1, keepdims=True)
    e_lane = lane - N_GROUPS
    in_group = (e_lane >= 0) & (e_lane < N_EXPERTS) & (e_lane // EXPERTS_PER_GROUP == g_idx)
    el = jnp.where(in_group, lg, NEG)
    v1 = jnp.max(el, axis=-1, keepdims=True)
    i1 = jnp.min(jnp.where(el == v1, lane, big), axis=-1, keepdims=True)
    el2 = jnp.where(lane == i1, NEG, el)
    v2 = jnp.max(el2, axis=-1, keepdims=True)
    i2 = jnp.min(jnp.where(el2 == v2, lane, big), axis=-1, keepdims=True)
    r21 = jnp.exp(v2 - v1)
    w1 = g_w / (1.0 + r21)
    w2 = g_w * r21 / (1.0 + r21)
    e1 = i1 - N_GROUPS
    e2 = i2 - N_GROUPS
    return lane, e1, e2, w1, w2, lane == e1, lane == e2


def _router_kernel(lg_ref, offs_ref, mi_ref, mw_ref, blk_ref, carry):
    @pl.when(pl.program_id(0) == 0)
    def _():
        carry[...] = offs_ref[0:1, :]

    lane, e1, e2, w1, w2, hot1, hot2 = _route_select(lg_ref[...])
    tr = lane.shape[0]
    hot = (hot1 | hot2).astype(BF16)
    start = carry[...]
    cnt = jnp.sum(hot.astype(F32), axis=0, keepdims=True)
    carry[...] = start + cnt
    r = lax.broadcasted_iota(jnp.int32, (tr, tr), 0)
    c = lax.broadcasted_iota(jnp.int32, (tr, tr), 1)
    within = _dot((r > c).astype(BF16), hot)
    lane1 = lax.broadcasted_iota(jnp.int32, (SUBLANES, LANES), 1)
    incl = jnp.broadcast_to(cnt, (SUBLANES, LANES))
    for sh in (1, 2, 4, 8, 16):
        incl = incl + jnp.where(lane1 >= sh, pltpu.roll(incl, sh, 1), 0.0)
    local = incl[0:1, :] - cnt
    pick = lambda hot_k, row: jnp.sum(jnp.where(hot_k, row, 0.0), axis=-1, keepdims=True)
    cols = [e1, e2]
    for hot_k in (hot1, hot2):
        cols.append((pick(hot_k, within) + pick(hot_k, start)).astype(jnp.int32))
    for hot_k in (hot1, hot2):
        cols.append((pick(hot_k, within) + pick(hot_k, local)).astype(jnp.int32))
    mi = cols[-1]
    for k in range(len(cols) - 2, -1, -1):
        mi = jnp.where(lane == k, cols[k], mi)
    mi_ref[...] = mi
    mw_ref[...] = jnp.where(lane == 0, w1, w2)
    sub = lax.broadcasted_iota(jnp.int32, (SUBLANES, LANES), 0)
    tbl = jnp.where(sub == 0, start, jnp.where(sub == 1, cnt, jnp.where(sub == 2, local, 0.0)))
    blk_ref[...] = tbl.astype(jnp.int32)


def _router(logits, seg_start, t, tr):
    assert t % tr == 0 and tr % SUBLANES == 0
    row = pl.BlockSpec((tr, LANES), lambda i: (i, 0))
    small = pl.BlockSpec((SUBLANES, LANES), lambda i: (i, 0))
    return pl.pallas_call(
        _router_kernel,
        grid=(t // tr,),
        in_specs=[row, pl.BlockSpec((SUBLANES, LANES), lambda i: (0, 0))],
        out_specs=[row, row, small],
        out_shape=[jax.ShapeDtypeStruct((t, LANES), jnp.int32), jax.ShapeDtypeStruct((t, LANES), F32),
                   jax.ShapeDtypeStruct((t // tr * SUBLANES, LANES), jnp.int32)],
        scratch_shapes=[pltpu.VMEM((1, LANES), F32)],
        compiler_params=_params(("arbitrary",)),
        name="router",
    )(logits, seg_start)


def _run_copies(src, dst, sem, src_row, dst_row, n_rows, wait=False):
    nsub = D_MODEL // LANES

    def copy(s0, d0, rows):
        dma = pltpu.make_async_copy(src.at[pl.ds(pl.multiple_of(s0 * nsub, nsub), rows * nsub), :],
                                    dst.at[pl.ds(pl.multiple_of(d0 * nsub, nsub), rows * nsub), :], sem)
        dma.wait() if wait else dma.start()

    def chunk(k, carry):
        copy(src_row + k * SUBLANES, dst_row + k * SUBLANES, SUBLANES)
        return carry

    n_chunks = n_rows // SUBLANES
    lax.fori_loop(0, n_chunks, chunk, 0)
    done = n_chunks * SUBLANES
    for rows in (4, 2, 1):
        @pl.when((n_rows & rows) != 0)
        def _(rows=rows, done=done):
            copy(src_row + done, dst_row + done, rows)

        done = done + (n_rows & rows)


def _dispatch_kernel(start_ref, cnt_ref, loc_ref, end_ref, tot_ref, h_ref, mi_ref, xs_hbm, stage, zeros, sem, zsem,
                     *, tm):
    i = pl.program_id(0)
    n = pl.num_programs(0)
    nsub = D_MODEL // LANES
    tb = h_ref.shape[0]
    slot = i % 2

    def wait_stage(sl):
        pltpu.make_async_copy(stage.at[sl], stage.at[sl], sem.at[sl]).wait()

    def pad_rows(wait):
        def one(e, carry):
            tot = tot_ref[e]
            n_pad = (tot + tm - 1) // tm * tm - tot
            _run_copies(zeros, xs_hbm, zsem, 0, end_ref[e] - n_pad, n_pad, wait=wait)
            return carry

        lax.fori_loop(0, end_ref.shape[0], one, 0)

        def spare(j, carry):
            dma = pltpu.make_async_copy(zeros, xs_hbm.at[pl.ds(pl.multiple_of(j * tm * nsub, nsub), tm * nsub), :], zsem)
            dma.wait() if wait else dma.start()
            return carry

        lax.fori_loop(end_ref[end_ref.shape[0] - 1] // tm, xs_hbm.shape[0] // (tm * nsub), spare, 0)

    @pl.when(i == 0)
    def _():
        zeros[...] = jnp.zeros_like(zeros)
        pad_rows(wait=False)

    @pl.when(i >= 2)
    def _():
        wait_stage(slot)

    li = mi_ref[...]
    rows_iota = lax.broadcasted_iota(jnp.int32, (tb, 2 * tb), 1)
    onehot = ((rows_iota == li[:, 4:5]) | (rows_iota == li[:, 5:6])).astype(BF16)
    srt = _dot_tn(onehot, h_ref[...])
    for c in range(nsub):
        stage[slot, pl.ds(c, 2 * tb, stride=nsub), :] = srt[:, c * LANES:(c + 1) * LANES]

    def run(e, carry):
        k = i * N_EXPERTS + e
        _run_copies(stage.at[slot], xs_hbm, sem.at[slot], loc_ref[k], start_ref[k], cnt_ref[k])
        return carry

    lax.fori_loop(0, N_EXPERTS, run, 0)

    @pl.when(i == n - 1)
    def _():
        wait_stage(slot)

        @pl.when(n >= 2)
        def _():
            wait_stage(1 - slot)

        pad_rows(wait=True)


def _dispatch(blk_start, blk_cnt, blk_loc, seg_end, seg_tot, h2, meta_i, n_slots, tb):
    t = meta_i.shape[0]
    nsub = D_MODEL // LANES
    assert t % tb == 0
    grid_spec = pltpu.PrefetchScalarGridSpec(
        num_scalar_prefetch=5,
        grid=(t // tb,),
        in_specs=[pl.BlockSpec((tb, D_MODEL), lambda i, *_: (i, 0)),
                  pl.BlockSpec((tb, LANES), lambda i, *_: (i, 0))],
        out_specs=pl.BlockSpec(memory_space=pl.ANY),
        scratch_shapes=[pltpu.VMEM((2, 2 * tb * nsub, LANES), F32),
                        pltpu.VMEM((EXPERT_TILE * nsub, LANES), F32),
                        pltpu.SemaphoreType.DMA((2,)), pltpu.SemaphoreType.DMA(())],
    )
    return pl.pallas_call(
        functools.partial(_dispatch_kernel, tm=EXPERT_TILE),
        grid_spec=grid_spec,
        out_shape=jax.ShapeDtypeStruct((n_slots * nsub, LANES), F32),
        compiler_params=_params(("arbitrary",)),
        name="dispatch",
    )(blk_start, blk_cnt, blk_loc, seg_end, seg_tot, h2, meta_i)


def _expert_kernel(te_ref, tv_ref, nxt_ref, xs_ref, wg_hbm, wu_hbm, wd_hbm, out_ref,
                   wg_st, wu_st, wd_st, wgb, wub, wdb, n_switch, sem):
    i = pl.program_id(0)
    tm = EXPERT_TILE
    nsub = D_MODEL // LANES
    valid = tv_ref[i] != 0
    e = te_ref[i]

    def weight_copies(expert, sl):
        return [pltpu.make_async_copy(hbm.at[0, expert], st.at[sl], sem.at[sl, k])
                for k, (hbm, st) in enumerate(((wg_hbm, wg_st), (wu_hbm, wu_st), (wd_hbm, wd_st)))]

    @pl.when(i == 0)
    def _():
        n_switch[0] = 0
        for dma in weight_copies(e, 0):
            dma.start()

    @pl.when(jnp.logical_not(valid))
    def _():
        out_ref[...] = jnp.zeros_like(out_ref)

    @pl.when(valid & ((i == 0) | (e != te_ref[jnp.maximum(i - 1, 0)])))
    def _():
        sl = n_switch[0] % 2
        n_switch[0] = n_switch[0] + 1
        for dma in weight_copies(e, sl):
            dma.wait()
        wgb[...] = wg_st[sl].astype(BF16)
        wub[...] = wu_st[sl].astype(BF16)
        wdb[...] = wd_st[sl].astype(BF16)

        @pl.when(nxt_ref[e] != e)
        def _():
            for dma in weight_copies(nxt_ref[e], 1 - sl):
                dma.start()

    @pl.when(valid)
    def _():
        h = jnp.concatenate([xs_ref[pl.ds(c, tm, stride=nsub), :] for c in range(nsub)], axis=1).astype(BF16)
        gate = _dot(h, wgb[...])
        up = _dot(h, wub[...])
        a = (gate * _sigmoid(gate) * up).astype(BF16)
        o = _dot(a, wdb[...])
        for c in range(nsub):
            out_ref[pl.ds(c, tm, stride=nsub), :] = o[:, c * LANES:(c + 1) * LANES]


def _experts(tile_expert, tile_valid, next_expert, xs, wg, wu, wd):
    n_tiles = tile_expert.shape[0]
    tm = EXPERT_TILE
    nsub = D_MODEL // LANES
    used = lambda i, tv: jnp.where(tv[i] != 0, i, 0)
    anywhere = pl.BlockSpec(memory_space=pl.ANY)
    up_shape, down_shape = (D_MODEL, D_EXPERT), (D_EXPERT, D_MODEL)
    grid_spec = pltpu.PrefetchScalarGridSpec(
        num_scalar_prefetch=3,
        grid=(n_tiles,),
        in_specs=[pl.BlockSpec((tm * nsub, LANES), lambda i, te, tv, nx: (used(i, tv), 0)),
                  anywhere, anywhere, anywhere],
        out_specs=pl.BlockSpec((tm * nsub, LANES), lambda i, te, tv, nx: (i, 0)),
        scratch_shapes=[pltpu.VMEM((2,) + up_shape, F32), pltpu.VMEM((2,) + up_shape, F32),
                        pltpu.VMEM((2,) + down_shape, F32),
                        pltpu.VMEM(up_shape, BF16), pltpu.VMEM(up_shape, BF16), pltpu.VMEM(down_shape, BF16),
                        pltpu.SMEM((1,), jnp.int32), pltpu.SemaphoreType.DMA((2, 3))],
    )
    return pl.pallas_call(
        _expert_kernel,
        grid_spec=grid_spec,
        out_shape=jax.ShapeDtypeStruct((n_tiles * tm * nsub, LANES), F32),
        compiler_params=_params(("arbitrary",)),
        name="experts",
    )(tile_expert, tile_valid, next_expert, xs, wg, wu, wd)


def _combine_kernel(p0_ref, p1_ref, x1_ref, mw_ref, rows_hbm, y_ref, g0, g1, sem, *, tc, blk_off):
    i = pl.program_id(0)
    n = pl.num_programs(0)
    nout = D_MODEL // LANES
    slot = i % 2

    def issue_all(step, sl):
        base = (step + blk_off) * tc

        def issue(j, carry):
            for pos_ref, dst, s in ((p0_ref, g0, 0), (p1_ref, g1, 1)):
                src0 = pl.multiple_of(pos_ref[base + j] * nout, nout)
                pltpu.make_async_copy(rows_hbm.at[pl.ds(src0, nout), :],
                                      dst.at[sl, pl.ds(pl.multiple_of(j * nout, nout), nout), :],
                                      sem.at[sl, s]).start()
            return carry

        lax.fori_loop(0, tc, issue, 0, unroll=4)

    @pl.when(i == 0)
    def _():
        issue_all(i, slot)

    @pl.when(i + 1 < n)
    def _():
        issue_all(i + 1, 1 - slot)

    for dst, s in ((g0, 0), (g1, 1)):
        pltpu.make_async_copy(rows_hbm.at[pl.ds(0, tc * nout), :], dst.at[slot], sem.at[slot, s]).wait()
    w0 = mw_ref[:, 0:1]
    w1 = mw_ref[:, 1:2]
    for c in range(nout):
        sl = slice(c * LANES, (c + 1) * LANES)
        y_ref[:, sl] = (x1_ref[:, sl] + w0 * g0[slot, pl.ds(c, tc, stride=nout), :]
                        + w1 * g1[slot, pl.ds(c, tc, stride=nout), :])


def _combine(pos0, pos1, x1, meta_w, rows, tc, tok_off):
    t = x1.shape[0]
    assert t % tc == 0 and tok_off % tc == 0
    blk_off = tok_off // tc
    nout = D_MODEL // LANES
    row = pl.BlockSpec((tc, D_MODEL), lambda i, p0, p1: (i, 0))
    grid_spec = pltpu.PrefetchScalarGridSpec(
        num_scalar_prefetch=2,
        grid=(t // tc,),
        in_specs=[row, pl.BlockSpec((tc, LANES), lambda i, p0, p1: (i + blk_off, 0)),
                  pl.BlockSpec(memory_space=pl.ANY)],
        out_specs=row,
        scratch_shapes=[pltpu.VMEM((2, tc * nout, LANES), F32), pltpu.VMEM((2, tc * nout, LANES), F32),
                        pltpu.SemaphoreType.DMA((2, 2))],
    )
    return pl.pallas_call(
        functools.partial(_combine_kernel, tc=tc, blk_off=blk_off),
        grid_spec=grid_spec,
        out_shape=jax.ShapeDtypeStruct((t, D_MODEL), F32),
        compiler_params=_params(("arbitrary",)),
        name="combine",
    )(pos0, pos1, x1, meta_w, rows)


def _rope_tables(pos):
    half = SWA_HEAD_DIM // 2
    inv_freq = ROPE_THETA ** (-np.arange(half, dtype=np.float64) / half)
    ang = pos.astype(np.float64)[:, None] * inv_freq[None, :]
    cos, sin = np.cos(ang), np.sin(ang)
    reps = LANES // SWA_HEAD_DIM
    table = lambda a, b: jnp.asarray(np.tile(np.concatenate([a, b], axis=1), (1, reps)).astype(np.float32))
    return table(cos, cos), table(-sin, sin)


def kernel(x_prompt, x_sample, state_gla, cache_swa_k0, cache_swa_v0, cache_swa_k1, cache_swa_v1, cache_swa_k2, cache_swa_v2, ln1_w, w_in, w_gla_lr, b_gla_lr, gla_onorm_w, q_norm_w, k_norm_w, w_branch_a, w_branch_b, w_out, ln2_w, w_router_group, b_router_group, w_router_expert, b_router_expert, w_exp_gate, w_exp_up, w_exp_down):
    b, s, d = x_prompt.shape
    bd, ls, _ = x_sample.shape
    tp, ts = b * s, bd * ls
    assert w_in.shape[0] == 1 and d == D_MODEL and ts % SUBLANES == 0
    k_caches = (cache_swa_k0, cache_swa_k1, cache_swa_k2)
    v_caches = (cache_swa_v0, cache_swa_v1, cache_swa_v2)

    w = w_in[0]
    cuts = np.cumsum((512, 512, 1024, GLA_RANK, 1024, 768, 768, 768, 1024, 1024))
    sec = lambda a: w[:, (0 if a == 0 else cuts[a - 1]):cuts[a]]
    lr_pad = jnp.pad(sec(3), ((0, 0), (0, LANES - GLA_RANK)))
    w_f32 = jnp.concatenate([sec(0), sec(1), sec(2), sec(4), sec(5), sec(6), sec(7), sec(8), sec(9), lr_pad], axis=1)
    w_packed = w_f32.astype(BF16)
    wlr_f32 = jnp.pad(w_gla_lr[0], ((0, LANES - GLA_RANK), (0, 0)))
    wlr = wlr_f32.astype(BF16)
    blr = b_gla_lr[0][None, :]
    nw = jnp.concatenate([jnp.tile(q_norm_w[0], SWA_WIDTH // SWA_HEAD_DIM), jnp.tile(k_norm_w[0], SWA_WIDTH // SWA_HEAD_DIM)])[None, :]
    gi = np.arange(256) // SWA_HEAD_DIM
    gmat_f32 = jnp.asarray((gi[:, None] == gi[None, :]).astype(np.float32) / SWA_HEAD_DIM)
    gmat = gmat_f32.astype(BF16)
    ln1 = ln1_w[0][None, :]
    ln2 = ln2_w[0][None, :]
    onw = gla_onorm_w[0][None, :]
    wa_f32, wb_f32, wo_f32 = w_branch_a[0], w_branch_b[0], w_out[0]
    wa, wb, wo = wa_f32.astype(BF16), wb_f32.astype(BF16), wo_f32.astype(BF16)
    wr = jnp.pad(jnp.concatenate([w_router_group[0], w_router_expert[0]], axis=1),
                 ((0, 0), (0, LANES - N_GROUPS - N_EXPERTS)))
    wr_hi, wr_lo = _split_bf16(wr)
    br = jnp.pad(jnp.concatenate([b_router_group[0], b_router_expert[0]]), (0, LANES - N_GROUPS - N_EXPERTS))[None, :]

    cos_p, sin_p = _rope_tables(np.arange(s))
    cos_s, sin_s = _rope_tables(PAST_LEN + np.arange(ts) % ls)

    tm_p = 512
    proj_p = _proj(x_prompt.reshape(tp, d), cos_p, sin_p, s // tm_p, ln1, w_packed, wlr, blr, nw, gmat, tm_p)
    proj_s = _proj(x_sample.reshape(ts, d), cos_s, sin_s, 1, ln1, w_f32, wlr_f32, blr, nw, gmat_f32, ts)
    gqkv_p, la_p, sog_p, q_p, k_p, v_p, sgab_p = proj_p
    gqkv_s, la_s, sog_s, q_s, k_s, v_s, sgab_s = proj_s

    r3 = lambda t, nb: t.reshape(nb, t.shape[0] // nb, t.shape[1])
    ya_p, st_p = _gla(r3(gqkv_p, b), r3(la_p, b), r3(sog_p, b),
                      jnp.zeros((b, GLA_HEADS, GLA_DK, GLA_DV), F32), onw, GLA_CHUNK, 512, b)
    pad_s = lambda t: jnp.pad(r3(t, bd), ((0, 0), (0, _SAMPLE_ROWS - ls), (0, 0)))
    ya_s, st_s = _gla(pad_s(gqkv_s), pad_s(la_s), pad_s(sog_s), state_gla[0], onw, _SAMPLE_ROWS, _SAMPLE_ROWS, 8)
    ya_s = ya_s[:, :ls].reshape(ts, d)

    q3, k3, v3 = r3(q_p, b), r3(k_p, b), r3(v_p, b)
    swa_p = [_swa_prompt(q3, k3, v3, g, dil) for g, (_, dil) in enumerate(SWA_GROUPS)]
    to_t = lambda c: jnp.transpose(c[0], (0, 2, 3, 1))
    caches_t = [(to_t(k_caches[g]), to_t(v_caches[g])) for g in range(len(SWA_GROUPS))]
    ob_s, *new_caches = _swa_sample(pad_s(q_s), pad_s(k_s), pad_s(v_s), caches_t, ls)
    ob_s = ob_s[:, :ls].reshape(ts, SWA_OUT)

    t = tp + ts
    mw = (wa, wb, wo, ln2, jnp.concatenate([wr_hi, wr_lo], axis=1), wr_hi, br)
    x1_p, h2c, lg, cnt_p = _merge(x_prompt.reshape(tp, d), ya_p.reshape(tp, d),
                                  [o for o, _ in swa_p] + [l for _, l in swa_p], sgab_p, mw, MERGE_TILE, 0, t)
    mw_f32 = (wa_f32, wb_f32, wo_f32) + mw[3:]
    x1_s, h2c, lg, cnt_s = _merge(x_sample.reshape(ts, d), ya_s, [ob_s], sgab_s, mw_f32, ts, tp, t,
                                  shared=(h2c, lg))

    tm = EXPERT_TILE
    n_tiles = (2 * t) // tm + N_EXPERTS
    counts = ((jnp.sum(cnt_p, axis=0) + jnp.sum(cnt_s, axis=0)) * (1.0 / SUBLANES)).astype(jnp.int32)
    ends_all = jnp.cumsum((counts + tm - 1) // tm * tm)
    seg_start = jnp.broadcast_to((ends_all - (counts + tm - 1) // tm * tm).astype(F32)[None, :], (SUBLANES, LANES))
    counts, ends = counts[:N_EXPERTS], ends_all[:N_EXPERTS]
    tb = DISPATCH_BLOCK
    meta_i, meta_w, blocks = _router(lg, seg_start, t, tb)
    blocks = blocks.reshape(t // tb, SUBLANES, LANES)[:, :, :N_EXPERTS]
    blk_start, blk_cnt, blk_loc = (blocks[:, r].reshape(-1) for r in range(3))
    pos0, pos1 = meta_i[:, 2], meta_i[:, 3]
    tile_start = jnp.arange(n_tiles, dtype=jnp.int32) * tm
    tile_valid = (tile_start < ends[-1]).astype(jnp.int32)
    last_slot = jnp.minimum(tile_start, ends[-1] - 1)
    tile_expert = jnp.minimum(jnp.sum((last_slot[:, None] >= ends[None, :]).astype(jnp.int32), axis=1), N_EXPERTS - 1)

    xs = _dispatch(blk_start, blk_cnt, blk_loc, ends, counts, h2c, meta_i, n_tiles * tm, tb)
    ids = jnp.arange(N_EXPERTS, dtype=jnp.int32)
    later_used = (counts[None, :] > 0) & (ids[None, :] > ids[:, None])
    next_expert = jnp.min(jnp.where(later_used, ids[None, :], N_EXPERTS), axis=1)
    next_expert = jnp.where(next_expert == N_EXPERTS, ids, next_expert)
    rows = _experts(tile_expert, tile_valid, next_expert, xs, w_exp_gate, w_exp_up, w_exp_down)
    y_p = _combine(pos0, pos1, x1_p, meta_w, rows, 512, 0)
    y_s = _combine(pos0, pos1, x1_s, meta_w, rows, ts, tp)

    heads = lambda a: a.reshape(1, a.shape[0], a.shape[1], SWA_HPG, SWA_HEAD_DIM)
    outs = [y_p.reshape(b, s, d), y_s.reshape(bd, ls, d), st_p[None].astype(x_prompt.dtype)]
    for g, (win, _) in enumerate(SWA_GROUPS):
        keep = min(win, s)
        gsl = slice(g * SWA_OUT, (g + 1) * SWA_OUT)
        outs += [heads(k3[:, s - keep:, gsl]), heads(v3[:, s - keep:, gsl])]
    outs.append(st_s[None].astype(state_gla.dtype))
    outs += [jnp.transpose(c, (0, 3, 1, 2))[None] for c in new_caches]
    return tuple(outs)
```

```python
import functools

import numpy as np
import jax
import jax.numpy as jnp
from jax import lax
from jax.experimental import pallas as pl
from jax.experimental.pallas import tpu as pltpu

F32 = jnp.float32
BF16 = jnp.bfloat16

D_MODEL = 1024
PAST_LEN = 16384
GLA_HEADS = 4
GLA_DK = 128
GLA_DV = 256
GLA_RANK = 16
GLA_TAU = 16.0
GLA_CHUNK = 64
SWA_GROUPS = ((128, 1), (512, 4), (2048, 16))
SWA_HPG = 4
SWA_HEAD_DIM = 64
SWA_WIDTH = 768
SWA_OUT = 256
SWA_BLOCK = 128
ROPE_THETA = 10000.0
N_GROUPS = 4
EXPERTS_PER_GROUP = 8
N_EXPERTS = 32
D_EXPERT = 512
EPS = 1e-6

LANES = 128
SUBLANES = 8
VMEM_LIMIT = 56 * 1024 * 1024
NEG = -1e30
EXPERT_TILE = 256
MERGE_TILE = 512
DISPATCH_BLOCK = 384

_PACKED_COLS = dict(gqkv=(0, 2048), gog=(2048, 3072), qk=(3072, 4608), v=(4608, 5376), gab=(5376, 7424),
                    lr=(7424, 7552))
_RAW_COLS = dict(gqkv=(0, 2048), lr=(2048, 2048 + LANES), gog=(2064, 3088), qk=(3088, 4624), v=(4624, 5392),
                 gab=(5392, 7440))


def _contract(a, b, dims):
    dg = lambda x, y: lax.dot_general(x, y, (dims, ((), ())), preferred_element_type=F32)
    if a.dtype == F32 and b.dtype == F32:
        a_hi, a_lo = _split(a, BF16)
        b_hi, b_lo = _split(b, BF16)
        return dg(a_hi, b_hi) + dg(a_hi, b_lo) + dg(a_lo, b_hi)
    return dg(a, b)


def _dot(a, b):
    return _contract(a, b, ((1,), (0,)))


def _dot_nt(a, b):
    return _contract(a, b, ((1,), (1,)))


def _dot_tn(a, b):
    return _contract(a, b, ((0,), (0,)))


def _sigmoid(x):
    return 1.0 / (1.0 + jnp.exp(-x))


def _split(x, dt):
    hi = x.astype(dt)
    lo = (x - hi.astype(F32)).astype(dt)
    return hi, lo


def _split_bf16(x):
    return _split(x, BF16)


def _params(sem):
    return pltpu.CompilerParams(dimension_semantics=sem, vmem_limit_bytes=VMEM_LIMIT)


def _resident(shape):
    nd = len(shape)
    return pl.BlockSpec(shape, lambda *_: (0,) * nd, pipeline_mode=pl.Buffered(1))


def _proj_kernel(x_ref, cos_ref, sin_ref, ln_ref, w_ref, wlr_ref, blr_ref, nw_ref, g_ref,
                 gqkv_ref, la_ref, sog_ref, q_ref, k_ref, v_ref, sgab_ref, *, cols):
    cdt = w_ref.dtype
    x = x_ref[...]
    h = (x * lax.rsqrt(jnp.mean(x * x, axis=-1, keepdims=True) + EPS) * ln_ref[...]).astype(cdt)
    section = lambda name: w_ref[:, cols[name][0]:cols[name][1]]
    gqkv_ref[...] = _dot(h, section("gqkv")).astype(cdt)
    og = _dot(h, section("gog"))
    sog_ref[...] = (og * _sigmoid(og)).astype(cdt)
    lr = _dot(h, section("lr")).astype(cdt)
    z = _dot(lr, wlr_ref[...]) + blr_ref[...]
    la_ref[...] = (jnp.minimum(z, 0.0) - jnp.log(1.0 + jnp.exp(-jnp.abs(z)))) / GLA_TAU
    qk = _dot(h, section("qk"))
    sq = (qk * qk).astype(cdt)
    ms = jnp.concatenate([_dot(sq[:, c * 256:(c + 1) * 256], g_ref[...]) for c in range(6)], axis=1)
    qn = qk * lax.rsqrt(ms + EPS) * nw_ref[...]
    width = 2 * SWA_WIDTH
    cos = jnp.tile(cos_ref[...], (1, width // LANES))
    sin = jnp.tile(sin_ref[...], (1, width // LANES))
    lane = lax.broadcasted_iota(jnp.int32, qn.shape, 1)
    half = SWA_HEAD_DIM // 2
    rot = jnp.where(lane % SWA_HEAD_DIM < half, pltpu.roll(qn, width - half, 1), pltpu.roll(qn, half, 1))
    qr = qn * cos + rot * sin
    q_ref[...] = qr[:, :SWA_WIDTH]
    k_ref[...] = qr[:, SWA_WIDTH:]
    v_ref[...] = _dot(h, section("v"))
    gab = _dot(h, section("gab"))
    sgab_ref[...] = _sigmoid(gab).astype(cdt)


def _proj(x, cos, sin, rope_blocks, ln, w, cols, wlr, blr, nw, g, tm):
    t = x.shape[0]
    assert t % tm == 0 and w.dtype == wlr.dtype == g.dtype
    row = lambda width: pl.BlockSpec((tm, width), lambda i: (i, 0))
    outs = [(2048, w.dtype), (512, F32), (1024, w.dtype), (768, F32), (768, F32), (768, F32), (2048, w.dtype)]
    return pl.pallas_call(
        functools.partial(_proj_kernel, cols=cols),
        grid=(t // tm,),
        in_specs=[row(D_MODEL),
                  pl.BlockSpec((tm, LANES), lambda i: (i % rope_blocks, 0)),
                  pl.BlockSpec((tm, LANES), lambda i: (i % rope_blocks, 0)),
                  _resident(ln.shape), _resident(w.shape), _resident(wlr.shape), _resident(blr.shape),
                  _resident(nw.shape), _resident(g.shape)],
        out_specs=[row(wd) for wd, _ in outs],
        out_shape=[jax.ShapeDtypeStruct((t, wd), dt) for wd, dt in outs],
        compiler_params=_params(("arbitrary",)),
        name="proj",
    )(x, cos, sin, ln, w, wlr, blr, nw, g)


def _gla_kernel(gqkv_ref, la_ref, sog_ref, s0_ref, onw_ref, y_ref, sfin_ref, st_scr, *, chunk, n_chunks):
    j = pl.program_id(1)
    nb = gqkv_ref.shape[0]

    @pl.when(j == 0)
    def _():
        for bi in range(nb):
            for h in range(GLA_HEADS):
                st_scr[bi * GLA_HEADS + h] = s0_ref[bi, h].T

    finals = [_gla_block(bi, gqkv_ref, la_ref, sog_ref, onw_ref, y_ref, st_scr, chunk, n_chunks) for bi in range(nb)]

    @pl.when(j == pl.num_programs(1) - 1)
    def _():
        for bi in range(nb):
            for h in range(GLA_HEADS):
                sfin_ref[bi, h] = finals[bi][h].T


def _gla_block(bi, gqkv_ref, la_ref, sog_ref, onw_ref, y_ref, st_scr, chunk, n_chunks):
    r = lax.broadcasted_iota(jnp.int32, (chunk, chunk), 0)
    c = lax.broadcasted_iota(jnp.int32, (chunk, chunk), 1)
    causal = r >= c
    cdt = gqkv_ref.dtype
    tri = causal.astype(cdt)
    hk = GLA_HEADS * GLA_DK
    rows = [slice(ci * chunk, (ci + 1) * chunk) for ci in range(n_chunks)]

    la_hi, la_lo = _split(la_ref[bi], cdt)
    b_chunks = [_dot(tri, la_hi[rs]) + _dot(tri, la_lo[rs]) for rs in rows]
    last = [bc[chunk - 1:chunk, :] for bc in b_chunks]
    b = jnp.concatenate(b_chunks, axis=0)
    b_end = jnp.concatenate([jnp.broadcast_to(x, (chunk, hk)) for x in last], axis=0)
    q = gqkv_ref[bi, :, :hk].astype(F32) * GLA_DK ** -0.5
    k = gqkv_ref[bi, :, hk:2 * hk].astype(F32)
    qd = (q * jnp.exp(b)).astype(cdt)
    kd = (k * jnp.exp(-b)).astype(cdt)
    kdec = (k * jnp.exp(b_end - b)).astype(cdt)
    decay = [jnp.exp(x) for x in last]

    state = [st_scr[bi * GLA_HEADS + h] for h in range(GLA_HEADS)]
    for ci, rs in enumerate(rows):
        outs = []
        for h in range(GLA_HEADS):
            sl = slice(h * GLA_DK, (h + 1) * GLA_DK)
            v_h = gqkv_ref[bi, rs, 2 * hk + h * GLA_DV:2 * hk + (h + 1) * GLA_DV]
            att = jnp.where(causal, _dot_nt(qd[rs, sl], kd[rs, sl]), 0.0).astype(cdt)
            o = _dot(att, v_h) + _dot_nt(qd[rs, sl], state[h].astype(cdt))
            state[h] = state[h] * decay[ci][:, sl] + _dot_tn(v_h, kdec[rs, sl])
            ms = jnp.mean(o * o, axis=-1, keepdims=True)
            outs.append(o * lax.rsqrt(ms + EPS) * onw_ref[...])
        o_all = jnp.concatenate(outs, axis=1) * sog_ref[bi, rs, :].astype(F32)
        y_ref[bi, rs, :] = o_all.astype(cdt)
    for h in range(GLA_HEADS):
        st_scr[bi * GLA_HEADS + h] = state[h]
    return state


def _gla(gqkv, la, sog, s0, onw, chunk, block, nb):
    b, l, _ = gqkv.shape
    assert b % nb == 0 and l % block == 0 and block % chunk == 0
    tok = lambda width: pl.BlockSpec((nb, block, width), lambda bi, j: (bi, j, 0))
    st = pl.BlockSpec((nb, GLA_HEADS, GLA_DK, GLA_DV), lambda bi, j: (bi, 0, 0, 0))
    return pl.pallas_call(
        functools.partial(_gla_kernel, chunk=chunk, n_chunks=block // chunk),
        grid=(b // nb, l // block),
        in_specs=[tok(2048), tok(512), tok(1024), st, _resident(onw.shape)],
        out_specs=[tok(1024), st],
        out_shape=[jax.ShapeDtypeStruct((b, l, 1024), gqkv.dtype),
                   jax.ShapeDtypeStruct((b, GLA_HEADS, GLA_DK, GLA_DV), F32)],
        scratch_shapes=[pltpu.VMEM((nb * GLA_HEADS, GLA_DV, GLA_DK), F32)],
        compiler_params=_params(("arbitrary", "arbitrary")),
        name="gla",
    )(gqkv, la, sog, s0, onw)


def _band_heads(q, kw, vw, valid):
    n = q.shape[0]
    low = lax.broadcasted_iota(jnp.int32, (n, LANES), 1) < SWA_HEAD_DIM
    ones = jnp.ones((kw.shape[0], LANES), kw.dtype)
    outs, lses = [], []
    for pair in range(SWA_OUT // LANES):
        cols = slice(pair * LANES, (pair + 1) * LANES)
        v_aug = jnp.concatenate([vw[:, cols], ones], axis=1)
        res = []
        for mine in (low, jnp.logical_not(low)):
            qm = jnp.where(mine, q[:, cols], 0.0).astype(kw.dtype)
            s = _dot_nt(qm, kw[:, cols]) * SWA_HEAD_DIM ** -0.5
            s = jnp.where(valid, s, NEG)
            m = jnp.max(s, axis=-1, keepdims=True)
            r = _dot(jnp.exp(s - m).astype(vw.dtype), v_aug)
            den = r[:, LANES:]
            res.append((r[:, :LANES] / den, m + jnp.log(den)))
        outs.append(jnp.where(low, res[0][0], res[1][0]))
        lses.append(jnp.where(low, res[0][1], res[1][1]))
    return jnp.concatenate(outs, axis=1), jnp.concatenate(lses, axis=1)


_SWA_TOKENS = 2048


def _swa_kernel(q_ref, k_ref, v_ref, kp_ref, vp_ref, o_ref, lse_ref, *stage, dil):
    blk = SWA_BLOCK
    nsub = q_ref.shape[1] // (blk * dil)
    first = pl.program_id(1) == 0
    qi = lax.broadcasted_iota(jnp.int32, (blk, 2 * blk), 0)
    kj = lax.broadcasted_iota(jnp.int32, (blk, 2 * blk), 1)
    band = (kj >= qi) & (kj <= qi + blk)
    halves = SWA_OUT // LANES

    if dil > 1:
        ins = (q_ref, k_ref, v_ref, kp_ref, vp_ref)
        q_ref, k_ref, v_ref, kp_ref, vp_ref, o_st, lse_st = stage
        for src, dst in zip(ins, stage):
            for hf in range(halves):
                dst[hf] = src[0, :, hf * LANES:(hf + 1) * LANES]

    def rows(ref, start):
        if dil == 1:
            return ref[0, pl.ds(start, blk), :]
        return jnp.concatenate([ref[hf, pl.ds(start, blk, stride=dil), :] for hf in range(halves)], axis=1)

    def unit(u, carry):
        r = u // nsub
        j = u % nsub
        start = r + dil * blk * j
        inside = r + dil * blk * jnp.maximum(j - 1, 0)
        if dil == 1:
            r, start, inside = 0, pl.multiple_of(start, blk), pl.multiple_of(inside, blk)
        head = j == 0
        kprev = jnp.where(head, rows(kp_ref, r), rows(k_ref, inside))
        vprev = jnp.where(head, rows(vp_ref, r), rows(v_ref, inside))
        kw = jnp.concatenate([kprev, rows(k_ref, start)], axis=0).astype(BF16)
        vw = jnp.concatenate([vprev, rows(v_ref, start)], axis=0).astype(BF16)
        valid = band & (kj >= jnp.where(head & first, blk, 0))
        o, lse = _band_heads(rows(q_ref, start), kw, vw, valid)
        if dil == 1:
            o_ref[0, pl.ds(start, blk), :] = o
            lse_ref[0, pl.ds(start, blk), :] = lse
        else:
            for hf in range(halves):
                o_st[hf, pl.ds(start, blk, stride=dil), :] = o[:, hf * LANES:(hf + 1) * LANES]
                lse_st[hf, pl.ds(start, blk, stride=dil), :] = lse[:, hf * LANES:(hf + 1) * LANES]
        return carry

    lax.fori_loop(0, dil * nsub, unit, 0)
    if dil > 1:
        for hf in range(halves):
            o_ref[0, :, hf * LANES:(hf + 1) * LANES] = o_st[hf]
            lse_ref[0, :, hf * LANES:(hf + 1) * LANES] = lse_st[hf]


def _swa_prompt(q, k, v, g, dil):
    b, s, _ = q.shape
    tb = _SWA_TOKENS
    back = SWA_BLOCK * dil
    assert s % tb == 0 and tb % back == 0
    cur = pl.BlockSpec((1, tb, SWA_OUT), lambda bi, i: (bi, i, g))
    prev = pl.BlockSpec((1, back, SWA_OUT), lambda bi, i: (bi, jnp.maximum(i * (tb // back) - 1, 0), g))
    out = pl.BlockSpec((1, tb, SWA_OUT), lambda bi, i: (bi, i, 0))
    halves = SWA_OUT // LANES
    stage = [pltpu.VMEM((halves, n, LANES), F32) for n in (tb, tb, tb, back, back, tb, tb)] if dil > 1 else []
    o, lse = pl.pallas_call(
        functools.partial(_swa_kernel, dil=dil),
        grid=(b, s // tb),
        in_specs=[cur, cur, cur, prev, prev],
        out_specs=[out, out],
        out_shape=[jax.ShapeDtypeStruct((b, s, SWA_OUT), F32)] * 2,
        scratch_shapes=stage,
        compiler_params=_params(("arbitrary", "arbitrary")),
        name=f"swa_prompt_g{g}",
    )(q, k, v, k, v)
    return o.reshape(b * s, SWA_OUT), lse.reshape(b * s, SWA_OUT)


_SAMPLE_ROWS = 16


def _swa_sample_kernel(q_ref, kn_ref, vn_ref, k0_ref, v0_ref, k1_ref, v1_ref, k2_ref, v2_ref,
                       ob_ref, ok0_ref, ov0_ref, ok1_ref, ov1_ref, ok2_ref, ov2_ref, *, n_new):
    rows = _SAMPLE_ROWS
    in_refs = ((k0_ref, v0_ref), (k1_ref, v1_ref), (k2_ref, v2_ref))
    out_refs = ((ok0_ref, ov0_ref), (ok1_ref, ov1_ref), (ok2_ref, ov2_ref))
    scale = SWA_HEAD_DIM ** -0.5
    jn = lax.broadcasted_iota(jnp.int32, (rows, rows), 1)
    ln = lax.broadcasted_iota(jnp.int32, (rows, rows), 0)
    tail = lax.broadcasted_iota(jnp.int32, (SWA_HEAD_DIM, LANES), 1) >= LANES - n_new
    sel_row = lax.broadcasted_iota(jnp.int32, (rows, LANES), 0)
    sel_lane = lax.broadcasted_iota(jnp.int32, (rows, LANES), 1)
    selector = ((sel_lane == sel_row + (LANES - n_new)) & (sel_row < n_new)).astype(F32)
    knt = _dot_tn(kn_ref[0], selector)
    vnt = _dot_tn(vn_ref[0], selector)
    o_g, lse_g = [], []
    for g, (win, dil) in enumerate(SWA_GROUPS):
        jc = lax.broadcasted_iota(jnp.int32, (rows, win), 1)
        lc = lax.broadcasted_iota(jnp.int32, (rows, win), 0)
        valid_c = (jc >= lc) & (((jc - lc) & (dil - 1)) == 0)
        valid_n = (jn <= ln) & (((ln - jn) & (dil - 1)) == 0) & (jn < n_new)
        o_h, lse_h = [], []
        for h in range(SWA_HPG):
            col = g * SWA_OUT + h * SWA_HEAD_DIM
            hsl = slice(col, col + SWA_HEAD_DIM)
            qh = q_ref[0, :, hsl]
            knh = kn_ref[0, :, hsl]
            vnh = vn_ref[0, :, hsl]
            for (src, dst, new_t) in ((in_refs[g][0], out_refs[g][0], knt), (in_refs[g][1], out_refs[g][1], vnt)):
                old = src[0, h]
                moved = pltpu.roll(old, win - n_new, 1)
                if win > LANES:
                    dst[0, h, :, 0:win - LANES] = moved[:, 0:win - LANES]
                dst[0, h, :, win - LANES:win] = jnp.where(tail, new_t[hsl, :], moved[:, win - LANES:win])
            kt = in_refs[g][0][0, h]
            vt = in_refs[g][1][0, h]
            s_c = jnp.where(valid_c, _dot(qh, kt) * scale, NEG)
            s_n = jnp.where(valid_n, _dot_nt(qh, knh) * scale, NEG)
            m = jnp.maximum(jnp.max(s_c, axis=-1, keepdims=True), jnp.max(s_n, axis=-1, keepdims=True))
            p_c = jnp.exp(s_c - m)
            p_n = jnp.exp(s_n - m)
            den = jnp.sum(p_c, axis=-1, keepdims=True) + jnp.sum(p_n, axis=-1, keepdims=True)
            o_h.append((_dot_nt(p_c, vt) + _dot(p_n, vnh)) / den)
            lse_h.append(jnp.broadcast_to(m + jnp.log(den), (rows, SWA_HEAD_DIM)))
        o_g.append(jnp.concatenate(o_h, axis=1))
        lse_g.append(jnp.concatenate(lse_h, axis=1))
    lmax = jnp.maximum(jnp.maximum(lse_g[0], lse_g[1]), lse_g[2])
    e = [jnp.exp(x - lmax) for x in lse_g]
    ob_ref[0] = (e[0] * o_g[0] + e[1] * o_g[1] + e[2] * o_g[2]) / (e[0] + e[1] + e[2])


def _swa_sample(q, kn, vn, caches_t, n_new):
    bd = q.shape[0]
    rows = _SAMPLE_ROWS
    tok = pl.BlockSpec((1, rows, SWA_WIDTH), lambda bi: (bi, 0, 0))
    specs, args = [tok, tok, tok], [q, kn, vn]
    out_specs = [pl.BlockSpec((1, rows, SWA_OUT), lambda bi: (bi, 0, 0))]
    out_shape = [jax.ShapeDtypeStruct((bd, rows, SWA_OUT), F32)]
    for g, (win, dil) in enumerate(SWA_GROUPS):
        for t in caches_t[g]:
            assert t.shape == (bd, SWA_HPG, SWA_HEAD_DIM, win) and win == SWA_BLOCK * dil and win % LANES == 0
            spec = pl.BlockSpec((1, SWA_HPG, SWA_HEAD_DIM, win), lambda bi: (bi, 0, 0, 0))
            args.append(t)
            specs.append(spec)
            out_specs.append(spec)
            out_shape.append(jax.ShapeDtypeStruct(t.shape, t.dtype))
    return pl.pallas_call(
        functools.partial(_swa_sample_kernel, n_new=n_new),
        grid=(bd,),
        in_specs=specs,
        out_specs=out_specs,
        out_shape=out_shape,
        compiler_params=_params(("arbitrary",)),
        name="swa_sample",
    )(*args)


def _merge_kernel(*refs, combine, n_alias, n_real):
    x1_ref, h2c_ref, lg_ref, cnt_ref = refs[-4:]
    refs = refs[:len(refs) - 4 - n_alias]

    @pl.when(pl.program_id(0) >= n_real)
    def _():
        h2c_ref[...] = jnp.zeros_like(h2c_ref)
        lg_ref[...] = jnp.zeros_like(lg_ref)
        cnt_ref[...] = jnp.zeros_like(cnt_ref)

    pl.when(pl.program_id(0) < n_real)(
        functools.partial(_merge_tile, refs, x1_ref, h2c_ref, lg_ref, cnt_ref, combine))


def _merge_tile(refs, x1_ref, h2c_ref, lg_ref, cnt_ref, combine):
    if combine:
        (x_ref, ya_ref, o0, o1, o2, l0, l1, l2, sgab_ref, wa_ref, wb_ref, wo_ref, ln_ref, wr_cat_ref, wr_hi_ref,
         br_ref) = refs
        lmax = jnp.maximum(jnp.maximum(l0[...], l1[...]), l2[...])
        e0, e1, e2 = jnp.exp(l0[...] - lmax), jnp.exp(l1[...] - lmax), jnp.exp(l2[...] - lmax)
        ob = ((e0 * o0[...] + e1 * o1[...] + e2 * o2[...]) / (e0 + e1 + e2)).astype(wb_ref.dtype)
    else:
        (x_ref, ya_ref, ob_ref, sgab_ref, wa_ref, wb_ref, wo_ref, ln_ref, wr_cat_ref, wr_hi_ref, br_ref) = refs
        ob = ob_ref[...]
    ya = _dot(ya_ref[...], wa_ref[...])
    yb = _dot(ob, wb_ref[...])
    sga = sgab_ref[:, :D_MODEL].astype(F32)
    sgb = sgab_ref[:, D_MODEL:].astype(F32)
    x1 = x_ref[...] + _dot((sga * ya + sgb * yb).astype(wo_ref.dtype), wo_ref[...])
    x1_ref[...] = x1
    h2 = x1 * lax.rsqrt(jnp.mean(x1 * x1, axis=-1, keepdims=True) + EPS) * ln_ref[...]
    h_hi, h_lo = _split_bf16(h2)
    both = _dot(h_hi, wr_cat_ref[...])
    lg = both[:, :LANES] + both[:, LANES:] + _dot(h_lo, wr_hi_ref[...]) + br_ref[...]
    lg_ref[...] = lg
    h2c_ref[...] = h_hi
    _, _, _, _, _, hot1, hot2 = _route_select(lg)
    cnt = jnp.sum((hot1 | hot2).astype(F32), axis=0, keepdims=True)
    cnt_ref[...] = jnp.broadcast_to(cnt, cnt_ref.shape)


def _merge(x, ya_in, swa, sgab, weights, tm, tok_off, t_all, shared=None):
    t = x.shape[0]
    assert t % tm == 0 and tok_off % tm == 0 and MERGE_TILE % tm == 0
    combine = len(swa) > 1
    blk_off = tok_off // tm
    t_buf = -(-t_all // MERGE_TILE) * MERGE_TILE
    n_real = t // tm
    n_fill = 0 if shared is not None else (t_buf - tok_off - t) // tm
    row = lambda width: pl.BlockSpec((tm, width), lambda i: (jnp.minimum(i, n_real - 1), 0))
    shared_in = [] if shared is None else list(shared)
    n_in = 3 + len(swa) + len(weights)
    return pl.pallas_call(
        functools.partial(_merge_kernel, combine=combine, n_alias=len(shared_in), n_real=n_real),
        grid=(n_real + n_fill,),
        in_specs=[row(D_MODEL), row(D_MODEL)] + [row(SWA_OUT)] * len(swa) + [row(2 * D_MODEL)]
                 + [_resident(w.shape) for w in weights] + [pl.BlockSpec(memory_space=pl.ANY)] * len(shared_in),
        out_specs=[row(D_MODEL),
                   pl.BlockSpec((tm, D_MODEL), lambda i: (i + blk_off, 0)),
                   pl.BlockSpec((tm, LANES), lambda i: (i + blk_off, 0)),
                   pl.BlockSpec((SUBLANES, LANES), lambda i: (i, 0))],
        out_shape=[jax.ShapeDtypeStruct((t, D_MODEL), F32),
                   jax.ShapeDtypeStruct((t_buf, D_MODEL), BF16),
                   jax.ShapeDtypeStruct((t_buf, LANES), F32),
                   jax.ShapeDtypeStruct(((n_real + n_fill) * SUBLANES, LANES), F32)],
        input_output_aliases={n_in + k: 1 + k for k in range(len(shared_in))},
        compiler_params=_params(("arbitrary",)),
        name="merge",
    )(x, ya_in, *swa, sgab, *weights, *shared_in)


def _route_select(lg):
    lane = lax.broadcasted_iota(jnp.int32, lg.shape, 1)
    big = jnp.int32(LANES)
    gl = jnp.where(lane < N_GROUPS, lg, NEG)
    gmax = jnp.max(gl, axis=-1, keepdims=True)
    g_idx = jnp.min(jnp.where(gl == gmax, lane, big), axis=-1, keepdims=True)
    g_w = 1.0 / jnp.sum(jnp.exp(gl - gmax), axis=-1, keepdims=True)
    e_lane = lane - N_GROUPS
    in_group = (e_lane >= 0) & (e_lane < N_EXPERTS) & (e_lane // EXPERTS_PER_GROUP == g_idx)
    el = jnp.where(in_group, lg, NEG)
    v1 = jnp.max(el, axis=-1, keepdims=True)
    i1 = jnp.min(jnp.where(el == v1, lane, big), axis=-1, keepdims=True)
    el2 = jnp.where(lane == i1, NEG, el)
    v2 = jnp.max(el2, axis=-1, keepdims=True)
    i2 = jnp.min(jnp.where(el2 == v2, lane, big), axis=-1, keepdims=True)
    r21 = jnp.exp(v2 - v1)
    w1 = g_w / (1.0 + r21)
    w2 = g_w * r21 / (1.0 + r21)
    e1 = i1 - N_GROUPS
    e2 = i2 - N_GROUPS
    return lane, e1, e2, w1, w2, lane == e1, lane == e2


def _router_kernel(lg_ref, offs_ref, mi_ref, mw_ref, blk_ref, carry):
    @pl.when(pl.program_id(0) == 0)
    def _():
        carry[...] = offs_ref[0:1, :]

    lane, e1, e2, w1, w2, hot1, hot2 = _route_select(lg_ref[...])
    tr = lane.shape[0]
    hot = (hot1 | hot2).astype(BF16)
    start = carry[...]
    cnt = jnp.sum(hot.astype(F32), axis=0, keepdims=True)
    carry[...] = start + cnt
    r = lax.broadcasted_iota(jnp.int32, (tr, tr), 0)
    c = lax.broadcasted_iota(jnp.int32, (tr, tr), 1)
    within = _dot((r > c).astype(BF16), hot)
    lane1 = lax.broadcasted_iota(jnp.int32, (SUBLANES, LANES), 1)
    incl = jnp.broadcast_to(cnt, (SUBLANES, LANES))
    for sh in (1, 2, 4, 8, 16):
        incl = incl + jnp.where(lane1 >= sh, pltpu.roll(incl, sh, 1), 0.0)
    local = incl[0:1, :] - cnt
    pick = lambda hot_k, row: jnp.sum(jnp.where(hot_k, row, 0.0), axis=-1, keepdims=True)
    cols = [e1, e2]
    for hot_k in (hot1, hot2):
        cols.append((pick(hot_k, within) + pick(hot_k, start)).astype(jnp.int32))
    for hot_k in (hot1, hot2):
        cols.append((pick(hot_k, within) + pick(hot_k, local)).astype(jnp.int32))
    mi = cols[-1]
    for k in range(len(cols) - 2, -1, -1):
        mi = jnp.where(lane == k, cols[k], mi)
    mi_ref[...] = mi
    mw_ref[...] = jnp.where(lane == 0, w1, w2)
    sub = lax.broadcasted_iota(jnp.int32, (SUBLANES, LANES), 0)
    tbl = jnp.where(sub == 0, start, jnp.where(sub == 1, cnt, jnp.where(sub == 2, local, 0.0)))
    blk_ref[...] = tbl.astype(jnp.int32)


def _router(logits, seg_start, t, tr):
    assert t % tr == 0 and tr % SUBLANES == 0
    row = pl.BlockSpec((tr, LANES), lambda i: (i, 0))
    small = pl.BlockSpec((SUBLANES, LANES), lambda i: (i, 0))
    return pl.pallas_call(
        _router_kernel,
        grid=(t // tr,),
        in_specs=[row, pl.BlockSpec((SUBLANES, LANES), lambda i: (0, 0))],
        out_specs=[row, row, small],
        out_shape=[jax.ShapeDtypeStruct((t, LANES), jnp.int32), jax.ShapeDtypeStruct((t, LANES), F32),
                   jax.ShapeDtypeStruct((t // tr * SUBLANES, LANES), jnp.int32)],
        scratch_shapes=[pltpu.VMEM((1, LANES), F32)],
        compiler_params=_params(("arbitrary",)),
        name="router",
    )(logits, seg_start)


def _run_copies(src, dst, sem, src_row, dst_row, n_rows, wait=False):
    nsub = D_MODEL // LANES

    def copy(s0, d0, rows):
        dma = pltpu.make_async_copy(src.at[pl.ds(pl.multiple_of(s0 * nsub, nsub), rows * nsub), :],
                                    dst.at[pl.ds(pl.multiple_of(d0 * nsub, nsub), rows * nsub), :], sem)
        dma.wait() if wait else dma.start()

    def chunk(k, carry):
        copy(src_row + k * SUBLANES, dst_row + k * SUBLANES, SUBLANES)
        return carry

    n_chunks = n_rows // SUBLANES
    lax.fori_loop(0, n_chunks, chunk, 0)
    done = n_chunks * SUBLANES
    for rows in (4, 2, 1):
        @pl.when((n_rows & rows) != 0)
        def _(rows=rows, done=done):
            copy(src_row + done, dst_row + done, rows)

        done = done + (n_rows & rows)


def _dispatch_kernel(start_ref, cnt_ref, loc_ref, end_ref, tot_ref, h_ref, mi_ref, xs_hbm, stage, zeros, sem, zsem,
                     *, tm):
    i = pl.program_id(0)
    n = pl.num_programs(0)
    nsub = D_MODEL // LANES
    tb = h_ref.shape[0]
    slot = i % 2

    def wait_stage(sl):
        pltpu.make_async_copy(stage.at[sl], stage.at[sl], sem.at[sl]).wait()

    def pad_rows(wait):
        def one(e, carry):
            tot = tot_ref[e]
            n_pad = (tot + tm - 1) // tm * tm - tot
            _run_copies(zeros, xs_hbm, zsem, 0, end_ref[e] - n_pad, n_pad, wait=wait)
            return carry

        lax.fori_loop(0, end_ref.shape[0], one, 0)

        def spare(j, carry):
            dma = pltpu.make_async_copy(zeros, xs_hbm.at[pl.ds(pl.multiple_of(j * tm * nsub, nsub), tm * nsub), :], zsem)
            dma.wait() if wait else dma.start()
            return carry

        lax.fori_loop(end_ref[end_ref.shape[0] - 1] // tm, xs_hbm.shape[0] // (tm * nsub), spare, 0)

    @pl.when(i == 0)
    def _():
        zeros[...] = jnp.zeros_like(zeros)
        pad_rows(wait=False)

    @pl.when(i >= 2)
    def _():
        wait_stage(slot)

    li = mi_ref[...]
    rows_iota = lax.broadcasted_iota(jnp.int32, (tb, 2 * tb), 1)
    onehot = ((rows_iota == li[:, 4:5]) | (rows_iota == li[:, 5:6])).astype(BF16)
    srt = _dot_tn(onehot, h_ref[...])
    for c in range(nsub):
        stage[slot, pl.ds(c, 2 * tb, stride=nsub), :] = srt[:, c * LANES:(c + 1) * LANES]

    def run(e, carry):
        k = i * N_EXPERTS + e
        _run_copies(stage.at[slot], xs_hbm, sem.at[slot], loc_ref[k], start_ref[k], cnt_ref[k])
        return carry

    lax.fori_loop(0, N_EXPERTS, run, 0)

    @pl.when(i == n - 1)
    def _():
        wait_stage(slot)

        @pl.when(n >= 2)
        def _():
            wait_stage(1 - slot)

        pad_rows(wait=True)


def _dispatch(blk_start, blk_cnt, blk_loc, seg_end, seg_tot, h2, meta_i, n_slots, tb):
    t = meta_i.shape[0]
    nsub = D_MODEL // LANES
    assert t % tb == 0
    grid_spec = pltpu.PrefetchScalarGridSpec(
        num_scalar_prefetch=5,
        grid=(t // tb,),
        in_specs=[pl.BlockSpec((tb, D_MODEL), lambda i, *_: (i, 0)),
                  pl.BlockSpec((tb, LANES), lambda i, *_: (i, 0))],
        out_specs=pl.BlockSpec(memory_space=pl.ANY),
        scratch_shapes=[pltpu.VMEM((2, 2 * tb * nsub, LANES), F32),
                        pltpu.VMEM((EXPERT_TILE * nsub, LANES), F32),
                        pltpu.SemaphoreType.DMA((2,)), pltpu.SemaphoreType.DMA(())],
    )
    return pl.pallas_call(
        functools.partial(_dispatch_kernel, tm=EXPERT_TILE),
        grid_spec=grid_spec,
        out_shape=jax.ShapeDtypeStruct((n_slots * nsub, LANES), F32),
        compiler_params=_params(("arbitrary",)),
        name="dispatch",
    )(blk_start, blk_cnt, blk_loc, seg_end, seg_tot, h2, meta_i)


def _expert_kernel(te_ref, tv_ref, nxt_ref, xs_ref, wg_hbm, wu_hbm, wd_hbm, out_ref,
                   wg_st, wu_st, wd_st, wgb, wub, wdb, n_switch, sem):
    i = pl.program_id(0)
    tm = EXPERT_TILE
    nsub = D_MODEL // LANES
    valid = tv_ref[i] != 0
    e = te_ref[i]

    def weight_copies(expert, sl):
        return [pltpu.make_async_copy(hbm.at[0, expert], st.at[sl], sem.at[sl, k])
                for k, (hbm, st) in enumerate(((wg_hbm, wg_st), (wu_hbm, wu_st), (wd_hbm, wd_st)))]

    @pl.when(i == 0)
    def _():
        n_switch[0] = 0
        for dma in weight_copies(e, 0):
            dma.start()

    @pl.when(jnp.logical_not(valid))
    def _():
        out_ref[...] = jnp.zeros_like(out_ref)

    @pl.when(valid & ((i == 0) | (e != te_ref[jnp.maximum(i - 1, 0)])))
    def _():
        sl = n_switch[0] % 2
        n_switch[0] = n_switch[0] + 1
        for dma in weight_copies(e, sl):
            dma.wait()
        wgb[...] = wg_st[sl].astype(BF16)
        wub[...] = wu_st[sl].astype(BF16)
        wdb[...] = wd_st[sl].astype(BF16)

        @pl.when(nxt_ref[e] != e)
        def _():
            for dma in weight_copies(nxt_ref[e], 1 - sl):
                dma.start()

    @pl.when(valid)
    def _():
        h = jnp.concatenate([xs_ref[pl.ds(c, tm, stride=nsub), :] for c in range(nsub)], axis=1).astype(BF16)
        gate = _dot(h, wgb[...])
        up = _dot(h, wub[...])
        a = (gate * _sigmoid(gate) * up).astype(BF16)
        o = _dot(a, wdb[...])
        for c in range(nsub):
            out_ref[pl.ds(c, tm, stride=nsub), :] = o[:, c * LANES:(c + 1) * LANES]


def _experts(tile_expert, tile_valid, next_expert, xs, wg, wu, wd):
    n_tiles = tile_expert.shape[0]
    tm = EXPERT_TILE
    nsub = D_MODEL // LANES
    used = lambda i, tv: jnp.where(tv[i] != 0, i, 0)
    anywhere = pl.BlockSpec(memory_space=pl.ANY)
    up_shape, down_shape = (D_MODEL, D_EXPERT), (D_EXPERT, D_MODEL)
    grid_spec = pltpu.PrefetchScalarGridSpec(
        num_scalar_prefetch=3,
        grid=(n_tiles,),
        in_specs=[pl.BlockSpec((tm * nsub, LANES), lambda i, te, tv, nx: (used(i, tv), 0)),
                  anywhere, anywhere, anywhere],
        out_specs=pl.BlockSpec((tm * nsub, LANES), lambda i, te, tv, nx: (i, 0)),
        scratch_shapes=[pltpu.VMEM((2,) + up_shape, F32), pltpu.VMEM((2,) + up_shape, F32),
                        pltpu.VMEM((2,) + down_shape, F32),
                        pltpu.VMEM(up_shape, BF16), pltpu.VMEM(up_shape, BF16), pltpu.VMEM(down_shape, BF16),
                        pltpu.SMEM((1,), jnp.int32), pltpu.SemaphoreType.DMA((2, 3))],
    )
    return pl.pallas_call(
        _expert_kernel,
        grid_spec=grid_spec,
        out_shape=jax.ShapeDtypeStruct((n_tiles * tm * nsub, LANES), F32),
        compiler_params=_params(("arbitrary",)),
        name="experts",
    )(tile_expert, tile_valid, next_expert, xs, wg, wu, wd)


def _combine_kernel(p0_ref, p1_ref, x1_ref, mw_ref, rows_hbm, y_ref, g0, g1, sem, *, tc, blk_off):
    i = pl.program_id(0)
    n = pl.num_programs(0)
    nout = D_MODEL // LANES
    slot = i % 2

    def issue_all(step, sl):
        base = (step + blk_off) * tc

        def issue(j, carry):
            for pos_ref, dst, s in ((p0_ref, g0, 0), (p1_ref, g1, 1)):
                src0 = pl.multiple_of(pos_ref[base + j] * nout, nout)
                pltpu.make_async_copy(rows_hbm.at[pl.ds(src0, nout), :],
                                      dst.at[sl, pl.ds(pl.multiple_of(j * nout, nout), nout), :],
                                      sem.at[sl, s]).start()
            return carry

        lax.fori_loop(0, tc, issue, 0, unroll=4)

    @pl.when(i == 0)
    def _():
        issue_all(i, slot)

    @pl.when(i + 1 < n)
    def _():
        issue_all(i + 1, 1 - slot)

    for dst, s in ((g0, 0), (g1, 1)):
        pltpu.make_async_copy(rows_hbm.at[pl.ds(0, tc * nout), :], dst.at[slot], sem.at[slot, s]).wait()
    w0 = mw_ref[:, 0:1]
    w1 = mw_ref[:, 1:2]
    for c in range(nout):
        sl = slice(c * LANES, (c + 1) * LANES)
        y_ref[:, sl] = (x1_ref[:, sl] + w0 * g0[slot, pl.ds(c, tc, stride=nout), :]
                        + w1 * g1[slot, pl.ds(c, tc, stride=nout), :])


def _combine(pos0, pos1, x1, meta_w, rows, tc, tok_off):
    t = x1.shape[0]
    assert t % tc == 0 and tok_off % tc == 0
    blk_off = tok_off // tc
    nout = D_MODEL // LANES
    row = pl.BlockSpec((tc, D_MODEL), lambda i, p0, p1: (i, 0))
    grid_spec = pltpu.PrefetchScalarGridSpec(
        num_scalar_prefetch=2,
        grid=(t // tc,),
        in_specs=[row, pl.BlockSpec((tc, LANES), lambda i, p0, p1: (i + blk_off, 0)),
                  pl.BlockSpec(memory_space=pl.ANY)],
        out_specs=row,
        scratch_shapes=[pltpu.VMEM((2, tc * nout, LANES), F32), pltpu.VMEM((2, tc * nout, LANES), F32),
                        pltpu.SemaphoreType.DMA((2, 2))],
    )
    return pl.pallas_call(
        functools.partial(_combine_kernel, tc=tc, blk_off=blk_off),
        grid_spec=grid_spec,
        out_shape=jax.ShapeDtypeStruct((t, D_MODEL), F32),
        compiler_params=_params(("arbitrary",)),
        name="combine",
    )(pos0, pos1, x1, meta_w, rows)


def _rope_tables(pos):
    half = SWA_HEAD_DIM // 2
    inv_freq = ROPE_THETA ** (-np.arange(half, dtype=np.float64) / half)
    ang = pos.astype(np.float64)[:, None] * inv_freq[None, :]
    cos, sin = np.cos(ang), np.sin(ang)
    reps = LANES // SWA_HEAD_DIM
    table = lambda a, b: jnp.asarray(np.tile(np.concatenate([a, b], axis=1), (1, reps)).astype(np.float32))
    return table(cos, cos), table(-sin, sin)


def kernel(x_prompt, x_sample, state_gla, cache_swa_k0, cache_swa_v0, cache_swa_k1, cache_swa_v1, cache_swa_k2, cache_swa_v2, ln1_w, w_in, w_gla_lr, b_gla_lr, gla_onorm_w, q_norm_w, k_norm_w, w_branch_a, w_branch_b, w_out, ln2_w, w_router_group, b_router_group, w_router_expert, b_router_expert, w_exp_gate, w_exp_up, w_exp_down):
    b, s, d = x_prompt.shape
    bd, ls, _ = x_sample.shape
    tp, ts = b * s, bd * ls
    assert w_in.shape[0] == 1 and d == D_MODEL and ts % SUBLANES == 0
    k_caches = (cache_swa_k0, cache_swa_k1, cache_swa_k2)
    v_caches = (cache_swa_v0, cache_swa_v1, cache_swa_v2)

    w = w_in[0]
    cuts = np.cumsum((512, 512, 1024, GLA_RANK, 1024, 768, 768, 768, 1024, 1024))
    sec = lambda a: w[:, (0 if a == 0 else cuts[a - 1]):cuts[a]]
    lr_pad = jnp.pad(sec(3), ((0, 0), (0, LANES - GLA_RANK)))
    w_packed = jnp.concatenate([t.astype(BF16) for t in (sec(0), sec(1), sec(2), sec(4), sec(5), sec(6), sec(7),
                                                         sec(8), sec(9), lr_pad)], axis=1)
    wlr_f32 = jnp.pad(w_gla_lr[0], ((0, LANES - GLA_RANK), (0, 0)))
    wlr = wlr_f32.astype(BF16)
    blr = b_gla_lr[0][None, :]
    nw = jnp.concatenate([jnp.tile(q_norm_w[0], SWA_WIDTH // SWA_HEAD_DIM), jnp.tile(k_norm_w[0], SWA_WIDTH // SWA_HEAD_DIM)])[None, :]
    gi = np.arange(256) // SWA_HEAD_DIM
    gmat_f32 = jnp.asarray((gi[:, None] == gi[None, :]).astype(np.float32) / SWA_HEAD_DIM)
    gmat = gmat_f32.astype(BF16)
    ln1 = ln1_w[0][None, :]
    ln2 = ln2_w[0][None, :]
    onw = gla_onorm_w[0][None, :]
    wa_f32, wb_f32, wo_f32 = w_branch_a[0], w_branch_b[0], w_out[0]
    wa, wb, wo = wa_f32.astype(BF16), wb_f32.astype(BF16), wo_f32.astype(BF16)
    wr = jnp.pad(jnp.concatenate([w_router_group[0], w_router_expert[0]], axis=1),
                 ((0, 0), (0, LANES - N_GROUPS - N_EXPERTS)))
    wr_hi, wr_lo = _split_bf16(wr)
    br = jnp.pad(jnp.concatenate([b_router_group[0], b_router_expert[0]]), (0, LANES - N_GROUPS - N_EXPERTS))[None, :]

    cos_p, sin_p = _rope_tables(np.arange(s))
    cos_s, sin_s = _rope_tables(PAST_LEN + np.arange(ts) % ls)

    tm_p = 512
    proj_p = _proj(x_prompt.reshape(tp, d), cos_p, sin_p, s // tm_p, ln1, w_packed, _PACKED_COLS, wlr, blr, nw, gmat,
                   tm_p)
    proj_s = _proj(x_sample.reshape(ts, d), cos_s, sin_s, 1, ln1, w, _RAW_COLS, wlr_f32, blr, nw, gmat_f32, ts)
    gqkv_p, la_p, sog_p, q_p, k_p, v_p, sgab_p = proj_p
    gqkv_s, la_s, sog_s, q_s, k_s, v_s, sgab_s = proj_s

    r3 = lambda t, nb: t.reshape(nb, t.shape[0] // nb, t.shape[1])
    ya_p, st_p = _gla(r3(gqkv_p, b), r3(la_p, b), r3(sog_p, b),
                      jnp.zeros((b, GLA_HEADS, GLA_DK, GLA_DV), F32), onw, GLA_CHUNK, 512, b)
    pad_s = lambda t: jnp.pad(r3(t, bd), ((0, 0), (0, _SAMPLE_ROWS - ls), (0, 0)))
    ya_s, st_s = _gla(pad_s(gqkv_s), pad_s(la_s), pad_s(sog_s), state_gla[0], onw, _SAMPLE_ROWS, _SAMPLE_ROWS, 8)
    ya_s = ya_s[:, :ls].reshape(ts, d)

    q3, k3, v3 = r3(q_p, b), r3(k_p, b), r3(v_p, b)
    swa_p = [_swa_prompt(q3, k3, v3, g, dil) for g, (_, dil) in enumerate(SWA_GROUPS)]
    to_t = lambda c: jnp.transpose(c[0], (0, 2, 3, 1))
    caches_t = [(to_t(k_caches[g]), to_t(v_caches[g])) for g in range(len(SWA_GROUPS))]
    ob_s, *new_caches = _swa_sample(pad_s(q_s), pad_s(k_s), pad_s(v_s), caches_t, ls)
    ob_s = ob_s[:, :ls].reshape(ts, SWA_OUT)

    t = tp + ts
    mw = (wa, wb, wo, ln2, jnp.concatenate([wr_hi, wr_lo], axis=1), wr_hi, br)
    x1_p, h2c, lg, cnt_p = _merge(x_prompt.reshape(tp, d), ya_p.reshape(tp, d),
                                  [o for o, _ in swa_p] + [l for _, l in swa_p], sgab_p, mw, MERGE_TILE, 0, t)
    mw_f32 = (wa_f32, wb_f32, wo_f32) + mw[3:]
    x1_s, h2c, lg, cnt_s = _merge(x_sample.reshape(ts, d), ya_s, [ob_s], sgab_s, mw_f32, ts, tp, t,
                                  shared=(h2c, lg))

    tm = EXPERT_TILE
    n_tiles = (2 * t) // tm + N_EXPERTS
    counts = ((jnp.sum(cnt_p, axis=0) + jnp.sum(cnt_s, axis=0)) * (1.0 / SUBLANES)).astype(jnp.int32)
    ends_all = jnp.cumsum((counts + tm - 1) // tm * tm)
    seg_start = jnp.broadcast_to((ends_all - (counts + tm - 1) // tm * tm).astype(F32)[None, :], (SUBLANES, LANES))
    counts, ends = counts[:N_EXPERTS], ends_all[:N_EXPERTS]
    tb = DISPATCH_BLOCK
    meta_i, meta_w, blocks = _router(lg, seg_start, t, tb)
    blocks = blocks.reshape(t // tb, SUBLANES, LANES)[:, :, :N_EXPERTS]
    blk_start, blk_cnt, blk_loc = (blocks[:, r].reshape(-1) for r in range(3))
    pos0, pos1 = meta_i[:, 2], meta_i[:, 3]
    tile_start = jnp.arange(n_tiles, dtype=jnp.int32) * tm
    tile_valid = (tile_start < ends[-1]).astype(jnp.int32)
    last_slot = jnp.minimum(tile_start, ends[-1] - 1)
    tile_expert = jnp.minimum(jnp.sum((last_slot[:, None] >= ends[None, :]).astype(jnp.int32), axis=1), N_EXPERTS - 1)

    xs = _dispatch(blk_start, blk_cnt, blk_loc, ends, counts, h2c, meta_i, n_tiles * tm, tb)
    ids = jnp.arange(N_EXPERTS, dtype=jnp.int32)
    later_used = (counts[None, :] > 0) & (ids[None, :] > ids[:, None])
    next_expert = jnp.min(jnp.where(later_used, ids[None, :], N_EXPERTS), axis=1)
    next_expert = jnp.where(next_expert == N_EXPERTS, ids, next_expert)
    rows = _experts(tile_expert, tile_valid, next_expert, xs, w_exp_gate, w_exp_up, w_exp_down)
    y_p = _combine(pos0, pos1, x1_p, meta_w, rows, 512, 0)
    y_s = _combine(pos0, pos1, x1_s, meta_w, rows, ts, tp)

    heads = lambda a: a.reshape(1, a.shape[0], a.shape[1], SWA_HPG, SWA_HEAD_DIM)
    outs = [y_p.reshape(b, s, d), y_s.reshape(bd, ls, d), st_p[None].astype(x_prompt.dtype)]
    for g, (win, _) in enumerate(SWA_GROUPS):
        keep = min(win, s)
        gsl = slice(g * SWA_OUT, (g + 1) * SWA_OUT)
        outs += [heads(k3[:, s - keep:, gsl]), heads(v3[:, s - keep:, gsl])]
    outs.append(st_s[None].astype(state_gla.dtype))
    outs += [jnp.transpose(c, (0, 3, 1, 2))[None] for c in new_caches]
    return tuple(outs)
```
